```python
import math
import jax, jax.numpy as jnp
from jax import lax
import numpy as np


D_MODEL = 1024
BATCH = 2
SEQ = 8192
DEPTH = 2

D_MIX = D_MODEL
N_MIXERS = 4
GROUP_W = D_MIX // N_MIXERS
Q_BLOCK = 128
EPS = 1e-6

MOBA_HEADS = 4
MOBA_HD = GROUP_W // MOBA_HEADS
MOBA_BLOCK = 256
MOBA_TOPK = 3

SSM_HEADS = 4
SSM_HD = GROUP_W // SSM_HEADS
SSM_STATE = 128
SSM_GROUPS = 2
SSM_CONV = 4
SSM_CHUNK = 128
SSM_XBC = GROUP_W + 2 * SSM_GROUPS * SSM_STATE

MLSTM_HEADS = 4
MLSTM_HD = GROUP_W // MLSTM_HEADS
MLSTM_CONV = 4
MLSTM_CHUNK = 128

MLA_HEADS = 4
MLA_Q_LORA = 256
MLA_KV_LORA = 128
MLA_NOPE = 64
MLA_ROPE = 32
MLA_V = GROUP_W // MLA_HEADS
ROPE_THETA = 10000.0

N_EXPERT_GROUPS = 4
EXPERTS_PER_GROUP = 4
N_EXPERTS = N_EXPERT_GROUPS * EXPERTS_PER_GROUP
TOP_K_IN_GROUP = 2
D_EXPERT = 256

IN_SIZES = (3 * GROUP_W,
            GROUP_W,
            SSM_XBC,
            SSM_HEADS,
            2 * GROUP_W,
            GROUP_W,
            MLSTM_HEADS,
            MLSTM_HEADS,
            GROUP_W,
            MLA_Q_LORA,
            MLA_KV_LORA,
            MLA_ROPE)
N_IN = sum(IN_SIZES)

kernel_name = 'hybrid_moba_ssd_mlstm_mla_hmoe'

F32 = jnp.float32


def rms_norm(x, g):
    xf = x.astype(F32)
    y = xf * lax.rsqrt(jnp.mean(xf * xf, axis=-1, keepdims=True) + EPS)
    return (y * g.astype(F32)).astype(x.dtype)


def causal_conv(x, w, b):
    k = w.shape[0]
    y = lax.conv_general_dilated(x, w[:, None, :].astype(x.dtype), window_strides=(1,),
                                 padding=[(k - 1, 0)], dimension_numbers=('NWC', 'WIO', 'NWC'),
                                 feature_group_count=x.shape[-1])
    return y + b.astype(x.dtype)


def rope_tables(positions):
    inv = ROPE_THETA ** (-jnp.arange(0, MLA_ROPE, 2, dtype=F32) / MLA_ROPE)
    ang = positions.astype(F32)[..., None] * inv
    return jnp.cos(ang), jnp.sin(ang)


def apply_rope(x, cos, sin):
    half = x.shape[-1] // 2
    x1 = x[..., :half].astype(F32)
    x2 = x[..., half:].astype(F32)
    return jnp.concatenate([x1 * cos - x2 * sin, x2 * cos + x1 * sin], axis=-1).astype(x.dtype)


def moba_attention(q, k, v):
    bsz, s, h, dh = q.shape
    nb = -(-s // MOBA_BLOCK)
    pad = nb * MOBA_BLOCK - s
    topk = min(MOBA_TOPK, nb)
    nq = s // Q_BLOCK
    scale = dh ** -0.5
    kb = jnp.pad(k, ((0, 0), (0, pad), (0, 0), (0, 0))).reshape(bsz, nb, MOBA_BLOCK, h, dh).transpose(0, 3, 1, 2, 4)
    vb = jnp.pad(v, ((0, 0), (0, pad), (0, 0), (0, 0))).reshape(bsz, nb, MOBA_BLOCK, h, dh).transpose(0, 3, 1, 2, 4)
    k_mean = jnp.mean(kb.astype(F32), axis=3)
    q_blocks = q.reshape(bsz, nq, Q_BLOCK, h, dh).transpose(1, 0, 3, 2, 4)
    take_blocks = jax.vmap(jax.vmap(lambda blocks, idx: blocks[idx]))

    def one_block(args):
        ci, qi = args
        q_pos = ci * Q_BLOCK + jnp.arange(Q_BLOCK)
        cur = (ci * Q_BLOCK) // MOBA_BLOCK
        gate = jnp.einsum('bhqd,bhnd->bhqn', qi.astype(F32), k_mean)
        gate = jnp.where(jnp.arange(nb) < cur, gate, -jnp.inf)
        _, sel = lax.top_k(gate, topk)
        valid = sel < cur
        k_sel = take_blocks(kb, sel)
        v_sel = take_blocks(vb, sel)
        s_sel = jnp.einsum('bhqd,bhqnjd->bhqnj', qi, k_sel).astype(F32) * scale
        s_sel = jnp.where(valid[..., None], s_sel, -jnp.inf).reshape(bsz, h, Q_BLOCK, topk * MOBA_BLOCK)
        k_own = lax.dynamic_index_in_dim(kb, cur, axis=2, keepdims=False)
        v_own = lax.dynamic_index_in_dim(vb, cur, axis=2, keepdims=False)
        k_pos = cur * MOBA_BLOCK + jnp.arange(MOBA_BLOCK)
        s_own = jnp.einsum('bhqd,bhjd->bhqj', qi, k_own).astype(F32) * scale
        s_own = jnp.where(k_pos[None, :] <= q_pos[:, None], s_own, -jnp.inf)
        p = jax.nn.softmax(jnp.concatenate([s_sel, s_own], axis=-1), axis=-1).astype(v.dtype)
        p_sel = p[..., :topk * MOBA_BLOCK].reshape(bsz, h, Q_BLOCK, topk, MOBA_BLOCK)
        p_own = p[..., topk * MOBA_BLOCK:]
        return (jnp.einsum('bhqnj,bhqnjd->bhqd', p_sel, v_sel)
                + jnp.einsum('bhqj,bhjd->bhqd', p_own, v_own))

    out = lax.map(one_block, (jnp.arange(nq), q_blocks))
    return out.transpose(1, 0, 3, 2, 4).reshape(bsz, s, h * dh)


def mamba2_ssd(z, xbc, dt_raw, conv_w, conv_b, dt_bias, a_log, d_skip, norm_g):
    bsz, s, _ = xbc.shape
    hh, pp, nn, ll = SSM_HEADS, SSM_HD, SSM_STATE, SSM_CHUNK
    nc = s // ll
    xbc = jax.nn.silu(causal_conv(xbc, conv_w, conv_b)).astype(F32)
    x = xbc[..., :GROUP_W].reshape(bsz, s, hh, pp)
    rep = hh // SSM_GROUPS
    bm = jnp.repeat(xbc[..., GROUP_W:GROUP_W + SSM_GROUPS * nn].reshape(bsz, s, SSM_GROUPS, nn), rep, axis=2)
    cm = jnp.repeat(xbc[..., GROUP_W + SSM_GROUPS * nn:].reshape(bsz, s, SSM_GROUPS, nn), rep, axis=2)
    dt = jax.nn.softplus(dt_raw.astype(F32) + dt_bias.astype(F32))
    a = -jnp.exp(a_log.astype(F32))
    xc = x.reshape(bsz, nc, ll, hh, pp)
    bc = bm.reshape(bsz, nc, ll, hh, nn)
    cc = cm.reshape(bsz, nc, ll, hh, nn)
    dtc = dt.reshape(bsz, nc, ll, hh)
    a_cum = jnp.cumsum(dtc * a, axis=2)
    causal = jnp.tril(jnp.ones((ll, ll), dtype=bool))
    seg = a_cum[:, :, :, None, :] - a_cum[:, :, None, :, :]
    decay_in = jnp.exp(jnp.where(causal[None, None, :, :, None], seg, -jnp.inf))
    xdt = xc * dtc[..., None]
    scores = jnp.einsum('bclhn,bcshn->bclsh', cc, bc) * decay_in
    y_diag = jnp.einsum('bclsh,bcshp->bclhp', scores, xdt)
    decay_out = jnp.exp(a_cum[:, :, -1:, :] - a_cum)
    chunk_states = jnp.einsum('bclhn,bclh,bclhp->bchpn', bc, decay_out, xdt)
    chunk_decay = jnp.exp(a_cum[:, :, -1, :])

    def carry_state(state, inp):
        st, dec = inp
        return state * dec[:, :, None, None] + st, state

    _, prev = lax.scan(carry_state, jnp.zeros((bsz, hh, pp, nn), F32),
                       (jnp.moveaxis(chunk_states, 1, 0), jnp.moveaxis(chunk_decay, 1, 0)))
    prev = jnp.moveaxis(prev, 0, 1)
    y_off = jnp.einsum('bclhn,bchpn,bclh->bclhp', cc, prev, jnp.exp(a_cum))
    y = (y_diag + y_off).reshape(bsz, s, hh, pp) + x * d_skip.astype(F32)[:, None]
    y = y.reshape(bsz, s, GROUP_W) * jax.nn.silu(z.astype(F32))
    gsz = GROUP_W // SSM_GROUPS
    y = rms_norm(y.reshape(bsz, s, SSM_GROUPS, gsz), norm_g.reshape(SSM_GROUPS, gsz))
    return y.reshape(bsz, s, GROUP_W).astype(z.dtype)


def mlstm_mixer(qk, v, i_pre, f_pre, o_pre, conv_w, conv_b, ig_bias, fg_bias, norm_g):
    bsz, s, _ = v.shape
    hh, dd, ll = MLSTM_HEADS, MLSTM_HD, MLSTM_CHUNK
    nc = s // ll
    qk = jax.nn.silu(causal_conv(qk, conv_w, conv_b)).astype(F32)
    q = qk[..., :GROUP_W]
    k = qk[..., GROUP_W:] * (dd ** -0.5)

    def heads(t):
        return t.reshape(bsz, nc, ll, hh, dd).transpose(1, 0, 3, 2, 4)

    def gates(t):
        return t.reshape(bsz, nc, ll, hh).transpose(1, 0, 3, 2)

    qc, kc, vc = heads(q), heads(k), heads(v.astype(F32))
    igc = gates(i_pre.astype(F32) + ig_bias.astype(F32))
    lfc = gates(jax.nn.log_sigmoid(f_pre.astype(F32) + fg_bias.astype(F32)))
    causal = jnp.tril(jnp.ones((ll, ll), dtype=bool))

    def chunk_step(carry, inp):
        c_st, n_st, m_st = carry
        qb, kb, vb, ig, lf = inp
        b = jnp.cumsum(lf, axis=-1)
        dmat = jnp.where(causal, b[..., :, None] - b[..., None, :] + ig[..., None, :], -jnp.inf)
        inter = b + m_st[..., None]
        m_t = jnp.maximum(inter, jnp.max(dmat, axis=-1))
        sc = jnp.einsum('bhld,bhsd->bhls', qb, kb) * jnp.exp(dmat - m_t[..., None])
        w_inter = jnp.exp(inter - m_t)
        num = jnp.einsum('bhls,bhse->bhle', sc, vb) + w_inter[..., None] * jnp.einsum('bhld,bhde->bhle', qb, c_st)
        den = jnp.sum(sc, axis=-1) + w_inter * jnp.einsum('bhld,bhd->bhl', qb, n_st)
        h = num / jnp.maximum(jnp.abs(den), jnp.exp(-m_t))[..., None]
        g = b[..., -1:] - b + ig
        m_new = jnp.maximum(b[..., -1] + m_st, jnp.max(g, axis=-1))
        w_state = jnp.exp(g - m_new[..., None])
        keep = jnp.exp(b[..., -1] + m_st - m_new)
        c_new = keep[..., None, None] * c_st + jnp.einsum('bhs,bhsd,bhse->bhde', w_state, kb, vb)
        n_new = keep[..., None] * n_st + jnp.einsum('bhs,bhsd->bhd', w_state, kb)
        return (c_new, n_new, m_new), h

    init = (jnp.zeros((bsz, hh, dd, dd), F32), jnp.zeros((bsz, hh, dd), F32), jnp.zeros((bsz, hh), F32))
    _, hs = lax.scan(chunk_step, init, (qc, kc, vc, igc, lfc))
    h = hs.transpose(1, 0, 3, 2, 4).reshape(bsz, s, hh, dd)
    h = jax.nn.sigmoid(o_pre.astype(F32)).reshape(bsz, s, hh, dd) * h
    h = rms_norm(h, norm_g.reshape(hh, dd))
    return h.reshape(bsz, s, GROUP_W).astype(v.dtype)


def mla_attention(c_q, c_kv, k_rope, cos, sin, q_norm_g, w_uq, kv_norm_g, w_ukv):
    bsz, s, _ = c_q.shape
    nq = s // Q_BLOCK
    hh = MLA_HEADS
    q = (rms_norm(c_q, q_norm_g) @ w_uq).reshape(bsz, s, hh, MLA_NOPE + MLA_ROPE)
    kv = (rms_norm(c_kv, kv_norm_g) @ w_ukv).reshape(bsz, s, hh, MLA_NOPE + MLA_V)
    q_nope = q[..., :MLA_NOPE]
    q_pe = apply_rope(q[..., MLA_NOPE:], cos[:, :, None, :], sin[:, :, None, :])
    k_nope = kv[..., :MLA_NOPE].transpose(0, 2, 1, 3)
    v = kv[..., MLA_NOPE:].transpose(0, 2, 1, 3)
    k_pe = apply_rope(k_rope, cos, sin)
    qn_b = q_nope.reshape(bsz, nq, Q_BLOCK, hh, MLA_NOPE).transpose(1, 0, 3, 2, 4)
    qp_b = q_pe.reshape(bsz, nq, Q_BLOCK, hh, MLA_ROPE).transpose(1, 0, 3, 2, 4)
    scale = (MLA_NOPE + MLA_ROPE) ** -0.5
    k_pos = jnp.arange(s)

    def one_block(args):
        ci, qn, qp = args
        q_pos = ci * Q_BLOCK + jnp.arange(Q_BLOCK)
        sc = (jnp.einsum('bhqd,bhkd->bhqk', qn, k_nope)
              + jnp.einsum('bhqr,bkr->bhqk', qp, k_pe)).astype(F32) * scale
        sc = jnp.where(k_pos[None, :] <= q_pos[:, None], sc, -jnp.inf)
        p = jax.nn.softmax(sc, axis=-1).astype(v.dtype)
        return jnp.einsum('bhqk,bhkd->bhqd', p, v)

    out = lax.map(one_block, (jnp.arange(nq), qn_b, qp_b))
    return out.transpose(1, 0, 3, 2, 4).reshape(bsz, s, hh * MLA_V)


def hier_moe(x, w_group, b_group, w_router, b_router, w_gate, w_up, w_down):
    bsz, s, d = x.shape
    t = x.reshape(bsz * s, d)
    n_tok = t.shape[0]
    g_prob = jax.nn.softmax((t @ w_group).astype(F32) + b_group.astype(F32), axis=-1)
    g_w, g_idx = lax.top_k(g_prob, 1)
    e_logits = ((t @ w_router).astype(F32) + b_router.astype(F32)).reshape(n_tok, N_EXPERT_GROUPS, EXPERTS_PER_GROUP)
    idx = jnp.broadcast_to(g_idx[:, :, None], (n_tok, 1, EXPERTS_PER_GROUP))
    e_logits = jnp.take_along_axis(e_logits, idx, axis=1)[:, 0]
    e_w, e_idx = lax.top_k(jax.nn.softmax(e_logits, axis=-1), TOP_K_IN_GROUP)
    e_w = e_w / jnp.sum(e_w, axis=-1, keepdims=True)
    expert_id = g_idx * EXPERTS_PER_GROUP + e_idx
    combine = jnp.sum(jax.nn.one_hot(expert_id, N_EXPERTS, dtype=F32) * (g_w * e_w)[..., None], axis=1)
    hid = jax.nn.silu(jnp.einsum('td,edf->tef', t, w_gate)) * jnp.einsum('td,edf->tef', t, w_up)
    hid = hid * combine[:, :, None].astype(hid.dtype)
    return jnp.einsum('tef,efd->td', hid, w_down).reshape(bsz, s, d)


def setup_inputs(seed: int = 0) -> dict:
    key = jax.random.key(seed)
    ks = jax.random.split(key, 32)
    L = DEPTH

    def normal(k, shape, scale):
        return jax.random.normal(k, shape, F32) * scale

    def gain(k, shape):
        return 1.0 + 0.01 * jax.random.normal(k, shape, F32)

    x = normal(ks[0], (BATCH, SEQ, D_MODEL), 1.0)
    start = jax.random.randint(ks[1], (BATCH, 1), 0, 1024, dtype=jnp.int32)
    positions = (start + jnp.arange(SEQ, dtype=jnp.int32)[None, :]).astype(jnp.int32)
    dt0 = jnp.exp(jax.random.uniform(ks[6], (L, SSM_HEADS), F32, math.log(1e-3), math.log(1e-1)))
    return {
        'x': x,
        'positions': positions,
        'norm_mix_g': gain(ks[2], (L, D_MODEL)),
        'w_in': normal(ks[3], (L, D_MODEL, N_IN), D_MODEL ** -0.5),
        'ssm_conv_w': normal(ks[4], (L, SSM_CONV, SSM_XBC), SSM_CONV ** -0.5),
        'ssm_conv_b': normal(ks[5], (L, SSM_XBC), 0.01),
        'ssm_dt_bias': dt0 + jnp.log(-jnp.expm1(-dt0)),
        'ssm_a_log': jnp.log(jax.random.uniform(ks[7], (L, SSM_HEADS), F32, 1.0, 16.0)),
        'ssm_d': gain(ks[8], (L, SSM_HEADS)),
        'ssm_norm_g': gain(ks[9], (L, GROUP_W)),
        'ml_conv_w': normal(ks[10], (L, MLSTM_CONV, 2 * GROUP_W), MLSTM_CONV ** -0.5),
        'ml_conv_b': normal(ks[11], (L, 2 * GROUP_W), 0.01),
        'ml_ig_bias': normal(ks[12], (L, MLSTM_HEADS), 0.1),
        'ml_fg_bias': 3.0 + 3.0 * jax.random.uniform(ks[13], (L, MLSTM_HEADS), F32),
        'ml_norm_g': gain(ks[14], (L, GROUP_W)),
        'mla_q_norm_g': gain(ks[15], (L, MLA_Q_LORA)),
        'mla_w_uq': normal(ks[16], (L, MLA_Q_LORA, MLA_HEADS * (MLA_NOPE + MLA_ROPE)), MLA_Q_LORA ** -0.5),
        'mla_kv_norm_g': gain(ks[17], (L, MLA_KV_LORA)),
        'mla_w_ukv': normal(ks[18], (L, MLA_KV_LORA, MLA_HEADS * (MLA_NOPE + MLA_V)), MLA_KV_LORA ** -0.5),
        'w_out': normal(ks[19], (L, D_MIX, D_MODEL), D_MIX ** -0.5),
        'norm_ffn_g': gain(ks[20], (L, D_MODEL)),
        'moe_w_group': normal(ks[21], (L, D_MODEL, N_EXPERT_GROUPS), D_MODEL ** -0.5),
        'moe_b_group': normal(ks[22], (L, N_EXPERT_GROUPS), 0.01),
        'moe_w_router': normal(ks[23], (L, D_MODEL, N_EXPERTS), D_MODEL ** -0.5),
        'moe_b_router': normal(ks[24], (L, N_EXPERTS), 0.01),
        'moe_w_gate': normal(ks[25], (L, N_EXPERTS, D_MODEL, D_EXPERT), D_MODEL ** -0.5),
        'moe_w_up': normal(ks[26], (L, N_EXPERTS, D_MODEL, D_EXPERT), D_MODEL ** -0.5),
        'moe_w_down': normal(ks[27], (L, N_EXPERTS, D_EXPERT, D_MODEL), D_EXPERT ** -0.5),
        'final_norm_g': gain(ks[28], (D_MODEL,)),
    }


def reference(x, positions, norm_mix_g, w_in, ssm_conv_w, ssm_conv_b, ssm_dt_bias, ssm_a_log, ssm_d,
              ssm_norm_g, ml_conv_w, ml_conv_b, ml_ig_bias, ml_fg_bias, ml_norm_g, mla_q_norm_g, mla_w_uq,
              mla_kv_norm_g, mla_w_ukv, w_out, norm_ffn_g, moe_w_group, moe_b_group, moe_w_router,
              moe_b_router, moe_w_gate, moe_w_up, moe_w_down, final_norm_g):
    bsz, s, _ = x.shape
    cos, sin = rope_tables(positions)
    split_at = [int(i) for i in np.cumsum(IN_SIZES)[:-1]]
    for l in range(DEPTH):
        h = rms_norm(x, norm_mix_g[l])
        proj = h @ w_in[l]
        (a_qkv, s_z, s_xbc, s_dt, m_qk, m_v, m_i, m_f, m_o,
         d_cq, d_ckv, d_kr) = jnp.split(proj, split_at, axis=-1)
        aq, ak, av = jnp.split(a_qkv.reshape(bsz, s, 3, MOBA_HEADS, MOBA_HD), 3, axis=2)
        y_a = moba_attention(aq[:, :, 0], ak[:, :, 0], av[:, :, 0]).astype(x.dtype)
        y_b = mamba2_ssd(s_z, s_xbc, s_dt, ssm_conv_w[l], ssm_conv_b[l], ssm_dt_bias[l],
                         ssm_a_log[l], ssm_d[l], ssm_norm_g[l]).astype(x.dtype)
        y_c = mlstm_mixer(m_qk, m_v, m_i, m_f, m_o, ml_conv_w[l], ml_conv_b[l], ml_ig_bias[l],
                          ml_fg_bias[l], ml_norm_g[l]).astype(x.dtype)
        y_d = mla_attention(d_cq, d_ckv, d_kr, cos, sin, mla_q_norm_g[l], mla_w_uq[l],
                            mla_kv_norm_g[l], mla_w_ukv[l]).astype(x.dtype)
        x = x + jnp.concatenate([y_a, y_b, y_c, y_d], axis=-1) @ w_out[l]
        h2 = rms_norm(x, norm_ffn_g[l])
        x = x + hier_moe(h2, moe_w_group[l], moe_b_group[l], moe_w_router[l], moe_b_router[l],
                         moe_w_gate[l], moe_w_up[l], moe_w_down[l]).astype(x.dtype)
    return rms_norm(x, final_norm_g)
```

```python
import functools
import math

import numpy as np
import jax
import jax.numpy as jnp
from jax import lax
from jax.experimental import pallas as pl
from jax.experimental.pallas import tpu as pltpu

F32 = jnp.float32
BF16 = jnp.bfloat16
HIGHEST = lax.Precision.HIGHEST

EPS = 1e-6
NEG = -1e30
LOG2E = 1.4426950408889634

D_MODEL = 1024
GROUP_W = 256
HEADS = 4
HD = 64
HP = 128
MOBA_BLOCK = 256
MOBA_TOPK = 3
SSM_STATE = 128
SSM_XBC = 768
CONV_K = 4
CHUNK = 128
MLA_NOPE = 64
MLA_ROPE = 32
ROPE_THETA = 10000.0
N_EXPERTS = 16
EXPERTS_PER_GROUP = 4
D_EXPERT = 256

COL_ZX = 0
COL_ML = 1024
COL_CQ = 2048
COL_CKV = 2304
COL_SMALL = 2432
COL_KA = 2560
N_MAIN = 3072

V7X_VMEM_LIMIT = 56 * 1024 * 1024

NT_DIMS = (((1,), (1,)), ((), ()))


def _cparams(sem):
    return pltpu.CompilerParams(dimension_semantics=sem, vmem_limit_bytes=V7X_VMEM_LIMIT)


def _rms(x, g):
    ms = jnp.mean(x * x, axis=-1, keepdims=True)
    return x * lax.rsqrt(ms + EPS) * g


def _sigmoid(x):
    return 1.0 / (1.0 + jnp.exp(-x))


def _softplus(x):
    return jnp.maximum(x, 0.0) + jnp.log1p(jnp.exp(-jnp.abs(x)))


def _tril(n):
    r = lax.broadcasted_iota(jnp.int32, (n, n), 0)
    c = lax.broadcasted_iota(jnp.int32, (n, n), 1)
    return c <= r


def _in_proj_body(x_ref, g_ref, wm_ref, wq_ref, wv_ref, om_ref, oq_ref, ov_ref):
    h = _rms(x_ref[...], g_ref[...]).astype(BF16)
    n = om_ref.shape[1]
    for c in range(0, n, 512):
        om_ref[:, c:c + 512] = jnp.dot(h, wm_ref[:, c:c + 512], preferred_element_type=F32)
    oq_ref[0] = lax.dot_general(wq_ref[...], h, NT_DIMS, preferred_element_type=F32)
    ov_ref[0] = lax.dot_general(wv_ref[...], h, NT_DIMS, preferred_element_type=F32).astype(BF16)


def _in_proj(x2, g, w_main, w_qT, w_vT, bsz, seq, tm=512):
    t = x2.shape[0]
    nst = seq // tm
    return pl.pallas_call(
        _in_proj_body,
        grid=(t // tm,),
        in_specs=[
            pl.BlockSpec((tm, D_MODEL), lambda i: (i, 0)),
            pl.BlockSpec((1, D_MODEL), lambda i: (0, 0)),
            pl.BlockSpec((D_MODEL, N_MAIN), lambda i: (0, 0)),
            pl.BlockSpec((HEADS * HP, D_MODEL), lambda i: (0, 0)),
            pl.BlockSpec((GROUP_W, D_MODEL), lambda i: (0, 0)),
        ],
        out_specs=[
            pl.BlockSpec((tm, N_MAIN), lambda i: (i, 0)),
            pl.BlockSpec((1, HEADS * HP, tm), lambda i: (i // nst, 0, i % nst)),
            pl.BlockSpec((1, GROUP_W, tm), lambda i: (i // nst, 0, i % nst)),
        ],
        out_shape=[
            jax.ShapeDtypeStruct((t, N_MAIN), F32),
            jax.ShapeDtypeStruct((bsz, HEADS * HP, seq), F32),
            jax.ShapeDtypeStruct((bsz, GROUP_W, seq), BF16),
        ],
        compiler_params=_cparams(("arbitrary",)),
        name="in_proj",
    )(x2, g, w_main, w_qT, w_vT)


def _kmean_body(k_ref, km_ref, kb_ref):
    k = k_ref[...]
    km_ref[0] = jnp.mean(k, axis=0, keepdims=True)
    kb_ref[...] = k.astype(BF16)


def _kmean(main, bsz, seq):
    t = main.shape[0]
    nb = seq // MOBA_BLOCK
    kcol = COL_KA // (HEADS * HP)
    return pl.pallas_call(
        _kmean_body,
        grid=(bsz * nb,),
        in_specs=[pl.BlockSpec((MOBA_BLOCK, HEADS * HP), lambda i: (i, kcol))],
        out_specs=[
            pl.BlockSpec((1, 1, HEADS * HP), lambda i: (i, 0, 0)),
            pl.BlockSpec((MOBA_BLOCK, HEADS * HP), lambda i: (i, 0)),
        ],
        out_shape=[
            jax.ShapeDtypeStruct((bsz * nb, 1, HEADS * HP), F32),
            jax.ShapeDtypeStruct((t, HEADS * HP), BF16),
        ],
        compiler_params=_cparams(("arbitrary",)),
        name="moba_kmean",
    )(main)


def _moba_select_body(qT_ref, km_ref, bias_ref, qb_ref, *, ts, nb, qscale):
    s_idx = pl.program_id(1)
    qT = qT_ref[0]
    km = km_ref[0]
    pos = s_idx * ts + lax.broadcasted_iota(jnp.int32, (nb, ts), 1)
    cur = lax.shift_right_logical(pos, int(math.log2(MOBA_BLOCK)))
    blk = lax.broadcasted_iota(jnp.int32, (nb, ts), 0)
    past = blk < cur
    for h in range(HEADS):
        g = jnp.dot(km[:, h * HP:(h + 1) * HP], qT[h * HP:(h + 1) * HP, :],
                    precision=HIGHEST, preferred_element_type=F32)
        g = jnp.where(past, g, -jnp.inf)
        bias = jnp.full((nb, ts), NEG, F32)
        for _ in range(MOBA_TOPK):
            m = jnp.max(g, axis=0, keepdims=True)
            cand = jnp.logical_and(g == m, g > -jnp.inf)
            idx = jnp.min(jnp.where(cand, blk, nb), axis=0, keepdims=True)
            pick = blk == idx
            bias = jnp.where(pick, 0.0, bias)
            g = jnp.where(pick, -jnp.inf, g)
        bias_ref[0, h] = bias
    qb_ref[0] = (qT * qscale).astype(BF16)


def _moba_select(qT, kmean, bsz, seq, ts=512):
    nb = seq // MOBA_BLOCK
    qscale = (HD ** -0.5) * LOG2E
    return pl.pallas_call(
        functools.partial(_moba_select_body, ts=ts, nb=nb, qscale=qscale),
        grid=(bsz, seq // ts),
        in_specs=[
            pl.BlockSpec((1, HEADS * HP, ts), lambda b, s: (b, 0, s)),
            pl.BlockSpec((1, nb, HEADS * HP), lambda b, s: (b, 0, 0)),
        ],
        out_specs=[
            pl.BlockSpec((1, HEADS, nb, ts), lambda b, s: (b, 0, 0, s)),
            pl.BlockSpec((1, HEADS * HP, ts), lambda b, s: (b, 0, s)),
        ],
        out_shape=[
            jax.ShapeDtypeStruct((bsz, HEADS, nb, seq), F32),
            jax.ShapeDtypeStruct((bsz, HEADS * HP, seq), BF16),
        ],
        compiler_params=_cparams(("arbitrary", "arbitrary")),
        name="moba_select",
    )(qT, kmean)


def _rope_body(pos_ref, inv_ref, sgn_ref, c_ref, s_ref, ct_ref, st_ref):
    ang = pos_ref[...].astype(F32) * inv_ref[...]
    c = jnp.cos(ang)
    s = jnp.sin(ang) * sgn_ref[...]
    c_ref[...] = c
    s_ref[...] = s
    ct_ref[0] = c.T
    st_ref[0] = s.T


def _rope_tables(pos2, inv_full, sgn_full, bsz, seq, ts=512):
    t = pos2.shape[0]
    nst = seq // ts
    tok = pl.BlockSpec((ts, HP), lambda i: (i, 0))
    tr = pl.BlockSpec((1, HP, ts), lambda i: (i // nst, 0, i % nst))
    row = pl.BlockSpec((1, HP), lambda i: (0, 0))
    return pl.pallas_call(
        _rope_body,
        grid=(t // ts,),
        in_specs=[pl.BlockSpec((ts, 1), lambda i: (i, 0)), row, row],
        out_specs=[tok, tok, tr, tr],
        out_shape=[
            jax.ShapeDtypeStruct((t, HP), F32),
            jax.ShapeDtypeStruct((t, HP), F32),
            jax.ShapeDtypeStruct((bsz, HP, seq), F32),
            jax.ShapeDtypeStruct((bsz, HP, seq), F32),
        ],
        compiler_params=_cparams(("arbitrary",)),
        name="rope_tables",
    )(pos2, inv_full, sgn_full)


def _mla_prep_body(cq_ref, ckv_ref, sm_ref, c_ref, s_ref, ct_ref, st_ref, gq_ref, wq_ref, gkv_ref,
                   wk_ref, wv_ref, msk_ref, qo_ref, ko_ref, vo_ref, *, qscale):
    qn = _rms(cq_ref[...], gq_ref[...]).astype(BF16)
    q2 = lax.dot_general(wq_ref[...], qn, NT_DIMS, preferred_element_type=F32)
    nq = HEADS * HP
    ct4 = jnp.concatenate([ct_ref[0]] * HEADS, axis=0)
    st4 = jnp.concatenate([st_ref[0]] * HEADS, axis=0)
    q = (q2[:nq] * ct4 + q2[nq:] * st4) * qscale
    qo_ref[0] = q.astype(BF16)
    kvn = _rms(ckv_ref[...], gkv_ref[...]).astype(BF16)
    kn = jnp.dot(kvn, wk_ref[...], preferred_element_type=F32)
    sm = sm_ref[...]
    pe = (sm * c_ref[...] + pltpu.roll(sm, HP - MLA_ROPE, 1) * s_ref[...]) * msk_ref[...]
    ko_ref[...] = (kn + jnp.concatenate([pe] * HEADS, axis=1)).astype(BF16)
    vo_ref[0] = lax.dot_general(wv_ref[...], kvn, NT_DIMS, preferred_element_type=F32).astype(BF16)


def _mla_prep(main, c_tok, s_tok, c_tr, s_tr, gq, wq2T, gkv, wk, wvT, pe_mask, bsz, seq, ts=512):
    t = main.shape[0]
    nst = seq // ts
    qscale = ((MLA_NOPE + MLA_ROPE) ** -0.5) * LOG2E
    tok128 = pl.BlockSpec((ts, HP), lambda i: (i, 0))
    tr128 = pl.BlockSpec((1, HP, ts), lambda i: (i // nst, 0, i % nst))

    def full(shape):
        return pl.BlockSpec(shape, lambda i: tuple(0 for _ in shape))

    return pl.pallas_call(
        functools.partial(_mla_prep_body, qscale=qscale),
        grid=(t // ts,),
        in_specs=[
            pl.BlockSpec((ts, 256), lambda i: (i, COL_CQ // 256)),
            pl.BlockSpec((ts, 128), lambda i: (i, COL_CKV // 128)),
            pl.BlockSpec((ts, 128), lambda i: (i, COL_SMALL // 128)),
            tok128, tok128, tr128, tr128,
            full((1, 256)), full((2 * HEADS * HP, 256)), full((1, 128)),
            full((128, HEADS * HP)), full((GROUP_W, 128)), full((1, HP)),
        ],
        out_specs=[
            pl.BlockSpec((1, HEADS * HP, ts), lambda i: (i // nst, 0, i % nst)),
            pl.BlockSpec((ts, HEADS * HP), lambda i: (i, 0)),
            pl.BlockSpec((1, GROUP_W, ts), lambda i: (i // nst, 0, i % nst)),
        ],
        out_shape=[
            jax.ShapeDtypeStruct((bsz, HEADS * HP, seq), BF16),
            jax.ShapeDtypeStruct((t, HEADS * HP), BF16),
            jax.ShapeDtypeStruct((bsz, GROUP_W, seq), BF16),
        ],
        compiler_params=_cparams(("arbitrary",)),
        name="mla_prep",
    )(main, main, main, c_tok, s_tok, c_tr, s_tr, gq, wq2T, gkv, wk, wvT, pe_mask)


def _attn_body(ii_ref, jj_ref, qT_ref, k_ref, vT_ref, *rest, has_bias, tq):
    if has_bias:
        bias_ref, o_ref, m_sc, l_sc, acc_sc = rest
    else:
        o_ref, m_sc, l_sc, acc_sc = rest
        bias_ref = None
    p = pl.program_id(1)
    i = ii_ref[p]
    j = jj_ref[p]

    @pl.when(j == 0)
    def _():
        m_sc[...] = jnp.full(m_sc.shape, NEG, F32)
        l_sc[...] = jnp.zeros(l_sc.shape, F32)
        acc_sc[...] = jnp.zeros(acc_sc.shape, F32)

    def step(diag):
        for h in range(HEADS):
            kh = k_ref[:, h * HP:(h + 1) * HP]
            qh = qT_ref[0, h * HP:(h + 1) * HP, :]
            s = jnp.dot(kh, qh, preferred_element_type=F32)
            if diag:
                kpos = lax.broadcasted_iota(jnp.int32, s.shape, 0)
                qpos = lax.broadcasted_iota(jnp.int32, s.shape, 1)
                s = jnp.where(kpos <= qpos, s, NEG)
            elif has_bias:
                s = s + bias_ref[0, h, pl.ds(j, 1), :]
            m_old = m_sc[h:h + 1, :]
            m_new = jnp.maximum(m_old, jnp.max(s, axis=0, keepdims=True))
            alpha = jnp.exp2(m_old - m_new)
            pm = jnp.exp2(s - m_new)
            l_sc[h:h + 1, :] = alpha * l_sc[h:h + 1, :] + jnp.sum(pm, axis=0, keepdims=True)
            vh = vT_ref[0, h * HD:(h + 1) * HD, :]
            acc_sc[h * HD:(h + 1) * HD, :] = (alpha * acc_sc[h * HD:(h + 1) * HD, :]
                                              + jnp.dot(vh, pm.astype(BF16), preferred_element_type=F32))
            m_sc[h:h + 1, :] = m_new

    @pl.when(j < i)
    def _():
        step(False)

    @pl.when(j == i)
    def _():
        step(True)
        parts = []
        for h in range(HEADS):
            parts.append(acc_sc[h * HD:(h + 1) * HD, :] * (1.0 / l_sc[h:h + 1, :]))
        o_ref[...] = jnp.concatenate(parts, axis=0).T


def _attention(qT, k, vT, bias, bsz, seq, tq=256):
    t = k.shape[0]
    nq = seq // tq
    ii = np.concatenate([np.full(i + 1, i, np.int32) for i in range(nq)])
    jj = np.concatenate([np.arange(i + 1, dtype=np.int32) for i in range(nq)])
    has_bias = bias is not None
    in_specs = [
        pl.BlockSpec((1, HEADS * HP, tq), lambda b, p, ii, jj: (b, 0, ii[p])),
        pl.BlockSpec((tq, HEADS * HP), lambda b, p, ii, jj: (b * nq + jj[p], 0)),
        pl.BlockSpec((1, GROUP_W, tq), lambda b, p, ii, jj: (b, 0, jj[p])),
    ]
    args = [qT, k, vT]
    if has_bias:
        nb = bias.shape[2]
        in_specs.append(pl.BlockSpec((1, HEADS, nb, tq), lambda b, p, ii, jj: (b, 0, 0, ii[p])))
        args.append(bias)
    grid_spec = pltpu.PrefetchScalarGridSpec(
        num_scalar_prefetch=2,
        grid=(bsz, len(ii)),
        in_specs=in_specs,
        out_specs=pl.BlockSpec((tq, GROUP_W), lambda b, p, ii, jj: (b * nq + ii[p], 0)),
        scratch_shapes=[
            pltpu.VMEM((8, tq), F32),
            pltpu.VMEM((8, tq), F32),
            pltpu.VMEM((GROUP_W, tq), F32),
        ],
    )
    return pl.pallas_call(
        functools.partial(_attn_body, has_bias=has_bias, tq=tq),
        grid_spec=grid_spec,
        out_shape=jax.ShapeDtypeStruct((t, GROUP_W), F32),
        compiler_params=_cparams(("arbitrary", "arbitrary")),
        name="attn_moba" if has_bias else "attn_mla",
    )(jnp.asarray(ii), jnp.asarray(jj), *args)


def _conv_silu(xpad, src, cw_ref, cb_ref, first, ls):
    width = xpad.shape[1]

    @pl.when(first)
    def _():
        xpad[0:8, :] = jnp.zeros((8, width), F32)

    @pl.when(jnp.logical_not(first))
    def _():
        xpad[0:8, :] = xpad[ls:ls + 8, :]

    xpad[8:ls + 8, :] = src
    acc = cb_ref[...] + cw_ref[0:1, :] * xpad[pl.ds(8 - CONV_K + 1, ls), :]
    for kk in range(1, CONV_K):
        acc = acc + cw_ref[kk:kk + 1, :] * xpad[pl.ds(8 - CONV_K + 1 + kk, ls), :]
    return acc * _sigmoid(acc)


def _ssd_body(zx_ref, sm_ref, cw_ref, cb_ref, dtb_ref, arow_ref, dexp_ref, ng_ref, tri_ref, o_ref,
              xpad, state, *, ls):
    first = pl.program_id(1) == 0

    @pl.when(first)
    def _():
        state[...] = jnp.zeros(state.shape, F32)

    xc = _conv_silu(xpad, zx_ref[:, GROUP_W:GROUP_W + SSM_XBC], cw_ref, cb_ref, first, ls)
    dt_all = _softplus(sm_ref[...] + dtb_ref[...])
    da_all = dt_all * arow_ref[...]
    lane = lax.broadcasted_iota(jnp.int32, (1, 128), 1)
    half = lane < HD
    tril = _tril(CHUNK)
    for c in range(ls // CHUNK):
        r0 = c * CHUNK
        dt_c = dt_all[r0:r0 + CHUNK]
        acum = jnp.dot(tri_ref[...], da_all[r0:r0 + CHUNK], precision=HIGHEST,
                       preferred_element_type=F32)
        acum_t = acum.T
        alast = acum[CHUNK - 1:CHUNK, :]
        for g in range(2):
            h0, h1 = 2 * g, 2 * g + 1
            xg = xc[r0:r0 + CHUNK, g * 128:(g + 1) * 128]
            bg = xc[r0:r0 + CHUNK, GROUP_W + g * 128:GROUP_W + (g + 1) * 128]
            cg = xc[r0:r0 + CHUNK, 2 * GROUP_W + g * 128:2 * GROUP_W + (g + 1) * 128]
            cgb = cg.astype(BF16)
            gram = lax.dot_general(cgb, bg.astype(BF16), NT_DIMS, preferred_element_type=F32)
            a0 = acum[:, h0:h0 + 1]
            a1 = acum[:, h1:h1 + 1]
            xdt = xg * jnp.where(half, dt_c[:, h0:h0 + 1], dt_c[:, h1:h1 + 1])
            ydiag = jnp.zeros((CHUNK, 128), F32)
            for hh, col, msk in ((h0, a0, half), (h1, a1, jnp.logical_not(half))):
                dec = jnp.exp(jnp.where(tril, col - acum_t[hh:hh + 1, :], -jnp.inf))
                ydiag = ydiag + jnp.dot((gram * dec).astype(BF16),
                                        jnp.where(msk, xdt, 0.0).astype(BF16),
                                        preferred_element_type=F32)
            st = state[g]
            yoff = jnp.dot(cgb, st.astype(BF16), preferred_element_type=F32) * jnp.where(
                half, jnp.exp(a0), jnp.exp(a1))
            l0 = alast[:, h0:h0 + 1]
            l1 = alast[:, h1:h1 + 1]
            dout = jnp.where(half, jnp.exp(l0 - a0), jnp.exp(l1 - a1))
            state[g] = (jnp.where(half, jnp.exp(l0), jnp.exp(l1)) * st
                        + jnp.dot(bg.T.astype(BF16), (xdt * dout).astype(BF16), preferred_element_type=F32))
            y = ydiag + yoff + xg * dexp_ref[:, g * 128:(g + 1) * 128]
            zg = zx_ref[r0:r0 + CHUNK, g * 128:(g + 1) * 128]
            y = y * (zg * _sigmoid(zg))
            o_ref[r0:r0 + CHUNK, g * 128:(g + 1) * 128] = _rms(y, ng_ref[:, g * 128:(g + 1) * 128])


def _ssd(main, cw, cb, dtb, arow, dexp, ng, tri, bsz, seq, ls=512):
    t = main.shape[0]
    ns = seq // ls

    def full(shape):
        return pl.BlockSpec(shape, lambda b, s: tuple(0 for _ in shape))

    return pl.pallas_call(
        functools.partial(_ssd_body, ls=ls),
        grid=(bsz, ns),
        in_specs=[
            pl.BlockSpec((ls, 1024), lambda b, s: (b * ns + s, COL_ZX // 1024)),
            pl.BlockSpec((ls, 128), lambda b, s: (b * ns + s, COL_SMALL // 128)),
            full((CONV_K, SSM_XBC)), full((1, SSM_XBC)), full((1, 128)), full((1, 128)),
            full((1, GROUP_W)), full((1, GROUP_W)), full((CHUNK, CHUNK)),
        ],
        out_specs=pl.BlockSpec((ls, GROUP_W), lambda b, s: (b * ns + s, 0)),
        out_shape=jax.ShapeDtypeStruct((t, GROUP_W), F32),
        scratch_shapes=[pltpu.VMEM((ls + 8, SSM_XBC), F32), pltpu.VMEM((2, SSM_STATE, 128), F32)],
        compiler_params=_cparams(("arbitrary", "arbitrary")),
        name="ssd",
    )(main, main, cw, cb, dtb, arow, dexp, ng, tri)


def _mlstm_body(c_ref, sm_ref, cw_ref, cb_ref, igb_ref, fgb_ref, ng_ref, tri_ref, seg_ref, segm_ref,
                bmask_ref, o_ref, xpad, cst, nst, mst, *, ls):
    first = pl.program_id(1) == 0

    @pl.when(first)
    def _():
        cst[...] = jnp.zeros(cst.shape, F32)
        nst[...] = jnp.zeros(nst.shape, F32)
        mst[...] = jnp.zeros(mst.shape, F32)

    qk = _conv_silu(xpad, c_ref[:, 0:2 * GROUP_W], cw_ref, cb_ref, first, ls)
    sm = sm_ref[...]
    ig_all = sm + igb_ref[...]
    fx = sm + fgb_ref[...]
    lf_all = jnp.minimum(fx, 0.0) - jnp.log1p(jnp.exp(-jnp.abs(fx)))
    lane = lax.broadcasted_iota(jnp.int32, (1, GROUP_W), 1)
    tril = _tril(CHUNK)
    for c in range(ls // CHUNK):
        r0 = c * CHUNK
        q = qk[r0:r0 + CHUNK, 0:GROUP_W]
        k = qk[r0:r0 + CHUNK, GROUP_W:2 * GROUP_W] * (HD ** -0.5)
        v = c_ref[r0:r0 + CHUNK, 2 * GROUP_W:3 * GROUP_W]
        o_pre = c_ref[r0:r0 + CHUNK, 3 * GROUP_W:4 * GROUP_W]
        ig = ig_all[r0:r0 + CHUNK]
        bcs = jnp.dot(tri_ref[...], lf_all[r0:r0 + CHUNK], precision=HIGHEST,
                      preferred_element_type=F32)
        bcs_t = bcs.T
        ig_t = ig.T
        kb = k.astype(BF16)
        q_c = jnp.dot(q.astype(BF16), cst[...].astype(BF16), preferred_element_type=F32)
        q_n = jnp.dot(q * nst[0:1, :], seg_ref[...], precision=HIGHEST, preferred_element_type=F32)
        num = jnp.zeros((CHUNK, GROUP_W), F32)
        w_inter_x = jnp.zeros((CHUNK, GROUP_W), F32)
        inv_den_x = jnp.zeros((CHUNK, GROUP_W), F32)
        w_state_x = jnp.zeros((CHUNK, GROUP_W), F32)
        keep_x = jnp.zeros((1, GROUP_W), F32)
        for h in range(HEADS):
            hm = jnp.logical_and(lane >= h * HD, lane < (h + 1) * HD)
            bcol = bcs[:, 8 + h:9 + h]
            igcol = ig[:, 4 + h:5 + h]
            m_prev = mst[h:h + 1, 0:1]
            dmat = jnp.where(tril, bcol - bcs_t[8 + h:9 + h, :] + ig_t[4 + h:5 + h, :], -jnp.inf)
            inter = bcol + m_prev
            m_t = jnp.maximum(inter, jnp.max(dmat, axis=1, keepdims=True))
            s_qk = lax.dot_general(jnp.where(hm, q, 0.0).astype(BF16), kb, NT_DIMS,
                                   preferred_element_type=F32)
            sc = s_qk * jnp.exp(dmat - m_t)
            w_inter = jnp.exp(inter - m_t)
            den = jnp.sum(sc, axis=1, keepdims=True) + w_inter * q_n[:, h:h + 1]
            inv_den = 1.0 / jnp.maximum(jnp.abs(den), jnp.exp(-m_t))
            num = num + jnp.dot(sc.astype(BF16), jnp.where(hm, v, 0.0).astype(BF16),
                                preferred_element_type=F32)
            w_inter_x = w_inter_x + jnp.where(hm, w_inter, 0.0)
            inv_den_x = inv_den_x + jnp.where(hm, inv_den, 0.0)
            blast = bcs[CHUNK - 1:CHUNK, 8 + h:9 + h]
            gcol = blast - bcol + igcol
            m_new = jnp.maximum(blast + m_prev, jnp.max(gcol, axis=0, keepdims=True))
            w_state_x = w_state_x + jnp.where(hm, jnp.exp(gcol - m_new), 0.0)
            keep_x = keep_x + jnp.where(hm, jnp.exp(blast + m_prev - m_new), 0.0)
            mst[h:h + 1, :] = jnp.broadcast_to(m_new, (1, 128))
        hout = (num + w_inter_x * q_c) * inv_den_x
        kw = k * w_state_x
        cst[...] = keep_x * cst[...] + bmask_ref[...] * jnp.dot(
            kw.T.astype(BF16), v.astype(BF16), preferred_element_type=F32)
        nst[0:1, :] = keep_x * nst[0:1, :] + jnp.sum(kw, axis=0, keepdims=True)
        hh = hout * _sigmoid(o_pre)
        ms = jnp.dot(hh * hh, segm_ref[...], precision=HIGHEST, preferred_element_type=F32)
        o_ref[r0:r0 + CHUNK, :] = hh * lax.rsqrt(ms + EPS) * ng_ref[...]


def _mlstm(main, cw, cb, igb, fgb, ng, tri, seg, segm, bmask, bsz, seq, ls=512):
    t = main.shape[0]
    ns = seq // ls

    def full(shape):
        return pl.BlockSpec(shape, lambda b, s: tuple(0 for _ in shape))

    return pl.pallas_call(
        functools.partial(_mlstm_body, ls=ls),
        grid=(bsz, ns),
        in_specs=[
            pl.BlockSpec((ls, 1024), lambda b, s: (b * ns + s, COL_ML // 1024)),
            pl.BlockSpec((ls, 128), lambda b, s: (b * ns + s, COL_SMALL // 128)),
            full((CONV_K, 2 * GROUP_W)), full((1, 2 * GROUP_W)), full((1, 128)), full((1, 128)),
            full((1, GROUP_W)), full((CHUNK, CHUNK)), full((GROUP_W, 128)), full((GROUP_W, GROUP_W)),
            full((GROUP_W, GROUP_W)),
        ],
        out_specs=pl.BlockSpec((ls, GROUP_W), lambda b, s: (b * ns + s, 0)),
        out_shape=jax.ShapeDtypeStruct((t, GROUP_W), F32),
        scratch_shapes=[
            pltpu.VMEM((ls + 8, 2 * GROUP_W), F32),
            pltpu.VMEM((GROUP_W, GROUP_W), F32),
            pltpu.VMEM((8, GROUP_W), F32),
            pltpu.VMEM((8, 128), F32),
        ],
        compiler_params=_cparams(("arbitrary", "arbitrary")),
        name="mlstm",
    )(main, main, cw, cb, igb, fgb, ng, tri, seg, segm, bmask)


def _outproj_body(x_ref, ya_ref, yb_ref, yc_ref, yd_ref, wo_ref, g2_ref, wr_ref, br_ref,
                  xn_ref, h2_ref, cb_ref):
    acc = x_ref[...]
    for gi, y_ref in enumerate((ya_ref, yb_ref, yc_ref, yd_ref)):
        acc = acc + jnp.dot(y_ref[...].astype(BF16), wo_ref[gi * GROUP_W:(gi + 1) * GROUP_W, :],
                            preferred_element_type=F32)
    xn_ref[...] = acc
    h2 = _rms(acc, g2_ref[...])
    h2_ref[...] = h2.astype(BF16)
    logits = jnp.dot(h2, wr_ref[...], precision=HIGHEST, preferred_element_type=F32) + br_ref[...]
    lane = lax.broadcasted_iota(jnp.int32, logits.shape, 1)
    big = jnp.int32(1 << 20)
    n_grp = N_EXPERTS // EXPERTS_PER_GROUP
    gl = jnp.where(jnp.logical_and(lane >= N_EXPERTS, lane < N_EXPERTS + n_grp), logits, -jnp.inf)
    gmax = jnp.max(gl, axis=1, keepdims=True)
    g_w = 1.0 / jnp.sum(jnp.exp(gl - gmax), axis=1, keepdims=True)
    gidx = jnp.min(jnp.where(gl == gmax, lane, big), axis=1, keepdims=True) - N_EXPERTS
    in_grp = jnp.logical_and(lane < N_EXPERTS,
                             lax.shift_right_logical(lane, int(math.log2(EXPERTS_PER_GROUP))) == gidx)
    el = jnp.where(in_grp, logits, -jnp.inf)
    emax = jnp.max(el, axis=1, keepdims=True)
    esum = jnp.sum(jnp.exp(el - emax), axis=1, keepdims=True)
    i1 = jnp.min(jnp.where(el == emax, lane, big), axis=1, keepdims=True)
    el2 = jnp.where(lane == i1, -jnp.inf, el)
    emax2 = jnp.max(el2, axis=1, keepdims=True)
    i2 = jnp.min(jnp.where(el2 == emax2, lane, big), axis=1, keepdims=True)
    p1 = 1.0 / esum
    p2 = jnp.exp(emax2 - emax) / esum
    w1 = p1 / (p1 + p2)
    w2 = p2 / (p1 + p2)
    cb_ref[...] = g_w * (jnp.where(lane == i1, w1, 0.0) + jnp.where(lane == i2, w2, 0.0))


def _outproj(x2, ya, yb, yc, yd, wo, g2, wr, br, tm=512):
    t = x2.shape[0]
    row = pl.BlockSpec((tm, D_MODEL), lambda i: (i, 0))
    grp = pl.BlockSpec((tm, GROUP_W), lambda i: (i, 0))

    def full(shape):
        return pl.BlockSpec(shape, lambda i: tuple(0 for _ in shape))

    return pl.pallas_call(
        _outproj_body,
        grid=(t // tm,),
        in_specs=[row, grp, grp, grp, grp, full((D_MODEL, D_MODEL)), full((1, D_MODEL)),
                  full((D_MODEL, 128)), full((1, 128))],
        out_specs=[row, row, pl.BlockSpec((tm, 128), lambda i: (i, 0))],
        out_shape=[
            jax.ShapeDtypeStruct((t, D_MODEL), F32),
            jax.ShapeDtypeStruct((t, D_MODEL), BF16),
            jax.ShapeDtypeStruct((t, 128), F32),
        ],
        compiler_params=_cparams(("arbitrary",)),
        name="outproj_router",
    )(x2, ya, yb, yc, yd, wo, g2, wr, br)


def _moe_body(h_ref, cb_ref, xn_ref, wgu_ref, wd_ref, gf_ref, o_ref, acc, *, final_norm):
    e = pl.program_id(1)

    @pl.when(e == 0)
    def _():
        acc[...] = jnp.zeros(acc.shape, F32)

    gu = jnp.dot(h_ref[...], wgu_ref[0], preferred_element_type=F32)
    g = gu[:, :D_EXPERT]
    u = gu[:, D_EXPERT:]
    lane = lax.broadcasted_iota(jnp.int32, cb_ref.shape, 1)
    col = jnp.sum(jnp.where(lane == e, cb_ref[...], 0.0), axis=1, keepdims=True)
    hid = (g * _sigmoid(g)) * u * col
    acc[...] += jnp.dot(hid.astype(BF16), wd_ref[0], preferred_element_type=F32)

    @pl.when(e == N_EXPERTS - 1)
    def _():
        y = xn_ref[...] + acc[...]
        if final_norm:
            y = _rms(y, gf_ref[...])
        o_ref[...] = y


def _moe(h2, comb, xn, wgu, wd, gf, final_norm, tm=1024):
    t = h2.shape[0]
    return pl.pallas_call(
        functools.partial(_moe_body, final_norm=final_norm),
        grid=(t // tm, N_EXPERTS),
        in_specs=[
            pl.BlockSpec((tm, D_MODEL), lambda i, e: (i, 0)),
            pl.BlockSpec((tm, 128), lambda i, e: (i, 0)),
            pl.BlockSpec((tm, D_MODEL), lambda i, e: (i, 0)),
            pl.BlockSpec((1, D_MODEL, 2 * D_EXPERT), lambda i, e: (e, 0, 0)),
            pl.BlockSpec((1, D_EXPERT, D_MODEL), lambda i, e: (e, 0, 0)),
            pl.BlockSpec((1, D_MODEL), lambda i, e: (0, 0)),
        ],
        out_specs=pl.BlockSpec((tm, D_MODEL), lambda i, e: (i, 0)),
        out_shape=jax.ShapeDtypeStruct((t, D_MODEL), F32),
        scratch_shapes=[pltpu.VMEM((tm, D_MODEL), F32)],
        compiler_params=_cparams(("arbitrary", "arbitrary")),
        name="moe",
    )(h2, comb, xn, wgu, wd, gf)


def _pad_heads(w, width):
    rows = w.shape[0]
    w = w.reshape(rows, HEADS, width)
    return jnp.pad(w, ((0, 0), (0, 0), (0, HP - width))).reshape(rows, HEADS * HP)


def _lane_row(vals, offset, width=128):
    return jnp.zeros((1, width), F32).at[0, offset:offset + vals.shape[0]].set(vals.astype(F32))


def _split_w_in(w):
    sizes = (3 * GROUP_W, GROUP_W, SSM_XBC, HEADS, 2 * GROUP_W, GROUP_W, HEADS, HEADS, GROUP_W,
             256, 128, MLA_ROPE)
    offs = np.cumsum((0,) + sizes)
    return [w[:, int(offs[i]):int(offs[i + 1])] for i in range(len(sizes))]


def _layer_weights(l, p):
    a_qkv, s_z, s_xbc, s_dt, m_qk, m_v, m_i, m_f, m_o, d_cq, d_ckv, d_kr = _split_w_in(p["w_in"][l])
    aq, ak, av = a_qkv[:, :GROUP_W], a_qkv[:, GROUP_W:2 * GROUP_W], a_qkv[:, 2 * GROUP_W:]
    half = MLA_ROPE // 2
    small = jnp.zeros((D_MODEL, 128), F32)
    small = small.at[:, 0:4].set(s_dt).at[:, 4:8].set(m_i).at[:, 8:12].set(m_f)
    small = small.at[:, 64:96].set(d_kr)
    small = small.at[:, 96:128].set(jnp.concatenate([d_kr[:, half:], d_kr[:, :half]], axis=1))
    w_main = jnp.concatenate([s_z, s_xbc, m_qk, m_v, m_o, d_cq, d_ckv, small, _pad_heads(ak, HD)],
                             axis=1).astype(BF16)
    w = {"w_main": w_main,
         "w_qT": _pad_heads(aq, HD).T.astype(BF16),
         "w_vT": av.T.astype(BF16),
         "g_mix": p["norm_mix_g"][l][None, :]}
    w["ssm_cw"] = p["ssm_conv_w"][l]
    w["ssm_cb"] = p["ssm_conv_b"][l][None, :]
    w["ssm_dtb"] = _lane_row(p["ssm_dt_bias"][l], 0)
    w["ssm_arow"] = _lane_row(-jnp.exp(p["ssm_a_log"][l].astype(F32)), 0)
    w["ssm_dexp"] = jnp.repeat(p["ssm_d"][l].astype(F32), HD)[None, :]
    w["ssm_ng"] = p["ssm_norm_g"][l][None, :]
    w["ml_cw"] = p["ml_conv_w"][l]
    w["ml_cb"] = p["ml_conv_b"][l][None, :]
    w["ml_igb"] = _lane_row(p["ml_ig_bias"][l], 4)
    w["ml_fgb"] = _lane_row(p["ml_fg_bias"][l], 8)
    w["ml_ng"] = p["ml_norm_g"][l][None, :]
    wuq = p["mla_w_uq"][l].reshape(256, HEADS, MLA_NOPE + MLA_ROPE)
    zeros_nope = jnp.zeros((256, HEADS, MLA_NOPE), F32)
    swapped = jnp.concatenate([zeros_nope, wuq[:, :, MLA_NOPE + half:], wuq[:, :, MLA_NOPE:MLA_NOPE + half]],
                              axis=2)
    padw = ((0, 0), (0, 0), (0, HP - MLA_NOPE - MLA_ROPE))
    wq = jnp.pad(wuq, padw).reshape(256, HEADS * HP)
    wqs = jnp.pad(swapped, padw).reshape(256, HEADS * HP)
    w["mla_wq2T"] = jnp.concatenate([wq.T, wqs.T], axis=0).astype(BF16)
    wukv = p["mla_w_ukv"][l].reshape(128, HEADS, 2 * HD)
    w["mla_wk"] = jnp.pad(wukv[:, :, :MLA_NOPE], ((0, 0), (0, 0), (0, HP - MLA_NOPE))).reshape(
        128, HEADS * HP).astype(BF16)
    w["mla_wvT"] = wukv[:, :, MLA_NOPE:].reshape(128, GROUP_W).T.astype(BF16)
    w["mla_gq"] = p["mla_q_norm_g"][l][None, :]
    w["mla_gkv"] = p["mla_kv_norm_g"][l][None, :]
    w["w_out"] = p["w_out"][l].astype(BF16)
    w["g_ffn"] = p["norm_ffn_g"][l][None, :]
    wr = jnp.zeros((D_MODEL, 128), F32)
    wr = wr.at[:, :N_EXPERTS].set(p["moe_w_router"][l]).at[:, N_EXPERTS:N_EXPERTS + 4].set(p["moe_w_group"][l])
    w["w_router"] = wr
    w["b_router"] = _lane_row(p["moe_b_router"][l], 0) + _lane_row(p["moe_b_group"][l], N_EXPERTS)
    w["w_gu"] = jnp.concatenate([p["moe_w_gate"][l], p["moe_w_up"][l]], axis=2).astype(BF16)
    w["w_down"] = p["moe_w_down"][l].astype(BF16)
    return w


def kernel(x, positions, norm_mix_g, w_in, ssm_conv_w, ssm_conv_b, ssm_dt_bias, ssm_a_log, ssm_d, ssm_norm_g, ml_conv_w, ml_conv_b, ml_ig_bias, ml_fg_bias, ml_norm_g, mla_q_norm_g, mla_w_uq, mla_kv_norm_g, mla_w_ukv, w_out, norm_ffn_g, moe_w_group, moe_b_group, moe_w_router, moe_b_router, moe_w_gate, moe_w_up, moe_w_down, final_norm_g):
    p = dict(norm_mix_g=norm_mix_g, w_in=w_in, ssm_conv_w=ssm_conv_w, ssm_conv_b=ssm_conv_b,
             ssm_dt_bias=ssm_dt_bias, ssm_a_log=ssm_a_log, ssm_d=ssm_d, ssm_norm_g=ssm_norm_g,
             ml_conv_w=ml_conv_w, ml_conv_b=ml_conv_b, ml_ig_bias=ml_ig_bias, ml_fg_bias=ml_fg_bias,
             ml_norm_g=ml_norm_g, mla_q_norm_g=mla_q_norm_g, mla_w_uq=mla_w_uq,
             mla_kv_norm_g=mla_kv_norm_g, mla_w_ukv=mla_w_ukv, w_out=w_out, norm_ffn_g=norm_ffn_g,
             moe_w_group=moe_w_group, moe_b_group=moe_b_group, moe_w_router=moe_w_router,
             moe_b_router=moe_b_router, moe_w_gate=moe_w_gate, moe_w_up=moe_w_up, moe_w_down=moe_w_down)
    bsz, seq, _ = x.shape
    depth = w_in.shape[0]
    nb = seq // MOBA_BLOCK
    x2 = x.reshape(bsz * seq, D_MODEL)

    half = MLA_ROPE // 2
    inv = ROPE_THETA ** (-jnp.arange(0, MLA_ROPE, 2, dtype=F32) / MLA_ROPE)
    inv_full = _lane_row(jnp.concatenate([inv, inv]), MLA_NOPE)
    sgn_full = _lane_row(jnp.concatenate([-jnp.ones((half,), F32), jnp.ones((half,), F32)]), MLA_NOPE)
    pe_mask = _lane_row(jnp.ones((MLA_ROPE,), F32), MLA_NOPE)
    c_tok, s_tok, c_tr, s_tr = _rope_tables(positions.reshape(bsz * seq, 1), inv_full, sgn_full, bsz, seq)
    tri = jnp.tril(jnp.ones((CHUNK, CHUNK), F32))
    head_of = jnp.arange(GROUP_W) // HD
    seg = (head_of[:, None] == jnp.arange(128)[None, :]).astype(F32)
    bmask = (head_of[:, None] == head_of[None, :]).astype(F32)
    segm = bmask / HD

    for l in range(depth):
        w = _layer_weights(l, p)
        main, qT_a, vT_a = _in_proj(x2, w["g_mix"], w["w_main"], w["w_qT"], w["w_vT"], bsz, seq)
        kmean, k_a = _kmean(main, bsz, seq)
        bias, qTb_a = _moba_select(qT_a, kmean.reshape(bsz, nb, HEADS * HP), bsz, seq)
        y_a = _attention(qTb_a, k_a, vT_a, bias, bsz, seq)
        y_b = _ssd(main, w["ssm_cw"], w["ssm_cb"], w["ssm_dtb"], w["ssm_arow"], w["ssm_dexp"], w["ssm_ng"],
                   tri, bsz, seq)
        y_c = _mlstm(main, w["ml_cw"], w["ml_cb"], w["ml_igb"], w["ml_fgb"], w["ml_ng"], tri, seg, segm,
                     bmask, bsz, seq)
        qT_d, k_d, vT_d = _mla_prep(main, c_tok, s_tok, c_tr, s_tr, w["mla_gq"], w["mla_wq2T"], w["mla_gkv"],
                                    w["mla_wk"], w["mla_wvT"], pe_mask, bsz, seq)
        y_d = _attention(qT_d, k_d, vT_d, None, bsz, seq)
        xn, h2, comb = _outproj(x2, y_a, y_b, y_c, y_d, w["w_out"], w["g_ffn"], w["w_router"], w["b_router"])
        x2 = _moe(h2, comb, xn, w["w_gu"], w["w_down"], final_norm_g[None, :], final_norm=(l == depth - 1))
    return x2.reshape(bsz, seq, D_MODEL)
```

```python
import functools
import math

import numpy as np
import jax
import jax.numpy as jnp
from jax import lax
from jax.experimental import pallas as pl
from jax.experimental.pallas import tpu as pltpu

F32 = jnp.float32
BF16 = jnp.bfloat16
HIGHEST = lax.Precision.HIGHEST

EPS = 1e-6
NEG = -1e30
LOG2E = 1.4426950408889634

D_MODEL = 1024
GROUP_W = 256
HEADS = 4
HD = 64
HP = 128
MOBA_BLOCK = 256
MOBA_TOPK = 3
SSM_STATE = 128
SSM_XBC = 768
CONV_K = 4
CHUNK = 128
MLA_NOPE = 64
MLA_ROPE = 32
ROPE_THETA = 10000.0
N_EXPERTS = 16
EXPERTS_PER_GROUP = 4
D_EXPERT = 256

COL_ZX = 0
COL_ML = 1024
COL_CQ = 2048
COL_CKV = 2304
COL_SMALL = 2432
COL_KA = 2560
N_MAIN = 3072
VR = 80
ATT_UQ = 256
ATT_UK = 256

V7X_VMEM_LIMIT = 56 * 1024 * 1024

NT_DIMS = (((1,), (1,)), ((), ()))


def _cparams(sem):
    return pltpu.CompilerParams(dimension_semantics=sem, vmem_limit_bytes=V7X_VMEM_LIMIT)


def _rms(x, g):
    ms = jnp.mean(x * x, axis=-1, keepdims=True)
    return x * lax.rsqrt(ms + EPS) * g


def _sigmoid(x):
    return 1.0 / (1.0 + jnp.exp(-x))


def _softplus(x):
    return jnp.maximum(x, 0.0) + jnp.log1p(jnp.exp(-jnp.abs(x)))


def _tril(n):
    r = lax.broadcasted_iota(jnp.int32, (n, n), 0)
    c = lax.broadcasted_iota(jnp.int32, (n, n), 1)
    return c <= r


def _ones_rows(shape):
    row = lax.broadcasted_iota(jnp.int32, shape, 0)
    hit = row == HD
    for h in range(1, HEADS):
        hit = jnp.logical_or(hit, row == h * VR + HD)
    return jnp.where(hit, 1.0, 0.0)


def _in_proj_body(x_ref, g_ref, wm_ref, wq_ref, wv_ref, om_ref, oq_ref, ov_ref):
    h = _rms(x_ref[...], g_ref[...]).astype(BF16)
    n = om_ref.shape[1]
    for c in range(0, n, 512):
        om_ref[:, c:c + 512] = jnp.dot(h, wm_ref[:, c:c + 512], preferred_element_type=F32)
    oq_ref[0] = lax.dot_general(wq_ref[...], h, NT_DIMS, preferred_element_type=F32)
    vt = lax.dot_general(wv_ref[...], h, NT_DIMS, preferred_element_type=F32)
    ov_ref[0] = (vt + _ones_rows(vt.shape)).astype(BF16)


def _in_proj(x2, g, w_main, w_qT, w_vT, bsz, seq, tm=512):
    t = x2.shape[0]
    nst = seq // tm
    return pl.pallas_call(
        _in_proj_body,
        grid=(t // tm,),
        in_specs=[
            pl.BlockSpec((tm, D_MODEL), lambda i: (i, 0)),
            pl.BlockSpec((1, D_MODEL), lambda i: (0, 0)),
            pl.BlockSpec((D_MODEL, N_MAIN), lambda i: (0, 0)),
            pl.BlockSpec((HEADS * HP, D_MODEL), lambda i: (0, 0)),
            pl.BlockSpec((HEADS * VR, D_MODEL), lambda i: (0, 0)),
        ],
        out_specs=[
            pl.BlockSpec((tm, N_MAIN), lambda i: (i, 0)),
            pl.BlockSpec((1, HEADS * HP, tm), lambda i: (i // nst, 0, i % nst)),
            pl.BlockSpec((1, HEADS * VR, tm), lambda i: (i // nst, 0, i % nst)),
        ],
        out_shape=[
            jax.ShapeDtypeStruct((t, N_MAIN), F32),
            jax.ShapeDtypeStruct((bsz, HEADS * HP, seq), F32),
            jax.ShapeDtypeStruct((bsz, HEADS * VR, seq), BF16),
        ],
        compiler_params=_cparams(("arbitrary",)),
        name="in_proj",
    )(x2, g, w_main, w_qT, w_vT)


def _kmean_body(k_ref, km_ref, kb_ref, *, nb):
    k = k_ref[...]
    km_ref[0] = jnp.mean(k, axis=0, keepdims=True)
    blk = pl.program_id(0) % nb
    lane = lax.broadcasted_iota(jnp.int32, k.shape, 1)
    kb_ref[...] = jnp.where((lane & (HP - 1)) == HD + blk, 1.0, k).astype(BF16)


def _kmean(main, bsz, seq):
    t = main.shape[0]
    nb = seq // MOBA_BLOCK
    assert nb <= HP - HD
    kcol = COL_KA // (HEADS * HP)
    return pl.pallas_call(
        functools.partial(_kmean_body, nb=nb),
        grid=(bsz * nb,),
        in_specs=[pl.BlockSpec((MOBA_BLOCK, HEADS * HP), lambda i: (i, kcol))],
        out_specs=[
            pl.BlockSpec((1, 1, HEADS * HP), lambda i: (i, 0, 0)),
            pl.BlockSpec((MOBA_BLOCK, HEADS * HP), lambda i: (i, 0)),
        ],
        out_shape=[
            jax.ShapeDtypeStruct((bsz * nb, 1, HEADS * HP), F32),
            jax.ShapeDtypeStruct((t, HEADS * HP), BF16),
        ],
        compiler_params=_cparams(("arbitrary",)),
        name="moba_kmean",
    )(main)


def _moba_select_body(qT_ref, km_ref, qb_ref, *, ts, nb, qscale):
    s_idx = pl.program_id(1)
    qT = qT_ref[0]
    km = km_ref[0]
    pos = s_idx * ts + lax.broadcasted_iota(jnp.int32, (nb, ts), 1)
    cur = lax.shift_right_logical(pos, int(math.log2(MOBA_BLOCK)))
    blk = lax.broadcasted_iota(jnp.int32, (nb, ts), 0)
    past = blk < cur
    for h in range(HEADS):
        qh = qT[h * HP:(h + 1) * HP, :]
        g = jnp.dot(km[:, h * HP:(h + 1) * HP], qh, precision=HIGHEST,
                    preferred_element_type=F32)
        g = jnp.where(past, g, -jnp.inf)
        bias = jnp.where(blk == cur, 0.0, NEG)
        for _ in range(MOBA_TOPK):
            m = jnp.max(g, axis=0, keepdims=True)
            cand = jnp.logical_and(g == m, g > -jnp.inf)
            idx = jnp.min(jnp.where(cand, blk, nb), axis=0, keepdims=True)
            pick = blk == idx
            bias = jnp.where(pick, 0.0, bias)
            g = jnp.where(pick, -jnp.inf, g)
        rows = [qh[:HD] * qscale, bias]
        if HP - HD - nb:
            rows.append(jnp.zeros((HP - HD - nb, ts), F32))
        qb_ref[0, h * HP:(h + 1) * HP, :] = jnp.concatenate(rows, axis=0).astype(BF16)


def _moba_select(qT, kmean, bsz, seq, ts=512):
    nb = seq // MOBA_BLOCK
    qscale = (HD ** -0.5) * LOG2E
    return pl.pallas_call(
        functools.partial(_moba_select_body, ts=ts, nb=nb, qscale=qscale),
        grid=(bsz, seq // ts),
        in_specs=[
            pl.BlockSpec((1, HEADS * HP, ts), lambda b, s: (b, 0, s)),
            pl.BlockSpec((1, nb, HEADS * HP), lambda b, s: (b, 0, 0)),
        ],
        out_specs=pl.BlockSpec((1, HEADS * HP, ts), lambda b, s: (b, 0, s)),
        out_shape=jax.ShapeDtypeStruct((bsz, HEADS * HP, seq), BF16),
        compiler_params=_cparams(("arbitrary", "arbitrary")),
        name="moba_select",
    )(qT, kmean)


def _rope_body(pos_ref, inv_ref, sgn_ref, c_ref, s_ref, ct_ref, st_ref):
    ang = pos_ref[...].astype(F32) * inv_ref[...]
    c = jnp.cos(ang)
    s = jnp.sin(ang) * sgn_ref[...]
    c_ref[...] = c
    s_ref[...] = s
    ct_ref[0] = c.T
    st_ref[0] = s.T


def _rope_tables(pos2, inv_full, sgn_full, bsz, seq, ts=512):
    t = pos2.shape[0]
    nst = seq // ts
    tok = pl.BlockSpec((ts, HP), lambda i: (i, 0))
    tr = pl.BlockSpec((1, HP, ts), lambda i: (i // nst, 0, i % nst))
    row = pl.BlockSpec((1, HP), lambda i: (0, 0))
    return pl.pallas_call(
        _rope_body,
        grid=(t // ts,),
        in_specs=[pl.BlockSpec((ts, 1), lambda i: (i, 0)), row, row],
        out_specs=[tok, tok, tr, tr],
        out_shape=[
            jax.ShapeDtypeStruct((t, HP), F32),
            jax.ShapeDtypeStruct((t, HP), F32),
            jax.ShapeDtypeStruct((bsz, HP, seq), F32),
            jax.ShapeDtypeStruct((bsz, HP, seq), F32),
        ],
        compiler_params=_cparams(("arbitrary",)),
        name="rope_tables",
    )(pos2, inv_full, sgn_full)


def _mla_prep_body(cq_ref, ckv_ref, sm_ref, c_ref, s_ref, ct_ref, st_ref, gq_ref, wq_ref, gkv_ref,
                   wk_ref, wv_ref, msk_ref, qo_ref, ko_ref, vo_ref, *, qscale):
    qn = _rms(cq_ref[...], gq_ref[...]).astype(BF16)
    q2 = lax.dot_general(wq_ref[...], qn, NT_DIMS, preferred_element_type=F32)
    nq = HEADS * HP
    ct4 = jnp.concatenate([ct_ref[0]] * HEADS, axis=0)
    st4 = jnp.concatenate([st_ref[0]] * HEADS, axis=0)
    q = (q2[:nq] * ct4 + q2[nq:] * st4) * qscale
    qo_ref[0] = q.astype(BF16)
    kvn = _rms(ckv_ref[...], gkv_ref[...]).astype(BF16)
    kn = jnp.dot(kvn, wk_ref[...], preferred_element_type=F32)
    sm = sm_ref[...]
    pe = (sm * c_ref[...] + pltpu.roll(sm, HP - MLA_ROPE, 1) * s_ref[...]) * msk_ref[...]
    ko_ref[...] = (kn + jnp.concatenate([pe] * HEADS, axis=1)).astype(BF16)
    vt = lax.dot_general(wv_ref[...], kvn, NT_DIMS, preferred_element_type=F32)
    vo_ref[0] = (vt + _ones_rows(vt.shape)).astype(BF16)


def _mla_prep(main, c_tok, s_tok, c_tr, s_tr, gq, wq2T, gkv, wk, wvT, pe_mask, bsz, seq, ts=512):
    t = main.shape[0]
    nst = seq // ts
    qscale = ((MLA_NOPE + MLA_ROPE) ** -0.5) * LOG2E
    tok128 = pl.BlockSpec((ts, HP), lambda i: (i, 0))
    tr128 = pl.BlockSpec((1, HP, ts), lambda i: (i // nst, 0, i % nst))

    def full(shape):
        return pl.BlockSpec(shape, lambda i: tuple(0 for _ in shape))

    return pl.pallas_call(
        functools.partial(_mla_prep_body, qscale=qscale),
        grid=(t // ts,),
        in_specs=[
            pl.BlockSpec((ts, 256), lambda i: (i, COL_CQ // 256)),
            pl.BlockSpec((ts, 128), lambda i: (i, COL_CKV // 128)),
            pl.BlockSpec((ts, 128), lambda i: (i, COL_SMALL // 128)),
            tok128, tok128, tr128, tr128,
            full((1, 256)), full((2 * HEADS * HP, 256)), full((1, 128)),
            full((128, HEADS * HP)), full((HEADS * VR, 128)), full((1, HP)),
        ],
        out_specs=[
            pl.BlockSpec((1, HEADS * HP, ts), lambda i: (i // nst, 0, i % nst)),
            pl.BlockSpec((ts, HEADS * HP), lambda i: (i, 0)),
            pl.BlockSpec((1, HEADS * VR, ts), lambda i: (i // nst, 0, i % nst)),
        ],
        out_shape=[
            jax.ShapeDtypeStruct((bsz, HEADS * HP, seq), BF16),
            jax.ShapeDtypeStruct((t, HEADS * HP), BF16),
            jax.ShapeDtypeStruct((bsz, HEADS * VR, seq), BF16),
        ],
        compiler_params=_cparams(("arbitrary",)),
        name="mla_prep",
    )(main, main, main, c_tok, s_tok, c_tr, s_tr, gq, wq2T, gkv, wk, wvT, pe_mask)


def _attn_body(ii_ref, jj_ref, qT_ref, k_ref, vT_ref, o_ref, m_sc, acc_sc, *, tq, tk):
    p = pl.program_id(1)
    i = ii_ref[p]
    j = jj_ref[p]

    @pl.when(j == 0)
    def _():
        m_sc[...] = jnp.full(m_sc.shape, NEG, F32)
        acc_sc[...] = jnp.zeros(acc_sc.shape, F32)

    def scores(u):
        h, qc, kb, _ = u
        return jnp.dot(k_ref[kb * ATT_UK:(kb + 1) * ATT_UK, h * HP:(h + 1) * HP],
                       qT_ref[0, h * HP:(h + 1) * HP, qc * ATT_UQ:(qc + 1) * ATT_UQ],
                       preferred_element_type=F32)

    def update(u, s):
        h, qc, kb, masked = u
        if masked:
            kpos = kb * ATT_UK + lax.broadcasted_iota(jnp.int32, s.shape, 0)
            qpos = qc * ATT_UQ + lax.broadcasted_iota(jnp.int32, s.shape, 1)
            s = jnp.where(kpos <= qpos, s, NEG)
        alphas, pms = [], []
        for c in range(ATT_UQ // 128):
            ql = slice(qc * ATT_UQ + c * 128, qc * ATT_UQ + (c + 1) * 128)
            sc = s[:, c * 128:(c + 1) * 128]
            m_old = m_sc[h:h + 1, ql]
            m_new = jnp.maximum(m_old, jnp.max(sc, axis=0, keepdims=True))
            alphas.append(jnp.exp2(m_old - m_new))
            pms.append(jnp.exp2(sc - m_new).astype(BF16))
            m_sc[h:h + 1, ql] = m_new
        ql = slice(qc * ATT_UQ, (qc + 1) * ATT_UQ)
        rows = slice(h * VR, (h + 1) * VR)
        acc_sc[rows, ql] = jnp.concatenate(alphas, axis=1) * acc_sc[rows, ql] + jnp.dot(
            vT_ref[0, rows, kb * ATT_UK:(kb + 1) * ATT_UK], jnp.concatenate(pms, axis=1),
            preferred_element_type=F32)

    def run(units):
        ahead = 8
        pending = [scores(u) for u in units[:ahead]]
        for n, u in enumerate(units):
            s = pending.pop(0)
            if n + ahead < len(units):
                pending.append(scores(units[n + ahead]))
            update(u, s)

    def unit_list(diag):
        units = []
        for kb in range(tk // ATT_UK):
            for h in range(HEADS):
                for qc in range(tq // ATT_UQ):
                    k_lo, k_hi = kb * ATT_UK, (kb + 1) * ATT_UK - 1
                    q_lo, q_hi = qc * ATT_UQ, (qc + 1) * ATT_UQ - 1
                    if diag and k_lo > q_hi:
                        continue
                    units.append((h, qc, kb, diag and k_hi > q_lo))
        return units

    @pl.when(j < i)
    def _():
        run(unit_list(False))

    @pl.when(j == i)
    def _():
        run(unit_list(True))
        parts = []
        for h in range(HEADS):
            den = acc_sc[h * VR + HD:h * VR + HD + 1, :]
            parts.append(acc_sc[h * VR:h * VR + HD, :] * (1.0 / den))
        o_ref[...] = jnp.concatenate(parts, axis=0).T


def _attention(qT, k, vT, bsz, seq, name, tq=512):
    t = k.shape[0]
    nq = seq // tq
    ii = np.concatenate([np.full(i + 1, i, np.int32) for i in range(nq)])
    jj = np.concatenate([np.arange(i + 1, dtype=np.int32) for i in range(nq)])
    grid_spec = pltpu.PrefetchScalarGridSpec(
        num_scalar_prefetch=2,
        grid=(bsz, len(ii)),
        in_specs=[
            pl.BlockSpec((1, HEADS * HP, tq), lambda b, p, ii, jj: (b, 0, ii[p])),
            pl.BlockSpec((tq, HEADS * HP), lambda b, p, ii, jj: (b * nq + jj[p], 0)),
            pl.BlockSpec((1, HEADS * VR, tq), lambda b, p, ii, jj: (b, 0, jj[p])),
        ],
        out_specs=pl.BlockSpec((tq, GROUP_W), lambda b, p, ii, jj: (b * nq + ii[p], 0)),
        scratch_shapes=[
            pltpu.VMEM((8, tq), F32),
            pltpu.VMEM((HEADS * VR, tq), F32),
        ],
    )
    return pl.pallas_call(
        functools.partial(_attn_body, tq=tq, tk=tq),
        grid_spec=grid_spec,
        out_shape=jax.ShapeDtypeStruct((t, GROUP_W), F32),
        compiler_params=_cparams(("arbitrary", "arbitrary")),
        name=name,
    )(jnp.asarray(ii), jnp.asarray(jj), qT, k, vT)


def _conv_silu(xpad, src, cw_ref, cb_ref, first, ls):
    width = xpad.shape[1]

    @pl.when(first)
    def _():
        xpad[0:8, :] = jnp.zeros((8, width), F32)

    @pl.when(jnp.logical_not(first))
    def _():
        xpad[0:8, :] = xpad[ls:ls + 8, :]

    xpad[8:ls + 8, :] = src
    acc = cb_ref[...] + cw_ref[0:1, :] * xpad[pl.ds(8 - CONV_K + 1, ls), :]
    for kk in range(1, CONV_K):
        acc = acc + cw_ref[kk:kk + 1, :] * xpad[pl.ds(8 - CONV_K + 1 + kk, ls), :]
    return acc * _sigmoid(acc)


def _ssd_body(zx_ref, sm_ref, cw_ref, cb_ref, dtb_ref, arow_ref, dexp_ref, ng_ref, tri_ref, o_ref,
              xpad, state, *, ls):
    first = pl.program_id(1) == 0

    @pl.when(first)
    def _():
        state[...] = jnp.zeros(state.shape, F32)

    xc = _conv_silu(xpad, zx_ref[:, GROUP_W:GROUP_W + SSM_XBC], cw_ref, cb_ref, first, ls)
    dt_all = _softplus(sm_ref[...] + dtb_ref[...])
    da_all = dt_all * arow_ref[...]
    lane = lax.broadcasted_iota(jnp.int32, (1, 128), 1)
    half = lane < HD
    tril = _tril(CHUNK)
    for c in range(ls // CHUNK):
        r0 = c * CHUNK
        dt_c = dt_all[r0:r0 + CHUNK]
        acum = jnp.dot(tri_ref[...], da_all[r0:r0 + CHUNK], precision=HIGHEST,
                       preferred_element_type=F32)
        acum_t = acum.T
        alast = acum[CHUNK - 1:CHUNK, :]
        for g in range(2):
            h0, h1 = 2 * g, 2 * g + 1
            xg = xc[r0:r0 + CHUNK, g * 128:(g + 1) * 128]
            bg = xc[r0:r0 + CHUNK, GROUP_W + g * 128:GROUP_W + (g + 1) * 128]
            cg = xc[r0:r0 + CHUNK, 2 * GROUP_W + g * 128:2 * GROUP_W + (g + 1) * 128]
            cgb = cg.astype(BF16)
            gram = lax.dot_general(cgb, bg.astype(BF16), NT_DIMS, preferred_element_type=F32)
            a0 = acum[:, h0:h0 + 1]
            a1 = acum[:, h1:h1 + 1]
            xdt = xg * jnp.where(half, dt_c[:, h0:h0 + 1], dt_c[:, h1:h1 + 1])
            ydiag = jnp.zeros((CHUNK, 128), F32)
            for hh, col, msk in ((h0, a0, half), (h1, a1, jnp.logical_not(half))):
                dec = jnp.exp(jnp.where(tril, col - acum_t[hh:hh + 1, :], -jnp.inf))
                ydiag = ydiag + jnp.dot((gram * dec).astype(BF16),
                                        jnp.where(msk, xdt, 0.0).astype(BF16),
                                        preferred_element_type=F32)
            st = state[g]
            yoff = jnp.dot(cgb, st.astype(BF16), preferred_element_type=F32) * jnp.where(
                half, jnp.exp(a0), jnp.exp(a1))
            l0 = alast[:, h0:h0 + 1]
            l1 = alast[:, h1:h1 + 1]
            dout = jnp.where(half, jnp.exp(l0 - a0), jnp.exp(l1 - a1))
            state[g] = (jnp.where(half, jnp.exp(l0), jnp.exp(l1)) * st
                        + jnp.dot(bg.T.astype(BF16), (xdt * dout).astype(BF16), preferred_element_type=F32))
            y = ydiag + yoff + xg * dexp_ref[:, g * 128:(g + 1) * 128]
            zg = zx_ref[r0:r0 + CHUNK, g * 128:(g + 1) * 128]
            y = y * (zg * _sigmoid(zg))
            o_ref[r0:r0 + CHUNK, g * 128:(g + 1) * 128] = _rms(y, ng_ref[:, g * 128:(g + 1) * 128])


def _ssd(main, cw, cb, dtb, arow, dexp, ng, tri, bsz, seq, ls=512):
    t = main.shape[0]
    ns = seq // ls

    def full(shape):
        return pl.BlockSpec(shape, lambda b, s: tuple(0 for _ in shape))

    return pl.pallas_call(
        functools.partial(_ssd_body, ls=ls),
        grid=(bsz, ns),
        in_specs=[
            pl.BlockSpec((ls, 1024), lambda b, s: (b * ns + s, COL_ZX // 1024)),
            pl.BlockSpec((ls, 128), lambda b, s: (b * ns + s, COL_SMALL // 128)),
            full((CONV_K, SSM_XBC)), full((1, SSM_XBC)), full((1, 128)), full((1, 128)),
            full((1, GROUP_W)), full((1, GROUP_W)), full((CHUNK, CHUNK)),
        ],
        out_specs=pl.BlockSpec((ls, GROUP_W), lambda b, s: (b * ns + s, 0)),
        out_shape=jax.ShapeDtypeStruct((t, GROUP_W), F32),
        scratch_shapes=[pltpu.VMEM((ls + 8, SSM_XBC), F32), pltpu.VMEM((2, SSM_STATE, 128), F32)],
        compiler_params=_cparams(("arbitrary", "arbitrary")),
        name="ssd",
    )(main, main, cw, cb, dtb, arow, dexp, ng, tri)


def _mlstm_body(c_ref, sm_ref, cw_ref, cb_ref, igb_ref, fgb_ref, ng_ref, tri_ref, seg_ref, segm_ref,
                bmask_ref, o_ref, xpad, cst, nst, mst, *, ls):
    first = pl.program_id(1) == 0

    @pl.when(first)
    def _():
        cst[...] = jnp.zeros(cst.shape, F32)
        nst[...] = jnp.zeros(nst.shape, F32)
        mst[...] = jnp.zeros(mst.shape, F32)

    qk = _conv_silu(xpad, c_ref[:, 0:2 * GROUP_W], cw_ref, cb_ref, first, ls)
    sm = sm_ref[...]
    ig_all = sm + igb_ref[...]
    fx = sm + fgb_ref[...]
    lf_all = jnp.minimum(fx, 0.0) - jnp.log1p(jnp.exp(-jnp.abs(fx)))
    lane = lax.broadcasted_iota(jnp.int32, (1, GROUP_W), 1)
    tril = _tril(CHUNK)
    for c in range(ls // CHUNK):
        r0 = c * CHUNK
        q = qk[r0:r0 + CHUNK, 0:GROUP_W]
        k = qk[r0:r0 + CHUNK, GROUP_W:2 * GROUP_W] * (HD ** -0.5)
        v = c_ref[r0:r0 + CHUNK, 2 * GROUP_W:3 * GROUP_W]
        o_pre = c_ref[r0:r0 + CHUNK, 3 * GROUP_W:4 * GROUP_W]
        ig = ig_all[r0:r0 + CHUNK]
        bcs = jnp.dot(tri_ref[...], lf_all[r0:r0 + CHUNK], precision=HIGHEST,
                      preferred_element_type=F32)
        bcs_t = bcs.T
        ig_t = ig.T
        kb = k.astype(BF16)
        q_c = jnp.dot(q.astype(BF16), cst[...].astype(BF16), preferred_element_type=F32)
        q_n = jnp.dot(q * nst[0:1, :], seg_ref[...], precision=HIGHEST, preferred_element_type=F32)
        num = jnp.zeros((CHUNK, GROUP_W), F32)
        w_inter_x = jnp.zeros((CHUNK, GROUP_W), F32)
        inv_den_x = jnp.zeros((CHUNK, GROUP_W), F32)
        w_state_x = jnp.zeros((CHUNK, GROUP_W), F32)
        keep_x = jnp.zeros((1, GROUP_W), F32)
        for h in range(HEADS):
            hm = jnp.logical_and(lane >= h * HD, lane < (h + 1) * HD)
            bcol = bcs[:, 8 + h:9 + h]
            igcol = ig[:, 4 + h:5 + h]
            m_prev = mst[h:h + 1, 0:1]
            dmat = jnp.where(tril, bcol - bcs_t[8 + h:9 + h, :] + ig_t[4 + h:5 + h, :], -jnp.inf)
            inter = bcol + m_prev
            m_t = jnp.maximum(inter, jnp.max(dmat, axis=1, keepdims=True))
            s_qk = lax.dot_general(jnp.where(hm, q, 0.0).astype(BF16), kb, NT_DIMS,
                                   preferred_element_type=F32)
            sc = s_qk * jnp.exp(dmat - m_t)
            w_inter = jnp.exp(inter - m_t)
            den = jnp.sum(sc, axis=1, keepdims=True) + w_inter * q_n[:, h:h + 1]
            inv_den = 1.0 / jnp.maximum(jnp.abs(den), jnp.exp(-m_t))
            num = num + jnp.dot(sc.astype(BF16), jnp.where(hm, v, 0.0).astype(BF16),
                                preferred_element_type=F32)
            w_inter_x = w_inter_x + jnp.where(hm, w_inter, 0.0)
            inv_den_x = inv_den_x + jnp.where(hm, inv_den, 0.0)
            blast = bcs[CHUNK - 1:CHUNK, 8 + h:9 + h]
            gcol = blast - bcol + igcol
            m_new = jnp.maximum(blast + m_prev, jnp.max(gcol, axis=0, keepdims=True))
            w_state_x = w_state_x + jnp.where(hm, jnp.exp(gcol - m_new), 0.0)
            keep_x = keep_x + jnp.where(hm, jnp.exp(blast + m_prev - m_new), 0.0)
            mst[h:h + 1, :] = jnp.broadcast_to(m_new, (1, 128))
        hout = (num + w_inter_x * q_c) * inv_den_x
        kw = k * w_state_x
        cst[...] = keep_x * cst[...] + bmask_ref[...] * jnp.dot(
            kw.T.astype(BF16), v.astype(BF16), preferred_element_type=F32)
        nst[0:1, :] = keep_x * nst[0:1, :] + jnp.sum(kw, axis=0, keepdims=True)
        hh = hout * _sigmoid(o_pre)
        ms = jnp.dot(hh * hh, segm_ref[...], precision=HIGHEST, preferred_element_type=F32)
        o_ref[r0:r0 + CHUNK, :] = hh * lax.rsqrt(ms + EPS) * ng_ref[...]


def _mlstm(main, cw, cb, igb, fgb, ng, tri, seg, segm, bmask, bsz, seq, ls=512):
    t = main.shape[0]
    ns = seq // ls

    def full(shape):
        return pl.BlockSpec(shape, lambda b, s: tuple(0 for _ in shape))

    return pl.pallas_call(
        functools.partial(_mlstm_body, ls=ls),
        grid=(bsz, ns),
        in_specs=[
            pl.BlockSpec((ls, 1024), lambda b, s: (b * ns + s, COL_ML // 1024)),
            pl.BlockSpec((ls, 128), lambda b, s: (b * ns + s, COL_SMALL // 128)),
            full((CONV_K, 2 * GROUP_W)), full((1, 2 * GROUP_W)), full((1, 128)), full((1, 128)),
            full((1, GROUP_W)), full((CHUNK, CHUNK)), full((GROUP_W, 128)), full((GROUP_W, GROUP_W)),
            full((GROUP_W, GROUP_W)),
        ],
        out_specs=pl.BlockSpec((ls, GROUP_W), lambda b, s: (b * ns + s, 0)),
        out_shape=jax.ShapeDtypeStruct((t, GROUP_W), F32),
        scratch_shapes=[
            pltpu.VMEM((ls + 8, 2 * GROUP_W), F32),
            pltpu.VMEM((GROUP_W, GROUP_W), F32),
            pltpu.VMEM((8, GROUP_W), F32),
            pltpu.VMEM((8, 128), F32),
        ],
        compiler_params=_cparams(("arbitrary", "arbitrary")),
        name="mlstm",
    )(main, main, cw, cb, igb, fgb, ng, tri, seg, segm, bmask)


def _outproj_body(x_ref, ya_ref, yb_ref, yc_ref, yd_ref, wo_ref, g2_ref, wr_ref, br_ref,
                  xn_ref, h2_ref, cb_ref):
    acc = x_ref[...]
    for gi, y_ref in enumerate((ya_ref, yb_ref, yc_ref, yd_ref)):
        acc = acc + jnp.dot(y_ref[...].astype(BF16), wo_ref[gi * GROUP_W:(gi + 1) * GROUP_W, :],
                            preferred_element_type=F32)
    xn_ref[...] = acc
    h2 = _rms(acc, g2_ref[...])
    h2_ref[...] = h2.astype(BF16)
    logits = jnp.dot(h2, wr_ref[...], precision=HIGHEST, preferred_element_type=F32) + br_ref[...]
    lane = lax.broadcasted_iota(jnp.int32, logits.shape, 1)
    big = jnp.int32(1 << 20)
    n_grp = N_EXPERTS // EXPERTS_PER_GROUP
    gl = jnp.where(jnp.logical_and(lane >= N_EXPERTS, lane < N_EXPERTS + n_grp), logits, -jnp.inf)
    gmax = jnp.max(gl, axis=1, keepdims=True)
    g_w = 1.0 / jnp.sum(jnp.exp(gl - gmax), axis=1, keepdims=True)
    gidx = jnp.min(jnp.where(gl == gmax, lane, big), axis=1, keepdims=True) - N_EXPERTS
    in_grp = jnp.logical_and(lane < N_EXPERTS,
                             lax.shift_right_logical(lane, int(math.log2(EXPERTS_PER_GROUP))) == gidx)
    el = jnp.where(in_grp, logits, -jnp.inf)
    emax = jnp.max(el, axis=1, keepdims=True)
    esum = jnp.sum(jnp.exp(el - emax), axis=1, keepdims=True)
    i1 = jnp.min(jnp.where(el == emax, lane, big), axis=1, keepdims=True)
    el2 = jnp.where(lane == i1, -jnp.inf, el)
    emax2 = jnp.max(el2, axis=1, keepdims=True)
    i2 = jnp.min(jnp.where(el2 == emax2, lane, big), axis=1, keepdims=True)
    p1 = 1.0 / esum
    p2 = jnp.exp(emax2 - emax) / esum
    w1 = p1 / (p1 + p2)
    w2 = p2 / (p1 + p2)
    cb_ref[...] = g_w * (jnp.where(lane == i1, w1, 0.0) + jnp.where(lane == i2, w2, 0.0))


def _outproj(x2, ya, yb, yc, yd, wo, g2, wr, br, tm=512):
    t = x2.shape[0]
    row = pl.BlockSpec((tm, D_MODEL), lambda i: (i, 0))
    grp = pl.BlockSpec((tm, GROUP_W), lambda i: (i, 0))

    def full(shape):
        return pl.BlockSpec(shape, lambda i: tuple(0 for _ in shape))

    return pl.pallas_call(
        _outproj_body,
        grid=(t // tm,),
        in_specs=[row, grp, grp, grp, grp, full((D_MODEL, D_MODEL)), full((1, D_MODEL)),
                  full((D_MODEL, 128)), full((1, 128))],
        out_specs=[row, row, pl.BlockSpec((tm, 128), lambda i: (i, 0))],
        out_shape=[
            jax.ShapeDtypeStruct((t, D_MODEL), F32),
            jax.ShapeDtypeStruct((t, D_MODEL), BF16),
            jax.ShapeDtypeStruct((t, 128), F32),
        ],
        compiler_params=_cparams(("arbitrary",)),
        name="outproj_router",
    )(x2, ya, yb, yc, yd, wo, g2, wr, br)


def _moe_body(h_ref, cb_ref, xn_ref, wgu_ref, wd_ref, gf_ref, o_ref, acc, *, final_norm):
    e = pl.program_id(1)

    @pl.when(e == 0)
    def _():
        acc[...] = jnp.zeros(acc.shape, F32)

    gu = jnp.dot(h_ref[...], wgu_ref[0], preferred_element_type=F32)
    g = gu[:, :D_EXPERT]
    u = gu[:, D_EXPERT:]
    lane = lax.broadcasted_iota(jnp.int32, cb_ref.shape, 1)
    col = jnp.sum(jnp.where(lane == e, cb_ref[...], 0.0), axis=1, keepdims=True)
    hid = (g * _sigmoid(g)) * u * col
    acc[...] += jnp.dot(hid.astype(BF16), wd_ref[0], preferred_element_type=F32)

    @pl.when(e == N_EXPERTS - 1)
    def _():
        y = xn_ref[...] + acc[...]
        if final_norm:
            y = _rms(y, gf_ref[...])
        o_ref[...] = y


def _moe(h2, comb, xn, wgu, wd, gf, final_norm, tm=1024):
    t = h2.shape[0]
    return pl.pallas_call(
        functools.partial(_moe_body, final_norm=final_norm),
        grid=(t // tm, N_EXPERTS),
        in_specs=[
            pl.BlockSpec((tm, D_MODEL), lambda i, e: (i, 0)),
            pl.BlockSpec((tm, 128), lambda i, e: (i, 0)),
            pl.BlockSpec((tm, D_MODEL), lambda i, e: (i, 0)),
            pl.BlockSpec((1, D_MODEL, 2 * D_EXPERT), lambda i, e: (e, 0, 0)),
            pl.BlockSpec((1, D_EXPERT, D_MODEL), lambda i, e: (e, 0, 0)),
            pl.BlockSpec((1, D_MODEL), lambda i, e: (0, 0)),
        ],
        out_specs=pl.BlockSpec((tm, D_MODEL), lambda i, e: (i, 0)),
        out_shape=jax.ShapeDtypeStruct((t, D_MODEL), F32),
        scratch_shapes=[pltpu.VMEM((tm, D_MODEL), F32)],
        compiler_params=_cparams(("arbitrary", "arbitrary")),
        name="moe",
    )(h2, comb, xn, wgu, wd, gf)


def _pad_heads(w, width):
    rows = w.shape[0]
    w = w.reshape(rows, HEADS, width)
    return jnp.pad(w, ((0, 0), (0, 0), (0, HP - width))).reshape(rows, HEADS * HP)


def _v_rows_t(w):
    rows = w.shape[0]
    w = jnp.pad(w.reshape(rows, HEADS, HD), ((0, 0), (0, 0), (0, VR - HD)))
    return w.reshape(rows, HEADS * VR).T


def _lane_row(vals, offset, width=128):
    return jnp.zeros((1, width), F32).at[0, offset:offset + vals.shape[0]].set(vals.astype(F32))


def _split_w_in(w):
    sizes = (3 * GROUP_W, GROUP_W, SSM_XBC, HEADS, 2 * GROUP_W, GROUP_W, HEADS, HEADS, GROUP_W,
             256, 128, MLA_ROPE)
    offs = np.cumsum((0,) + sizes)
    return [w[:, int(offs[i]):int(offs[i + 1])] for i in range(len(sizes))]


def _layer_weights(l, p):
    a_qkv, s_z, s_xbc, s_dt, m_qk, m_v, m_i, m_f, m_o, d_cq, d_ckv, d_kr = _split_w_in(p["w_in"][l])
    aq, ak, av = a_qkv[:, :GROUP_W], a_qkv[:, GROUP_W:2 * GROUP_W], a_qkv[:, 2 * GROUP_W:]
    half = MLA_ROPE // 2
    small = jnp.zeros((D_MODEL, 128), F32)
    small = small.at[:, 0:4].set(s_dt).at[:, 4:8].set(m_i).at[:, 8:12].set(m_f)
    small = small.at[:, 64:96].set(d_kr)
    small = small.at[:, 96:128].set(jnp.concatenate([d_kr[:, half:], d_kr[:, :half]], axis=1))
    w_main = jnp.concatenate([s_z, s_xbc, m_qk, m_v, m_o, d_cq, d_ckv, small, _pad_heads(ak, HD)],
                             axis=1).astype(BF16)
    w = {"w_main": w_main,
         "w_qT": _pad_heads(aq, HD).T.astype(BF16),
         "w_vT": _v_rows_t(av).astype(BF16),
         "g_mix": p["norm_mix_g"][l][None, :]}
    w["ssm_cw"] = p["ssm_conv_w"][l]
    w["ssm_cb"] = p["ssm_conv_b"][l][None, :]
    w["ssm_dtb"] = _lane_row(p["ssm_dt_bias"][l], 0)
    w["ssm_arow"] = _lane_row(-jnp.exp(p["ssm_a_log"][l].astype(F32)), 0)
    w["ssm_dexp"] = jnp.repeat(p["ssm_d"][l].astype(F32), HD)[None, :]
    w["ssm_ng"] = p["ssm_norm_g"][l][None, :]
    w["ml_cw"] = p["ml_conv_w"][l]
    w["ml_cb"] = p["ml_conv_b"][l][None, :]
    w["ml_igb"] = _lane_row(p["ml_ig_bias"][l], 4)
    w["ml_fgb"] = _lane_row(p["ml_fg_bias"][l], 8)
    w["ml_ng"] = p["ml_norm_g"][l][None, :]
    wuq = p["mla_w_uq"][l].reshape(256, HEADS, MLA_NOPE + MLA_ROPE)
    zeros_nope = jnp.zeros((256, HEADS, MLA_NOPE), F32)
    swapped = jnp.concatenate([zeros_nope, wuq[:, :, MLA_NOPE + half:], wuq[:, :, MLA_NOPE:MLA_NOPE + half]],
                              axis=2)
    padw = ((0, 0), (0, 0), (0, HP - MLA_NOPE - MLA_ROPE))
    wq = jnp.pad(wuq, padw).reshape(256, HEADS * HP)
    wqs = jnp.pad(swapped, padw).reshape(256, HEADS * HP)
    w["mla_wq2T"] = jnp.concatenate([wq.T, wqs.T], axis=0).astype(BF16)
    wukv = p["mla_w_ukv"][l].reshape(128, HEADS, 2 * HD)
    w["mla_wk"] = jnp.pad(wukv[:, :, :MLA_NOPE], ((0, 0), (0, 0), (0, HP - MLA_NOPE))).reshape(
        128, HEADS * HP).astype(BF16)
    w["mla_wvT"] = _v_rows_t(wukv[:, :, MLA_NOPE:].reshape(128, GROUP_W)).astype(BF16)
    w["mla_gq"] = p["mla_q_norm_g"][l][None, :]
    w["mla_gkv"] = p["mla_kv_norm_g"][l][None, :]
    w["w_out"] = p["w_out"][l].astype(BF16)
    w["g_ffn"] = p["norm_ffn_g"][l][None, :]
    wr = jnp.zeros((D_MODEL, 128), F32)
    wr = wr.at[:, :N_EXPERTS].set(p["moe_w_router"][l]).at[:, N_EXPERTS:N_EXPERTS + 4].set(p["moe_w_group"][l])
    w["w_router"] = wr
    w["b_router"] = _lane_row(p["moe_b_router"][l], 0) + _lane_row(p["moe_b_group"][l], N_EXPERTS)
    w["w_gu"] = jnp.concatenate([p["moe_w_gate"][l], p["moe_w_up"][l]], axis=2).astype(BF16)
    w["w_down"] = p["moe_w_down"][l].astype(BF16)
    return w


def kernel(x, positions, norm_mix_g, w_in, ssm_conv_w, ssm_conv_b, ssm_dt_bias, ssm_a_log, ssm_d, ssm_norm_g, ml_conv_w, ml_conv_b, ml_ig_bias, ml_fg_bias, ml_norm_g, mla_q_norm_g, mla_w_uq, mla_kv_norm_g, mla_w_ukv, w_out, norm_ffn_g, moe_w_group, moe_b_group, moe_w_router, moe_b_router, moe_w_gate, moe_w_up, moe_w_down, final_norm_g):
    p = dict(norm_mix_g=norm_mix_g, w_in=w_in, ssm_conv_w=ssm_conv_w, ssm_conv_b=ssm_conv_b,
             ssm_dt_bias=ssm_dt_bias, ssm_a_log=ssm_a_log, ssm_d=ssm_d, ssm_norm_g=ssm_norm_g,
             ml_conv_w=ml_conv_w, ml_conv_b=ml_conv_b, ml_ig_bias=ml_ig_bias, ml_fg_bias=ml_fg_bias,
             ml_norm_g=ml_norm_g, mla_q_norm_g=mla_q_norm_g, mla_w_uq=mla_w_uq,
             mla_kv_norm_g=mla_kv_norm_g, mla_w_ukv=mla_w_ukv, w_out=w_out, norm_ffn_g=norm_ffn_g,
             moe_w_group=moe_w_group, moe_b_group=moe_b_group, moe_w_router=moe_w_router,
             moe_b_router=moe_b_router, moe_w_gate=moe_w_gate, moe_w_up=moe_w_up, moe_w_down=moe_w_down)
    bsz, seq, _ = x.shape
    depth = w_in.shape[0]
    nb = seq // MOBA_BLOCK
    x2 = x.reshape(bsz * seq, D_MODEL)

    half = MLA_ROPE // 2
    inv = ROPE_THETA ** (-jnp.arange(0, MLA_ROPE, 2, dtype=F32) / MLA_ROPE)
    inv_full = _lane_row(jnp.concatenate([inv, inv]), MLA_NOPE)
    sgn_full = _lane_row(jnp.concatenate([-jnp.ones((half,), F32), jnp.ones((half,), F32)]), MLA_NOPE)
    pe_mask = _lane_row(jnp.ones((MLA_ROPE,), F32), MLA_NOPE)
    c_tok, s_tok, c_tr, s_tr = _rope_tables(positions.reshape(bsz * seq, 1), inv_full, sgn_full, bsz, seq)
    tri = jnp.tril(jnp.ones((CHUNK, CHUNK), F32))
    head_of = jnp.arange(GROUP_W) // HD
    seg = (head_of[:, None] == jnp.arange(128)[None, :]).astype(F32)
    bmask = (head_of[:, None] == head_of[None, :]).astype(F32)
    segm = bmask / HD

    for l in range(depth):
        w = _layer_weights(l, p)
        main, qT_a, vT_a = _in_proj(x2, w["g_mix"], w["w_main"], w["w_qT"], w["w_vT"], bsz, seq)
        kmean, k_a = _kmean(main, bsz, seq)
        qTb_a = _moba_select(qT_a, kmean.reshape(bsz, nb, HEADS * HP), bsz, seq)
        y_a = _attention(qTb_a, k_a, vT_a, bsz, seq, "attn_moba")
        y_b = _ssd(main, w["ssm_cw"], w["ssm_cb"], w["ssm_dtb"], w["ssm_arow"], w["ssm_dexp"], w["ssm_ng"],
                   tri, bsz, seq)
        y_c = _mlstm(main, w["ml_cw"], w["ml_cb"], w["ml_igb"], w["ml_fgb"], w["ml_ng"], tri, seg, segm,
                     bmask, bsz, seq)
        qT_d, k_d, vT_d = _mla_prep(main, c_tok, s_tok, c_tr, s_tr, w["mla_gq"], w["mla_wq2T"], w["mla_gkv"],
                                    w["mla_wk"], w["mla_wvT"], pe_mask, bsz, seq)
        y_d = _attention(qT_d, k_d, vT_d, bsz, seq, "attn_mla")
        xn, h2, comb = _outproj(x2, y_a, y_b, y_c, y_d, w["w_out"], w["g_ffn"], w["w_router"], w["b_router"])
        x2 = _moe(h2, comb, xn, w["w_gu"], w["w_down"], final_norm_g[None, :], final_norm=(l == depth - 1))
    return x2.reshape(bsz, seq, D_MODEL)
```

```python
import functools
import math

import numpy as np
import jax
import jax.numpy as jnp
from jax import lax
from jax.experimental import pallas as pl
from jax.experimental.pallas import tpu as pltpu

F32 = jnp.float32
BF16 = jnp.bfloat16
HIGHEST = lax.Precision.HIGHEST

EPS = 1e-6
NEG = -1e30
LOG2E = 1.4426950408889634

D_MODEL = 1024
GROUP_W = 256
HEADS = 4
HD = 64
HP = 128
MOBA_BLOCK = 256
MOBA_TOPK = 3
SSM_STATE = 128
SSM_XBC = 768
CONV_K = 4
CHUNK = 128
MLA_NOPE = 64
MLA_ROPE = 32
ROPE_THETA = 10000.0
N_EXPERTS = 16
EXPERTS_PER_GROUP = 4
D_EXPERT = 256

COL_ZX = 0
COL_ML = 1024
COL_CQ = 2048
COL_CKV = 2304
COL_SMALL = 2432
COL_KA = 2560
N_MAIN = 3072
VR = 80
V_COLS = 384
ATT_UQ = 256
ATT_UK = 256

V7X_VMEM_LIMIT = 56 * 1024 * 1024

NT_DIMS = (((1,), (1,)), ((), ()))


def _cparams(sem):
    return pltpu.CompilerParams(dimension_semantics=sem, vmem_limit_bytes=V7X_VMEM_LIMIT)


def _rms(x, g):
    ms = jnp.mean(x * x, axis=-1, keepdims=True)
    return x * lax.rsqrt(ms + EPS) * g


def _sigmoid(x):
    return 1.0 / (1.0 + jnp.exp(-x))


def _softplus(x):
    return jnp.maximum(x, 0.0) + jnp.log1p(jnp.exp(-jnp.abs(x)))


def _tril(n):
    r = lax.broadcasted_iota(jnp.int32, (n, n), 0)
    c = lax.broadcasted_iota(jnp.int32, (n, n), 1)
    return c <= r


def _ones_rows(shape):
    row = lax.broadcasted_iota(jnp.int32, shape, 0)
    hit = row == HD
    for h in range(1, HEADS):
        hit = jnp.logical_or(hit, row == h * VR + HD)
    return jnp.where(hit, 1.0, 0.0)


def _in_proj_body(x_ref, g_ref, wm_ref, wqv_ref, om_ref, oq_ref, ov_ref):
    h = _rms(x_ref[...], g_ref[...]).astype(BF16)
    n = om_ref.shape[1]
    for c in range(0, n, 512):
        om_ref[:, c:c + 512] = jnp.dot(h, wm_ref[:, c:c + 512], preferred_element_type=F32)
    qv = jnp.dot(h, wqv_ref[...], preferred_element_type=F32)
    nq = HEADS * HP
    oq_ref[0] = qv[:, :nq].T
    vt = qv[:, nq:].T[:HEADS * VR]
    ov_ref[0] = (vt + _ones_rows(vt.shape)).astype(BF16)


def _in_proj(x2, g, w_main, w_qv, bsz, seq, tm=512):
    t = x2.shape[0]
    nst = seq // tm
    return pl.pallas_call(
        _in_proj_body,
        grid=(t // tm,),
        in_specs=[
            pl.BlockSpec((tm, D_MODEL), lambda i: (i, 0)),
            pl.BlockSpec((1, D_MODEL), lambda i: (0, 0)),
            pl.BlockSpec((D_MODEL, N_MAIN), lambda i: (0, 0)),
            pl.BlockSpec((D_MODEL, HEADS * HP + V_COLS), lambda i: (0, 0)),
        ],
        out_specs=[
            pl.BlockSpec((tm, N_MAIN), lambda i: (i, 0)),
            pl.BlockSpec((1, HEADS * HP, tm), lambda i: (i // nst, 0, i % nst)),
            pl.BlockSpec((1, HEADS * VR, tm), lambda i: (i // nst, 0, i % nst)),
        ],
        out_shape=[
            jax.ShapeDtypeStruct((t, N_MAIN), F32),
            jax.ShapeDtypeStruct((bsz, HEADS * HP, seq), F32),
            jax.ShapeDtypeStruct((bsz, HEADS * VR, seq), BF16),
        ],
        compiler_params=_cparams(("arbitrary",)),
        name="in_proj",
    )(x2, g, w_main, w_qv)


def _kmean_body(k_ref, km_ref, kb_ref, *, nb):
    k = k_ref[...]
    km_ref[0] = jnp.mean(k, axis=0, keepdims=True)
    blk = pl.program_id(0) % nb
    lane = lax.broadcasted_iota(jnp.int32, k.shape, 1)
    kb_ref[...] = jnp.where((lane & (HP - 1)) == HD + blk, 1.0, k).astype(BF16)


def _kmean(main, bsz, seq):
    t = main.shape[0]
    nb = seq // MOBA_BLOCK
    assert nb <= HP - HD
    kcol = COL_KA // (HEADS * HP)
    return pl.pallas_call(
        functools.partial(_kmean_body, nb=nb),
        grid=(bsz * nb,),
        in_specs=[pl.BlockSpec((MOBA_BLOCK, HEADS * HP), lambda i: (i, kcol))],
        out_specs=[
            pl.BlockSpec((1, 1, HEADS * HP), lambda i: (i, 0, 0)),
            pl.BlockSpec((MOBA_BLOCK, HEADS * HP), lambda i: (i, 0)),
        ],
        out_shape=[
            jax.ShapeDtypeStruct((bsz * nb, 1, HEADS * HP), F32),
            jax.ShapeDtypeStruct((t, HEADS * HP), BF16),
        ],
        compiler_params=_cparams(("arbitrary",)),
        name="moba_kmean",
    )(main)


def _moba_select_body(qT_ref, km_ref, qb_ref, *, ts, nb, qscale):
    s_idx = pl.program_id(1)
    qT = qT_ref[0]
    km = km_ref[0]
    pos = s_idx * ts + lax.broadcasted_iota(jnp.int32, (nb, ts), 1)
    cur = lax.shift_right_logical(pos, int(math.log2(MOBA_BLOCK)))
    blk = lax.broadcasted_iota(jnp.int32, (nb, ts), 0)
    past = blk < cur
    for h in range(HEADS):
        qh = qT[h * HP:(h + 1) * HP, :]
        g = jnp.dot(km[:, h * HP:(h + 1) * HP], qh, precision=HIGHEST,
                    preferred_element_type=F32)
        g = jnp.where(past, g, -jnp.inf)
        bias = jnp.where(blk == cur, 0.0, NEG)
        for _ in range(MOBA_TOPK):
            m = jnp.max(g, axis=0, keepdims=True)
            cand = jnp.logical_and(g == m, g > -jnp.inf)
            idx = jnp.min(jnp.where(cand, blk, nb), axis=0, keepdims=True)
            pick = blk == idx
            bias = jnp.where(pick, 0.0, bias)
            g = jnp.where(pick, -jnp.inf, g)
        rows = [qh[:HD] * qscale, bias]
        if HP - HD - nb:
            rows.append(jnp.zeros((HP - HD - nb, ts), F32))
        qb_ref[0, h * HP:(h + 1) * HP, :] = jnp.concatenate(rows, axis=0).astype(BF16)


def _moba_select(qT, kmean, bsz, seq, ts=512):
    nb = seq // MOBA_BLOCK
    qscale = (HD ** -0.5) * LOG2E
    return pl.pallas_call(
        functools.partial(_moba_select_body, ts=ts, nb=nb, qscale=qscale),
        grid=(bsz, seq // ts),
        in_specs=[
            pl.BlockSpec((1, HEADS * HP, ts), lambda b, s: (b, 0, s)),
            pl.BlockSpec((1, nb, HEADS * HP), lambda b, s: (b, 0, 0)),
        ],
        out_specs=pl.BlockSpec((1, HEADS * HP, ts), lambda b, s: (b, 0, s)),
        out_shape=jax.ShapeDtypeStruct((bsz, HEADS * HP, seq), BF16),
        compiler_params=_cparams(("arbitrary", "arbitrary")),
        name="moba_select",
    )(qT, kmean)


def _rope_body(pos_ref, inv_ref, sgn_ref, c_ref, s_ref, ct_ref, st_ref):
    ang = pos_ref[...].astype(F32) * inv_ref[...]
    c = jnp.cos(ang)
    s = jnp.sin(ang) * sgn_ref[...]
    c_ref[...] = c
    s_ref[...] = s
    ct_ref[0] = c.T
    st_ref[0] = s.T


def _rope_tables(pos2, inv_full, sgn_full, bsz, seq, ts=512):
    t = pos2.shape[0]
    nst = seq // ts
    tok = pl.BlockSpec((ts, HP), lambda i: (i, 0))
    tr = pl.BlockSpec((1, HP, ts), lambda i: (i // nst, 0, i % nst))
    row = pl.BlockSpec((1, HP), lambda i: (0, 0))
    return pl.pallas_call(
        _rope_body,
        grid=(t // ts,),
        in_specs=[pl.BlockSpec((ts, 1), lambda i: (i, 0)), row, row],
        out_specs=[tok, tok, tr, tr],
        out_shape=[
            jax.ShapeDtypeStruct((t, HP), F32),
            jax.ShapeDtypeStruct((t, HP), F32),
            jax.ShapeDtypeStruct((bsz, HP, seq), F32),
            jax.ShapeDtypeStruct((bsz, HP, seq), F32),
        ],
        compiler_params=_cparams(("arbitrary",)),
        name="rope_tables",
    )(pos2, inv_full, sgn_full)


def _mla_prep_body(cq_ref, ckv_ref, sm_ref, c_ref, s_ref, ct_ref, st_ref, gq_ref, wq_ref, gkv_ref,
                   wk_ref, wv_ref, msk_ref, qo_ref, ko_ref, vo_ref, *, qscale):
    qn = _rms(cq_ref[...], gq_ref[...]).astype(BF16)
    q2 = jnp.dot(qn, wq_ref[...], preferred_element_type=F32).T
    nq = HEADS * HP
    ct4 = jnp.concatenate([ct_ref[0]] * HEADS, axis=0)
    st4 = jnp.concatenate([st_ref[0]] * HEADS, axis=0)
    q = (q2[:nq] * ct4 + q2[nq:] * st4) * qscale
    qo_ref[0] = q.astype(BF16)
    kvn = _rms(ckv_ref[...], gkv_ref[...]).astype(BF16)
    kn = jnp.dot(kvn, wk_ref[...], preferred_element_type=F32)
    sm = sm_ref[...]
    pe = (sm * c_ref[...] + pltpu.roll(sm, HP - MLA_ROPE, 1) * s_ref[...]) * msk_ref[...]
    ko_ref[...] = (kn + jnp.concatenate([pe] * HEADS, axis=1)).astype(BF16)
    vt = jnp.dot(kvn, wv_ref[...], preferred_element_type=F32).T[:HEADS * VR]
    vo_ref[0] = (vt + _ones_rows(vt.shape)).astype(BF16)


def _mla_prep(main, c_tok, s_tok, c_tr, s_tr, gq, wq2T, gkv, wk, wvT, pe_mask, bsz, seq, ts=512):
    t = main.shape[0]
    nst = seq // ts
    qscale = ((MLA_NOPE + MLA_ROPE) ** -0.5) * LOG2E
    tok128 = pl.BlockSpec((ts, HP), lambda i: (i, 0))
    tr128 = pl.BlockSpec((1, HP, ts), lambda i: (i // nst, 0, i % nst))

    def full(shape):
        return pl.BlockSpec(shape, lambda i: tuple(0 for _ in shape))

    return pl.pallas_call(
        functools.partial(_mla_prep_body, qscale=qscale),
        grid=(t // ts,),
        in_specs=[
            pl.BlockSpec((ts, 256), lambda i: (i, COL_CQ // 256)),
            pl.BlockSpec((ts, 128), lambda i: (i, COL_CKV // 128)),
            pl.BlockSpec((ts, 128), lambda i: (i, COL_SMALL // 128)),
            tok128, tok128, tr128, tr128,
            full((1, 256)), full((256, 2 * HEADS * HP)), full((1, 128)),
            full((128, HEADS * HP)), full((128, V_COLS)), full((1, HP)),
        ],
        out_specs=[
            pl.BlockSpec((1, HEADS * HP, ts), lambda i: (i // nst, 0, i % nst)),
            pl.BlockSpec((ts, HEADS * HP), lambda i: (i, 0)),
            pl.BlockSpec((1, HEADS * VR, ts), lambda i: (i // nst, 0, i % nst)),
        ],
        out_shape=[
            jax.ShapeDtypeStruct((bsz, HEADS * HP, seq), BF16),
            jax.ShapeDtypeStruct((t, HEADS * HP), BF16),
            jax.ShapeDtypeStruct((bsz, HEADS * VR, seq), BF16),
        ],
        compiler_params=_cparams(("arbitrary",)),
        name="mla_prep",
    )(main, main, main, c_tok, s_tok, c_tr, s_tr, gq, wq2T, gkv, wk, wvT, pe_mask)


def _attn_body(ii_ref, jj_ref, qT_ref, k_ref, vT_ref, o_ref, m_sc, acc_sc, *, tq, tk):
    p = pl.program_id(1)
    i = ii_ref[p]
    j = jj_ref[p]

    @pl.when(j == 0)
    def _():
        m_sc[...] = jnp.full(m_sc.shape, NEG, F32)
        acc_sc[...] = jnp.zeros(acc_sc.shape, F32)

    def scores(u):
        h, qc, kb, _ = u
        return jnp.dot(k_ref[kb * ATT_UK:(kb + 1) * ATT_UK, h * HP:(h + 1) * HP],
                       qT_ref[0, h * HP:(h + 1) * HP, qc * ATT_UQ:(qc + 1) * ATT_UQ],
                       preferred_element_type=F32)

    def update(u, s):
        h, qc, kb, masked = u
        if masked:
            kpos = kb * ATT_UK + lax.broadcasted_iota(jnp.int32, s.shape, 0)
            qpos = qc * ATT_UQ + lax.broadcasted_iota(jnp.int32, s.shape, 1)
            s = jnp.where(kpos <= qpos, s, NEG)
        alphas, pms = [], []
        for c in range(ATT_UQ // 128):
            ql = slice(qc * ATT_UQ + c * 128, qc * ATT_UQ + (c + 1) * 128)
            sc = s[:, c * 128:(c + 1) * 128]
            m_old = m_sc[h:h + 1, ql]
            m_new = jnp.maximum(m_old, jnp.max(sc, axis=0, keepdims=True))
            alphas.append(jnp.exp2(m_old - m_new))
            pms.append(jnp.exp2(sc - m_new).astype(BF16))
            m_sc[h:h + 1, ql] = m_new
        ql = slice(qc * ATT_UQ, (qc + 1) * ATT_UQ)
        rows = slice(h * VR, (h + 1) * VR)
        acc_sc[rows, ql] = jnp.concatenate(alphas, axis=1) * acc_sc[rows, ql] + jnp.dot(
            vT_ref[0, rows, kb * ATT_UK:(kb + 1) * ATT_UK], jnp.concatenate(pms, axis=1),
            preferred_element_type=F32)

    def run(units):
        ahead = 8
        pending = [scores(u) for u in units[:ahead]]
        for n, u in enumerate(units):
            s = pending.pop(0)
            if n + ahead < len(units):
                pending.append(scores(units[n + ahead]))
            update(u, s)

    def unit_list(diag):
        units = []
        for kb in range(tk // ATT_UK):
            for h in range(HEADS):
                for qc in range(tq // ATT_UQ):
                    k_lo, k_hi = kb * ATT_UK, (kb + 1) * ATT_UK - 1
                    q_lo, q_hi = qc * ATT_UQ, (qc + 1) * ATT_UQ - 1
                    if diag and k_lo > q_hi:
                        continue
                    units.append((h, qc, kb, diag and k_hi > q_lo))
        return units

    @pl.when(j < i)
    def _():
        run(unit_list(False))

    @pl.when(j == i)
    def _():
        run(unit_list(True))
        parts = []
        for h in range(HEADS):
            den = acc_sc[h * VR + HD:h * VR + HD + 1, :]
            parts.append(acc_sc[h * VR:h * VR + HD, :] * (1.0 / den))
        o_ref[...] = jnp.concatenate(parts, axis=0).T


def _attention(qT, k, vT, bsz, seq, name, tq=512):
    t = k.shape[0]
    nq = seq // tq
    ii = np.concatenate([np.full(i + 1, i, np.int32) for i in range(nq)])
    jj = np.concatenate([np.arange(i + 1, dtype=np.int32) for i in range(nq)])
    grid_spec = pltpu.PrefetchScalarGridSpec(
        num_scalar_prefetch=2,
        grid=(bsz, len(ii)),
        in_specs=[
            pl.BlockSpec((1, HEADS * HP, tq), lambda b, p, ii, jj: (b, 0, ii[p])),
            pl.BlockSpec((tq, HEADS * HP), lambda b, p, ii, jj: (b * nq + jj[p], 0)),
            pl.BlockSpec((1, HEADS * VR, tq), lambda b, p, ii, jj: (b, 0, jj[p])),
        ],
        out_specs=pl.BlockSpec((tq, GROUP_W), lambda b, p, ii, jj: (b * nq + ii[p], 0)),
        scratch_shapes=[
            pltpu.VMEM((8, tq), F32),
            pltpu.VMEM((HEADS * VR, tq), F32),
        ],
    )
    return pl.pallas_call(
        functools.partial(_attn_body, tq=tq, tk=tq),
        grid_spec=grid_spec,
        out_shape=jax.ShapeDtypeStruct((t, GROUP_W), F32),
        compiler_params=_cparams(("arbitrary", "arbitrary")),
        name=name,
    )(jnp.asarray(ii), jnp.asarray(jj), qT, k, vT)


def _conv_silu(xpad, src, cw_ref, cb_ref, first, ls):
    width = xpad.shape[1]

    @pl.when(first)
    def _():
        xpad[0:8, :] = jnp.zeros((8, width), F32)

    @pl.when(jnp.logical_not(first))
    def _():
        xpad[0:8, :] = xpad[ls:ls + 8, :]

    xpad[8:ls + 8, :] = src
    acc = cb_ref[...] + cw_ref[0:1, :] * xpad[pl.ds(8 - CONV_K + 1, ls), :]
    for kk in range(1, CONV_K):
        acc = acc + cw_ref[kk:kk + 1, :] * xpad[pl.ds(8 - CONV_K + 1 + kk, ls), :]
    return acc * _sigmoid(acc)


def _ssd_body(zx_ref, sm_ref, cw_ref, cb_ref, dtb_ref, arow_ref, dexp_ref, ng_ref, tri_ref, o_ref,
              xpad, state, *, ls):
    first = pl.program_id(1) == 0

    @pl.when(first)
    def _():
        state[...] = jnp.zeros(state.shape, F32)

    xc = _conv_silu(xpad, zx_ref[:, GROUP_W:GROUP_W + SSM_XBC], cw_ref, cb_ref, first, ls)
    dt_all = _softplus(sm_ref[...] + dtb_ref[...])
    da_all = dt_all * arow_ref[...]
    lane = lax.broadcasted_iota(jnp.int32, (1, 128), 1)
    half = lane < HD
    tril = _tril(CHUNK)
    for c in range(ls // CHUNK):
        r0 = c * CHUNK
        dt_c = dt_all[r0:r0 + CHUNK]
        acum = jnp.dot(tri_ref[...], da_all[r0:r0 + CHUNK], precision=HIGHEST,
                       preferred_element_type=F32)
        acum_t = acum.T
        alast = acum[CHUNK - 1:CHUNK, :]
        for g in range(2):
            h0, h1 = 2 * g, 2 * g + 1
            xg = xc[r0:r0 + CHUNK, g * 128:(g + 1) * 128]
            bg = xc[r0:r0 + CHUNK, GROUP_W + g * 128:GROUP_W + (g + 1) * 128]
            cg = xc[r0:r0 + CHUNK, 2 * GROUP_W + g * 128:2 * GROUP_W + (g + 1) * 128]
            cgb = cg.astype(BF16)
            gram = lax.dot_general(cgb, bg.astype(BF16), NT_DIMS, preferred_element_type=F32)
            a0 = acum[:, h0:h0 + 1]
            a1 = acum[:, h1:h1 + 1]
            xdt = xg * jnp.where(half, dt_c[:, h0:h0 + 1], dt_c[:, h1:h1 + 1])
            ydiag = jnp.zeros((CHUNK, 128), F32)
            for hh, col, msk in ((h0, a0, half), (h1, a1, jnp.logical_not(half))):
                dec = jnp.exp(jnp.where(tril, col - acum_t[hh:hh + 1, :], -jnp.inf))
                ydiag = ydiag + jnp.dot((gram * dec).astype(BF16),
                                        jnp.where(msk, xdt, 0.0).astype(BF16),
                                        preferred_element_type=F32)
            st = state[g]
            yoff = jnp.dot(cgb, st.astype(BF16), preferred_element_type=F32) * jnp.where(
                half, jnp.exp(a0), jnp.exp(a1))
            l0 = alast[:, h0:h0 + 1]
            l1 = alast[:, h1:h1 + 1]
            dout = jnp.where(half, jnp.exp(l0 - a0), jnp.exp(l1 - a1))
            state[g] = (jnp.where(half, jnp.exp(l0), jnp.exp(l1)) * st
                        + jnp.dot(bg.T.astype(BF16), (xdt * dout).astype(BF16), preferred_element_type=F32))
            y = ydiag + yoff + xg * dexp_ref[:, g * 128:(g + 1) * 128]
            zg = zx_ref[r0:r0 + CHUNK, g * 128:(g + 1) * 128]
            y = y * (zg * _sigmoid(zg))
            o_ref[r0:r0 + CHUNK, g * 128:(g + 1) * 128] = _rms(y, ng_ref[:, g * 128:(g + 1) * 128])


def _ssd(main, cw, cb, dtb, arow, dexp, ng, tri, bsz, seq, ls=512):
    t = main.shape[0]
    ns = seq // ls

    def full(shape):
        return pl.BlockSpec(shape, lambda b, s: tuple(0 for _ in shape))

    return pl.pallas_call(
        functools.partial(_ssd_body, ls=ls),
        grid=(bsz, ns),
        in_specs=[
            pl.BlockSpec((ls, 1024), lambda b, s: (b * ns + s, COL_ZX // 1024)),
            pl.BlockSpec((ls, 128), lambda b, s: (b * ns + s, COL_SMALL // 128)),
            full((CONV_K, SSM_XBC)), full((1, SSM_XBC)), full((1, 128)), full((1, 128)),
            full((1, GROUP_W)), full((1, GROUP_W)), full((CHUNK, CHUNK)),
        ],
        out_specs=pl.BlockSpec((ls, GROUP_W), lambda b, s: (b * ns + s, 0)),
        out_shape=jax.ShapeDtypeStruct((t, GROUP_W), F32),
        scratch_shapes=[pltpu.VMEM((ls + 8, SSM_XBC), F32), pltpu.VMEM((2, SSM_STATE, 128), F32)],
        compiler_params=_cparams(("arbitrary", "arbitrary")),
        name="ssd",
    )(main, main, cw, cb, dtb, arow, dexp, ng, tri)


def _mlstm_body(c_ref, sm_ref, cw_ref, cb_ref, igb_ref, fgb_ref, ng_ref, tri_ref, seg_ref, segm_ref,
                bmask_ref, o_ref, xpad, cst, nst, mst, *, ls):
    first = pl.program_id(1) == 0

    @pl.when(first)
    def _():
        cst[...] = jnp.zeros(cst.shape, F32)
        nst[...] = jnp.zeros(nst.shape, F32)
        mst[...] = jnp.zeros(mst.shape, F32)

    qk = _conv_silu(xpad, c_ref[:, 0:2 * GROUP_W], cw_ref, cb_ref, first, ls)
    sm = sm_ref[...]
    ig_all = sm + igb_ref[...]
    fx = sm + fgb_ref[...]
    lf_all = jnp.minimum(fx, 0.0) - jnp.log1p(jnp.exp(-jnp.abs(fx)))
    lane = lax.broadcasted_iota(jnp.int32, (1, GROUP_W), 1)
    tril = _tril(CHUNK)
    for c in range(ls // CHUNK):
        r0 = c * CHUNK
        q = qk[r0:r0 + CHUNK, 0:GROUP_W]
        k = qk[r0:r0 + CHUNK, GROUP_W:2 * GROUP_W] * (HD ** -0.5)
        v = c_ref[r0:r0 + CHUNK, 2 * GROUP_W:3 * GROUP_W]
        o_pre = c_ref[r0:r0 + CHUNK, 3 * GROUP_W:4 * GROUP_W]
        ig = ig_all[r0:r0 + CHUNK]
        bcs = jnp.dot(tri_ref[...], lf_all[r0:r0 + CHUNK], precision=HIGHEST,
                      preferred_element_type=F32)
        bcs_t = bcs.T
        ig_t = ig.T
        kb = k.astype(BF16)
        q_c = jnp.dot(q.astype(BF16), cst[...].astype(BF16), preferred_element_type=F32)
        q_n = jnp.dot(q * nst[0:1, :], seg_ref[...], precision=HIGHEST, preferred_element_type=F32)
        num = jnp.zeros((CHUNK, GROUP_W), F32)
        w_inter_x = jnp.zeros((CHUNK, GROUP_W), F32)
        inv_den_x = jnp.zeros((CHUNK, GROUP_W), F32)
        w_state_x = jnp.zeros((CHUNK, GROUP_W), F32)
        keep_x = jnp.zeros((1, GROUP_W), F32)
        for h in range(HEADS):
            hm = jnp.logical_and(lane >= h * HD, lane < (h + 1) * HD)
            bcol = bcs[:, 8 + h:9 + h]
            igcol = ig[:, 4 + h:5 + h]
            m_prev = mst[h:h + 1, 0:1]
            dmat = jnp.where(tril, bcol - bcs_t[8 + h:9 + h, :] + ig_t[4 + h:5 + h, :], -jnp.inf)
            inter = bcol + m_prev
            m_t = jnp.maximum(inter, jnp.max(dmat, axis=1, keepdims=True))
            s_qk = lax.dot_general(jnp.where(hm, q, 0.0).astype(BF16), kb, NT_DIMS,
                                   preferred_element_type=F32)
            sc = s_qk * jnp.exp(dmat - m_t)
            w_inter = jnp.exp(inter - m_t)
            den = jnp.sum(sc, axis=1, keepdims=True) + w_inter * q_n[:, h:h + 1]
            inv_den = 1.0 / jnp.maximum(jnp.abs(den), jnp.exp(-m_t))
            num = num + jnp.dot(sc.astype(BF16), jnp.where(hm, v, 0.0).astype(BF16),
                                preferred_element_type=F32)
            w_inter_x = w_inter_x + jnp.where(hm, w_inter, 0.0)
            inv_den_x = inv_den_x + jnp.where(hm, inv_den, 0.0)
            blast = bcs[CHUNK - 1:CHUNK, 8 + h:9 + h]
            gcol = blast - bcol + igcol
            m_new = jnp.maximum(blast + m_prev, jnp.max(gcol, axis=0, keepdims=True))
            w_state_x = w_state_x + jnp.where(hm, jnp.exp(gcol - m_new), 0.0)
            keep_x = keep_x + jnp.where(hm, jnp.exp(blast + m_prev - m_new), 0.0)
            mst[h:h + 1, :] = jnp.broadcast_to(m_new, (1, 128))
        hout = (num + w_inter_x * q_c) * inv_den_x
        kw = k * w_state_x
        cst[...] = keep_x * cst[...] + bmask_ref[...] * jnp.dot(
            kw.T.astype(BF16), v.astype(BF16), preferred_element_type=F32)
        nst[0:1, :] = keep_x * nst[0:1, :] + jnp.sum(kw, axis=0, keepdims=True)
        hh = hout * _sigmoid(o_pre)
        ms = jnp.dot(hh * hh, segm_ref[...], precision=HIGHEST, preferred_element_type=F32)
        o_ref[r0:r0 + CHUNK, :] = hh * lax.rsqrt(ms + EPS) * ng_ref[...]


def _mlstm(main, cw, cb, igb, fgb, ng, tri, seg, segm, bmask, bsz, seq, ls=512):
    t = main.shape[0]
    ns = seq // ls

    def full(shape):
        return pl.BlockSpec(shape, lambda b, s: tuple(0 for _ in shape))

    return pl.pallas_call(
        functools.partial(_mlstm_body, ls=ls),
        grid=(bsz, ns),
        in_specs=[
            pl.BlockSpec((ls, 1024), lambda b, s: (b * ns + s, COL_ML // 1024)),
            pl.BlockSpec((ls, 128), lambda b, s: (b * ns + s, COL_SMALL // 128)),
            full((CONV_K, 2 * GROUP_W)), full((1, 2 * GROUP_W)), full((1, 128)), full((1, 128)),
            full((1, GROUP_W)), full((CHUNK, CHUNK)), full((GROUP_W, 128)), full((GROUP_W, GROUP_W)),
            full((GROUP_W, GROUP_W)),
        ],
        out_specs=pl.BlockSpec((ls, GROUP_W), lambda b, s: (b * ns + s, 0)),
        out_shape=jax.ShapeDtypeStruct((t, GROUP_W), F32),
        scratch_shapes=[
            pltpu.VMEM((ls + 8, 2 * GROUP_W), F32),
            pltpu.VMEM((GROUP_W, GROUP_W), F32),
            pltpu.VMEM((8, GROUP_W), F32),
            pltpu.VMEM((8, 128), F32),
        ],
        compiler_params=_cparams(("arbitrary", "arbitrary")),
        name="mlstm",
    )(main, main, cw, cb, igb, fgb, ng, tri, seg, segm, bmask)


def _outproj_body(x_ref, ya_ref, yb_ref, yc_ref, yd_ref, wo_ref, g2_ref, wr_ref, br_ref,
                  xn_ref, h2_ref, cb_ref):
    acc = x_ref[...]
    for gi, y_ref in enumerate((ya_ref, yb_ref, yc_ref, yd_ref)):
        acc = acc + jnp.dot(y_ref[...].astype(BF16), wo_ref[gi * GROUP_W:(gi + 1) * GROUP_W, :],
                            preferred_element_type=F32)
    xn_ref[...] = acc
    h2 = _rms(acc, g2_ref[...])
    h_hi = h2.astype(BF16)
    h2_ref[...] = h_hi
    h_lo = (h2 - h_hi.astype(F32)).astype(BF16)
    hh = jnp.dot(h_hi, wr_ref[...], preferred_element_type=F32)
    lh = jnp.dot(h_lo, wr_ref[:, :128], preferred_element_type=F32)
    logits = hh[:, :128] + hh[:, 128:] + lh + br_ref[...]
    lane = lax.broadcasted_iota(jnp.int32, logits.shape, 1)
    big = jnp.int32(1 << 20)
    n_grp = N_EXPERTS // EXPERTS_PER_GROUP
    gl = jnp.where(jnp.logical_and(lane >= N_EXPERTS, lane < N_EXPERTS + n_grp), logits, -jnp.inf)
    gmax = jnp.max(gl, axis=1, keepdims=True)
    g_w = 1.0 / jnp.sum(jnp.exp(gl - gmax), axis=1, keepdims=True)
    gidx = jnp.min(jnp.where(gl == gmax, lane, big), axis=1, keepdims=True) - N_EXPERTS
    in_grp = jnp.logical_and(lane < N_EXPERTS,
                             lax.shift_right_logical(lane, int(math.log2(EXPERTS_PER_GROUP))) == gidx)
    el = jnp.where(in_grp, logits, -jnp.inf)
    emax = jnp.max(el, axis=1, keepdims=True)
    esum = jnp.sum(jnp.exp(el - emax), axis=1, keepdims=True)
    i1 = jnp.min(jnp.where(el == emax, lane, big), axis=1, keepdims=True)
    el2 = jnp.where(lane == i1, -jnp.inf, el)
    emax2 = jnp.max(el2, axis=1, keepdims=True)
    i2 = jnp.min(jnp.where(el2 == emax2, lane, big), axis=1, keepdims=True)
    p1 = 1.0 / esum
    p2 = jnp.exp(emax2 - emax) / esum
    w1 = p1 / (p1 + p2)
    w2 = p2 / (p1 + p2)
    cb_ref[...] = g_w * (jnp.where(lane == i1, w1, 0.0) + jnp.where(lane == i2, w2, 0.0))


def _outproj(x2, ya, yb, yc, yd, wo, g2, wr, br, tm=512):
    t = x2.shape[0]
    row = pl.BlockSpec((tm, D_MODEL), lambda i: (i, 0))
    grp = pl.BlockSpec((tm, GROUP_W), lambda i: (i, 0))

    def full(shape):
        return pl.BlockSpec(shape, lambda i: tuple(0 for _ in shape))

    return pl.pallas_call(
        _outproj_body,
        grid=(t // tm,),
        in_specs=[row, grp, grp, grp, grp, full((D_MODEL, D_MODEL)), full((1, D_MODEL)),
                  full((D_MODEL, 256)), full((1, 128))],
        out_specs=[row, row, pl.BlockSpec((tm, 128), lambda i: (i, 0))],
        out_shape=[
            jax.ShapeDtypeStruct((t, D_MODEL), F32),
            jax.ShapeDtypeStruct((t, D_MODEL), BF16),
            jax.ShapeDtypeStruct((t, 128), F32),
        ],
        compiler_params=_cparams(("arbitrary",)),
        name="outproj_router",
    )(x2, ya, yb, yc, yd, wo, g2, wr, br)


def _moe_body(h_ref, cb_ref, xn_ref, wgu_ref, wd_ref, gf_ref, o_ref, acc, *, final_norm):
    e = pl.program_id(1)

    @pl.when(e == 0)
    def _():
        acc[...] = jnp.zeros(acc.shape, F32)

    gu = jnp.dot(h_ref[...], wgu_ref[0], preferred_element_type=F32)
    g = gu[:, :D_EXPERT]
    u = gu[:, D_EXPERT:]
    lane = lax.broadcasted_iota(jnp.int32, cb_ref.shape, 1)
    col = jnp.sum(jnp.where(lane == e, cb_ref[...], 0.0), axis=1, keepdims=True)
    hid = (g * _sigmoid(g)) * u * col
    acc[...] += jnp.dot(hid.astype(BF16), wd_ref[0], preferred_element_type=F32)

    @pl.when(e == N_EXPERTS - 1)
    def _():
        y = xn_ref[...] + acc[...]
        if final_norm:
            y = _rms(y, gf_ref[...])
        o_ref[...] = y


def _moe(h2, comb, xn, wgu, wd, gf, final_norm, tm=1024):
    t = h2.shape[0]
    return pl.pallas_call(
        functools.partial(_moe_body, final_norm=final_norm),
        grid=(t // tm, N_EXPERTS),
        in_specs=[
            pl.BlockSpec((tm, D_MODEL), lambda i, e: (i, 0)),
            pl.BlockSpec((tm, 128), lambda i, e: (i, 0)),
            pl.BlockSpec((tm, D_MODEL), lambda i, e: (i, 0)),
            pl.BlockSpec((1, D_MODEL, 2 * D_EXPERT), lambda i, e: (e, 0, 0)),
            pl.BlockSpec((1, D_EXPERT, D_MODEL), lambda i, e: (e, 0, 0)),
            pl.BlockSpec((1, D_MODEL), lambda i, e: (0, 0)),
        ],
        out_specs=pl.BlockSpec((tm, D_MODEL), lambda i, e: (i, 0)),
        out_shape=jax.ShapeDtypeStruct((t, D_MODEL), F32),
        scratch_shapes=[pltpu.VMEM((tm, D_MODEL), F32)],
        compiler_params=_cparams(("arbitrary", "arbitrary")),
        name="moe",
    )(h2, comb, xn, wgu, wd, gf)


def _spread_heads(w, width, slab, total):
    rows = w.shape[0]
    pieces = []
    for h in range(HEADS):
        pieces.append(w[:, h * width:(h + 1) * width])
        pieces.append(jnp.zeros((rows, slab - width), w.dtype))
    if total > HEADS * slab:
        pieces.append(jnp.zeros((rows, total - HEADS * slab), w.dtype))
    return jnp.concatenate(pieces, axis=1)


def _pad_heads(w, width):
    return _spread_heads(w, width, HP, HEADS * HP)


def _v_cols(w):
    return _spread_heads(w, HD, VR, V_COLS)


def _lane_row(vals, offset, width=128):
    return jnp.zeros((1, width), F32).at[0, offset:offset + vals.shape[0]].set(vals.astype(F32))


def _split_w_in(w):
    sizes = (3 * GROUP_W, GROUP_W, SSM_XBC, HEADS, 2 * GROUP_W, GROUP_W, HEADS, HEADS, GROUP_W,
             256, 128, MLA_ROPE)
    offs = np.cumsum((0,) + sizes)
    return [w[:, int(offs[i]):int(offs[i + 1])] for i in range(len(sizes))]


def _layer_weights(l, p):
    a_qkv, s_z, s_xbc, s_dt, m_qk, m_v, m_i, m_f, m_o, d_cq, d_ckv, d_kr = _split_w_in(p["w_in"][l])
    aq, ak, av = a_qkv[:, :GROUP_W], a_qkv[:, GROUP_W:2 * GROUP_W], a_qkv[:, 2 * GROUP_W:]
    half = MLA_ROPE // 2
    small = jnp.zeros((D_MODEL, 128), F32)
    small = small.at[:, 0:4].set(s_dt).at[:, 4:8].set(m_i).at[:, 8:12].set(m_f)
    small = small.at[:, 64:96].set(d_kr)
    small = small.at[:, 96:128].set(jnp.concatenate([d_kr[:, half:], d_kr[:, :half]], axis=1))
    w_main = jnp.concatenate([s_z, s_xbc, m_qk, m_v, m_o, d_cq, d_ckv, small, _pad_heads(ak, HD)],
                             axis=1).astype(BF16)
    w = {"w_main": w_main,
         "w_qv": jnp.concatenate([_pad_heads(aq, HD), _v_cols(av)], axis=1).astype(BF16),
         "g_mix": p["norm_mix_g"][l][None, :]}
    w["ssm_cw"] = p["ssm_conv_w"][l]
    w["ssm_cb"] = p["ssm_conv_b"][l][None, :]
    w["ssm_dtb"] = _lane_row(p["ssm_dt_bias"][l], 0)
    w["ssm_arow"] = _lane_row(-jnp.exp(p["ssm_a_log"][l].astype(F32)), 0)
    w["ssm_dexp"] = jnp.repeat(p["ssm_d"][l].astype(F32), HD)[None, :]
    w["ssm_ng"] = p["ssm_norm_g"][l][None, :]
    w["ml_cw"] = p["ml_conv_w"][l]
    w["ml_cb"] = p["ml_conv_b"][l][None, :]
    w["ml_igb"] = _lane_row(p["ml_ig_bias"][l], 4)
    w["ml_fgb"] = _lane_row(p["ml_fg_bias"][l], 8)
    w["ml_ng"] = p["ml_norm_g"][l][None, :]
    wuq = p["mla_w_uq"][l]
    qd = MLA_NOPE + MLA_ROPE
    swapped = []
    for h in range(HEADS):
        c0 = h * qd + MLA_NOPE
        swapped += [jnp.zeros((256, MLA_NOPE), F32), wuq[:, c0 + half:c0 + 2 * half], wuq[:, c0:c0 + half]]
    wq = _spread_heads(wuq, qd, HP, HEADS * HP)
    wqs = _spread_heads(jnp.concatenate(swapped, axis=1), qd, HP, HEADS * HP)
    w["mla_wq2"] = jnp.concatenate([wq, wqs], axis=1).astype(BF16)
    wukv = p["mla_w_ukv"][l]
    wk_cols = jnp.concatenate([wukv[:, h * 2 * HD:h * 2 * HD + MLA_NOPE] for h in range(HEADS)], axis=1)
    wv_cols = jnp.concatenate([wukv[:, h * 2 * HD + MLA_NOPE:(h + 1) * 2 * HD] for h in range(HEADS)], axis=1)
    w["mla_wk"] = _pad_heads(wk_cols, MLA_NOPE).astype(BF16)
    w["mla_wv"] = _v_cols(wv_cols).astype(BF16)
    w["mla_gq"] = p["mla_q_norm_g"][l][None, :]
    w["mla_gkv"] = p["mla_kv_norm_g"][l][None, :]
    w["w_out"] = p["w_out"][l].astype(BF16)
    w["g_ffn"] = p["norm_ffn_g"][l][None, :]
    wr = jnp.zeros((D_MODEL, 128), F32)
    wr = wr.at[:, :N_EXPERTS].set(p["moe_w_router"][l]).at[:, N_EXPERTS:N_EXPERTS + 4].set(p["moe_w_group"][l])
    wr_hi = wr.astype(BF16)
    w["w_router"] = jnp.concatenate([wr_hi, (wr - wr_hi.astype(F32)).astype(BF16)], axis=1)
    w["b_router"] = _lane_row(p["moe_b_router"][l], 0) + _lane_row(p["moe_b_group"][l], N_EXPERTS)
    w["w_gu"] = jnp.concatenate([p["moe_w_gate"][l], p["moe_w_up"][l]], axis=2).astype(BF16)
    w["w_down"] = p["moe_w_down"][l].astype(BF16)
    return w


def kernel(x, positions, norm_mix_g, w_in, ssm_conv_w, ssm_conv_b, ssm_dt_bias, ssm_a_log, ssm_d, ssm_norm_g, ml_conv_w, ml_conv_b, ml_ig_bias, ml_fg_bias, ml_norm_g, mla_q_norm_g, mla_w_uq, mla_kv_norm_g, mla_w_ukv, w_out, norm_ffn_g, moe_w_group, moe_b_group, moe_w_router, moe_b_router, moe_w_gate, moe_w_up, moe_w_down, final_norm_g):
    p = dict(norm_mix_g=norm_mix_g, w_in=w_in, ssm_conv_w=ssm_conv_w, ssm_conv_b=ssm_conv_b,
             ssm_dt_bias=ssm_dt_bias, ssm_a_log=ssm_a_log, ssm_d=ssm_d, ssm_norm_g=ssm_norm_g,
             ml_conv_w=ml_conv_w, ml_conv_b=ml_conv_b, ml_ig_bias=ml_ig_bias, ml_fg_bias=ml_fg_bias,
             ml_norm_g=ml_norm_g, mla_q_norm_g=mla_q_norm_g, mla_w_uq=mla_w_uq,
             mla_kv_norm_g=mla_kv_norm_g, mla_w_ukv=mla_w_ukv, w_out=w_out, norm_ffn_g=norm_ffn_g,
             moe_w_group=moe_w_group, moe_b_group=moe_b_group, moe_w_router=moe_w_router,
             moe_b_router=moe_b_router, moe_w_gate=moe_w_gate, moe_w_up=moe_w_up, moe_w_down=moe_w_down)
    bsz, seq, _ = x.shape
    depth = w_in.shape[0]
    nb = seq // MOBA_BLOCK
    x2 = x.reshape(bsz * seq, D_MODEL)

    half = MLA_ROPE // 2
    inv = ROPE_THETA ** (-jnp.arange(0, MLA_ROPE, 2, dtype=F32) / MLA_ROPE)
    inv_full = _lane_row(jnp.concatenate([inv, inv]), MLA_NOPE)
    sgn_full = _lane_row(jnp.concatenate([-jnp.ones((half,), F32), jnp.ones((half,), F32)]), MLA_NOPE)
    pe_mask = _lane_row(jnp.ones((MLA_ROPE,), F32), MLA_NOPE)
    c_tok, s_tok, c_tr, s_tr = _rope_tables(positions.reshape(bsz * seq, 1), inv_full, sgn_full, bsz, seq)
    tri = jnp.tril(jnp.ones((CHUNK, CHUNK), F32))
    head_of = jnp.arange(GROUP_W) // HD
    seg = (head_of[:, None] == jnp.arange(128)[None, :]).astype(F32)
    bmask = (head_of[:, None] == head_of[None, :]).astype(F32)
    segm = bmask / HD

    for l in range(depth):
        w = _layer_weights(l, p)
        main, qT_a, vT_a = _in_proj(x2, w["g_mix"], w["w_main"], w["w_qv"], bsz, seq)
        kmean, k_a = _kmean(main, bsz, seq)
        qTb_a = _moba_select(qT_a, kmean.reshape(bsz, nb, HEADS * HP), bsz, seq)
        y_a = _attention(qTb_a, k_a, vT_a, bsz, seq, "attn_moba")
        y_b = _ssd(main, w["ssm_cw"], w["ssm_cb"], w["ssm_dtb"], w["ssm_arow"], w["ssm_dexp"], w["ssm_ng"],
                   tri, bsz, seq)
        y_c = _mlstm(main, w["ml_cw"], w["ml_cb"], w["ml_igb"], w["ml_fgb"], w["ml_ng"], tri, seg, segm,
                     bmask, bsz, seq)
        qT_d, k_d, vT_d = _mla_prep(main, c_tok, s_tok, c_tr, s_tr, w["mla_gq"], w["mla_wq2"], w["mla_gkv"],
                                    w["mla_wk"], w["mla_wv"], pe_mask, bsz, seq)
        y_d = _attention(qT_d, k_d, vT_d, bsz, seq, "attn_mla")
        xn, h2, comb = _outproj(x2, y_a, y_b, y_c, y_d, w["w_out"], w["g_ffn"], w["w_router"], w["b_router"])
        x2 = _moe(h2, comb, xn, w["w_gu"], w["w_down"], final_norm_g[None, :], final_norm=(l == depth - 1))
    return x2.reshape(bsz, seq, D_MODEL)
```

```python
import functools
import math

import numpy as np
import jax
import jax.numpy as jnp
from jax import lax
from jax.experimental import pallas as pl
from jax.experimental.pallas import tpu as pltpu

F32 = jnp.float32
BF16 = jnp.bfloat16
HIGHEST = lax.Precision.HIGHEST

EPS = 1e-6
NEG = -1e30
LOG2E = 1.4426950408889634

D_MODEL = 1024
GROUP_W = 256
HEADS = 4
HD = 64
HP = 128
MOBA_BLOCK = 256
MOBA_TOPK = 3
SSM_STATE = 128
SSM_XBC = 768
CONV_K = 4
CHUNK = 128
MLA_NOPE = 64
MLA_ROPE = 32
ROPE_THETA = 10000.0
N_EXPERTS = 16
EXPERTS_PER_GROUP = 4
D_EXPERT = 256

COL_ZX = 0
COL_ML = 1024
COL_CQ = 2048
COL_CKV = 2304
COL_SMALL = 2432
COL_KA = 2560
N_MAIN = 3072
VR = 80
V_COLS = 384
ATT_UQ = 256
ATT_UK = 256

V7X_VMEM_LIMIT = 56 * 1024 * 1024

NT_DIMS = (((1,), (1,)), ((), ()))


def _cparams(sem):
    return pltpu.CompilerParams(dimension_semantics=sem, vmem_limit_bytes=V7X_VMEM_LIMIT)


def _rms(x, g):
    ms = jnp.mean(x * x, axis=-1, keepdims=True)
    return x * lax.rsqrt(ms + EPS) * g


def _sigmoid(x):
    return 0.5 * jnp.tanh(0.5 * x) + 0.5


def _softplus(x):
    return jnp.maximum(x, 0.0) + jnp.log1p(jnp.exp(-jnp.abs(x)))


def _tril(n):
    r = lax.broadcasted_iota(jnp.int32, (n, n), 0)
    c = lax.broadcasted_iota(jnp.int32, (n, n), 1)
    return c <= r


def _cast_body(w_ref, o_ref):
    o_ref[...] = w_ref[...].astype(BF16)


def _cast_bf16(w, rows=256):
    nl, r, c = w.shape
    return pl.pallas_call(
        _cast_body,
        grid=(nl, r // rows),
        in_specs=[pl.BlockSpec((1, rows, c), lambda l, i: (l, i, 0))],
        out_specs=pl.BlockSpec((1, rows, c), lambda l, i: (l, i, 0)),
        out_shape=jax.ShapeDtypeStruct(w.shape, BF16),
        compiler_params=_cparams(("arbitrary", "arbitrary")),
        name="cast_w_in",
    )(w)


def _ones_rows(shape):
    row = lax.broadcasted_iota(jnp.int32, shape, 0)
    hit = row == HD
    for h in range(1, HEADS):
        hit = jnp.logical_or(hit, row == h * VR + HD)
    return jnp.where(hit, 1.0, 0.0)


def _in_proj_body(x_ref, g_ref, wm_ref, wqv_ref, om_ref, oq_ref, ov_ref):
    h = _rms(x_ref[...], g_ref[...]).astype(BF16)
    n = om_ref.shape[1]
    for c in range(0, n, 512):
        om_ref[:, c:c + 512] = jnp.dot(h, wm_ref[:, c:c + 512], preferred_element_type=F32)
    qv = jnp.dot(h, wqv_ref[...], preferred_element_type=F32)
    nq = HEADS * HP
    oq_ref[0] = qv[:, :nq].T
    vt = qv[:, nq:].T[:HEADS * VR]
    ov_ref[0] = (vt + _ones_rows(vt.shape)).astype(BF16)


def _in_proj(x2, g, w_main, w_qv, bsz, seq, tm=512):
    t = x2.shape[0]
    nst = seq // tm
    return pl.pallas_call(
        _in_proj_body,
        grid=(t // tm,),
        in_specs=[
            pl.BlockSpec((tm, D_MODEL), lambda i: (i, 0)),
            pl.BlockSpec((1, D_MODEL), lambda i: (0, 0)),
            pl.BlockSpec((D_MODEL, N_MAIN), lambda i: (0, 0)),
            pl.BlockSpec((D_MODEL, HEADS * HP + V_COLS), lambda i: (0, 0)),
        ],
        out_specs=[
            pl.BlockSpec((tm, N_MAIN), lambda i: (i, 0)),
            pl.BlockSpec((1, HEADS * HP, tm), lambda i: (i // nst, 0, i % nst)),
            pl.BlockSpec((1, HEADS * VR, tm), lambda i: (i // nst, 0, i % nst)),
        ],
        out_shape=[
            jax.ShapeDtypeStruct((t, N_MAIN), F32),
            jax.ShapeDtypeStruct((bsz, HEADS * HP, seq), F32),
            jax.ShapeDtypeStruct((bsz, HEADS * VR, seq), BF16),
        ],
        compiler_params=_cparams(("arbitrary",)),
        name="in_proj",
    )(x2, g, w_main, w_qv)


def _kmean_body(k_ref, km_ref, kb_ref, *, nb):
    k = k_ref[...]
    km_ref[0] = jnp.mean(k, axis=0, keepdims=True)
    blk = pl.program_id(0) % nb
    lane = lax.broadcasted_iota(jnp.int32, k.shape, 1)
    kb_ref[...] = jnp.where((lane & (HP - 1)) == HD + blk, 1.0, k).astype(BF16)


def _kmean(main, bsz, seq):
    t = main.shape[0]
    nb = seq // MOBA_BLOCK
    assert nb <= HP - HD
    kcol = COL_KA // (HEADS * HP)
    return pl.pallas_call(
        functools.partial(_kmean_body, nb=nb),
        grid=(bsz * nb,),
        in_specs=[pl.BlockSpec((MOBA_BLOCK, HEADS * HP), lambda i: (i, kcol))],
        out_specs=[
            pl.BlockSpec((1, 1, HEADS * HP), lambda i: (i, 0, 0)),
            pl.BlockSpec((MOBA_BLOCK, HEADS * HP), lambda i: (i, 0)),
        ],
        out_shape=[
            jax.ShapeDtypeStruct((bsz * nb, 1, HEADS * HP), F32),
            jax.ShapeDtypeStruct((t, HEADS * HP), BF16),
        ],
        compiler_params=_cparams(("arbitrary",)),
        name="moba_kmean",
    )(main)


def _moba_select_body(qT_ref, km_ref, qb_ref, *, ts, nb, qscale):
    s_idx = pl.program_id(1)
    qT = qT_ref[0]
    km = km_ref[0]
    pos = s_idx * ts + lax.broadcasted_iota(jnp.int32, (nb, ts), 1)
    cur = lax.shift_right_logical(pos, int(math.log2(MOBA_BLOCK)))
    blk = lax.broadcasted_iota(jnp.int32, (nb, ts), 0)
    past = blk < cur
    for h in range(HEADS):
        qh = qT[h * HP:(h + 1) * HP, :]
        g = jnp.dot(km[:, h * HP:(h + 1) * HP], qh, precision=HIGHEST,
                    preferred_element_type=F32)
        g = jnp.where(past, g, -jnp.inf)
        bias = jnp.where(blk == cur, 0.0, NEG)
        for _ in range(MOBA_TOPK):
            m = jnp.max(g, axis=0, keepdims=True)
            cand = jnp.logical_and(g == m, g > -jnp.inf)
            idx = jnp.min(jnp.where(cand, blk, nb), axis=0, keepdims=True)
            pick = blk == idx
            bias = jnp.where(pick, 0.0, bias)
            g = jnp.where(pick, -jnp.inf, g)
        rows = [qh[:HD] * qscale, bias]
        if HP - HD - nb:
            rows.append(jnp.zeros((HP - HD - nb, ts), F32))
        qb_ref[0, h * HP:(h + 1) * HP, :] = jnp.concatenate(rows, axis=0).astype(BF16)


def _moba_select(qT, kmean, bsz, seq, ts=512):
    nb = seq // MOBA_BLOCK
    qscale = (HD ** -0.5) * LOG2E
    return pl.pallas_call(
        functools.partial(_moba_select_body, ts=ts, nb=nb, qscale=qscale),
        grid=(bsz, seq // ts),
        in_specs=[
            pl.BlockSpec((1, HEADS * HP, ts), lambda b, s: (b, 0, s)),
            pl.BlockSpec((1, nb, HEADS * HP), lambda b, s: (b, 0, 0)),
        ],
        out_specs=pl.BlockSpec((1, HEADS * HP, ts), lambda b, s: (b, 0, s)),
        out_shape=jax.ShapeDtypeStruct((bsz, HEADS * HP, seq), BF16),
        compiler_params=_cparams(("arbitrary", "arbitrary")),
        name="moba_select",
    )(qT, kmean)


def _rope_body(pos_ref, inv_ref, sgn_ref, c_ref, s_ref, ct_ref, st_ref):
    ang = pos_ref[...].astype(F32) * inv_ref[...]
    c = jnp.cos(ang)
    s = jnp.sin(ang) * sgn_ref[...]
    c_ref[...] = c
    s_ref[...] = s
    ct_ref[0] = c.T
    st_ref[0] = s.T


def _rope_tables(pos2, inv_full, sgn_full, bsz, seq, ts=512):
    t = pos2.shape[0]
    nst = seq // ts
    tok = pl.BlockSpec((ts, HP), lambda i: (i, 0))
    tr = pl.BlockSpec((1, HP, ts), lambda i: (i // nst, 0, i % nst))
    row = pl.BlockSpec((1, HP), lambda i: (0, 0))
    return pl.pallas_call(
        _rope_body,
        grid=(t // ts,),
        in_specs=[pl.BlockSpec((ts, 1), lambda i: (i, 0)), row, row],
        out_specs=[tok, tok, tr, tr],
        out_shape=[
            jax.ShapeDtypeStruct((t, HP), F32),
            jax.ShapeDtypeStruct((t, HP), F32),
            jax.ShapeDtypeStruct((bsz, HP, seq), F32),
            jax.ShapeDtypeStruct((bsz, HP, seq), F32),
        ],
        compiler_params=_cparams(("arbitrary",)),
        name="rope_tables",
    )(pos2, inv_full, sgn_full)


def _mla_prep_body(cq_ref, ckv_ref, sm_ref, c_ref, s_ref, ct_ref, st_ref, gq_ref, wq_ref, gkv_ref,
                   wk_ref, wv_ref, msk_ref, qo_ref, ko_ref, vo_ref, *, qscale):
    qn = _rms(cq_ref[...], gq_ref[...]).astype(BF16)
    q2 = jnp.dot(qn, wq_ref[...], preferred_element_type=F32).T
    nq = HEADS * HP
    ct4 = jnp.concatenate([ct_ref[0]] * HEADS, axis=0)
    st4 = jnp.concatenate([st_ref[0]] * HEADS, axis=0)
    q = (q2[:nq] * ct4 + q2[nq:] * st4) * qscale
    qo_ref[0] = q.astype(BF16)
    kvn = _rms(ckv_ref[...], gkv_ref[...]).astype(BF16)
    kn = jnp.dot(kvn, wk_ref[...], preferred_element_type=F32)
    sm = sm_ref[...]
    pe = (sm * c_ref[...] + pltpu.roll(sm, HP - MLA_ROPE, 1) * s_ref[...]) * msk_ref[...]
    ko_ref[...] = (kn + jnp.concatenate([pe] * HEADS, axis=1)).astype(BF16)
    vt = jnp.dot(kvn, wv_ref[...], preferred_element_type=F32).T[:HEADS * VR]
    vo_ref[0] = (vt + _ones_rows(vt.shape)).astype(BF16)


def _mla_prep(main, c_tok, s_tok, c_tr, s_tr, gq, wq2T, gkv, wk, wvT, pe_mask, bsz, seq, ts=512):
    t = main.shape[0]
    nst = seq // ts
    qscale = ((MLA_NOPE + MLA_ROPE) ** -0.5) * LOG2E
    tok128 = pl.BlockSpec((ts, HP), lambda i: (i, 0))
    tr128 = pl.BlockSpec((1, HP, ts), lambda i: (i // nst, 0, i % nst))

    def full(shape):
        return pl.BlockSpec(shape, lambda i: tuple(0 for _ in shape))

    return pl.pallas_call(
        functools.partial(_mla_prep_body, qscale=qscale),
        grid=(t // ts,),
        in_specs=[
            pl.BlockSpec((ts, 256), lambda i: (i, COL_CQ // 256)),
            pl.BlockSpec((ts, 128), lambda i: (i, COL_CKV // 128)),
            pl.BlockSpec((ts, 128), lambda i: (i, COL_SMALL // 128)),
            tok128, tok128, tr128, tr128,
            full((1, 256)), full((256, 2 * HEADS * HP)), full((1, 128)),
            full((128, HEADS * HP)), full((128, V_COLS)), full((1, HP)),
        ],
        out_specs=[
            pl.BlockSpec((1, HEADS * HP, ts), lambda i: (i // nst, 0, i % nst)),
            pl.BlockSpec((ts, HEADS * HP), lambda i: (i, 0)),
            pl.BlockSpec((1, HEADS * VR, ts), lambda i: (i // nst, 0, i % nst)),
        ],
        out_shape=[
            jax.ShapeDtypeStruct((bsz, HEADS * HP, seq), BF16),
            jax.ShapeDtypeStruct((t, HEADS * HP), BF16),
            jax.ShapeDtypeStruct((bsz, HEADS * VR, seq), BF16),
        ],
        compiler_params=_cparams(("arbitrary",)),
        name="mla_prep",
    )(main, main, main, c_tok, s_tok, c_tr, s_tr, gq, wq2T, gkv, wk, wvT, pe_mask)


def _attn_body(ii_ref, jj_ref, qT_ref, k_ref, vT_ref, o_ref, m_sc, acc_sc, *, tq, tk):
    p = pl.program_id(1)
    i = ii_ref[p]
    j = jj_ref[p]

    @pl.when(j == 0)
    def _():
        m_sc[...] = jnp.full(m_sc.shape, NEG, F32)
        acc_sc[...] = jnp.zeros(acc_sc.shape, F32)

    def scores(u):
        h, qc, kb, _ = u
        return jnp.dot(k_ref[kb * ATT_UK:(kb + 1) * ATT_UK, h * HP:(h + 1) * HP],
                       qT_ref[0, h * HP:(h + 1) * HP, qc * ATT_UQ:(qc + 1) * ATT_UQ],
                       preferred_element_type=F32)

    def update(u, s):
        h, qc, kb, masked = u
        if masked:
            kpos = kb * ATT_UK + lax.broadcasted_iota(jnp.int32, s.shape, 0)
            qpos = qc * ATT_UQ + lax.broadcasted_iota(jnp.int32, s.shape, 1)
            s = jnp.where(kpos <= qpos, s, NEG)
        alphas, pms = [], []
        for c in range(ATT_UQ // 128):
            ql = slice(qc * ATT_UQ + c * 128, qc * ATT_UQ + (c + 1) * 128)
            sc = s[:, c * 128:(c + 1) * 128]
            m_old = m_sc[h:h + 1, ql]
            m_new = jnp.maximum(m_old, jnp.max(sc, axis=0, keepdims=True))
            alphas.append(jnp.exp2(m_old - m_new))
            pms.append(jnp.exp2(sc - m_new).astype(BF16))
            m_sc[h:h + 1, ql] = m_new
        ql = slice(qc * ATT_UQ, (qc + 1) * ATT_UQ)
        rows = slice(h * VR, (h + 1) * VR)
        acc_sc[rows, ql] = jnp.concatenate(alphas, axis=1) * acc_sc[rows, ql] + jnp.dot(
            vT_ref[0, rows, kb * ATT_UK:(kb + 1) * ATT_UK], jnp.concatenate(pms, axis=1),
            preferred_element_type=F32)

    def run(units):
        ahead = 8
        pending = [scores(u) for u in units[:ahead]]
        for n, u in enumerate(units):
            s = pending.pop(0)
            if n + ahead < len(units):
                pending.append(scores(units[n + ahead]))
            update(u, s)

    def unit_list(diag):
        units = []
        for kb in range(tk // ATT_UK):
            for h in range(HEADS):
                for qc in range(tq // ATT_UQ):
                    k_lo, k_hi = kb * ATT_UK, (kb + 1) * ATT_UK - 1
                    q_lo, q_hi = qc * ATT_UQ, (qc + 1) * ATT_UQ - 1
                    if diag and k_lo > q_hi:
                        continue
                    units.append((h, qc, kb, diag and k_hi > q_lo))
        return units

    @pl.when(j < i)
    def _():
        run(unit_list(False))

    @pl.when(j == i)
    def _():
        run(unit_list(True))
        parts = []
        for h in range(HEADS):
            den = acc_sc[h * VR + HD:h * VR + HD + 1, :]
            parts.append(acc_sc[h * VR:h * VR + HD, :] * (1.0 / den))
        o_ref[...] = jnp.concatenate(parts, axis=0).T


def _attention(qT, k, vT, bsz, seq, name, tq=512):
    t = k.shape[0]
    nq = seq // tq
    ii = np.concatenate([np.full(i + 1, i, np.int32) for i in range(nq)])
    jj = np.concatenate([np.arange(i + 1, dtype=np.int32) for i in range(nq)])
    grid_spec = pltpu.PrefetchScalarGridSpec(
        num_scalar_prefetch=2,
        grid=(bsz, len(ii)),
        in_specs=[
            pl.BlockSpec((1, HEADS * HP, tq), lambda b, p, ii, jj: (b, 0, ii[p])),
            pl.BlockSpec((tq, HEADS * HP), lambda b, p, ii, jj: (b * nq + jj[p], 0)),
            pl.BlockSpec((1, HEADS * VR, tq), lambda b, p, ii, jj: (b, 0, jj[p])),
        ],
        out_specs=pl.BlockSpec((tq, GROUP_W), lambda b, p, ii, jj: (b * nq + ii[p], 0)),
        scratch_shapes=[
            pltpu.VMEM((8, tq), F32),
            pltpu.VMEM((HEADS * VR, tq), F32),
        ],
    )
    return pl.pallas_call(
        functools.partial(_attn_body, tq=tq, tk=tq),
        grid_spec=grid_spec,
        out_shape=jax.ShapeDtypeStruct((t, GROUP_W), F32),
        compiler_params=_cparams(("arbitrary", "arbitrary")),
        name=name,
    )(jnp.asarray(ii), jnp.asarray(jj), qT, k, vT)


def _conv_silu(xpad, src, cw_ref, cb_ref, first, ls):
    width = xpad.shape[1]

    @pl.when(first)
    def _():
        xpad[0:8, :] = jnp.zeros((8, width), F32)

    @pl.when(jnp.logical_not(first))
    def _():
        xpad[0:8, :] = xpad[ls:ls + 8, :]

    xpad[8:ls + 8, :] = src
    acc = cb_ref[...] + cw_ref[0:1, :] * xpad[pl.ds(8 - CONV_K + 1, ls), :]
    for kk in range(1, CONV_K):
        acc = acc + cw_ref[kk:kk + 1, :] * xpad[pl.ds(8 - CONV_K + 1 + kk, ls), :]
    return acc * _sigmoid(acc)


def _ssd_body(zx_ref, sm_ref, cw_ref, cb_ref, dtb_ref, arow_ref, dexp_ref, ng_ref, tri_ref, o_ref,
              xpad, state, *, ls):
    first = pl.program_id(1) == 0

    @pl.when(first)
    def _():
        state[...] = jnp.zeros(state.shape, F32)

    xc = _conv_silu(xpad, zx_ref[:, GROUP_W:GROUP_W + SSM_XBC], cw_ref, cb_ref, first, ls)
    dt_all = _softplus(sm_ref[...] + dtb_ref[...])
    da_all = dt_all * arow_ref[...]
    lane = lax.broadcasted_iota(jnp.int32, (1, 128), 1)
    half = lane < HD
    tril = _tril(CHUNK)
    for c in range(ls // CHUNK):
        r0 = c * CHUNK
        dt_c = dt_all[r0:r0 + CHUNK]
        acum = jnp.dot(tri_ref[...], da_all[r0:r0 + CHUNK], precision=HIGHEST,
                       preferred_element_type=F32)
        acum_t = acum.T
        alast = acum[CHUNK - 1:CHUNK, :]
        for g in range(2):
            h0, h1 = 2 * g, 2 * g + 1
            xg = xc[r0:r0 + CHUNK, g * 128:(g + 1) * 128]
            bg = xc[r0:r0 + CHUNK, GROUP_W + g * 128:GROUP_W + (g + 1) * 128]
            cg = xc[r0:r0 + CHUNK, 2 * GROUP_W + g * 128:2 * GROUP_W + (g + 1) * 128]
            cgb = cg.astype(BF16)
            gram = lax.dot_general(cgb, bg.astype(BF16), NT_DIMS, preferred_element_type=F32)
            a0 = acum[:, h0:h0 + 1]
            a1 = acum[:, h1:h1 + 1]
            xdt = xg * jnp.where(half, dt_c[:, h0:h0 + 1], dt_c[:, h1:h1 + 1])
            ydiag = jnp.zeros((CHUNK, 128), F32)
            for hh, col, msk in ((h0, a0, half), (h1, a1, jnp.logical_not(half))):
                dec = jnp.exp(jnp.where(tril, col - acum_t[hh:hh + 1, :], -jnp.inf))
                ydiag = ydiag + jnp.dot((gram * dec).astype(BF16),
                                        jnp.where(msk, xdt, 0.0).astype(BF16),
                                        preferred_element_type=F32)
            st = state[g]
            yoff = jnp.dot(cgb, st.astype(BF16), preferred_element_type=F32) * jnp.where(
                half, jnp.exp(a0), jnp.exp(a1))
            l0 = alast[:, h0:h0 + 1]
            l1 = alast[:, h1:h1 + 1]
            dout = jnp.where(half, jnp.exp(l0 - a0), jnp.exp(l1 - a1))
            state[g] = (jnp.where(half, jnp.exp(l0), jnp.exp(l1)) * st
                        + jnp.dot(bg.T.astype(BF16), (xdt * dout).astype(BF16), preferred_element_type=F32))
            y = ydiag + yoff + xg * dexp_ref[:, g * 128:(g + 1) * 128]
            zg = zx_ref[r0:r0 + CHUNK, g * 128:(g + 1) * 128]
            y = y * (zg * _sigmoid(zg))
            o_ref[r0:r0 + CHUNK, g * 128:(g + 1) * 128] = _rms(y, ng_ref[:, g * 128:(g + 1) * 128])


def _ssd(main, cw, cb, dtb, arow, dexp, ng, tri, bsz, seq, ls=512):
    t = main.shape[0]
    ns = seq // ls

    def full(shape):
        return pl.BlockSpec(shape, lambda b, s: tuple(0 for _ in shape))

    return pl.pallas_call(
        functools.partial(_ssd_body, ls=ls),
        grid=(bsz, ns),
        in_specs=[
            pl.BlockSpec((ls, 1024), lambda b, s: (b * ns + s, COL_ZX // 1024)),
            pl.BlockSpec((ls, 128), lambda b, s: (b * ns + s, COL_SMALL // 128)),
            full((CONV_K, SSM_XBC)), full((1, SSM_XBC)), full((1, 128)), full((1, 128)),
            full((1, GROUP_W)), full((1, GROUP_W)), full((CHUNK, CHUNK)),
        ],
        out_specs=pl.BlockSpec((ls, GROUP_W), lambda b, s: (b * ns + s, 0)),
        out_shape=jax.ShapeDtypeStruct((t, GROUP_W), F32),
        scratch_shapes=[pltpu.VMEM((ls + 8, SSM_XBC), F32), pltpu.VMEM((2, SSM_STATE, 128), F32)],
        compiler_params=_cparams(("arbitrary", "arbitrary")),
        name="ssd",
    )(main, main, cw, cb, dtb, arow, dexp, ng, tri)


def _mlstm_body(c_ref, sm_ref, cw_ref, cb_ref, igb_ref, fgb_ref, ng_ref, tri_ref, seg_ref, segm_ref,
                bmask_ref, o_ref, xpad, cst, nst, mst, *, ls):
    first = pl.program_id(1) == 0

    @pl.when(first)
    def _():
        cst[...] = jnp.zeros(cst.shape, F32)
        nst[...] = jnp.zeros(nst.shape, F32)
        mst[...] = jnp.zeros(mst.shape, F32)

    qk = _conv_silu(xpad, c_ref[:, 0:2 * GROUP_W], cw_ref, cb_ref, first, ls)
    sm = sm_ref[...]
    ig_all = sm + igb_ref[...]
    fx = sm + fgb_ref[...]
    lf_all = jnp.minimum(fx, 0.0) - jnp.log1p(jnp.exp(-jnp.abs(fx)))
    lane = lax.broadcasted_iota(jnp.int32, (1, GROUP_W), 1)
    hms = [jnp.logical_and(lane >= h * HD, lane < (h + 1) * HD) for h in range(HEADS)]
    tril = _tril(CHUNK)
    segb = seg_ref[...].astype(BF16)
    segm_b = segm_ref[...].astype(BF16)
    n_chunks = ls // CHUNK

    local = []
    for c in range(n_chunks):
        r0 = c * CHUNK
        q = qk[r0:r0 + CHUNK, 0:GROUP_W]
        k = qk[r0:r0 + CHUNK, GROUP_W:2 * GROUP_W] * (HD ** -0.5)
        v = c_ref[r0:r0 + CHUNK, 2 * GROUP_W:3 * GROUP_W]
        ig = ig_all[r0:r0 + CHUNK]
        bcs = jnp.dot(tri_ref[...], lf_all[r0:r0 + CHUNK], precision=HIGHEST,
                      preferred_element_type=F32)
        bcs_t = bcs.T
        ig_t = ig.T
        kb = k.astype(BF16)
        num_loc = jnp.zeros((CHUNK, GROUP_W), F32)
        w_loc_x = jnp.zeros((CHUNK, GROUP_W), F32)
        heads = []
        for h in range(HEADS):
            bcol = bcs[:, 8 + h:9 + h]
            dmat = jnp.where(tril, bcol - bcs_t[8 + h:9 + h, :] + ig_t[4 + h:5 + h, :], -jnp.inf)
            m_loc = jnp.max(dmat, axis=1, keepdims=True)
            s_qk = lax.dot_general(jnp.where(hms[h], q, 0.0).astype(BF16), kb, NT_DIMS,
                                   preferred_element_type=F32)
            sc = s_qk * jnp.exp(dmat - m_loc)
            den_loc = jnp.sum(sc, axis=1, keepdims=True)
            num_loc = num_loc + jnp.dot(sc.astype(BF16), jnp.where(hms[h], v, 0.0).astype(BF16),
                                        preferred_element_type=F32)
            blast = bcs[CHUNK - 1:CHUNK, 8 + h:9 + h]
            gcol = blast - bcol + ig[:, 4 + h:5 + h]
            gmax = jnp.max(gcol, axis=0, keepdims=True)
            w_loc_x = w_loc_x + jnp.where(hms[h], jnp.exp(gcol - gmax), 0.0)
            heads.append((bcol, m_loc, den_loc, blast, gmax))
        kw = k * w_loc_x
        kwv = bmask_ref[...] * jnp.dot(kw.T.astype(BF16), v.astype(BF16), preferred_element_type=F32)
        local.append((q, num_loc, heads, kwv, jnp.sum(kw, axis=0, keepdims=True)))

    houts = []
    for c in range(n_chunks):
        q, num_loc, heads, kwv, ksum = local[c]
        c_old = cst[...]
        n_old = nst[0:1, :]
        q_c = jnp.dot(q.astype(BF16), c_old.astype(BF16), preferred_element_type=F32)
        q_n = jnp.dot((q * n_old).astype(BF16), segb, preferred_element_type=F32)
        f_x = jnp.zeros((CHUNK, GROUP_W), F32)
        w_inter_x = jnp.zeros((CHUNK, GROUP_W), F32)
        inv_den_x = jnp.zeros((CHUNK, GROUP_W), F32)
        keep_x = jnp.zeros((1, GROUP_W), F32)
        sfac_x = jnp.zeros((1, GROUP_W), F32)
        for h in range(HEADS):
            bcol, m_loc, den_loc, blast, gmax = heads[h]
            m_prev = mst[h:h + 1, 0:1]
            inter = bcol + m_prev
            m_t = jnp.maximum(inter, m_loc)
            f = jnp.exp(m_loc - m_t)
            w_inter = jnp.exp(inter - m_t)
            den = f * den_loc + w_inter * q_n[:, h:h + 1]
            inv_den = 1.0 / jnp.maximum(jnp.abs(den), jnp.exp(-m_t))
            f_x = f_x + jnp.where(hms[h], f, 0.0)
            w_inter_x = w_inter_x + jnp.where(hms[h], w_inter, 0.0)
            inv_den_x = inv_den_x + jnp.where(hms[h], inv_den, 0.0)
            m_new = jnp.maximum(blast + m_prev, gmax)
            sfac_x = sfac_x + jnp.where(hms[h], jnp.exp(gmax - m_new), 0.0)
            keep_x = keep_x + jnp.where(hms[h], jnp.exp(blast + m_prev - m_new), 0.0)
            mst[h:h + 1, :] = jnp.broadcast_to(m_new, (1, 128))
        cst[...] = keep_x * c_old + sfac_x * kwv
        nst[0:1, :] = keep_x * n_old + sfac_x * ksum
        houts.append((f_x * num_loc + w_inter_x * q_c) * inv_den_x)

    for c in range(n_chunks):
        r0 = c * CHUNK
        hh = houts[c] * _sigmoid(c_ref[r0:r0 + CHUNK, 3 * GROUP_W:4 * GROUP_W])
        sq = hh * hh
        sq_hi = sq.astype(BF16)
        sq_lo = (sq - sq_hi.astype(F32)).astype(BF16)
        ms = (jnp.dot(sq_hi, segm_b, preferred_element_type=F32)
              + jnp.dot(sq_lo, segm_b, preferred_element_type=F32))
        o_ref[r0:r0 + CHUNK, :] = hh * lax.rsqrt(ms + EPS) * ng_ref[...]


def _mlstm(main, cw, cb, igb, fgb, ng, tri, seg, segm, bmask, bsz, seq, ls=512):
    t = main.shape[0]
    ns = seq // ls

    def full(shape):
        return pl.BlockSpec(shape, lambda b, s: tuple(0 for _ in shape))

    return pl.pallas_call(
        functools.partial(_mlstm_body, ls=ls),
        grid=(bsz, ns),
        in_specs=[
            pl.BlockSpec((ls, 1024), lambda b, s: (b * ns + s, COL_ML // 1024)),
            pl.BlockSpec((ls, 128), lambda b, s: (b * ns + s, COL_SMALL // 128)),
            full((CONV_K, 2 * GROUP_W)), full((1, 2 * GROUP_W)), full((1, 128)), full((1, 128)),
            full((1, GROUP_W)), full((CHUNK, CHUNK)), full((GROUP_W, 128)), full((GROUP_W, GROUP_W)),
            full((GROUP_W, GROUP_W)),
        ],
        out_specs=pl.BlockSpec((ls, GROUP_W), lambda b, s: (b * ns + s, 0)),
        out_shape=jax.ShapeDtypeStruct((t, GROUP_W), F32),
        scratch_shapes=[
            pltpu.VMEM((ls + 8, 2 * GROUP_W), F32),
            pltpu.VMEM((GROUP_W, GROUP_W), F32),
            pltpu.VMEM((8, GROUP_W), F32),
            pltpu.VMEM((8, 128), F32),
        ],
        compiler_params=_cparams(("arbitrary", "arbitrary")),
        name="mlstm",
    )(main, main, cw, cb, igb, fgb, ng, tri, seg, segm, bmask)


def _outproj_body(x_ref, ya_ref, yb_ref, yc_ref, yd_ref, wo_ref, g2_ref, wr_ref, br_ref,
                  xn_ref, h2_ref, cb_ref):
    acc = x_ref[...]
    for gi, y_ref in enumerate((ya_ref, yb_ref, yc_ref, yd_ref)):
        acc = acc + jnp.dot(y_ref[...].astype(BF16), wo_ref[gi * GROUP_W:(gi + 1) * GROUP_W, :],
                            preferred_element_type=F32)
    xn_ref[...] = acc
    h2 = _rms(acc, g2_ref[...])
    h_hi = h2.astype(BF16)
    h2_ref[...] = h_hi
    h_lo = (h2 - h_hi.astype(F32)).astype(BF16)
    hh = jnp.dot(h_hi, wr_ref[...], preferred_element_type=F32)
    lh = jnp.dot(h_lo, wr_ref[...], preferred_element_type=F32)
    logits = hh[:, :128] + hh[:, 128:] + lh[:, :128] + br_ref[...]
    lane = lax.broadcasted_iota(jnp.int32, logits.shape, 1)
    big = jnp.int32(1 << 20)
    n_grp = N_EXPERTS // EXPERTS_PER_GROUP
    gl = jnp.where(jnp.logical_and(lane >= N_EXPERTS, lane < N_EXPERTS + n_grp), logits, -jnp.inf)
    gmax = jnp.max(gl, axis=1, keepdims=True)
    g_w = 1.0 / jnp.sum(jnp.exp(gl - gmax), axis=1, keepdims=True)
    gidx = jnp.min(jnp.where(gl == gmax, lane, big), axis=1, keepdims=True) - N_EXPERTS
    in_grp = jnp.logical_and(lane < N_EXPERTS,
                             lax.shift_right_logical(lane, int(math.log2(EXPERTS_PER_GROUP))) == gidx)
    el = jnp.where(in_grp, logits, -jnp.inf)
    emax = jnp.max(el, axis=1, keepdims=True)
    esum = jnp.sum(jnp.exp(el - emax), axis=1, keepdims=True)
    i1 = jnp.min(jnp.where(el == emax, lane, big), axis=1, keepdims=True)
    el2 = jnp.where(lane == i1, -jnp.inf, el)
    emax2 = jnp.max(el2, axis=1, keepdims=True)
    i2 = jnp.min(jnp.where(el2 == emax2, lane, big), axis=1, keepdims=True)
    p1 = 1.0 / esum
    p2 = jnp.exp(emax2 - emax) / esum
    w1 = p1 / (p1 + p2)
    w2 = p2 / (p1 + p2)
    cb_ref[...] = g_w * (jnp.where(lane == i1, w1, 0.0) + jnp.where(lane == i2, w2, 0.0))


def _outproj(x2, ya, yb, yc, yd, wo, g2, wr, br, tm=512):
    t = x2.shape[0]
    row = pl.BlockSpec((tm, D_MODEL), lambda i: (i, 0))
    grp = pl.BlockSpec((tm, GROUP_W), lambda i: (i, 0))

    def full(shape):
        return pl.BlockSpec(shape, lambda i: tuple(0 for _ in shape))

    return pl.pallas_call(
        _outproj_body,
        grid=(t // tm,),
        in_specs=[row, grp, grp, grp, grp, full((D_MODEL, D_MODEL)), full((1, D_MODEL)),
                  full((D_MODEL, 256)), full((1, 128))],
        out_specs=[row, row, pl.BlockSpec((tm, 128), lambda i: (i, 0))],
        out_shape=[
            jax.ShapeDtypeStruct((t, D_MODEL), F32),
            jax.ShapeDtypeStruct((t, D_MODEL), BF16),
            jax.ShapeDtypeStruct((t, 128), F32),
        ],
        compiler_params=_cparams(("arbitrary",)),
        name="outproj_router",
    )(x2, ya, yb, yc, yd, wo, g2, wr, br)


MOE_EB = 4


def _moe_body(h_ref, cb_ref, xn_ref, wgu_ref, wd_ref, gf_ref, o_ref, acc, *, final_norm):
    eb = pl.program_id(1)

    @pl.when(eb == 0)
    def _():
        acc[...] = jnp.zeros(acc.shape, F32)

    h = h_ref[...]
    lane = lax.broadcasted_iota(jnp.int32, cb_ref.shape, 1)

    def gate_up(j):
        return jnp.dot(h, wgu_ref[j], preferred_element_type=F32)

    def down(j, gu):
        g = gu[:, :D_EXPERT]
        u = gu[:, D_EXPERT:]
        col = jnp.sum(jnp.where(lane == eb * MOE_EB + j, cb_ref[...], 0.0), axis=1,
                      keepdims=True)
        hid = (g * _sigmoid(g)) * u * col
        return jnp.dot(hid.astype(BF16), wd_ref[j], preferred_element_type=F32)

    total = jnp.zeros(acc.shape, F32)
    gu = gate_up(0)
    for j in range(MOE_EB):
        gu_next = gate_up(j + 1) if j + 1 < MOE_EB else None
        total = total + down(j, gu)
        gu = gu_next
    acc[...] += total

    @pl.when(eb == N_EXPERTS // MOE_EB - 1)
    def _():
        y = xn_ref[...] + acc[...]
        if final_norm:
            y = _rms(y, gf_ref[...])
        o_ref[...] = y


def _moe(h2, comb, xn, wgu, wd, gf, final_norm, tm=1024):
    t = h2.shape[0]
    return pl.pallas_call(
        functools.partial(_moe_body, final_norm=final_norm),
        grid=(t // tm, N_EXPERTS // MOE_EB),
        in_specs=[
            pl.BlockSpec((tm, D_MODEL), lambda i, e: (i, 0)),
            pl.BlockSpec((tm, 128), lambda i, e: (i, 0)),
            pl.BlockSpec((tm, D_MODEL), lambda i, e: (i, 0)),
            pl.BlockSpec((MOE_EB, D_MODEL, 2 * D_EXPERT), lambda i, e: (e, 0, 0)),
            pl.BlockSpec((MOE_EB, D_EXPERT, D_MODEL), lambda i, e: (e, 0, 0)),
            pl.BlockSpec((1, D_MODEL), lambda i, e: (0, 0)),
        ],
        out_specs=pl.BlockSpec((tm, D_MODEL), lambda i, e: (i, 0)),
        out_shape=jax.ShapeDtypeStruct((t, D_MODEL), F32),
        scratch_shapes=[pltpu.VMEM((tm, D_MODEL), F32)],
        compiler_params=_cparams(("arbitrary", "arbitrary")),
        name="moe",
    )(h2, comb, xn, wgu, wd, gf)


def _spread_heads(w, width, slab, total):
    rows = w.shape[0]
    pieces = []
    for h in range(HEADS):
        pieces.append(w[:, h * width:(h + 1) * width])
        pieces.append(jnp.zeros((rows, slab - width), w.dtype))
    if total > HEADS * slab:
        pieces.append(jnp.zeros((rows, total - HEADS * slab), w.dtype))
    return jnp.concatenate(pieces, axis=1)


def _pad_heads(w, width):
    return _spread_heads(w, width, HP, HEADS * HP)


def _v_cols(w):
    return _spread_heads(w, HD, VR, V_COLS)


def _lane_row(vals, offset, width=128):
    return jnp.zeros((1, width), F32).at[0, offset:offset + vals.shape[0]].set(vals.astype(F32))


def _split_w_in(w):
    sizes = (3 * GROUP_W, GROUP_W, SSM_XBC, HEADS, 2 * GROUP_W, GROUP_W, HEADS, HEADS, GROUP_W,
             256, 128, MLA_ROPE)
    offs = np.cumsum((0,) + sizes)
    return [w[:, int(offs[i]):int(offs[i + 1])] for i in range(len(sizes))]


def _layer_weights(l, p):
    a_qkv, s_z, s_xbc, s_dt, m_qk, m_v, m_i, m_f, m_o, d_cq, d_ckv, d_kr = _split_w_in(p["w_in_bf16"][l])
    aq, ak, av = a_qkv[:, :GROUP_W], a_qkv[:, GROUP_W:2 * GROUP_W], a_qkv[:, 2 * GROUP_W:]
    half = MLA_ROPE // 2
    small = jnp.concatenate([s_dt, m_i, m_f, jnp.zeros((D_MODEL, 64 - 12), BF16),
                             d_kr, d_kr[:, half:], d_kr[:, :half]], axis=1)
    w_main = jnp.concatenate([s_z, s_xbc, m_qk, m_v, m_o, d_cq, d_ckv, small, _pad_heads(ak, HD)], axis=1)
    w = {"w_main": w_main,
         "w_qv": jnp.concatenate([_pad_heads(aq, HD), _v_cols(av)], axis=1),
         "g_mix": p["norm_mix_g"][l][None, :]}
    w["ssm_cw"] = p["ssm_conv_w"][l]
    w["ssm_cb"] = p["ssm_conv_b"][l][None, :]
    w["ssm_dtb"] = _lane_row(p["ssm_dt_bias"][l], 0)
    w["ssm_arow"] = _lane_row(-jnp.exp(p["ssm_a_log"][l].astype(F32)), 0)
    w["ssm_dexp"] = jnp.repeat(p["ssm_d"][l].astype(F32), HD)[None, :]
    w["ssm_ng"] = p["ssm_norm_g"][l][None, :]
    w["ml_cw"] = p["ml_conv_w"][l]
    w["ml_cb"] = p["ml_conv_b"][l][None, :]
    w["ml_igb"] = _lane_row(p["ml_ig_bias"][l], 4)
    w["ml_fgb"] = _lane_row(p["ml_fg_bias"][l], 8)
    w["ml_ng"] = p["ml_norm_g"][l][None, :]
    wuq = p["mla_w_uq"][l]
    qd = MLA_NOPE + MLA_ROPE
    swapped = []
    for h in range(HEADS):
        c0 = h * qd + MLA_NOPE
        swapped += [jnp.zeros((256, MLA_NOPE), F32), wuq[:, c0 + half:c0 + 2 * half], wuq[:, c0:c0 + half]]
    wq = _spread_heads(wuq, qd, HP, HEADS * HP)
    wqs = _spread_heads(jnp.concatenate(swapped, axis=1), qd, HP, HEADS * HP)
    w["mla_wq2"] = jnp.concatenate([wq, wqs], axis=1).astype(BF16)
    wukv = p["mla_w_ukv"][l]
    wk_cols = jnp.concatenate([wukv[:, h * 2 * HD:h * 2 * HD + MLA_NOPE] for h in range(HEADS)], axis=1)
    wv_cols = jnp.concatenate([wukv[:, h * 2 * HD + MLA_NOPE:(h + 1) * 2 * HD] for h in range(HEADS)], axis=1)
    w["mla_wk"] = _pad_heads(wk_cols, MLA_NOPE).astype(BF16)
    w["mla_wv"] = _v_cols(wv_cols).astype(BF16)
    w["mla_gq"] = p["mla_q_norm_g"][l][None, :]
    w["mla_gkv"] = p["mla_kv_norm_g"][l][None, :]
    w["w_out"] = p["w_out"][l].astype(BF16)
    w["g_ffn"] = p["norm_ffn_g"][l][None, :]
    wr = jnp.zeros((D_MODEL, 128), F32)
    wr = wr.at[:, :N_EXPERTS].set(p["moe_w_router"][l]).at[:, N_EXPERTS:N_EXPERTS + 4].set(p["moe_w_group"][l])
    wr_hi = wr.astype(BF16)
    w["w_router"] = jnp.concatenate([wr_hi, (wr - wr_hi.astype(F32)).astype(BF16)], axis=1)
    w["b_router"] = _lane_row(p["moe_b_router"][l], 0) + _lane_row(p["moe_b_group"][l], N_EXPERTS)
    w["w_gu"] = jnp.concatenate([p["moe_w_gate"][l], p["moe_w_up"][l]], axis=2).astype(BF16)
    w["w_down"] = p["moe_w_down"][l].astype(BF16)
    return w


def kernel(x, positions, norm_mix_g, w_in, ssm_conv_w, ssm_conv_b, ssm_dt_bias, ssm_a_log, ssm_d, ssm_norm_g, ml_conv_w, ml_conv_b, ml_ig_bias, ml_fg_bias, ml_norm_g, mla_q_norm_g, mla_w_uq, mla_kv_norm_g, mla_w_ukv, w_out, norm_ffn_g, moe_w_group, moe_b_group, moe_w_router, moe_b_router, moe_w_gate, moe_w_up, moe_w_down, final_norm_g):
    p = dict(norm_mix_g=norm_mix_g, w_in_bf16=_cast_bf16(w_in), ssm_conv_w=ssm_conv_w, ssm_conv_b=ssm_conv_b,
             ssm_dt_bias=ssm_dt_bias, ssm_a_log=ssm_a_log, ssm_d=ssm_d, ssm_norm_g=ssm_norm_g,
             ml_conv_w=ml_conv_w, ml_conv_b=ml_conv_b, ml_ig_bias=ml_ig_bias, ml_fg_bias=ml_fg_bias,
             ml_norm_g=ml_norm_g, mla_q_norm_g=mla_q_norm_g, mla_w_uq=mla_w_uq,
             mla_kv_norm_g=mla_kv_norm_g, mla_w_ukv=mla_w_ukv, w_out=w_out, norm_ffn_g=norm_ffn_g,
             moe_w_group=moe_w_group, moe_b_group=moe_b_group, moe_w_router=moe_w_router,
             moe_b_router=moe_b_router, moe_w_gate=moe_w_gate, moe_w_up=moe_w_up, moe_w_down=moe_w_down)
    bsz, seq, _ = x.shape
    depth = w_in.shape[0]
    nb = seq // MOBA_BLOCK
    x2 = x.reshape(bsz * seq, D_MODEL)

    half = MLA_ROPE // 2
    inv = ROPE_THETA ** (-jnp.arange(0, MLA_ROPE, 2, dtype=F32) / MLA_ROPE)
    inv_full = _lane_row(jnp.concatenate([inv, inv]), MLA_NOPE)
    sgn_full = _lane_row(jnp.concatenate([-jnp.ones((half,), F32), jnp.ones((half,), F32)]), MLA_NOPE)
    pe_mask = _lane_row(jnp.ones((MLA_ROPE,), F32), MLA_NOPE)
    c_tok, s_tok, c_tr, s_tr = _rope_tables(positions.reshape(bsz * seq, 1), inv_full, sgn_full, bsz, seq)
    tri = jnp.tril(jnp.ones((CHUNK, CHUNK), F32))
    head_of = jnp.arange(GROUP_W) // HD
    seg = (head_of[:, None] == jnp.arange(128)[None, :]).astype(F32)
    bmask = (head_of[:, None] == head_of[None, :]).astype(F32)
    segm = bmask / HD

    for l in range(depth):
        w = _layer_weights(l, p)
        main, qT_a, vT_a = _in_proj(x2, w["g_mix"], w["w_main"], w["w_qv"], bsz, seq)
        kmean, k_a = _kmean(main, bsz, seq)
        qTb_a = _moba_select(qT_a, kmean.reshape(bsz, nb, HEADS * HP), bsz, seq)
        y_a = _attention(qTb_a, k_a, vT_a, bsz, seq, "attn_moba")
        y_b = _ssd(main, w["ssm_cw"], w["ssm_cb"], w["ssm_dtb"], w["ssm_arow"], w["ssm_dexp"], w["ssm_ng"],
                   tri, bsz, seq)
        y_c = _mlstm(main, w["ml_cw"], w["ml_cb"], w["ml_igb"], w["ml_fgb"], w["ml_ng"], tri, seg, segm,
                     bmask, bsz, seq)
        qT_d, k_d, vT_d = _mla_prep(main, c_tok, s_tok, c_tr, s_tr, w["mla_gq"], w["mla_wq2"], w["mla_gkv"],
                                    w["mla_wk"], w["mla_wv"], pe_mask, bsz, seq)
        y_d = _attention(qT_d, k_d, vT_d, bsz, seq, "attn_mla")
        xn, h2, comb = _outproj(x2, y_a, y_b, y_c, y_d, w["w_out"], w["g_ffn"], w["w_router"], w["b_router"])
        x2 = _moe(h2, comb, xn, w["w_gu"], w["w_down"], final_norm_g[None, :], final_norm=(l == depth - 1))
    return x2.reshape(bsz, seq, D_MODEL)
```

```python
import functools
import math

import numpy as np
import jax
import jax.numpy as jnp
from jax import lax
from jax.experimental import pallas as pl
from jax.experimental.pallas import tpu as pltpu

F32 = jnp.float32
BF16 = jnp.bfloat16
HIGHEST = lax.Precision.HIGHEST

EPS = 1e-6
NEG = -1e30
LOG2E = 1.4426950408889634

D_MODEL = 1024
GROUP_W = 256
HEADS = 4
HD = 64
HP = 128
MOBA_BLOCK = 256
MOBA_TOPK = 3
SSM_STATE = 128
SSM_XBC = 768
CONV_K = 4
CHUNK = 128
MLA_NOPE = 64
MLA_ROPE = 32
ROPE_THETA = 10000.0
N_EXPERTS = 16
EXPERTS_PER_GROUP = 4
D_EXPERT = 256

COL_ZX = 0
COL_ML = 1024
COL_CQ = 2048
COL_CKV = 2304
COL_SMALL = 2432
COL_KA = 2560
N_MAIN = 3072
VR = 80
V_COLS = 384
ATT_UQ = 256
ATT_UK = 256

V7X_VMEM_LIMIT = 56 * 1024 * 1024

NT_DIMS = (((1,), (1,)), ((), ()))


def _cparams(sem):
    return pltpu.CompilerParams(dimension_semantics=sem, vmem_limit_bytes=V7X_VMEM_LIMIT)


def _rms(x, g):
    ms = jnp.mean(x * x, axis=-1, keepdims=True)
    return x * lax.rsqrt(ms + EPS) * g


def _sigmoid(x):
    return 0.5 * jnp.tanh(0.5 * x) + 0.5


def _softplus(x):
    return jnp.maximum(x, 0.0) + jnp.log1p(jnp.exp(-jnp.abs(x)))


def _tril(n):
    r = lax.broadcasted_iota(jnp.int32, (n, n), 0)
    c = lax.broadcasted_iota(jnp.int32, (n, n), 1)
    return c <= r


def _cast_body(w_ref, o_ref):
    o_ref[...] = w_ref[...].astype(BF16)


def _cast_bf16(w, rows=256):
    nl, r, c = w.shape
    return pl.pallas_call(
        _cast_body,
        grid=(nl, r // rows),
        in_specs=[pl.BlockSpec((1, rows, c), lambda l, i: (l, i, 0))],
        out_specs=pl.BlockSpec((1, rows, c), lambda l, i: (l, i, 0)),
        out_shape=jax.ShapeDtypeStruct(w.shape, BF16),
        compiler_params=_cparams(("arbitrary", "arbitrary")),
        name="cast_w_in",
    )(w)


def _ones_rows(shape):
    row = lax.broadcasted_iota(jnp.int32, shape, 0)
    hit = row == HD
    for h in range(1, HEADS):
        hit = jnp.logical_or(hit, row == h * VR + HD)
    return jnp.where(hit, 1.0, 0.0)


def _in_proj_body(x_ref, g_ref, wm_ref, wqv_ref, om_ref, oq_ref, ov_ref):
    h = _rms(x_ref[...], g_ref[...]).astype(BF16)
    n = om_ref.shape[1]
    for c in range(0, n, 512):
        om_ref[:, c:c + 512] = jnp.dot(h, wm_ref[:, c:c + 512], preferred_element_type=F32)
    qv = jnp.dot(h, wqv_ref[...], preferred_element_type=F32)
    nq = HEADS * HP
    oq_ref[0] = qv[:, :nq].T
    vt = qv[:, nq:].T[:HEADS * VR]
    ov_ref[0] = (vt + _ones_rows(vt.shape)).astype(BF16)


def _in_proj(x2, g, w_main, w_qv, bsz, seq, tm=512):
    t = x2.shape[0]
    nst = seq // tm
    return pl.pallas_call(
        _in_proj_body,
        grid=(t // tm,),
        in_specs=[
            pl.BlockSpec((tm, D_MODEL), lambda i: (i, 0)),
            pl.BlockSpec((1, D_MODEL), lambda i: (0, 0)),
            pl.BlockSpec((D_MODEL, N_MAIN), lambda i: (0, 0)),
            pl.BlockSpec((D_MODEL, HEADS * HP + V_COLS), lambda i: (0, 0)),
        ],
        out_specs=[
            pl.BlockSpec((tm, N_MAIN), lambda i: (i, 0)),
            pl.BlockSpec((1, HEADS * HP, tm), lambda i: (i // nst, 0, i % nst)),
            pl.BlockSpec((1, HEADS * VR, tm), lambda i: (i // nst, 0, i % nst)),
        ],
        out_shape=[
            jax.ShapeDtypeStruct((t, N_MAIN), F32),
            jax.ShapeDtypeStruct((bsz, HEADS * HP, seq), F32),
            jax.ShapeDtypeStruct((bsz, HEADS * VR, seq), BF16),
        ],
        compiler_params=_cparams(("arbitrary",)),
        name="in_proj",
    )(x2, g, w_main, w_qv)


def _kmean_body(k_ref, km_ref, kb_ref, *, nb):
    k = k_ref[...]
    km_ref[0] = jnp.mean(k, axis=0, keepdims=True)
    blk = pl.program_id(0) % nb
    lane = lax.broadcasted_iota(jnp.int32, k.shape, 1)
    kb_ref[...] = jnp.where((lane & (HP - 1)) == HD + blk, 1.0, k).astype(BF16)


def _kmean(main, bsz, seq):
    t = main.shape[0]
    nb = seq // MOBA_BLOCK
    assert nb <= HP - HD
    kcol = COL_KA // (HEADS * HP)
    return pl.pallas_call(
        functools.partial(_kmean_body, nb=nb),
        grid=(bsz * nb,),
        in_specs=[pl.BlockSpec((MOBA_BLOCK, HEADS * HP), lambda i: (i, kcol))],
        out_specs=[
            pl.BlockSpec((1, 1, HEADS * HP), lambda i: (i, 0, 0)),
            pl.BlockSpec((MOBA_BLOCK, HEADS * HP), lambda i: (i, 0)),
        ],
        out_shape=[
            jax.ShapeDtypeStruct((bsz * nb, 1, HEADS * HP), F32),
            jax.ShapeDtypeStruct((t, HEADS * HP), BF16),
        ],
        compiler_params=_cparams(("arbitrary",)),
        name="moba_kmean",
    )(main)


def _moba_select_body(qT_ref, km_ref, qb_ref, *, ts, nb, qscale):
    s_idx = pl.program_id(1)
    qT = qT_ref[0]
    km = km_ref[0]
    pos = s_idx * ts + lax.broadcasted_iota(jnp.int32, (nb, ts), 1)
    cur = lax.shift_right_logical(pos, int(math.log2(MOBA_BLOCK)))
    blk = lax.broadcasted_iota(jnp.int32, (nb, ts), 0)
    past = blk < cur
    for h in range(HEADS):
        qh = qT[h * HP:(h + 1) * HP, :]
        g = jnp.dot(km[:, h * HP:(h + 1) * HP], qh, precision=HIGHEST,
                    preferred_element_type=F32)
        g = jnp.where(past, g, -jnp.inf)
        bias = jnp.where(blk == cur, 0.0, NEG)
        for _ in range(MOBA_TOPK):
            m = jnp.max(g, axis=0, keepdims=True)
            cand = jnp.logical_and(g == m, g > -jnp.inf)
            idx = jnp.min(jnp.where(cand, blk, nb), axis=0, keepdims=True)
            pick = blk == idx
            bias = jnp.where(pick, 0.0, bias)
            g = jnp.where(pick, -jnp.inf, g)
        rows = [qh[:HD] * qscale, bias]
        if HP - HD - nb:
            rows.append(jnp.zeros((HP - HD - nb, ts), F32))
        qb_ref[0, h * HP:(h + 1) * HP, :] = jnp.concatenate(rows, axis=0).astype(BF16)


def _moba_select(qT, kmean, bsz, seq, ts=512):
    nb = seq // MOBA_BLOCK
    qscale = (HD ** -0.5) * LOG2E
    return pl.pallas_call(
        functools.partial(_moba_select_body, ts=ts, nb=nb, qscale=qscale),
        grid=(bsz, seq // ts),
        in_specs=[
            pl.BlockSpec((1, HEADS * HP, ts), lambda b, s: (b, 0, s)),
            pl.BlockSpec((1, nb, HEADS * HP), lambda b, s: (b, 0, 0)),
        ],
        out_specs=pl.BlockSpec((1, HEADS * HP, ts), lambda b, s: (b, 0, s)),
        out_shape=jax.ShapeDtypeStruct((bsz, HEADS * HP, seq), BF16),
        compiler_params=_cparams(("arbitrary", "arbitrary")),
        name="moba_select",
    )(qT, kmean)


def _rope_body(pos_ref, inv_ref, sgn_ref, c_ref, s_ref, ct_ref, st_ref):
    ang = pos_ref[...].astype(F32) * inv_ref[...]
    c = jnp.cos(ang)
    s = jnp.sin(ang) * sgn_ref[...]
    c_ref[...] = c
    s_ref[...] = s
    ct_ref[0] = c.T
    st_ref[0] = s.T


def _rope_tables(pos2, inv_full, sgn_full, bsz, seq, ts=512):
    t = pos2.shape[0]
    nst = seq // ts
    tok = pl.BlockSpec((ts, HP), lambda i: (i, 0))
    tr = pl.BlockSpec((1, HP, ts), lambda i: (i // nst, 0, i % nst))
    row = pl.BlockSpec((1, HP), lambda i: (0, 0))
    return pl.pallas_call(
        _rope_body,
        grid=(t // ts,),
        in_specs=[pl.BlockSpec((ts, 1), lambda i: (i, 0)), row, row],
        out_specs=[tok, tok, tr, tr],
        out_shape=[
            jax.ShapeDtypeStruct((t, HP), F32),
            jax.ShapeDtypeStruct((t, HP), F32),
            jax.ShapeDtypeStruct((bsz, HP, seq), F32),
            jax.ShapeDtypeStruct((bsz, HP, seq), F32),
        ],
        compiler_params=_cparams(("arbitrary",)),
        name="rope_tables",
    )(pos2, inv_full, sgn_full)


def _mla_prep_body(cq_ref, ckv_ref, sm_ref, c_ref, s_ref, ct_ref, st_ref, gq_ref, wq_ref, gkv_ref,
                   wk_ref, wv_ref, msk_ref, qo_ref, ko_ref, vo_ref, *, qscale):
    qn = _rms(cq_ref[...], gq_ref[...]).astype(BF16)
    q2 = jnp.dot(qn, wq_ref[...], preferred_element_type=F32).T
    nq = HEADS * HP
    ct4 = jnp.concatenate([ct_ref[0]] * HEADS, axis=0)
    st4 = jnp.concatenate([st_ref[0]] * HEADS, axis=0)
    q = (q2[:nq] * ct4 + q2[nq:] * st4) * qscale
    qo_ref[0] = q.astype(BF16)
    kvn = _rms(ckv_ref[...], gkv_ref[...]).astype(BF16)
    kn = jnp.dot(kvn, wk_ref[...], preferred_element_type=F32)
    sm = sm_ref[...]
    pe = (sm * c_ref[...] + pltpu.roll(sm, HP - MLA_ROPE, 1) * s_ref[...]) * msk_ref[...]
    ko_ref[...] = (kn + jnp.concatenate([pe] * HEADS, axis=1)).astype(BF16)
    vt = jnp.dot(kvn, wv_ref[...], preferred_element_type=F32).T[:HEADS * VR]
    vo_ref[0] = (vt + _ones_rows(vt.shape)).astype(BF16)


def _mla_prep(main, c_tok, s_tok, c_tr, s_tr, gq, wq2T, gkv, wk, wvT, pe_mask, bsz, seq, ts=512):
    t = main.shape[0]
    nst = seq // ts
    qscale = ((MLA_NOPE + MLA_ROPE) ** -0.5) * LOG2E
    tok128 = pl.BlockSpec((ts, HP), lambda i: (i, 0))
    tr128 = pl.BlockSpec((1, HP, ts), lambda i: (i // nst, 0, i % nst))

    def full(shape):
        return pl.BlockSpec(shape, lambda i: tuple(0 for _ in shape))

    return pl.pallas_call(
        functools.partial(_mla_prep_body, qscale=qscale),
        grid=(t // ts,),
        in_specs=[
            pl.BlockSpec((ts, 256), lambda i: (i, COL_CQ // 256)),
            pl.BlockSpec((ts, 128), lambda i: (i, COL_CKV // 128)),
            pl.BlockSpec((ts, 128), lambda i: (i, COL_SMALL // 128)),
            tok128, tok128, tr128, tr128,
            full((1, 256)), full((256, 2 * HEADS * HP)), full((1, 128)),
            full((128, HEADS * HP)), full((128, V_COLS)), full((1, HP)),
        ],
        out_specs=[
            pl.BlockSpec((1, HEADS * HP, ts), lambda i: (i // nst, 0, i % nst)),
            pl.BlockSpec((ts, HEADS * HP), lambda i: (i, 0)),
            pl.BlockSpec((1, HEADS * VR, ts), lambda i: (i // nst, 0, i % nst)),
        ],
        out_shape=[
            jax.ShapeDtypeStruct((bsz, HEADS * HP, seq), BF16),
            jax.ShapeDtypeStruct((t, HEADS * HP), BF16),
            jax.ShapeDtypeStruct((bsz, HEADS * VR, seq), BF16),
        ],
        compiler_params=_cparams(("arbitrary",)),
        name="mla_prep",
    )(main, main, main, c_tok, s_tok, c_tr, s_tr, gq, wq2T, gkv, wk, wvT, pe_mask)


def _attn_body(ii_ref, jj_ref, qT_ref, k_ref, vT_ref, o_ref, m_sc, acc_sc, *, tq, tk):
    p = pl.program_id(1)
    i = ii_ref[p]
    j = jj_ref[p]

    @pl.when(j == 0)
    def _():
        m_sc[...] = jnp.full(m_sc.shape, NEG, F32)
        acc_sc[...] = jnp.zeros(acc_sc.shape, F32)

    def scores(u):
        h, qc, kb, _ = u
        return jnp.dot(k_ref[kb * ATT_UK:(kb + 1) * ATT_UK, h * HP:(h + 1) * HP],
                       qT_ref[0, h * HP:(h + 1) * HP, qc * ATT_UQ:(qc + 1) * ATT_UQ],
                       preferred_element_type=F32)

    def update(u, s):
        h, qc, kb, masked = u
        if masked:
            kpos = kb * ATT_UK + lax.broadcasted_iota(jnp.int32, s.shape, 0)
            qpos = qc * ATT_UQ + lax.broadcasted_iota(jnp.int32, s.shape, 1)
            s = jnp.where(kpos <= qpos, s, NEG)
        alphas, pms = [], []
        for c in range(ATT_UQ // 128):
            ql = slice(qc * ATT_UQ + c * 128, qc * ATT_UQ + (c + 1) * 128)
            sc = s[:, c * 128:(c + 1) * 128]
            m_old = m_sc[h:h + 1, ql]
            m_new = jnp.maximum(m_old, jnp.max(sc, axis=0, keepdims=True))
            alphas.append(jnp.exp2(m_old - m_new))
            pms.append(jnp.exp2(sc - m_new).astype(BF16))
            m_sc[h:h + 1, ql] = m_new
        ql = slice(qc * ATT_UQ, (qc + 1) * ATT_UQ)
        rows = slice(h * VR, (h + 1) * VR)
        acc_sc[rows, ql] = jnp.concatenate(alphas, axis=1) * acc_sc[rows, ql] + jnp.dot(
            vT_ref[0, rows, kb * ATT_UK:(kb + 1) * ATT_UK], jnp.concatenate(pms, axis=1),
            preferred_element_type=F32)

    def run(units):
        ahead = 8
        pending = [scores(u) for u in units[:ahead]]
        for n, u in enumerate(units):
            s = pending.pop(0)
            if n + ahead < len(units):
                pending.append(scores(units[n + ahead]))
            update(u, s)

    def unit_list(diag):
        units = []
        for kb in range(tk // ATT_UK):
            for h in range(HEADS):
                for qc in range(tq // ATT_UQ):
                    k_lo, k_hi = kb * ATT_UK, (kb + 1) * ATT_UK - 1
                    q_lo, q_hi = qc * ATT_UQ, (qc + 1) * ATT_UQ - 1
                    if diag and k_lo > q_hi:
                        continue
                    units.append((h, qc, kb, diag and k_hi > q_lo))
        return units

    @pl.when(j < i)
    def _():
        run(unit_list(False))

    @pl.when(j == i)
    def _():
        run(unit_list(True))
        parts = []
        for h in range(HEADS):
            den = acc_sc[h * VR + HD:h * VR + HD + 1, :]
            parts.append(acc_sc[h * VR:h * VR + HD, :] * (1.0 / den))
        o_ref[...] = jnp.concatenate(parts, axis=0).T


def _attention(qT, k, vT, bsz, seq, name, tq=1024):
    t = k.shape[0]
    nq = seq // tq
    ii = np.concatenate([np.full(i + 1, i, np.int32) for i in range(nq)])
    jj = np.concatenate([np.arange(i + 1, dtype=np.int32) for i in range(nq)])
    grid_spec = pltpu.PrefetchScalarGridSpec(
        num_scalar_prefetch=2,
        grid=(bsz, len(ii)),
        in_specs=[
            pl.BlockSpec((1, HEADS * HP, tq), lambda b, p, ii, jj: (b, 0, ii[p])),
            pl.BlockSpec((tq, HEADS * HP), lambda b, p, ii, jj: (b * nq + jj[p], 0)),
            pl.BlockSpec((1, HEADS * VR, tq), lambda b, p, ii, jj: (b, 0, jj[p])),
        ],
        out_specs=pl.BlockSpec((tq, GROUP_W), lambda b, p, ii, jj: (b * nq + ii[p], 0)),
        scratch_shapes=[
            pltpu.VMEM((8, tq), F32),
            pltpu.VMEM((HEADS * VR, tq), F32),
        ],
    )
    return pl.pallas_call(
        functools.partial(_attn_body, tq=tq, tk=tq),
        grid_spec=grid_spec,
        out_shape=jax.ShapeDtypeStruct((t, GROUP_W), F32),
        compiler_params=_cparams(("arbitrary", "arbitrary")),
        name=name,
    )(jnp.asarray(ii), jnp.asarray(jj), qT, k, vT)


def _conv_silu(xpad, src, cw_ref, cb_ref, first, ls):
    width = xpad.shape[1]

    @pl.when(first)
    def _():
        xpad[0:8, :] = jnp.zeros((8, width), F32)

    @pl.when(jnp.logical_not(first))
    def _():
        xpad[0:8, :] = xpad[ls:ls + 8, :]

    xpad[8:ls + 8, :] = src
    acc = cb_ref[...] + cw_ref[0:1, :] * xpad[pl.ds(8 - CONV_K + 1, ls), :]
    for kk in range(1, CONV_K):
        acc = acc + cw_ref[kk:kk + 1, :] * xpad[pl.ds(8 - CONV_K + 1 + kk, ls), :]
    return acc * _sigmoid(acc)


def _ssd_body(zx_ref, sm_ref, cw_ref, cb_ref, dtb_ref, arow_ref, dexp_ref, ng_ref, tri_ref, o_ref,
              xpad, state, *, ls):
    first = pl.program_id(1) == 0

    @pl.when(first)
    def _():
        state[...] = jnp.zeros(state.shape, F32)

    xc = _conv_silu(xpad, zx_ref[:, GROUP_W:GROUP_W + SSM_XBC], cw_ref, cb_ref, first, ls)
    dt_all = _softplus(sm_ref[...] + dtb_ref[...])
    da_all = dt_all * arow_ref[...]
    lane = lax.broadcasted_iota(jnp.int32, (1, 128), 1)
    half = lane < HD
    tril = _tril(CHUNK)
    for c in range(ls // CHUNK):
        r0 = c * CHUNK
        dt_c = dt_all[r0:r0 + CHUNK]
        acum = jnp.dot(tri_ref[...], da_all[r0:r0 + CHUNK], precision=HIGHEST,
                       preferred_element_type=F32)
        acum_t = acum.T
        alast = acum[CHUNK - 1:CHUNK, :]
        for g in range(2):
            h0, h1 = 2 * g, 2 * g + 1
            xg = xc[r0:r0 + CHUNK, g * 128:(g + 1) * 128]
            bg = xc[r0:r0 + CHUNK, GROUP_W + g * 128:GROUP_W + (g + 1) * 128]
            cg = xc[r0:r0 + CHUNK, 2 * GROUP_W + g * 128:2 * GROUP_W + (g + 1) * 128]
            cgb = cg.astype(BF16)
            gram = lax.dot_general(cgb, bg.astype(BF16), NT_DIMS, preferred_element_type=F32)
            a0 = acum[:, h0:h0 + 1]
            a1 = acum[:, h1:h1 + 1]
            xdt = xg * jnp.where(half, dt_c[:, h0:h0 + 1], dt_c[:, h1:h1 + 1])
            ydiag = jnp.zeros((CHUNK, 128), F32)
            for hh, col, msk in ((h0, a0, half), (h1, a1, jnp.logical_not(half))):
                dec = jnp.exp(jnp.where(tril, col - acum_t[hh:hh + 1, :], -jnp.inf))
                ydiag = ydiag + jnp.dot((gram * dec).astype(BF16),
                                        jnp.where(msk, xdt, 0.0).astype(BF16),
                                        preferred_element_type=F32)
            st = state[g]
            yoff = jnp.dot(cgb, st.astype(BF16), preferred_element_type=F32) * jnp.where(
                half, jnp.exp(a0), jnp.exp(a1))
            l0 = alast[:, h0:h0 + 1]
            l1 = alast[:, h1:h1 + 1]
            dout = jnp.where(half, jnp.exp(l0 - a0), jnp.exp(l1 - a1))
            state[g] = (jnp.where(half, jnp.exp(l0), jnp.exp(l1)) * st
                        + jnp.dot(bg.T.astype(BF16), (xdt * dout).astype(BF16), preferred_element_type=F32))
            y = ydiag + yoff + xg * dexp_ref[:, g * 128:(g + 1) * 128]
            zg = zx_ref[r0:r0 + CHUNK, g * 128:(g + 1) * 128]
            y = y * (zg * _sigmoid(zg))
            o_ref[r0:r0 + CHUNK, g * 128:(g + 1) * 128] = _rms(y, ng_ref[:, g * 128:(g + 1) * 128])


def _ssd(main, cw, cb, dtb, arow, dexp, ng, tri, bsz, seq, ls=512):
    t = main.shape[0]
    ns = seq // ls

    def full(shape):
        return pl.BlockSpec(shape, lambda b, s: tuple(0 for _ in shape))

    return pl.pallas_call(
        functools.partial(_ssd_body, ls=ls),
        grid=(bsz, ns),
        in_specs=[
            pl.BlockSpec((ls, 1024), lambda b, s: (b * ns + s, COL_ZX // 1024)),
            pl.BlockSpec((ls, 128), lambda b, s: (b * ns + s, COL_SMALL // 128)),
            full((CONV_K, SSM_XBC)), full((1, SSM_XBC)), full((1, 128)), full((1, 128)),
            full((1, GROUP_W)), full((1, GROUP_W)), full((CHUNK, CHUNK)),
        ],
        out_specs=pl.BlockSpec((ls, GROUP_W), lambda b, s: (b * ns + s, 0)),
        out_shape=jax.ShapeDtypeStruct((t, GROUP_W), F32),
        scratch_shapes=[pltpu.VMEM((ls + 8, SSM_XBC), F32), pltpu.VMEM((2, SSM_STATE, 128), F32)],
        compiler_params=_cparams(("arbitrary", "arbitrary")),
        name="ssd",
    )(main, main, cw, cb, dtb, arow, dexp, ng, tri)


def _mlstm_body(c_ref, sm_ref, cw_ref, cb_ref, igb_ref, fgb_ref, ng_ref, tri_ref, seg_ref, segm_ref,
                bmask_ref, o_ref, xpad, cst, nst, mst, *, ls):
    first = pl.program_id(1) == 0

    @pl.when(first)
    def _():
        cst[...] = jnp.zeros(cst.shape, F32)
        nst[...] = jnp.zeros(nst.shape, F32)
        mst[...] = jnp.zeros(mst.shape, F32)

    qk = _conv_silu(xpad, c_ref[:, 0:2 * GROUP_W], cw_ref, cb_ref, first, ls)
    sm = sm_ref[...]
    ig_all = sm + igb_ref[...]
    fx = sm + fgb_ref[...]
    lf_all = jnp.minimum(fx, 0.0) - jnp.log1p(jnp.exp(-jnp.abs(fx)))
    lane = lax.broadcasted_iota(jnp.int32, (1, GROUP_W), 1)
    hms = [jnp.logical_and(lane >= h * HD, lane < (h + 1) * HD) for h in range(HEADS)]
    tril = _tril(CHUNK)
    segb = seg_ref[...].astype(BF16)
    segm_b = segm_ref[...].astype(BF16)
    n_chunks = ls // CHUNK

    local = []
    for c in range(n_chunks):
        r0 = c * CHUNK
        q = qk[r0:r0 + CHUNK, 0:GROUP_W]
        k = qk[r0:r0 + CHUNK, GROUP_W:2 * GROUP_W] * (HD ** -0.5)
        v = c_ref[r0:r0 + CHUNK, 2 * GROUP_W:3 * GROUP_W]
        ig = ig_all[r0:r0 + CHUNK]
        bcs = jnp.dot(tri_ref[...], lf_all[r0:r0 + CHUNK], precision=HIGHEST,
                      preferred_element_type=F32)
        bcs_t = bcs.T
        ig_t = ig.T
        kb = k.astype(BF16)
        num_loc = jnp.zeros((CHUNK, GROUP_W), F32)
        w_loc_x = jnp.zeros((CHUNK, GROUP_W), F32)
        heads = []
        for h in range(HEADS):
            bcol = bcs[:, 8 + h:9 + h]
            dmat = jnp.where(tril, bcol - bcs_t[8 + h:9 + h, :] + ig_t[4 + h:5 + h, :], -jnp.inf)
            m_loc = jnp.max(dmat, axis=1, keepdims=True)
            s_qk = lax.dot_general(jnp.where(hms[h], q, 0.0).astype(BF16), kb, NT_DIMS,
                                   preferred_element_type=F32)
            sc = s_qk * jnp.exp(dmat - m_loc)
            den_loc = jnp.sum(sc, axis=1, keepdims=True)
            num_loc = num_loc + jnp.dot(sc.astype(BF16), jnp.where(hms[h], v, 0.0).astype(BF16),
                                        preferred_element_type=F32)
            blast = bcs[CHUNK - 1:CHUNK, 8 + h:9 + h]
            gcol = blast - bcol + ig[:, 4 + h:5 + h]
            gmax = jnp.max(gcol, axis=0, keepdims=True)
            w_loc_x = w_loc_x + jnp.where(hms[h], jnp.exp(gcol - gmax), 0.0)
            heads.append((bcol, m_loc, den_loc, blast, gmax))
        kw = k * w_loc_x
        kwv = bmask_ref[...] * jnp.dot(kw.T.astype(BF16), v.astype(BF16), preferred_element_type=F32)
        local.append((q, num_loc, heads, kwv, jnp.sum(kw, axis=0, keepdims=True)))

    houts = []
    for c in range(n_chunks):
        q, num_loc, heads, kwv, ksum = local[c]
        c_old = cst[...]
        n_old = nst[0:1, :]
        q_c = jnp.dot(q.astype(BF16), c_old.astype(BF16), preferred_element_type=F32)
        q_n = jnp.dot((q * n_old).astype(BF16), segb, preferred_element_type=F32)
        f_x = jnp.zeros((CHUNK, GROUP_W), F32)
        w_inter_x = jnp.zeros((CHUNK, GROUP_W), F32)
        inv_den_x = jnp.zeros((CHUNK, GROUP_W), F32)
        keep_x = jnp.zeros((1, GROUP_W), F32)
        sfac_x = jnp.zeros((1, GROUP_W), F32)
        for h in range(HEADS):
            bcol, m_loc, den_loc, blast, gmax = heads[h]
            m_prev = mst[h:h + 1, 0:1]
            inter = bcol + m_prev
            m_t = jnp.maximum(inter, m_loc)
            f = jnp.exp(m_loc - m_t)
            w_inter = jnp.exp(inter - m_t)
            den = f * den_loc + w_inter * q_n[:, h:h + 1]
            inv_den = 1.0 / jnp.maximum(jnp.abs(den), jnp.exp(-m_t))
            f_x = f_x + jnp.where(hms[h], f, 0.0)
            w_inter_x = w_inter_x + jnp.where(hms[h], w_inter, 0.0)
            inv_den_x = inv_den_x + jnp.where(hms[h], inv_den, 0.0)
            m_new = jnp.maximum(blast + m_prev, gmax)
            sfac_x = sfac_x + jnp.where(hms[h], jnp.exp(gmax - m_new), 0.0)
            keep_x = keep_x + jnp.where(hms[h], jnp.exp(blast + m_prev - m_new), 0.0)
            mst[h:h + 1, :] = jnp.broadcast_to(m_new, (1, 128))
        cst[...] = keep_x * c_old + sfac_x * kwv
        nst[0:1, :] = keep_x * n_old + sfac_x * ksum
        houts.append((f_x * num_loc + w_inter_x * q_c) * inv_den_x)

    for c in range(n_chunks):
        r0 = c * CHUNK
        hh = houts[c] * _sigmoid(c_ref[r0:r0 + CHUNK, 3 * GROUP_W:4 * GROUP_W])
        sq = hh * hh
        sq_hi = sq.astype(BF16)
        sq_lo = (sq - sq_hi.astype(F32)).astype(BF16)
        ms = (jnp.dot(sq_hi, segm_b, preferred_element_type=F32)
              + jnp.dot(sq_lo, segm_b, preferred_element_type=F32))
        o_ref[r0:r0 + CHUNK, :] = hh * lax.rsqrt(ms + EPS) * ng_ref[...]


def _mlstm(main, cw, cb, igb, fgb, ng, tri, seg, segm, bmask, bsz, seq, ls=512):
    t = main.shape[0]
    ns = seq // ls

    def full(shape):
        return pl.BlockSpec(shape, lambda b, s: tuple(0 for _ in shape))

    return pl.pallas_call(
        functools.partial(_mlstm_body, ls=ls),
        grid=(bsz, ns),
        in_specs=[
            pl.BlockSpec((ls, 1024), lambda b, s: (b * ns + s, COL_ML // 1024)),
            pl.BlockSpec((ls, 128), lambda b, s: (b * ns + s, COL_SMALL // 128)),
            full((CONV_K, 2 * GROUP_W)), full((1, 2 * GROUP_W)), full((1, 128)), full((1, 128)),
            full((1, GROUP_W)), full((CHUNK, CHUNK)), full((GROUP_W, 128)), full((GROUP_W, GROUP_W)),
            full((GROUP_W, GROUP_W)),
        ],
        out_specs=pl.BlockSpec((ls, GROUP_W), lambda b, s: (b * ns + s, 0)),
        out_shape=jax.ShapeDtypeStruct((t, GROUP_W), F32),
        scratch_shapes=[
            pltpu.VMEM((ls + 8, 2 * GROUP_W), F32),
            pltpu.VMEM((GROUP_W, GROUP_W), F32),
            pltpu.VMEM((8, GROUP_W), F32),
            pltpu.VMEM((8, 128), F32),
        ],
        compiler_params=_cparams(("arbitrary", "arbitrary")),
        name="mlstm",
    )(main, main, cw, cb, igb, fgb, ng, tri, seg, segm, bmask)


def _outproj_body(x_ref, ya_ref, yb_ref, yc_ref, yd_ref, wo_ref, g2_ref, wr_ref, br_ref,
                  xn_ref, h2_ref, cb_ref):
    acc = x_ref[...]
    for gi, y_ref in enumerate((ya_ref, yb_ref, yc_ref, yd_ref)):
        acc = acc + jnp.dot(y_ref[...].astype(BF16), wo_ref[gi * GROUP_W:(gi + 1) * GROUP_W, :],
                            preferred_element_type=F32)
    xn_ref[...] = acc
    h2 = _rms(acc, g2_ref[...])
    h_hi = h2.astype(BF16)
    h2_ref[...] = h_hi
    h_lo = (h2 - h_hi.astype(F32)).astype(BF16)
    hh = jnp.dot(h_hi, wr_ref[...], preferred_element_type=F32)
    lh = jnp.dot(h_lo, wr_ref[...], preferred_element_type=F32)
    logits = hh[:, :128] + hh[:, 128:] + lh[:, :128] + br_ref[...]
    lane = lax.broadcasted_iota(jnp.int32, logits.shape, 1)
    big = jnp.int32(1 << 20)
    n_grp = N_EXPERTS // EXPERTS_PER_GROUP
    gl = jnp.where(jnp.logical_and(lane >= N_EXPERTS, lane < N_EXPERTS + n_grp), logits, -jnp.inf)
    gmax = jnp.max(gl, axis=1, keepdims=True)
    g_w = 1.0 / jnp.sum(jnp.exp(gl - gmax), axis=1, keepdims=True)
    gidx = jnp.min(jnp.where(gl == gmax, lane, big), axis=1, keepdims=True) - N_EXPERTS
    in_grp = jnp.logical_and(lane < N_EXPERTS,
                             lax.shift_right_logical(lane, int(math.log2(EXPERTS_PER_GROUP))) == gidx)
    el = jnp.where(in_grp, logits, -jnp.inf)
    emax = jnp.max(el, axis=1, keepdims=True)
    esum = jnp.sum(jnp.exp(el - emax), axis=1, keepdims=True)
    i1 = jnp.min(jnp.where(el == emax, lane, big), axis=1, keepdims=True)
    el2 = jnp.where(lane == i1, -jnp.inf, el)
    emax2 = jnp.max(el2, axis=1, keepdims=True)
    i2 = jnp.min(jnp.where(el2 == emax2, lane, big), axis=1, keepdims=True)
    p1 = 1.0 / esum
    p2 = jnp.exp(emax2 - emax) / esum
    w1 = p1 / (p1 + p2)
    w2 = p2 / (p1 + p2)
    cb_ref[...] = g_w * (jnp.where(lane == i1, w1, 0.0) + jnp.where(lane == i2, w2, 0.0))


def _outproj(x2, ya, yb, yc, yd, wo, g2, wr, br, tm=512):
    t = x2.shape[0]
    row = pl.BlockSpec((tm, D_MODEL), lambda i: (i, 0))
    grp = pl.BlockSpec((tm, GROUP_W), lambda i: (i, 0))

    def full(shape):
        return pl.BlockSpec(shape, lambda i: tuple(0 for _ in shape))

    return pl.pallas_call(
        _outproj_body,
        grid=(t // tm,),
        in_specs=[row, grp, grp, grp, grp, full((D_MODEL, D_MODEL)), full((1, D_MODEL)),
                  full((D_MODEL, 256)), full((1, 128))],
        out_specs=[row, row, pl.BlockSpec((tm, 128), lambda i: (i, 0))],
        out_shape=[
            jax.ShapeDtypeStruct((t, D_MODEL), F32),
            jax.ShapeDtypeStruct((t, D_MODEL), BF16),
            jax.ShapeDtypeStruct((t, 128), F32),
        ],
        compiler_params=_cparams(("arbitrary",)),
        name="outproj_router",
    )(x2, ya, yb, yc, yd, wo, g2, wr, br)


MOE_EB = 4


def _moe_body(h_ref, cb_ref, xn_ref, wgu_ref, wd_ref, gf_ref, o_ref, acc, *, final_norm):
    eb = pl.program_id(1)

    @pl.when(eb == 0)
    def _():
        acc[...] = jnp.zeros(acc.shape, F32)

    h = h_ref[...]
    lane = lax.broadcasted_iota(jnp.int32, cb_ref.shape, 1)

    def gate_up(j):
        return jnp.dot(h, wgu_ref[j], preferred_element_type=F32)

    def down(j, gu):
        g = gu[:, :D_EXPERT]
        u = gu[:, D_EXPERT:]
        col = jnp.sum(jnp.where(lane == eb * MOE_EB + j, cb_ref[...], 0.0), axis=1,
                      keepdims=True)
        hid = (g * _sigmoid(g)) * u * col
        return jnp.dot(hid.astype(BF16), wd_ref[j], preferred_element_type=F32)

    total = jnp.zeros(acc.shape, F32)
    gu = gate_up(0)
    for j in range(MOE_EB):
        gu_next = gate_up(j + 1) if j + 1 < MOE_EB else None
        total = total + down(j, gu)
        gu = gu_next
    acc[...] += total

    @pl.when(eb == N_EXPERTS // MOE_EB - 1)
    def _():
        y = xn_ref[...] + acc[...]
        if final_norm:
            y = _rms(y, gf_ref[...])
        o_ref[...] = y


def _moe(h2, comb, xn, wgu, wd, gf, final_norm, tm=1024):
    t = h2.shape[0]
    return pl.pallas_call(
        functools.partial(_moe_body, final_norm=final_norm),
        grid=(t // tm, N_EXPERTS // MOE_EB),
        in_specs=[
            pl.BlockSpec((tm, D_MODEL), lambda i, e: (i, 0)),
            pl.BlockSpec((tm, 128), lambda i, e: (i, 0)),
            pl.BlockSpec((tm, D_MODEL), lambda i, e: (i, 0)),
            pl.BlockSpec((MOE_EB, D_MODEL, 2 * D_EXPERT), lambda i, e: (e, 0, 0)),
            pl.BlockSpec((MOE_EB, D_EXPERT, D_MODEL), lambda i, e: (e, 0, 0)),
            pl.BlockSpec((1, D_MODEL), lambda i, e: (0, 0)),
        ],
        out_specs=pl.BlockSpec((tm, D_MODEL), lambda i, e: (i, 0)),
        out_shape=jax.ShapeDtypeStruct((t, D_MODEL), F32),
        scratch_shapes=[pltpu.VMEM((tm, D_MODEL), F32)],
        compiler_params=_cparams(("arbitrary", "arbitrary")),
        name="moe",
    )(h2, comb, xn, wgu, wd, gf)


def _spread_heads(w, width, slab, total):
    rows = w.shape[0]
    pieces = []
    for h in range(HEADS):
        pieces.append(w[:, h * width:(h + 1) * width])
        pieces.append(jnp.zeros((rows, slab - width), w.dtype))
    if total > HEADS * slab:
        pieces.append(jnp.zeros((rows, total - HEADS * slab), w.dtype))
    return jnp.concatenate(pieces, axis=1)


def _pad_heads(w, width):
    return _spread_heads(w, width, HP, HEADS * HP)


def _v_cols(w):
    return _spread_heads(w, HD, VR, V_COLS)


def _lane_row(vals, offset, width=128):
    return jnp.zeros((1, width), F32).at[0, offset:offset + vals.shape[0]].set(vals.astype(F32))


def _split_w_in(w):
    sizes = (3 * GROUP_W, GROUP_W, SSM_XBC, HEADS, 2 * GROUP_W, GROUP_W, HEADS, HEADS, GROUP_W,
             256, 128, MLA_ROPE)
    offs = np.cumsum((0,) + sizes)
    return [w[:, int(offs[i]):int(offs[i + 1])] for i in range(len(sizes))]


def _layer_weights(l, p):
    a_qkv, s_z, s_xbc, s_dt, m_qk, m_v, m_i, m_f, m_o, d_cq, d_ckv, d_kr = _split_w_in(p["w_in_bf16"][l])
    aq, ak, av = a_qkv[:, :GROUP_W], a_qkv[:, GROUP_W:2 * GROUP_W], a_qkv[:, 2 * GROUP_W:]
    half = MLA_ROPE // 2
    small = jnp.concatenate([s_dt, m_i, m_f, jnp.zeros((D_MODEL, 64 - 12), BF16),
                             d_kr, d_kr[:, half:], d_kr[:, :half]], axis=1)
    w_main = jnp.concatenate([s_z, s_xbc, m_qk, m_v, m_o, d_cq, d_ckv, small, _pad_heads(ak, HD)], axis=1)
    w = {"w_main": w_main,
         "w_qv": jnp.concatenate([_pad_heads(aq, HD), _v_cols(av)], axis=1),
         "g_mix": p["norm_mix_g"][l][None, :]}
    w["ssm_cw"] = p["ssm_conv_w"][l]
    w["ssm_cb"] = p["ssm_conv_b"][l][None, :]
    w["ssm_dtb"] = _lane_row(p["ssm_dt_bias"][l], 0)
    w["ssm_arow"] = _lane_row(-jnp.exp(p["ssm_a_log"][l].astype(F32)), 0)
    w["ssm_dexp"] = jnp.repeat(p["ssm_d"][l].astype(F32), HD)[None, :]
    w["ssm_ng"] = p["ssm_norm_g"][l][None, :]
    w["ml_cw"] = p["ml_conv_w"][l]
    w["ml_cb"] = p["ml_conv_b"][l][None, :]
    w["ml_igb"] = _lane_row(p["ml_ig_bias"][l], 4)
    w["ml_fgb"] = _lane_row(p["ml_fg_bias"][l], 8)
    w["ml_ng"] = p["ml_norm_g"][l][None, :]
    wuq = p["mla_w_uq"][l]
    qd = MLA_NOPE + MLA_ROPE
    swapped = []
    for h in range(HEADS):
        c0 = h * qd + MLA_NOPE
        swapped += [jnp.zeros((256, MLA_NOPE), F32), wuq[:, c0 + half:c0 + 2 * half], wuq[:, c0:c0 + half]]
    wq = _spread_heads(wuq, qd, HP, HEADS * HP)
    wqs = _spread_heads(jnp.concatenate(swapped, axis=1), qd, HP, HEADS * HP)
    w["mla_wq2"] = jnp.concatenate([wq, wqs], axis=1).astype(BF16)
    wukv = p["mla_w_ukv"][l]
    wk_cols = jnp.concatenate([wukv[:, h * 2 * HD:h * 2 * HD + MLA_NOPE] for h in range(HEADS)], axis=1)
    wv_cols = jnp.concatenate([wukv[:, h * 2 * HD + MLA_NOPE:(h + 1) * 2 * HD] for h in range(HEADS)], axis=1)
    w["mla_wk"] = _pad_heads(wk_cols, MLA_NOPE).astype(BF16)
    w["mla_wv"] = _v_cols(wv_cols).astype(BF16)
    w["mla_gq"] = p["mla_q_norm_g"][l][None, :]
    w["mla_gkv"] = p["mla_kv_norm_g"][l][None, :]
    w["w_out"] = p["w_out"][l].astype(BF16)
    w["g_ffn"] = p["norm_ffn_g"][l][None, :]
    wr = jnp.zeros((D_MODEL, 128), F32)
    wr = wr.at[:, :N_EXPERTS].set(p["moe_w_router"][l]).at[:, N_EXPERTS:N_EXPERTS + 4].set(p["moe_w_group"][l])
    wr_hi = wr.astype(BF16)
    w["w_router"] = jnp.concatenate([wr_hi, (wr - wr_hi.astype(F32)).astype(BF16)], axis=1)
    w["b_router"] = _lane_row(p["moe_b_router"][l], 0) + _lane_row(p["moe_b_group"][l], N_EXPERTS)
    w["w_gu"] = jnp.concatenate([p["moe_w_gate"][l], p["moe_w_up"][l]], axis=2).astype(BF16)
    w["w_down"] = p["moe_w_down"][l].astype(BF16)
    return w


def kernel(x, positions, norm_mix_g, w_in, ssm_conv_w, ssm_conv_b, ssm_dt_bias, ssm_a_log, ssm_d, ssm_norm_g, ml_conv_w, ml_conv_b, ml_ig_bias, ml_fg_bias, ml_norm_g, mla_q_norm_g, mla_w_uq, mla_kv_norm_g, mla_w_ukv, w_out, norm_ffn_g, moe_w_group, moe_b_group, moe_w_router, moe_b_router, moe_w_gate, moe_w_up, moe_w_down, final_norm_g):
    p = dict(norm_mix_g=norm_mix_g, w_in_bf16=_cast_bf16(w_in), ssm_conv_w=ssm_conv_w, ssm_conv_b=ssm_conv_b,
             ssm_dt_bias=ssm_dt_bias, ssm_a_log=ssm_a_log, ssm_d=ssm_d, ssm_norm_g=ssm_norm_g,
             ml_conv_w=ml_conv_w, ml_conv_b=ml_conv_b, ml_ig_bias=ml_ig_bias, ml_fg_bias=ml_fg_bias,
             ml_norm_g=ml_norm_g, mla_q_norm_g=mla_q_norm_g, mla_w_uq=mla_w_uq,
             mla_kv_norm_g=mla_kv_norm_g, mla_w_ukv=mla_w_ukv, w_out=w_out, norm_ffn_g=norm_ffn_g,
             moe_w_group=moe_w_group, moe_b_group=moe_b_group, moe_w_router=moe_w_router,
             moe_b_router=moe_b_router, moe_w_gate=moe_w_gate, moe_w_up=moe_w_up, moe_w_down=moe_w_down)
    bsz, seq, _ = x.shape
    depth = w_in.shape[0]
    nb = seq // MOBA_BLOCK
    x2 = x.reshape(bsz * seq, D_MODEL)

    half = MLA_ROPE // 2
    inv = ROPE_THETA ** (-jnp.arange(0, MLA_ROPE, 2, dtype=F32) / MLA_ROPE)
    inv_full = _lane_row(jnp.concatenate([inv, inv]), MLA_NOPE)
    sgn_full = _lane_row(jnp.concatenate([-jnp.ones((half,), F32), jnp.ones((half,), F32)]), MLA_NOPE)
    pe_mask = _lane_row(jnp.ones((MLA_ROPE,), F32), MLA_NOPE)
    c_tok, s_tok, c_tr, s_tr = _rope_tables(positions.reshape(bsz * seq, 1), inv_full, sgn_full, bsz, seq)
    tri = jnp.tril(jnp.ones((CHUNK, CHUNK), F32))
    head_of = jnp.arange(GROUP_W) // HD
    seg = (head_of[:, None] == jnp.arange(128)[None, :]).astype(F32)
    bmask = (head_of[:, None] == head_of[None, :]).astype(F32)
    segm = bmask / HD

    for l in range(depth):
        w = _layer_weights(l, p)
        main, qT_a, vT_a = _in_proj(x2, w["g_mix"], w["w_main"], w["w_qv"], bsz, seq)
        kmean, k_a = _kmean(main, bsz, seq)
        qTb_a = _moba_select(qT_a, kmean.reshape(bsz, nb, HEADS * HP), bsz, seq)
        y_a = _attention(qTb_a, k_a, vT_a, bsz, seq, "attn_moba")
        y_b = _ssd(main, w["ssm_cw"], w["ssm_cb"], w["ssm_dtb"], w["ssm_arow"], w["ssm_dexp"], w["ssm_ng"],
                   tri, bsz, seq)
        y_c = _mlstm(main, w["ml_cw"], w["ml_cb"], w["ml_igb"], w["ml_fgb"], w["ml_ng"], tri, seg, segm,
                     bmask, bsz, seq)
        qT_d, k_d, vT_d = _mla_prep(main, c_tok, s_tok, c_tr, s_tr, w["mla_gq"], w["mla_wq2"], w["mla_gkv"],
                                    w["mla_wk"], w["mla_wv"], pe_mask, bsz, seq)
        y_d = _attention(qT_d, k_d, vT_d, bsz, seq, "attn_mla")
        xn, h2, comb = _outproj(x2, y_a, y_b, y_c, y_d, w["w_out"], w["g_ffn"], w["w_router"], w["b_router"])
        x2 = _moe(h2, comb, xn, w["w_gu"], w["w_down"], final_norm_g[None, :], final_norm=(l == depth - 1))
    return x2.reshape(bsz, seq, D_MODEL)
```

```python
import functools
import math

import numpy as np
import jax
import jax.numpy as jnp
from jax import lax
from jax.experimental import pallas as pl
from jax.experimental.pallas import tpu as pltpu

F32 = jnp.float32
BF16 = jnp.bfloat16
HIGHEST = lax.Precision.HIGHEST

EPS = 1e-6
NEG = -1e30
LOG2E = 1.4426950408889634

D_MODEL = 1024
GROUP_W = 256
HEADS = 4
HD = 64
HP = 128
MOBA_BLOCK = 256
MOBA_TOPK = 3
SSM_STATE = 128
SSM_XBC = 768
CONV_K = 4
CHUNK = 128
MLA_NOPE = 64
MLA_ROPE = 32
ROPE_THETA = 10000.0
N_EXPERTS = 16
EXPERTS_PER_GROUP = 4
D_EXPERT = 256
GID_LANE = 20
MOE_TM = 1024

COL_ZX = 0
COL_ML = 1024
COL_CQ = 2048
COL_CKV = 2304
COL_SMALL = 2432
COL_KA = 2560
N_MAIN = 3072
VR = 80
V_COLS = 384
ATT_UQ = 256
ATT_UK = 256

V7X_VMEM_LIMIT = 56 * 1024 * 1024

NT_DIMS = (((1,), (1,)), ((), ()))


def _cparams(sem):
    return pltpu.CompilerParams(dimension_semantics=sem, vmem_limit_bytes=V7X_VMEM_LIMIT)


def _rms(x, g):
    ms = jnp.mean(x * x, axis=-1, keepdims=True)
    return x * lax.rsqrt(ms + EPS) * g


def _sigmoid(x):
    return 0.5 * jnp.tanh(0.5 * x) + 0.5


def _softplus(x):
    return jnp.maximum(x, 0.0) + jnp.log1p(jnp.exp(-jnp.abs(x)))


def _tril(n):
    r = lax.broadcasted_iota(jnp.int32, (n, n), 0)
    c = lax.broadcasted_iota(jnp.int32, (n, n), 1)
    return c <= r


def _cast_body(w_ref, o_ref):
    o_ref[...] = w_ref[...].astype(BF16)


def _cast_bf16(w, rows=256):
    nl, r, c = w.shape
    return pl.pallas_call(
        _cast_body,
        grid=(nl, r // rows),
        in_specs=[pl.BlockSpec((1, rows, c), lambda l, i: (l, i, 0))],
        out_specs=pl.BlockSpec((1, rows, c), lambda l, i: (l, i, 0)),
        out_shape=jax.ShapeDtypeStruct(w.shape, BF16),
        compiler_params=_cparams(("arbitrary", "arbitrary")),
        name="cast_w_in",
    )(w)


def _ones_rows(shape):
    row = lax.broadcasted_iota(jnp.int32, shape, 0)
    hit = row == HD
    for h in range(1, HEADS):
        hit = jnp.logical_or(hit, row == h * VR + HD)
    return jnp.where(hit, 1.0, 0.0)


def _in_proj_body(x_ref, g_ref, wm_ref, wqv_ref, om_ref, oq_ref, ov_ref):
    h = _rms(x_ref[...], g_ref[...]).astype(BF16)
    n = om_ref.shape[1]
    for c in range(0, n, 512):
        om_ref[:, c:c + 512] = jnp.dot(h, wm_ref[:, c:c + 512], preferred_element_type=F32)
    qv = jnp.dot(h, wqv_ref[...], preferred_element_type=F32)
    nq = HEADS * HP
    oq_ref[0] = qv[:, :nq].T
    vt = qv[:, nq:].T[:HEADS * VR]
    ov_ref[0] = (vt + _ones_rows(vt.shape)).astype(BF16)


def _in_proj(x2, g, w_main, w_qv, bsz, seq, tm=512):
    t = x2.shape[0]
    nst = seq // tm
    return pl.pallas_call(
        _in_proj_body,
        grid=(t // tm,),
        in_specs=[
            pl.BlockSpec((tm, D_MODEL), lambda i: (i, 0)),
            pl.BlockSpec((1, D_MODEL), lambda i: (0, 0)),
            pl.BlockSpec((D_MODEL, N_MAIN), lambda i: (0, 0)),
            pl.BlockSpec((D_MODEL, HEADS * HP + V_COLS), lambda i: (0, 0)),
        ],
        out_specs=[
            pl.BlockSpec((tm, N_MAIN), lambda i: (i, 0)),
            pl.BlockSpec((1, HEADS * HP, tm), lambda i: (i // nst, 0, i % nst)),
            pl.BlockSpec((1, HEADS * VR, tm), lambda i: (i // nst, 0, i % nst)),
        ],
        out_shape=[
            jax.ShapeDtypeStruct((t, N_MAIN), F32),
            jax.ShapeDtypeStruct((bsz, HEADS * HP, seq), F32),
            jax.ShapeDtypeStruct((bsz, HEADS * VR, seq), BF16),
        ],
        compiler_params=_cparams(("arbitrary",)),
        name="in_proj",
    )(x2, g, w_main, w_qv)


def _kmean_body(k_ref, km_ref, kb_ref, *, nb):
    k = k_ref[...]
    km_ref[0] = jnp.mean(k, axis=0, keepdims=True)
    blk = pl.program_id(0) % nb
    lane = lax.broadcasted_iota(jnp.int32, k.shape, 1)
    kb_ref[...] = jnp.where((lane & (HP - 1)) == HD + blk, 1.0, k).astype(BF16)


def _kmean(main, bsz, seq):
    t = main.shape[0]
    nb = seq // MOBA_BLOCK
    assert nb <= HP - HD
    kcol = COL_KA // (HEADS * HP)
    return pl.pallas_call(
        functools.partial(_kmean_body, nb=nb),
        grid=(bsz * nb,),
        in_specs=[pl.BlockSpec((MOBA_BLOCK, HEADS * HP), lambda i: (i, kcol))],
        out_specs=[
            pl.BlockSpec((1, 1, HEADS * HP), lambda i: (i, 0, 0)),
            pl.BlockSpec((MOBA_BLOCK, HEADS * HP), lambda i: (i, 0)),
        ],
        out_shape=[
            jax.ShapeDtypeStruct((bsz * nb, 1, HEADS * HP), F32),
            jax.ShapeDtypeStruct((t, HEADS * HP), BF16),
        ],
        compiler_params=_cparams(("arbitrary",)),
        name="moba_kmean",
    )(main)


def _moba_select_body(qT_ref, km_ref, qb_ref, *, ts, nb, qscale):
    s_idx = pl.program_id(1)
    qT = qT_ref[0]
    km = km_ref[0]
    pos = s_idx * ts + lax.broadcasted_iota(jnp.int32, (nb, ts), 1)
    cur = lax.shift_right_logical(pos, int(math.log2(MOBA_BLOCK)))
    blk = lax.broadcasted_iota(jnp.int32, (nb, ts), 0)
    past = blk < cur
    for h in range(HEADS):
        qh = qT[h * HP:(h + 1) * HP, :]
        g = jnp.dot(km[:, h * HP:(h + 1) * HP], qh, precision=HIGHEST,
                    preferred_element_type=F32)
        g = jnp.where(past, g, -jnp.inf)
        bias = jnp.where(blk == cur, 0.0, NEG)
        for _ in range(MOBA_TOPK):
            m = jnp.max(g, axis=0, keepdims=True)
            cand = jnp.logical_and(g == m, g > -jnp.inf)
            idx = jnp.min(jnp.where(cand, blk, nb), axis=0, keepdims=True)
            pick = blk == idx
            bias = jnp.where(pick, 0.0, bias)
            g = jnp.where(pick, -jnp.inf, g)
        rows = [qh[:HD] * qscale, bias]
        if HP - HD - nb:
            rows.append(jnp.zeros((HP - HD - nb, ts), F32))
        qb_ref[0, h * HP:(h + 1) * HP, :] = jnp.concatenate(rows, axis=0).astype(BF16)


def _moba_select(qT, kmean, bsz, seq, ts=512):
    nb = seq // MOBA_BLOCK
    qscale = (HD ** -0.5) * LOG2E
    return pl.pallas_call(
        functools.partial(_moba_select_body, ts=ts, nb=nb, qscale=qscale),
        grid=(bsz, seq // ts),
        in_specs=[
            pl.BlockSpec((1, HEADS * HP, ts), lambda b, s: (b, 0, s)),
            pl.BlockSpec((1, nb, HEADS * HP), lambda b, s: (b, 0, 0)),
        ],
        out_specs=pl.BlockSpec((1, HEADS * HP, ts), lambda b, s: (b, 0, s)),
        out_shape=jax.ShapeDtypeStruct((bsz, HEADS * HP, seq), BF16),
        compiler_params=_cparams(("arbitrary", "arbitrary")),
        name="moba_select",
    )(qT, kmean)


def _rope_body(pos_ref, inv_ref, sgn_ref, c_ref, s_ref, ct_ref, st_ref):
    ang = pos_ref[...].astype(F32) * inv_ref[...]
    c = jnp.cos(ang)
    s = jnp.sin(ang) * sgn_ref[...]
    c_ref[...] = c
    s_ref[...] = s
    ct_ref[0] = c.T
    st_ref[0] = s.T


def _rope_tables(pos2, inv_full, sgn_full, bsz, seq, ts=512):
    t = pos2.shape[0]
    nst = seq // ts
    tok = pl.BlockSpec((ts, HP), lambda i: (i, 0))
    tr = pl.BlockSpec((1, HP, ts), lambda i: (i // nst, 0, i % nst))
    row = pl.BlockSpec((1, HP), lambda i: (0, 0))
    return pl.pallas_call(
        _rope_body,
        grid=(t // ts,),
        in_specs=[pl.BlockSpec((ts, 1), lambda i: (i, 0)), row, row],
        out_specs=[tok, tok, tr, tr],
        out_shape=[
            jax.ShapeDtypeStruct((t, HP), F32),
            jax.ShapeDtypeStruct((t, HP), F32),
            jax.ShapeDtypeStruct((bsz, HP, seq), F32),
            jax.ShapeDtypeStruct((bsz, HP, seq), F32),
        ],
        compiler_params=_cparams(("arbitrary",)),
        name="rope_tables",
    )(pos2, inv_full, sgn_full)


def _mla_prep_body(cq_ref, ckv_ref, sm_ref, c_ref, s_ref, ct_ref, st_ref, gq_ref, wq_ref, gkv_ref,
                   wk_ref, wv_ref, msk_ref, qo_ref, ko_ref, vo_ref, *, qscale):
    qn = _rms(cq_ref[...], gq_ref[...]).astype(BF16)
    q2 = jnp.dot(qn, wq_ref[...], preferred_element_type=F32).T
    nq = HEADS * HP
    ct4 = jnp.concatenate([ct_ref[0]] * HEADS, axis=0)
    st4 = jnp.concatenate([st_ref[0]] * HEADS, axis=0)
    q = (q2[:nq] * ct4 + q2[nq:] * st4) * qscale
    qo_ref[0] = q.astype(BF16)
    kvn = _rms(ckv_ref[...], gkv_ref[...]).astype(BF16)
    kn = jnp.dot(kvn, wk_ref[...], preferred_element_type=F32)
    sm = sm_ref[...]
    pe = (sm * c_ref[...] + pltpu.roll(sm, HP - MLA_ROPE, 1) * s_ref[...]) * msk_ref[...]
    ko_ref[...] = (kn + jnp.concatenate([pe] * HEADS, axis=1)).astype(BF16)
    vt = jnp.dot(kvn, wv_ref[...], preferred_element_type=F32).T[:HEADS * VR]
    vo_ref[0] = (vt + _ones_rows(vt.shape)).astype(BF16)


def _mla_prep(main, c_tok, s_tok, c_tr, s_tr, gq, wq2T, gkv, wk, wvT, pe_mask, bsz, seq, ts=512):
    t = main.shape[0]
    nst = seq // ts
    qscale = ((MLA_NOPE + MLA_ROPE) ** -0.5) * LOG2E
    tok128 = pl.BlockSpec((ts, HP), lambda i: (i, 0))
    tr128 = pl.BlockSpec((1, HP, ts), lambda i: (i // nst, 0, i % nst))

    def full(shape):
        return pl.BlockSpec(shape, lambda i: tuple(0 for _ in shape))

    return pl.pallas_call(
        functools.partial(_mla_prep_body, qscale=qscale),
        grid=(t // ts,),
        in_specs=[
            pl.BlockSpec((ts, 256), lambda i: (i, COL_CQ // 256)),
            pl.BlockSpec((ts, 128), lambda i: (i, COL_CKV // 128)),
            pl.BlockSpec((ts, 128), lambda i: (i, COL_SMALL // 128)),
            tok128, tok128, tr128, tr128,
            full((1, 256)), full((256, 2 * HEADS * HP)), full((1, 128)),
            full((128, HEADS * HP)), full((128, V_COLS)), full((1, HP)),
        ],
        out_specs=[
            pl.BlockSpec((1, HEADS * HP, ts), lambda i: (i // nst, 0, i % nst)),
            pl.BlockSpec((ts, HEADS * HP), lambda i: (i, 0)),
            pl.BlockSpec((1, HEADS * VR, ts), lambda i: (i // nst, 0, i % nst)),
        ],
        out_shape=[
            jax.ShapeDtypeStruct((bsz, HEADS * HP, seq), BF16),
            jax.ShapeDtypeStruct((t, HEADS * HP), BF16),
            jax.ShapeDtypeStruct((bsz, HEADS * VR, seq), BF16),
        ],
        compiler_params=_cparams(("arbitrary",)),
        name="mla_prep",
    )(main, main, main, c_tok, s_tok, c_tr, s_tr, gq, wq2T, gkv, wk, wvT, pe_mask)


def _attn_body(ii_ref, jj_ref, qT_ref, k_ref, vT_ref, o_ref, m_sc, acc_sc, *, tq, tk):
    p = pl.program_id(1)
    i = ii_ref[p]
    j = jj_ref[p]

    @pl.when(j == 0)
    def _():
        m_sc[...] = jnp.full(m_sc.shape, NEG, F32)
        acc_sc[...] = jnp.zeros(acc_sc.shape, F32)

    def scores(u):
        h, qc, kb, _ = u
        return jnp.dot(k_ref[kb * ATT_UK:(kb + 1) * ATT_UK, h * HP:(h + 1) * HP],
                       qT_ref[0, h * HP:(h + 1) * HP, qc * ATT_UQ:(qc + 1) * ATT_UQ],
                       preferred_element_type=F32)

    def update(u, s):
        h, qc, kb, masked = u
        if masked:
            kpos = kb * ATT_UK + lax.broadcasted_iota(jnp.int32, s.shape, 0)
            qpos = qc * ATT_UQ + lax.broadcasted_iota(jnp.int32, s.shape, 1)
            s = jnp.where(kpos <= qpos, s, NEG)
        alphas, pms = [], []
        for c in range(ATT_UQ // 128):
            ql = slice(qc * ATT_UQ + c * 128, qc * ATT_UQ + (c + 1) * 128)
            sc = s[:, c * 128:(c + 1) * 128]
            m_old = m_sc[h:h + 1, ql]
            m_new = jnp.maximum(m_old, jnp.max(sc, axis=0, keepdims=True))
            alphas.append(jnp.exp2(m_old - m_new))
            pms.append(jnp.exp2(sc - m_new).astype(BF16))
            m_sc[h:h + 1, ql] = m_new
        ql = slice(qc * ATT_UQ, (qc + 1) * ATT_UQ)
        rows = slice(h * VR, (h + 1) * VR)
        acc_sc[rows, ql] = jnp.concatenate(alphas, axis=1) * acc_sc[rows, ql] + jnp.dot(
            vT_ref[0, rows, kb * ATT_UK:(kb + 1) * ATT_UK], jnp.concatenate(pms, axis=1),
            preferred_element_type=F32)

    def run(units):
        ahead = 8
        pending = [scores(u) for u in units[:ahead]]
        for n, u in enumerate(units):
            s = pending.pop(0)
            if n + ahead < len(units):
                pending.append(scores(units[n + ahead]))
            update(u, s)

    def unit_list(diag):
        units = []
        for kb in range(tk // ATT_UK):
            for h in range(HEADS):
                for qc in range(tq // ATT_UQ):
                    k_lo, k_hi = kb * ATT_UK, (kb + 1) * ATT_UK - 1
                    q_lo, q_hi = qc * ATT_UQ, (qc + 1) * ATT_UQ - 1
                    if diag and k_lo > q_hi:
                        continue
                    units.append((h, qc, kb, diag and k_hi > q_lo))
        return units

    @pl.when(j < i)
    def _():
        run(unit_list(False))

    @pl.when(j == i)
    def _():
        run(unit_list(True))
        parts = []
        for h in range(HEADS):
            den = acc_sc[h * VR + HD:h * VR + HD + 1, :]
            parts.append(acc_sc[h * VR:h * VR + HD, :] * (1.0 / den))
        o_ref[...] = jnp.concatenate(parts, axis=0).T


def _attention(qT, k, vT, bsz, seq, name, tq=1024):
    t = k.shape[0]
    nq = seq // tq
    ii = np.concatenate([np.full(i + 1, i, np.int32) for i in range(nq)])
    jj = np.concatenate([np.arange(i + 1, dtype=np.int32) for i in range(nq)])
    grid_spec = pltpu.PrefetchScalarGridSpec(
        num_scalar_prefetch=2,
        grid=(bsz, len(ii)),
        in_specs=[
            pl.BlockSpec((1, HEADS * HP, tq), lambda b, p, ii, jj: (b, 0, ii[p])),
            pl.BlockSpec((tq, HEADS * HP), lambda b, p, ii, jj: (b * nq + jj[p], 0)),
            pl.BlockSpec((1, HEADS * VR, tq), lambda b, p, ii, jj: (b, 0, jj[p])),
        ],
        out_specs=pl.BlockSpec((tq, GROUP_W), lambda b, p, ii, jj: (b * nq + ii[p], 0)),
        scratch_shapes=[
            pltpu.VMEM((8, tq), F32),
            pltpu.VMEM((HEADS * VR, tq), F32),
        ],
    )
    return pl.pallas_call(
        functools.partial(_attn_body, tq=tq, tk=tq),
        grid_spec=grid_spec,
        out_shape=jax.ShapeDtypeStruct((t, GROUP_W), F32),
        compiler_params=_cparams(("arbitrary", "arbitrary")),
        name=name,
    )(jnp.asarray(ii), jnp.asarray(jj), qT, k, vT)


def _conv_silu(xpad, src, cw_ref, cb_ref, first, ls):
    width = xpad.shape[1]

    @pl.when(first)
    def _():
        xpad[0:8, :] = jnp.zeros((8, width), F32)

    @pl.when(jnp.logical_not(first))
    def _():
        xpad[0:8, :] = xpad[ls:ls + 8, :]

    xpad[8:ls + 8, :] = src
    acc = cb_ref[...] + cw_ref[0:1, :] * xpad[pl.ds(8 - CONV_K + 1, ls), :]
    for kk in range(1, CONV_K):
        acc = acc + cw_ref[kk:kk + 1, :] * xpad[pl.ds(8 - CONV_K + 1 + kk, ls), :]
    return acc * _sigmoid(acc)


def _ssd_body(zx_ref, sm_ref, cw_ref, cb_ref, dtb_ref, arow_ref, dexp_ref, ng_ref, tri_ref, o_ref,
              xpad, state, *, ls):
    first = pl.program_id(1) == 0

    @pl.when(first)
    def _():
        state[...] = jnp.zeros(state.shape, F32)

    xc = _conv_silu(xpad, zx_ref[:, GROUP_W:GROUP_W + SSM_XBC], cw_ref, cb_ref, first, ls)
    dt_all = _softplus(sm_ref[...] + dtb_ref[...])
    da_all = dt_all * arow_ref[...]
    lane = lax.broadcasted_iota(jnp.int32, (1, 128), 1)
    half = lane < HD
    tril = _tril(CHUNK)
    for c in range(ls // CHUNK):
        r0 = c * CHUNK
        dt_c = dt_all[r0:r0 + CHUNK]
        acum = jnp.dot(tri_ref[...], da_all[r0:r0 + CHUNK], precision=HIGHEST,
                       preferred_element_type=F32)
        acum_t = acum.T
        alast = acum[CHUNK - 1:CHUNK, :]
        for g in range(2):
            h0, h1 = 2 * g, 2 * g + 1
            xg = xc[r0:r0 + CHUNK, g * 128:(g + 1) * 128]
            bg = xc[r0:r0 + CHUNK, GROUP_W + g * 128:GROUP_W + (g + 1) * 128]
            cg = xc[r0:r0 + CHUNK, 2 * GROUP_W + g * 128:2 * GROUP_W + (g + 1) * 128]
            cgb = cg.astype(BF16)
            gram = lax.dot_general(cgb, bg.astype(BF16), NT_DIMS, preferred_element_type=F32)
            a0 = acum[:, h0:h0 + 1]
            a1 = acum[:, h1:h1 + 1]
            xdt = xg * jnp.where(half, dt_c[:, h0:h0 + 1], dt_c[:, h1:h1 + 1])
            ydiag = jnp.zeros((CHUNK, 128), F32)
            for hh, col, msk in ((h0, a0, half), (h1, a1, jnp.logical_not(half))):
                dec = jnp.exp(jnp.where(tril, col - acum_t[hh:hh + 1, :], -jnp.inf))
                ydiag = ydiag + jnp.dot((gram * dec).astype(BF16),
                                        jnp.where(msk, xdt, 0.0).astype(BF16),
                                        preferred_element_type=F32)
            st = state[g]
            yoff = jnp.dot(cgb, st.astype(BF16), preferred_element_type=F32) * jnp.where(
                half, jnp.exp(a0), jnp.exp(a1))
            l0 = alast[:, h0:h0 + 1]
            l1 = alast[:, h1:h1 + 1]
            dout = jnp.where(half, jnp.exp(l0 - a0), jnp.exp(l1 - a1))
            state[g] = (jnp.where(half, jnp.exp(l0), jnp.exp(l1)) * st
                        + jnp.dot(bg.T.astype(BF16), (xdt * dout).astype(BF16), preferred_element_type=F32))
            y = ydiag + yoff + xg * dexp_ref[:, g * 128:(g + 1) * 128]
            zg = zx_ref[r0:r0 + CHUNK, g * 128:(g + 1) * 128]
            y = y * (zg * _sigmoid(zg))
            o_ref[r0:r0 + CHUNK, g * 128:(g + 1) * 128] = _rms(y, ng_ref[:, g * 128:(g + 1) * 128])


def _ssd(main, cw, cb, dtb, arow, dexp, ng, tri, bsz, seq, ls=512):
    t = main.shape[0]
    ns = seq // ls

    def full(shape):
        return pl.BlockSpec(shape, lambda b, s: tuple(0 for _ in shape))

    return pl.pallas_call(
        functools.partial(_ssd_body, ls=ls),
        grid=(bsz, ns),
        in_specs=[
            pl.BlockSpec((ls, 1024), lambda b, s: (b * ns + s, COL_ZX // 1024)),
            pl.BlockSpec((ls, 128), lambda b, s: (b * ns + s, COL_SMALL // 128)),
            full((CONV_K, SSM_XBC)), full((1, SSM_XBC)), full((1, 128)), full((1, 128)),
            full((1, GROUP_W)), full((1, GROUP_W)), full((CHUNK, CHUNK)),
        ],
        out_specs=pl.BlockSpec((ls, GROUP_W), lambda b, s: (b * ns + s, 0)),
        out_shape=jax.ShapeDtypeStruct((t, GROUP_W), F32),
        scratch_shapes=[pltpu.VMEM((ls + 8, SSM_XBC), F32), pltpu.VMEM((2, SSM_STATE, 128), F32)],
        compiler_params=_cparams(("arbitrary", "arbitrary")),
        name="ssd",
    )(main, main, cw, cb, dtb, arow, dexp, ng, tri)


def _mlstm_body(c_ref, sm_ref, cw_ref, cb_ref, igb_ref, fgb_ref, ng_ref, tri_ref, seg_ref, segm_ref,
                bmask_ref, o_ref, xpad, cst, nst, mst, *, ls):
    first = pl.program_id(1) == 0

    @pl.when(first)
    def _():
        cst[...] = jnp.zeros(cst.shape, F32)
        nst[...] = jnp.zeros(nst.shape, F32)
        mst[...] = jnp.zeros(mst.shape, F32)

    qk = _conv_silu(xpad, c_ref[:, 0:2 * GROUP_W], cw_ref, cb_ref, first, ls)
    sm = sm_ref[...]
    ig_all = sm + igb_ref[...]
    fx = sm + fgb_ref[...]
    lf_all = jnp.minimum(fx, 0.0) - jnp.log1p(jnp.exp(-jnp.abs(fx)))
    lane = lax.broadcasted_iota(jnp.int32, (1, GROUP_W), 1)
    hms = [jnp.logical_and(lane >= h * HD, lane < (h + 1) * HD) for h in range(HEADS)]
    tril = _tril(CHUNK)
    segb = seg_ref[...].astype(BF16)
    segm_b = segm_ref[...].astype(BF16)
    n_chunks = ls // CHUNK

    local = []
    for c in range(n_chunks):
        r0 = c * CHUNK
        q = qk[r0:r0 + CHUNK, 0:GROUP_W]
        k = qk[r0:r0 + CHUNK, GROUP_W:2 * GROUP_W] * (HD ** -0.5)
        v = c_ref[r0:r0 + CHUNK, 2 * GROUP_W:3 * GROUP_W]
        ig = ig_all[r0:r0 + CHUNK]
        bcs = jnp.dot(tri_ref[...], lf_all[r0:r0 + CHUNK], precision=HIGHEST,
                      preferred_element_type=F32)
        bcs_t = bcs.T
        ig_t = ig.T
        kb = k.astype(BF16)
        num_loc = jnp.zeros((CHUNK, GROUP_W), F32)
        w_loc_x = jnp.zeros((CHUNK, GROUP_W), F32)
        heads = []
        for h in range(HEADS):
            bcol = bcs[:, 8 + h:9 + h]
            dmat = jnp.where(tril, bcol - bcs_t[8 + h:9 + h, :] + ig_t[4 + h:5 + h, :], -jnp.inf)
            m_loc = jnp.max(dmat, axis=1, keepdims=True)
            s_qk = lax.dot_general(jnp.where(hms[h], q, 0.0).astype(BF16), kb, NT_DIMS,
                                   preferred_element_type=F32)
            sc = s_qk * jnp.exp(dmat - m_loc)
            den_loc = jnp.sum(sc, axis=1, keepdims=True)
            num_loc = num_loc + jnp.dot(sc.astype(BF16), jnp.where(hms[h], v, 0.0).astype(BF16),
                                        preferred_element_type=F32)
            blast = bcs[CHUNK - 1:CHUNK, 8 + h:9 + h]
            gcol = blast - bcol + ig[:, 4 + h:5 + h]
            gmax = jnp.max(gcol, axis=0, keepdims=True)
            w_loc_x = w_loc_x + jnp.where(hms[h], jnp.exp(gcol - gmax), 0.0)
            heads.append((bcol, m_loc, den_loc, blast, gmax))
        kw = k * w_loc_x
        kwv = bmask_ref[...] * jnp.dot(kw.T.astype(BF16), v.astype(BF16), preferred_element_type=F32)
        local.append((q, num_loc, heads, kwv, jnp.sum(kw, axis=0, keepdims=True)))

    houts = []
    for c in range(n_chunks):
        q, num_loc, heads, kwv, ksum = local[c]
        c_old = cst[...]
        n_old = nst[0:1, :]
        q_c = jnp.dot(q.astype(BF16), c_old.astype(BF16), preferred_element_type=F32)
        q_n = jnp.dot((q * n_old).astype(BF16), segb, preferred_element_type=F32)
        f_x = jnp.zeros((CHUNK, GROUP_W), F32)
        w_inter_x = jnp.zeros((CHUNK, GROUP_W), F32)
        inv_den_x = jnp.zeros((CHUNK, GROUP_W), F32)
        keep_x = jnp.zeros((1, GROUP_W), F32)
        sfac_x = jnp.zeros((1, GROUP_W), F32)
        for h in range(HEADS):
            bcol, m_loc, den_loc, blast, gmax = heads[h]
            m_prev = mst[h:h + 1, 0:1]
            inter = bcol + m_prev
            m_t = jnp.maximum(inter, m_loc)
            f = jnp.exp(m_loc - m_t)
            w_inter = jnp.exp(inter - m_t)
            den = f * den_loc + w_inter * q_n[:, h:h + 1]
            inv_den = 1.0 / jnp.maximum(jnp.abs(den), jnp.exp(-m_t))
            f_x = f_x + jnp.where(hms[h], f, 0.0)
            w_inter_x = w_inter_x + jnp.where(hms[h], w_inter, 0.0)
            inv_den_x = inv_den_x + jnp.where(hms[h], inv_den, 0.0)
            m_new = jnp.maximum(blast + m_prev, gmax)
            sfac_x = sfac_x + jnp.where(hms[h], jnp.exp(gmax - m_new), 0.0)
            keep_x = keep_x + jnp.where(hms[h], jnp.exp(blast + m_prev - m_new), 0.0)
            mst[h:h + 1, :] = jnp.broadcast_to(m_new, (1, 128))
        cst[...] = keep_x * c_old + sfac_x * kwv
        nst[0:1, :] = keep_x * n_old + sfac_x * ksum
        houts.append((f_x * num_loc + w_inter_x * q_c) * inv_den_x)

    for c in range(n_chunks):
        r0 = c * CHUNK
        hh = houts[c] * _sigmoid(c_ref[r0:r0 + CHUNK, 3 * GROUP_W:4 * GROUP_W])
        sq = hh * hh
        sq_hi = sq.astype(BF16)
        sq_lo = (sq - sq_hi.astype(F32)).astype(BF16)
        ms = (jnp.dot(sq_hi, segm_b, preferred_element_type=F32)
              + jnp.dot(sq_lo, segm_b, preferred_element_type=F32))
        o_ref[r0:r0 + CHUNK, :] = hh * lax.rsqrt(ms + EPS) * ng_ref[...]


def _mlstm(main, cw, cb, igb, fgb, ng, tri, seg, segm, bmask, bsz, seq, ls=512):
    t = main.shape[0]
    ns = seq // ls

    def full(shape):
        return pl.BlockSpec(shape, lambda b, s: tuple(0 for _ in shape))

    return pl.pallas_call(
        functools.partial(_mlstm_body, ls=ls),
        grid=(bsz, ns),
        in_specs=[
            pl.BlockSpec((ls, 1024), lambda b, s: (b * ns + s, COL_ML // 1024)),
            pl.BlockSpec((ls, 128), lambda b, s: (b * ns + s, COL_SMALL // 128)),
            full((CONV_K, 2 * GROUP_W)), full((1, 2 * GROUP_W)), full((1, 128)), full((1, 128)),
            full((1, GROUP_W)), full((CHUNK, CHUNK)), full((GROUP_W, 128)), full((GROUP_W, GROUP_W)),
            full((GROUP_W, GROUP_W)),
        ],
        out_specs=pl.BlockSpec((ls, GROUP_W), lambda b, s: (b * ns + s, 0)),
        out_shape=jax.ShapeDtypeStruct((t, GROUP_W), F32),
        scratch_shapes=[
            pltpu.VMEM((ls + 8, 2 * GROUP_W), F32),
            pltpu.VMEM((GROUP_W, GROUP_W), F32),
            pltpu.VMEM((8, GROUP_W), F32),
            pltpu.VMEM((8, 128), F32),
        ],
        compiler_params=_cparams(("arbitrary", "arbitrary")),
        name="mlstm",
    )(main, main, cw, cb, igb, fgb, ng, tri, seg, segm, bmask)


def _outproj_body(x_ref, ya_ref, yb_ref, yc_ref, yd_ref, wo_ref, g2_ref, wr_ref, br_ref,
                  xn_ref, h2_ref, cb_ref):
    acc = x_ref[...]
    for gi, y_ref in enumerate((ya_ref, yb_ref, yc_ref, yd_ref)):
        acc = acc + jnp.dot(y_ref[...].astype(BF16), wo_ref[gi * GROUP_W:(gi + 1) * GROUP_W, :],
                            preferred_element_type=F32)
    xn_ref[...] = acc
    h2 = _rms(acc, g2_ref[...])
    h_hi = h2.astype(BF16)
    h2_ref[...] = h_hi
    h_lo = (h2 - h_hi.astype(F32)).astype(BF16)
    hh = jnp.dot(h_hi, wr_ref[...], preferred_element_type=F32)
    lh = jnp.dot(h_lo, wr_ref[...], preferred_element_type=F32)
    logits = hh[:, :128] + hh[:, 128:] + lh[:, :128] + br_ref[...]
    lane = lax.broadcasted_iota(jnp.int32, logits.shape, 1)
    big = jnp.int32(1 << 20)
    n_grp = N_EXPERTS // EXPERTS_PER_GROUP
    gl = jnp.where(jnp.logical_and(lane >= N_EXPERTS, lane < N_EXPERTS + n_grp), logits, -jnp.inf)
    gmax = jnp.max(gl, axis=1, keepdims=True)
    g_w = 1.0 / jnp.sum(jnp.exp(gl - gmax), axis=1, keepdims=True)
    gidx = jnp.min(jnp.where(gl == gmax, lane, big), axis=1, keepdims=True) - N_EXPERTS
    in_grp = jnp.logical_and(lane < N_EXPERTS,
                             lax.shift_right_logical(lane, int(math.log2(EXPERTS_PER_GROUP))) == gidx)
    el = jnp.where(in_grp, logits, -jnp.inf)
    emax = jnp.max(el, axis=1, keepdims=True)
    esum = jnp.sum(jnp.exp(el - emax), axis=1, keepdims=True)
    i1 = jnp.min(jnp.where(el == emax, lane, big), axis=1, keepdims=True)
    el2 = jnp.where(lane == i1, -jnp.inf, el)
    emax2 = jnp.max(el2, axis=1, keepdims=True)
    i2 = jnp.min(jnp.where(el2 == emax2, lane, big), axis=1, keepdims=True)
    p1 = 1.0 / esum
    p2 = jnp.exp(emax2 - emax) / esum
    w1 = p1 / (p1 + p2)
    w2 = p2 / (p1 + p2)
    comb = g_w * (jnp.where(lane == i1, w1, 0.0) + jnp.where(lane == i2, w2, 0.0))
    cb_ref[...] = jnp.where(lane == GID_LANE, gidx.astype(F32), comb)


def _outproj(x2, ya, yb, yc, yd, wo, g2, wr, br, tm=512):
    t = x2.shape[0]
    row = pl.BlockSpec((tm, D_MODEL), lambda i: (i, 0))
    grp = pl.BlockSpec((tm, GROUP_W), lambda i: (i, 0))

    def full(shape):
        return pl.BlockSpec(shape, lambda i: tuple(0 for _ in shape))

    return pl.pallas_call(
        _outproj_body,
        grid=(t // tm,),
        in_specs=[row, grp, grp, grp, grp, full((D_MODEL, D_MODEL)), full((1, D_MODEL)),
                  full((D_MODEL, 256)), full((1, 128))],
        out_specs=[row, row, pl.BlockSpec((tm, 128), lambda i: (i, 0))],
        out_shape=[
            jax.ShapeDtypeStruct((t, D_MODEL), F32),
            jax.ShapeDtypeStruct((t, D_MODEL), BF16),
            jax.ShapeDtypeStruct((t, 128), F32),
        ],
        compiler_params=_cparams(("arbitrary",)),
        name="outproj_router",
    )(x2, ya, yb, yc, yd, wo, g2, wr, br)


MOE_EB = EXPERTS_PER_GROUP
MOE_RB = 128


def _moe_body(h_ref, cb_ref, xn_ref, tri_ref, wgu_ref, wd_ref, gf_ref, o_ref,
              perm_t, hs, cs, ys, seg_smem, *, final_norm):
    g = pl.program_id(1)
    tm = h_ref.shape[0]
    n_grp = N_EXPERTS // EXPERTS_PER_GROUP
    lane = lax.broadcasted_iota(jnp.int32, (tm, 128), 1)

    @pl.when(g == 0)
    def _():
        cb = cb_ref[...]
        gid = jnp.sum(jnp.where(lane == GID_LANE, cb, 0.0), axis=1, keepdims=True).astype(jnp.int32)
        onehot = jnp.where(lane == gid, 1.0, 0.0)
        oh16 = onehot.astype(BF16)
        tot = jnp.zeros((1, 128), F32)
        parts = []
        for r0 in range(0, tm, CHUNK):
            part = jnp.dot(tri_ref[...], oh16[r0:r0 + CHUNK], preferred_element_type=F32) + tot
            parts.append(part)
            tot = part[CHUNK - 1:CHUNK, :]
        cum = jnp.concatenate(parts, axis=0)
        lane1 = lane[0:1, :]
        base = jnp.zeros((1, 128), F32)
        run = jnp.zeros((1, 1), F32)
        for k in range(n_grp):
            base = base + jnp.where(lane1 == k, run, 0.0)
            seg_smem[k] = run[0, 0].astype(jnp.int32)
            seg_smem[n_grp + k] = tot[0, k].astype(jnp.int32)
            run = run + tot[:, k:k + 1]
        pos = jnp.sum(onehot * (base + cum - 1.0), axis=1, keepdims=True).astype(jnp.int32)
        pos_row = jnp.broadcast_to(pos.astype(F32), (tm, 128)).T[0:1, :].astype(jnp.int32)
        perm_t[...] = jnp.where(lax.broadcasted_iota(jnp.int32, (tm, tm), 1) == pos, 1.0, 0.0).astype(BF16)
        p = jnp.where(lax.broadcasted_iota(jnp.int32, (tm, tm), 0) == pos_row, 1.0, 0.0).astype(BF16)
        hs[...] = jnp.dot(p, h_ref[...], preferred_element_type=F32).astype(BF16)
        cb_hi = cb.astype(BF16)
        cb_lo = (cb - cb_hi.astype(F32)).astype(BF16)
        cs2 = jnp.dot(p, jnp.concatenate([cb_hi, cb_lo], axis=1), preferred_element_type=F32)
        cs[...] = cs2[:, :128] + cs2[:, 128:]
        ys[...] = jnp.zeros(ys.shape, F32)

    seg_lo = seg_smem[g]
    seg_hi = seg_lo + seg_smem[n_grp + g]
    lane_rb = lax.broadcasted_iota(jnp.int32, (MOE_RB, 128), 1)

    def run_block(lo):
        x = hs[lo:lo + MOE_RB, :]
        cw = cs[lo:lo + MOE_RB, :]

        def gate_up(j):
            return jnp.dot(x, wgu_ref[j], preferred_element_type=F32)

        def down(j, gu):
            col = jnp.sum(jnp.where(lane_rb == g * MOE_EB + j, cw, 0.0), axis=1,
                          keepdims=True)
            gt = gu[:, :D_EXPERT]
            hid = (gt * _sigmoid(gt)) * gu[:, D_EXPERT:] * col
            return jnp.dot(hid.astype(BF16), wd_ref[j], preferred_element_type=F32)

        total = jnp.zeros((MOE_RB, D_MODEL), F32)
        gu = gate_up(0)
        for j in range(MOE_EB):
            gu_next = gate_up(j + 1) if j + 1 < MOE_EB else None
            total = total + down(j, gu)
            gu = gu_next
        ys[lo:lo + MOE_RB, :] += total

    for b in range(tm // MOE_RB):
        lo = b * MOE_RB
        pl.when(jnp.logical_and(seg_lo < lo + MOE_RB, seg_hi > lo))(functools.partial(run_block, lo))

    @pl.when(g == n_grp - 1)
    def _():
        y = xn_ref[...] + jnp.dot(perm_t[...], ys[...].astype(BF16), preferred_element_type=F32)
        if final_norm:
            y = _rms(y, gf_ref[...])
        o_ref[...] = y


def _moe(h2, comb, xn, tri, wgu, wd, gf, final_norm, tm=1024):
    t = h2.shape[0]
    return pl.pallas_call(
        functools.partial(_moe_body, final_norm=final_norm),
        grid=(t // tm, N_EXPERTS // MOE_EB),
        in_specs=[
            pl.BlockSpec((tm, D_MODEL), lambda i, e: (i, 0)),
            pl.BlockSpec((tm, 128), lambda i, e: (i, 0)),
            pl.BlockSpec((tm, D_MODEL), lambda i, e: (i, 0)),
            pl.BlockSpec((CHUNK, CHUNK), lambda i, e: (0, 0)),
            pl.BlockSpec((MOE_EB, D_MODEL, 2 * D_EXPERT), lambda i, e: (e, 0, 0)),
            pl.BlockSpec((MOE_EB, D_EXPERT, D_MODEL), lambda i, e: (e, 0, 0)),
            pl.BlockSpec((1, D_MODEL), lambda i, e: (0, 0)),
        ],
        out_specs=pl.BlockSpec((tm, D_MODEL), lambda i, e: (i, 0)),
        out_shape=jax.ShapeDtypeStruct((t, D_MODEL), F32),
        scratch_shapes=[
            pltpu.VMEM((tm, tm), BF16),
            pltpu.VMEM((tm, D_MODEL), BF16),
            pltpu.VMEM((tm, 128), F32),
            pltpu.VMEM((tm, D_MODEL), F32),
            pltpu.SMEM((2 * (N_EXPERTS // MOE_EB),), jnp.int32),
        ],
        compiler_params=_cparams(("arbitrary", "arbitrary")),
        name="moe",
    )(h2, comb, xn, tri, wgu, wd, gf)


def _spread_heads(w, width, slab, total):
    rows = w.shape[0]
    pieces = []
    for h in range(HEADS):
        pieces.append(w[:, h * width:(h + 1) * width])
        pieces.append(jnp.zeros((rows, slab - width), w.dtype))
    if total > HEADS * slab:
        pieces.append(jnp.zeros((rows, total - HEADS * slab), w.dtype))
    return jnp.concatenate(pieces, axis=1)


def _pad_heads(w, width):
    return _spread_heads(w, width, HP, HEADS * HP)


def _v_cols(w):
    return _spread_heads(w, HD, VR, V_COLS)


def _lane_row(vals, offset, width=128):
    return jnp.zeros((1, width), F32).at[0, offset:offset + vals.shape[0]].set(vals.astype(F32))


def _split_w_in(w):
    sizes = (3 * GROUP_W, GROUP_W, SSM_XBC, HEADS, 2 * GROUP_W, GROUP_W, HEADS, HEADS, GROUP_W,
             256, 128, MLA_ROPE)
    offs = np.cumsum((0,) + sizes)
    return [w[:, int(offs[i]):int(offs[i + 1])] for i in range(len(sizes))]


def _layer_weights(l, p):
    a_qkv, s_z, s_xbc, s_dt, m_qk, m_v, m_i, m_f, m_o, d_cq, d_ckv, d_kr = _split_w_in(p["w_in_bf16"][l])
    aq, ak, av = a_qkv[:, :GROUP_W], a_qkv[:, GROUP_W:2 * GROUP_W], a_qkv[:, 2 * GROUP_W:]
    half = MLA_ROPE // 2
    small = jnp.concatenate([s_dt, m_i, m_f, jnp.zeros((D_MODEL, 64 - 12), BF16),
                             d_kr, d_kr[:, half:], d_kr[:, :half]], axis=1)
    w_main = jnp.concatenate([s_z, s_xbc, m_qk, m_v, m_o, d_cq, d_ckv, small, _pad_heads(ak, HD)], axis=1)
    w = {"w_main": w_main,
         "w_qv": jnp.concatenate([_pad_heads(aq, HD), _v_cols(av)], axis=1),
         "g_mix": p["norm_mix_g"][l][None, :]}
    w["ssm_cw"] = p["ssm_conv_w"][l]
    w["ssm_cb"] = p["ssm_conv_b"][l][None, :]
    w["ssm_dtb"] = _lane_row(p["ssm_dt_bias"][l], 0)
    w["ssm_arow"] = _lane_row(-jnp.exp(p["ssm_a_log"][l].astype(F32)), 0)
    w["ssm_dexp"] = jnp.repeat(p["ssm_d"][l].astype(F32), HD)[None, :]
    w["ssm_ng"] = p["ssm_norm_g"][l][None, :]
    w["ml_cw"] = p["ml_conv_w"][l]
    w["ml_cb"] = p["ml_conv_b"][l][None, :]
    w["ml_igb"] = _lane_row(p["ml_ig_bias"][l], 4)
    w["ml_fgb"] = _lane_row(p["ml_fg_bias"][l], 8)
    w["ml_ng"] = p["ml_norm_g"][l][None, :]
    wuq = p["mla_w_uq"][l]
    qd = MLA_NOPE + MLA_ROPE
    swapped = []
    for h in range(HEADS):
        c0 = h * qd + MLA_NOPE
        swapped += [jnp.zeros((256, MLA_NOPE), F32), wuq[:, c0 + half:c0 + 2 * half], wuq[:, c0:c0 + half]]
    wq = _spread_heads(wuq, qd, HP, HEADS * HP)
    wqs = _spread_heads(jnp.concatenate(swapped, axis=1), qd, HP, HEADS * HP)
    w["mla_wq2"] = jnp.concatenate([wq, wqs], axis=1).astype(BF16)
    wukv = p["mla_w_ukv"][l]
    wk_cols = jnp.concatenate([wukv[:, h * 2 * HD:h * 2 * HD + MLA_NOPE] for h in range(HEADS)], axis=1)
    wv_cols = jnp.concatenate([wukv[:, h * 2 * HD + MLA_NOPE:(h + 1) * 2 * HD] for h in range(HEADS)], axis=1)
    w["mla_wk"] = _pad_heads(wk_cols, MLA_NOPE).astype(BF16)
    w["mla_wv"] = _v_cols(wv_cols).astype(BF16)
    w["mla_gq"] = p["mla_q_norm_g"][l][None, :]
    w["mla_gkv"] = p["mla_kv_norm_g"][l][None, :]
    w["w_out"] = p["w_out"][l].astype(BF16)
    w["g_ffn"] = p["norm_ffn_g"][l][None, :]
    wr = jnp.zeros((D_MODEL, 128), F32)
    wr = wr.at[:, :N_EXPERTS].set(p["moe_w_router"][l]).at[:, N_EXPERTS:N_EXPERTS + 4].set(p["moe_w_group"][l])
    wr_hi = wr.astype(BF16)
    w["w_router"] = jnp.concatenate([wr_hi, (wr - wr_hi.astype(F32)).astype(BF16)], axis=1)
    w["b_router"] = _lane_row(p["moe_b_router"][l], 0) + _lane_row(p["moe_b_group"][l], N_EXPERTS)
    w["w_gu"] = jnp.concatenate([p["moe_w_gate"][l], p["moe_w_up"][l]], axis=2).astype(BF16)
    w["w_down"] = p["moe_w_down"][l].astype(BF16)
    return w


def kernel(x, positions, norm_mix_g, w_in, ssm_conv_w, ssm_conv_b, ssm_dt_bias, ssm_a_log, ssm_d, ssm_norm_g, ml_conv_w, ml_conv_b, ml_ig_bias, ml_fg_bias, ml_norm_g, mla_q_norm_g, mla_w_uq, mla_kv_norm_g, mla_w_ukv, w_out, norm_ffn_g, moe_w_group, moe_b_group, moe_w_router, moe_b_router, moe_w_gate, moe_w_up, moe_w_down, final_norm_g):
    p = dict(norm_mix_g=norm_mix_g, w_in_bf16=_cast_bf16(w_in), ssm_conv_w=ssm_conv_w, ssm_conv_b=ssm_conv_b,
             ssm_dt_bias=ssm_dt_bias, ssm_a_log=ssm_a_log, ssm_d=ssm_d, ssm_norm_g=ssm_norm_g,
             ml_conv_w=ml_conv_w, ml_conv_b=ml_conv_b, ml_ig_bias=ml_ig_bias, ml_fg_bias=ml_fg_bias,
             ml_norm_g=ml_norm_g, mla_q_norm_g=mla_q_norm_g, mla_w_uq=mla_w_uq,
             mla_kv_norm_g=mla_kv_norm_g, mla_w_ukv=mla_w_ukv, w_out=w_out, norm_ffn_g=norm_ffn_g,
             moe_w_group=moe_w_group, moe_b_group=moe_b_group, moe_w_router=moe_w_router,
             moe_b_router=moe_b_router, moe_w_gate=moe_w_gate, moe_w_up=moe_w_up, moe_w_down=moe_w_down)
    bsz, seq, _ = x.shape
    depth = w_in.shape[0]
    nb = seq // MOBA_BLOCK
    x2 = x.reshape(bsz * seq, D_MODEL)

    half = MLA_ROPE // 2
    inv = ROPE_THETA ** (-jnp.arange(0, MLA_ROPE, 2, dtype=F32) / MLA_ROPE)
    inv_full = _lane_row(jnp.concatenate([inv, inv]), MLA_NOPE)
    sgn_full = _lane_row(jnp.concatenate([-jnp.ones((half,), F32), jnp.ones((half,), F32)]), MLA_NOPE)
    pe_mask = _lane_row(jnp.ones((MLA_ROPE,), F32), MLA_NOPE)
    c_tok, s_tok, c_tr, s_tr = _rope_tables(positions.reshape(bsz * seq, 1), inv_full, sgn_full, bsz, seq)
    tri = jnp.tril(jnp.ones((CHUNK, CHUNK), F32))
    tri_moe = tri.astype(BF16)
    head_of = jnp.arange(GROUP_W) // HD
    seg = (head_of[:, None] == jnp.arange(128)[None, :]).astype(F32)
    bmask = (head_of[:, None] == head_of[None, :]).astype(F32)
    segm = bmask / HD

    for l in range(depth):
        w = _layer_weights(l, p)
        main, qT_a, vT_a = _in_proj(x2, w["g_mix"], w["w_main"], w["w_qv"], bsz, seq)
        kmean, k_a = _kmean(main, bsz, seq)
        qTb_a = _moba_select(qT_a, kmean.reshape(bsz, nb, HEADS * HP), bsz, seq)
        y_a = _attention(qTb_a, k_a, vT_a, bsz, seq, "attn_moba")
        y_b = _ssd(main, w["ssm_cw"], w["ssm_cb"], w["ssm_dtb"], w["ssm_arow"], w["ssm_dexp"], w["ssm_ng"],
                   tri, bsz, seq)
        y_c = _mlstm(main, w["ml_cw"], w["ml_cb"], w["ml_igb"], w["ml_fgb"], w["ml_ng"], tri, seg, segm,
                     bmask, bsz, seq)
        qT_d, k_d, vT_d = _mla_prep(main, c_tok, s_tok, c_tr, s_tr, w["mla_gq"], w["mla_wq2"], w["mla_gkv"],
                                    w["mla_wk"], w["mla_wv"], pe_mask, bsz, seq)
        y_d = _attention(qT_d, k_d, vT_d, bsz, seq, "attn_mla")
        xn, h2, comb = _outproj(x2, y_a, y_b, y_c, y_d, w["w_out"], w["g_ffn"], w["w_router"], w["b_router"])
        x2 = _moe(h2, comb, xn, tri_moe, w["w_gu"], w["w_down"], final_norm_g[None, :],
                  final_norm=(l == depth - 1), tm=MOE_TM)
    return x2.reshape(bsz, seq, D_MODEL)
```

```python
import functools
import math

import numpy as np
import jax
import jax.numpy as jnp
from jax import lax
from jax.experimental import pallas as pl
from jax.experimental.pallas import tpu as pltpu

F32 = jnp.float32
BF16 = jnp.bfloat16
HIGHEST = lax.Precision.HIGHEST

EPS = 1e-6
NEG = -1e30
LOG2E = 1.4426950408889634

D_MODEL = 1024
GROUP_W = 256
HEADS = 4
HD = 64
HP = 128
MOBA_BLOCK = 256
MOBA_TOPK = 3
SSM_STATE = 128
SSM_XBC = 768
CONV_K = 4
CHUNK = 128
MLA_NOPE = 64
MLA_ROPE = 32
ROPE_THETA = 10000.0
N_EXPERTS = 16
EXPERTS_PER_GROUP = 4
D_EXPERT = 256
GID_LANE = 20
MOE_TM = 1024

COL_ZX = 0
COL_ML = 1024
COL_SMALL = 2048
N_MAIN = 2176
VR = 80
V_COLS = 384
N_ALL = N_MAIN + 256 + 128 + 2 * HEADS * HP + V_COLS
ATT_UQ = 256
ATT_UK = 256

V7X_VMEM_LIMIT = 56 * 1024 * 1024

NT_DIMS = (((1,), (1,)), ((), ()))


def _cparams(sem):
    return pltpu.CompilerParams(dimension_semantics=sem, vmem_limit_bytes=V7X_VMEM_LIMIT)


def _rms(x, g):
    ms = jnp.mean(x * x, axis=-1, keepdims=True)
    return x * lax.rsqrt(ms + EPS) * g


def _sigmoid(x):
    return 0.5 * jnp.tanh(0.5 * x) + 0.5


def _softplus(x):
    return jnp.maximum(x, 0.0) + jnp.log1p(jnp.exp(-jnp.abs(x)))


def _tril(n):
    r = lax.broadcasted_iota(jnp.int32, (n, n), 0)
    c = lax.broadcasted_iota(jnp.int32, (n, n), 1)
    return c <= r


def _cast_body(w_ref, o_ref):
    o_ref[...] = w_ref[...].astype(BF16)


def _cast_bf16(w, rows=256):
    nl, r, c = w.shape
    return pl.pallas_call(
        _cast_body,
        grid=(nl, r // rows),
        in_specs=[pl.BlockSpec((1, rows, c), lambda l, i: (l, i, 0))],
        out_specs=pl.BlockSpec((1, rows, c), lambda l, i: (l, i, 0)),
        out_shape=jax.ShapeDtypeStruct(w.shape, BF16),
        compiler_params=_cparams(("arbitrary", "arbitrary")),
        name="cast_w_in",
    )(w)


def _ones_rows(shape):
    row = lax.broadcasted_iota(jnp.int32, shape, 0)
    hit = row == HD
    for h in range(1, HEADS):
        hit = jnp.logical_or(hit, row == h * VR + HD)
    return jnp.where(hit, 1.0, 0.0)


def _in_proj_body(x_ref, g_ref, w_ref, c_ref, s_ref, ct_ref, st_ref, gq_ref, wq_ref, gkv_ref, wk_ref,
                  wv_ref, msk_ref, om_ref, ka_ref, qa_ref, va_ref, qd_ref, kd_ref, vd_ref, km_sc,
                  *, nb, nst, qscale_a, qscale_d):
    s_idx = pl.program_id(0) % nst
    tm = x_ref.shape[0]

    @pl.when(s_idx == 0)
    def _():
        km_sc[...] = jnp.zeros(km_sc.shape, F32)

    h = _rms(x_ref[...], g_ref[...]).astype(BF16)
    for c in range(0, COL_SMALL, 512):
        om_ref[:, c:c + 512] = jnp.dot(h, w_ref[:, c:c + 512], preferred_element_type=F32)
    rest = jnp.dot(h, w_ref[:, COL_SMALL:], preferred_element_type=F32)
    sm = rest[:, 0:128]
    om_ref[:, COL_SMALL:COL_SMALL + 128] = sm
    cq = rest[:, 128:384]
    ckv = rest[:, 384:512]
    kp = rest[:, 512:1024]
    qp = rest[:, 1024:1536]
    vc = rest[:, 1536:1536 + V_COLS]

    nblk = tm // MOBA_BLOCK
    lane_k = lax.broadcasted_iota(jnp.int32, (MOBA_BLOCK, HEADS * HP), 1)
    for j in range(nblk):
        kj = kp[j * MOBA_BLOCK:(j + 1) * MOBA_BLOCK]
        blk_id = s_idx * nblk + j
        km_sc[pl.ds(blk_id, 1), :] = jnp.mean(kj, axis=0, keepdims=True)
        ka_ref[j * MOBA_BLOCK:(j + 1) * MOBA_BLOCK, :] = jnp.where(
            (lane_k & (HP - 1)) == HD + blk_id, 1.0, kj).astype(BF16)

    qT = qp.T
    km = km_sc[...]
    pos = s_idx * tm + lax.broadcasted_iota(jnp.int32, (nb, tm), 1)
    cur = lax.shift_right_logical(pos, int(math.log2(MOBA_BLOCK)))
    blk = lax.broadcasted_iota(jnp.int32, (nb, tm), 0)
    past = blk < cur
    for hd in range(HEADS):
        qh = qT[hd * HP:(hd + 1) * HP, :]
        gate = jnp.dot(km[:, hd * HP:(hd + 1) * HP], qh, precision=HIGHEST,
                       preferred_element_type=F32)
        gate = jnp.where(past, gate, -jnp.inf)
        bias = jnp.where(blk == cur, 0.0, NEG)
        for _ in range(MOBA_TOPK):
            m = jnp.max(gate, axis=0, keepdims=True)
            cand = jnp.logical_and(gate == m, gate > -jnp.inf)
            idx = jnp.min(jnp.where(cand, blk, nb), axis=0, keepdims=True)
            pick = blk == idx
            bias = jnp.where(pick, 0.0, bias)
            gate = jnp.where(pick, -jnp.inf, gate)
        rows = [qh[:HD] * qscale_a, bias]
        if HP - HD - nb:
            rows.append(jnp.zeros((HP - HD - nb, tm), F32))
        qa_ref[0, hd * HP:(hd + 1) * HP, :] = jnp.concatenate(rows, axis=0).astype(BF16)
    vt = vc.T[:HEADS * VR]
    va_ref[0] = (vt + _ones_rows(vt.shape)).astype(BF16)

    qn = _rms(cq, gq_ref[...]).astype(BF16)
    q2 = jnp.dot(qn, wq_ref[...], preferred_element_type=F32).T
    nq = HEADS * HP
    ct4 = jnp.concatenate([ct_ref[0]] * HEADS, axis=0)
    st4 = jnp.concatenate([st_ref[0]] * HEADS, axis=0)
    qd_ref[0] = ((q2[:nq] * ct4 + q2[nq:] * st4) * qscale_d).astype(BF16)
    kvn = _rms(ckv, gkv_ref[...]).astype(BF16)
    kn = jnp.dot(kvn, wk_ref[...], preferred_element_type=F32)
    pe = (sm * c_ref[...] + pltpu.roll(sm, HP - MLA_ROPE, 1) * s_ref[...]) * msk_ref[...]
    kd_ref[...] = (kn + jnp.concatenate([pe] * HEADS, axis=1)).astype(BF16)
    vtd = jnp.dot(kvn, wv_ref[...], preferred_element_type=F32).T[:HEADS * VR]
    vd_ref[0] = (vtd + _ones_rows(vtd.shape)).astype(BF16)


def _in_proj(x2, g, w_all, c_tok, s_tok, c_tr, s_tr, gq, wq2, gkv, wk, wv, pe_mask, bsz, seq, tm=512):
    t = x2.shape[0]
    nst = seq // tm
    nb = seq // MOBA_BLOCK
    assert nb <= HP - HD
    tok128 = pl.BlockSpec((tm, HP), lambda i: (i, 0))
    tr128 = pl.BlockSpec((1, HP, tm), lambda i: (i // nst, 0, i % nst))
    tr_q = pl.BlockSpec((1, HEADS * HP, tm), lambda i: (i // nst, 0, i % nst))
    tr_v = pl.BlockSpec((1, HEADS * VR, tm), lambda i: (i // nst, 0, i % nst))
    tok_k = pl.BlockSpec((tm, HEADS * HP), lambda i: (i, 0))

    def full(shape):
        return pl.BlockSpec(shape, lambda i: tuple(0 for _ in shape))

    q_shape = jax.ShapeDtypeStruct((bsz, HEADS * HP, seq), BF16)
    k_shape = jax.ShapeDtypeStruct((t, HEADS * HP), BF16)
    v_shape = jax.ShapeDtypeStruct((bsz, HEADS * VR, seq), BF16)
    return pl.pallas_call(
        functools.partial(_in_proj_body, nb=nb, nst=nst, qscale_a=(HD ** -0.5) * LOG2E,
                          qscale_d=((MLA_NOPE + MLA_ROPE) ** -0.5) * LOG2E),
        grid=(t // tm,),
        in_specs=[
            pl.BlockSpec((tm, D_MODEL), lambda i: (i, 0)),
            full((1, D_MODEL)), full((D_MODEL, N_ALL)),
            tok128, tok128, tr128, tr128,
            full((1, 256)), full((256, 2 * HEADS * HP)), full((1, 128)),
            full((128, HEADS * HP)), full((128, V_COLS)), full((1, HP)),
        ],
        out_specs=[pl.BlockSpec((tm, N_MAIN), lambda i: (i, 0)), tok_k, tr_q, tr_v, tr_q, tok_k, tr_v],
        out_shape=[jax.ShapeDtypeStruct((t, N_MAIN), F32), k_shape, q_shape, v_shape,
                   q_shape, k_shape, v_shape],
        scratch_shapes=[pltpu.VMEM((nb, HEADS * HP), F32)],
        compiler_params=_cparams(("arbitrary",)),
        name="in_proj",
    )(x2, g, w_all, c_tok, s_tok, c_tr, s_tr, gq, wq2, gkv, wk, wv, pe_mask)


def _rope_body(pos_ref, inv_ref, sgn_ref, c_ref, s_ref, ct_ref, st_ref):
    ang = pos_ref[...].astype(F32) * inv_ref[...]
    c = jnp.cos(ang)
    s = jnp.sin(ang) * sgn_ref[...]
    c_ref[...] = c
    s_ref[...] = s
    ct_ref[0] = c.T
    st_ref[0] = s.T


def _rope_tables(pos2, inv_full, sgn_full, bsz, seq, ts=512):
    t = pos2.shape[0]
    nst = seq // ts
    tok = pl.BlockSpec((ts, HP), lambda i: (i, 0))
    tr = pl.BlockSpec((1, HP, ts), lambda i: (i // nst, 0, i % nst))
    row = pl.BlockSpec((1, HP), lambda i: (0, 0))
    return pl.pallas_call(
        _rope_body,
        grid=(t // ts,),
        in_specs=[pl.BlockSpec((ts, 1), lambda i: (i, 0)), row, row],
        out_specs=[tok, tok, tr, tr],
        out_shape=[
            jax.ShapeDtypeStruct((t, HP), F32),
            jax.ShapeDtypeStruct((t, HP), F32),
            jax.ShapeDtypeStruct((bsz, HP, seq), F32),
            jax.ShapeDtypeStruct((bsz, HP, seq), F32),
        ],
        compiler_params=_cparams(("arbitrary",)),
        name="rope_tables",
    )(pos2, inv_full, sgn_full)


def _attn_body(ii_ref, jj_ref, qT_ref, k_ref, vT_ref, o_ref, m_sc, acc_sc, *, tq, tk):
    p = pl.program_id(1)
    i = ii_ref[p]
    j = jj_ref[p]

    @pl.when(j == 0)
    def _():
        m_sc[...] = jnp.full(m_sc.shape, NEG, F32)
        acc_sc[...] = jnp.zeros(acc_sc.shape, F32)

    def scores(u):
        h, qc, kb, _ = u
        return jnp.dot(k_ref[kb * ATT_UK:(kb + 1) * ATT_UK, h * HP:(h + 1) * HP],
                       qT_ref[0, h * HP:(h + 1) * HP, qc * ATT_UQ:(qc + 1) * ATT_UQ],
                       preferred_element_type=F32)

    def update(u, s):
        h, qc, kb, masked = u
        if masked:
            kpos = kb * ATT_UK + lax.broadcasted_iota(jnp.int32, s.shape, 0)
            qpos = qc * ATT_UQ + lax.broadcasted_iota(jnp.int32, s.shape, 1)
            s = jnp.where(kpos <= qpos, s, NEG)
        alphas, pms = [], []
        for c in range(ATT_UQ // 128):
            ql = slice(qc * ATT_UQ + c * 128, qc * ATT_UQ + (c + 1) * 128)
            sc = s[:, c * 128:(c + 1) * 128]
            m_old = m_sc[h:h + 1, ql]
            m_new = jnp.maximum(m_old, jnp.max(sc, axis=0, keepdims=True))
            alphas.append(jnp.exp2(m_old - m_new))
            pms.append(jnp.exp2(sc - m_new).astype(BF16))
            m_sc[h:h + 1, ql] = m_new
        ql = slice(qc * ATT_UQ, (qc + 1) * ATT_UQ)
        rows = slice(h * VR, (h + 1) * VR)
        acc_sc[rows, ql] = jnp.concatenate(alphas, axis=1) * acc_sc[rows, ql] + jnp.dot(
            vT_ref[0, rows, kb * ATT_UK:(kb + 1) * ATT_UK], jnp.concatenate(pms, axis=1),
            preferred_element_type=F32)

    def run(units):
        ahead = 8
        pending = [scores(u) for u in units[:ahead]]
        for n, u in enumerate(units):
            s = pending.pop(0)
            if n + ahead < len(units):
                pending.append(scores(units[n + ahead]))
            update(u, s)

    def unit_list(diag):
        units = []
        for kb in range(tk // ATT_UK):
            for h in range(HEADS):
                for qc in range(tq // ATT_UQ):
                    k_lo, k_hi = kb * ATT_UK, (kb + 1) * ATT_UK - 1
                    q_lo, q_hi = qc * ATT_UQ, (qc + 1) * ATT_UQ - 1
                    if diag and k_lo > q_hi:
                        continue
                    units.append((h, qc, kb, diag and k_hi > q_lo))
        return units

    @pl.when(j < i)
    def _():
        run(unit_list(False))

    @pl.when(j == i)
    def _():
        run(unit_list(True))
        parts = []
        for h in range(HEADS):
            den = acc_sc[h * VR + HD:h * VR + HD + 1, :]
            parts.append(acc_sc[h * VR:h * VR + HD, :] * (1.0 / den))
        o_ref[...] = jnp.concatenate(parts, axis=0).T


def _attention(qT, k, vT, bsz, seq, name, tq=1024):
    t = k.shape[0]
    nq = seq // tq
    ii = np.concatenate([np.full(i + 1, i, np.int32) for i in range(nq)])
    jj = np.concatenate([np.arange(i + 1, dtype=np.int32) for i in range(nq)])
    grid_spec = pltpu.PrefetchScalarGridSpec(
        num_scalar_prefetch=2,
        grid=(bsz, len(ii)),
        in_specs=[
            pl.BlockSpec((1, HEADS * HP, tq), lambda b, p, ii, jj: (b, 0, ii[p])),
            pl.BlockSpec((tq, HEADS * HP), lambda b, p, ii, jj: (b * nq + jj[p], 0)),
            pl.BlockSpec((1, HEADS * VR, tq), lambda b, p, ii, jj: (b, 0, jj[p])),
        ],
        out_specs=pl.BlockSpec((tq, GROUP_W), lambda b, p, ii, jj: (b * nq + ii[p], 0)),
        scratch_shapes=[
            pltpu.VMEM((8, tq), F32),
            pltpu.VMEM((HEADS * VR, tq), F32),
        ],
    )
    return pl.pallas_call(
        functools.partial(_attn_body, tq=tq, tk=tq),
        grid_spec=grid_spec,
        out_shape=jax.ShapeDtypeStruct((t, GROUP_W), F32),
        compiler_params=_cparams(("arbitrary", "arbitrary")),
        name=name,
    )(jnp.asarray(ii), jnp.asarray(jj), qT, k, vT)


def _conv_silu(xpad, src, cw_ref, cb_ref, first, ls):
    width = xpad.shape[1]

    @pl.when(first)
    def _():
        xpad[0:8, :] = jnp.zeros((8, width), F32)

    @pl.when(jnp.logical_not(first))
    def _():
        xpad[0:8, :] = xpad[ls:ls + 8, :]

    xpad[8:ls + 8, :] = src
    acc = cb_ref[...] + cw_ref[0:1, :] * xpad[pl.ds(8 - CONV_K + 1, ls), :]
    for kk in range(1, CONV_K):
        acc = acc + cw_ref[kk:kk + 1, :] * xpad[pl.ds(8 - CONV_K + 1 + kk, ls), :]
    return acc * _sigmoid(acc)


def _ssd_body(zx_ref, sm_ref, cw_ref, cb_ref, dtb_ref, arow_ref, dexp_ref, ng_ref, tri_ref, o_ref,
              xpad, state, *, ls):
    first = pl.program_id(1) == 0

    @pl.when(first)
    def _():
        state[...] = jnp.zeros(state.shape, F32)

    xc = _conv_silu(xpad, zx_ref[:, GROUP_W:GROUP_W + SSM_XBC], cw_ref, cb_ref, first, ls)
    dt_all = _softplus(sm_ref[...] + dtb_ref[...])
    da_all = dt_all * arow_ref[...]
    lane = lax.broadcasted_iota(jnp.int32, (1, 128), 1)
    half = lane < HD
    tril = _tril(CHUNK)
    for c in range(ls // CHUNK):
        r0 = c * CHUNK
        dt_c = dt_all[r0:r0 + CHUNK]
        acum = jnp.dot(tri_ref[...], da_all[r0:r0 + CHUNK], precision=HIGHEST,
                       preferred_element_type=F32)
        acum_t = acum.T
        alast = acum[CHUNK - 1:CHUNK, :]
        for g in range(2):
            h0, h1 = 2 * g, 2 * g + 1
            xg = xc[r0:r0 + CHUNK, g * 128:(g + 1) * 128]
            bg = xc[r0:r0 + CHUNK, GROUP_W + g * 128:GROUP_W + (g + 1) * 128]
            cg = xc[r0:r0 + CHUNK, 2 * GROUP_W + g * 128:2 * GROUP_W + (g + 1) * 128]
            cgb = cg.astype(BF16)
            gram = lax.dot_general(cgb, bg.astype(BF16), NT_DIMS, preferred_element_type=F32)
            a0 = acum[:, h0:h0 + 1]
            a1 = acum[:, h1:h1 + 1]
            xdt = xg * jnp.where(half, dt_c[:, h0:h0 + 1], dt_c[:, h1:h1 + 1])
            ydiag = jnp.zeros((CHUNK, 128), F32)
            for hh, col, msk in ((h0, a0, half), (h1, a1, jnp.logical_not(half))):
                dec = jnp.exp(jnp.where(tril, col - acum_t[hh:hh + 1, :], -jnp.inf))
                ydiag = ydiag + jnp.dot((gram * dec).astype(BF16),
                                        jnp.where(msk, xdt, 0.0).astype(BF16),
                                        preferred_element_type=F32)
            st = state[g]
            yoff = jnp.dot(cgb, st.astype(BF16), preferred_element_type=F32) * jnp.where(
                half, jnp.exp(a0), jnp.exp(a1))
            l0 = alast[:, h0:h0 + 1]
            l1 = alast[:, h1:h1 + 1]
            dout = jnp.where(half, jnp.exp(l0 - a0), jnp.exp(l1 - a1))
            state[g] = (jnp.where(half, jnp.exp(l0), jnp.exp(l1)) * st
                        + jnp.dot(bg.T.astype(BF16), (xdt * dout).astype(BF16), preferred_element_type=F32))
            y = ydiag + yoff + xg * dexp_ref[:, g * 128:(g + 1) * 128]
            zg = zx_ref[r0:r0 + CHUNK, g * 128:(g + 1) * 128]
            y = y * (zg * _sigmoid(zg))
            o_ref[r0:r0 + CHUNK, g * 128:(g + 1) * 128] = _rms(y, ng_ref[:, g * 128:(g + 1) * 128])


def _ssd(main, cw, cb, dtb, arow, dexp, ng, tri, bsz, seq, ls=512):
    t = main.shape[0]
    ns = seq // ls

    def full(shape):
        return pl.BlockSpec(shape, lambda b, s: tuple(0 for _ in shape))

    return pl.pallas_call(
        functools.partial(_ssd_body, ls=ls),
        grid=(bsz, ns),
        in_specs=[
            pl.BlockSpec((ls, 1024), lambda b, s: (b * ns + s, COL_ZX // 1024)),
            pl.BlockSpec((ls, 128), lambda b, s: (b * ns + s, COL_SMALL // 128)),
            full((CONV_K, SSM_XBC)), full((1, SSM_XBC)), full((1, 128)), full((1, 128)),
            full((1, GROUP_W)), full((1, GROUP_W)), full((CHUNK, CHUNK)),
        ],
        out_specs=pl.BlockSpec((ls, GROUP_W), lambda b, s: (b * ns + s, 0)),
        out_shape=jax.ShapeDtypeStruct((t, GROUP_W), F32),
        scratch_shapes=[pltpu.VMEM((ls + 8, SSM_XBC), F32), pltpu.VMEM((2, SSM_STATE, 128), F32)],
        compiler_params=_cparams(("arbitrary", "arbitrary")),
        name="ssd",
    )(main, main, cw, cb, dtb, arow, dexp, ng, tri)


def _mlstm_body(c_ref, sm_ref, cw_ref, cb_ref, igb_ref, fgb_ref, ng_ref, tri_ref, seg_ref, segm_ref,
                bmask_ref, o_ref, xpad, cst, nst, mst, *, ls):
    first = pl.program_id(1) == 0

    @pl.when(first)
    def _():
        cst[...] = jnp.zeros(cst.shape, F32)
        nst[...] = jnp.zeros(nst.shape, F32)
        mst[...] = jnp.zeros(mst.shape, F32)

    qk = _conv_silu(xpad, c_ref[:, 0:2 * GROUP_W], cw_ref, cb_ref, first, ls)
    sm = sm_ref[...]
    ig_all = sm + igb_ref[...]
    fx = sm + fgb_ref[...]
    lf_all = jnp.minimum(fx, 0.0) - jnp.log1p(jnp.exp(-jnp.abs(fx)))
    lane = lax.broadcasted_iota(jnp.int32, (1, GROUP_W), 1)
    hms = [jnp.logical_and(lane >= h * HD, lane < (h + 1) * HD) for h in range(HEADS)]
    tril = _tril(CHUNK)
    segb = seg_ref[...].astype(BF16)
    segm_b = segm_ref[...].astype(BF16)
    n_chunks = ls // CHUNK

    local = []
    for c in range(n_chunks):
        r0 = c * CHUNK
        q = qk[r0:r0 + CHUNK, 0:GROUP_W]
        k = qk[r0:r0 + CHUNK, GROUP_W:2 * GROUP_W] * (HD ** -0.5)
        v = c_ref[r0:r0 + CHUNK, 2 * GROUP_W:3 * GROUP_W]
        ig = ig_all[r0:r0 + CHUNK]
        bcs = jnp.dot(tri_ref[...], lf_all[r0:r0 + CHUNK], precision=HIGHEST,
                      preferred_element_type=F32)
        bcs_t = bcs.T
        ig_t = ig.T
        kb = k.astype(BF16)
        num_loc = jnp.zeros((CHUNK, GROUP_W), F32)
        w_loc_x = jnp.zeros((CHUNK, GROUP_W), F32)
        heads = []
        for h in range(HEADS):
            bcol = bcs[:, 8 + h:9 + h]
            dmat = jnp.where(tril, bcol - bcs_t[8 + h:9 + h, :] + ig_t[4 + h:5 + h, :], -jnp.inf)
            m_loc = jnp.max(dmat, axis=1, keepdims=True)
            s_qk = lax.dot_general(jnp.where(hms[h], q, 0.0).astype(BF16), kb, NT_DIMS,
                                   preferred_element_type=F32)
            sc = s_qk * jnp.exp(dmat - m_loc)
            den_loc = jnp.sum(sc, axis=1, keepdims=True)
            num_loc = num_loc + jnp.dot(sc.astype(BF16), jnp.where(hms[h], v, 0.0).astype(BF16),
                                        preferred_element_type=F32)
            blast = bcs[CHUNK - 1:CHUNK, 8 + h:9 + h]
            gcol = blast - bcol + ig[:, 4 + h:5 + h]
            gmax = jnp.max(gcol, axis=0, keepdims=True)
            w_loc_x = w_loc_x + jnp.where(hms[h], jnp.exp(gcol - gmax), 0.0)
            heads.append((bcol, m_loc, den_loc, blast, gmax))
        kw = k * w_loc_x
        kwv = bmask_ref[...] * jnp.dot(kw.T.astype(BF16), v.astype(BF16), preferred_element_type=F32)
        local.append((q, num_loc, heads, kwv, jnp.sum(kw, axis=0, keepdims=True)))

    houts = []
    for c in range(n_chunks):
        q, num_loc, heads, kwv, ksum = local[c]
        c_old = cst[...]
        n_old = nst[0:1, :]
        q_c = jnp.dot(q.astype(BF16), c_old.astype(BF16), preferred_element_type=F32)
        q_n = jnp.dot((q * n_old).astype(BF16), segb, preferred_element_type=F32)
        f_x = jnp.zeros((CHUNK, GROUP_W), F32)
        w_inter_x = jnp.zeros((CHUNK, GROUP_W), F32)
        inv_den_x = jnp.zeros((CHUNK, GROUP_W), F32)
        keep_x = jnp.zeros((1, GROUP_W), F32)
        sfac_x = jnp.zeros((1, GROUP_W), F32)
        for h in range(HEADS):
            bcol, m_loc, den_loc, blast, gmax = heads[h]
            m_prev = mst[h:h + 1, 0:1]
            inter = bcol + m_prev
            m_t = jnp.maximum(inter, m_loc)
            f = jnp.exp(m_loc - m_t)
            w_inter = jnp.exp(inter - m_t)
            den = f * den_loc + w_inter * q_n[:, h:h + 1]
            inv_den = 1.0 / jnp.maximum(jnp.abs(den), jnp.exp(-m_t))
            f_x = f_x + jnp.where(hms[h], f, 0.0)
            w_inter_x = w_inter_x + jnp.where(hms[h], w_inter, 0.0)
            inv_den_x = inv_den_x + jnp.where(hms[h], inv_den, 0.0)
            m_new = jnp.maximum(blast + m_prev, gmax)
            sfac_x = sfac_x + jnp.where(hms[h], jnp.exp(gmax - m_new), 0.0)
            keep_x = keep_x + jnp.where(hms[h], jnp.exp(blast + m_prev - m_new), 0.0)
            mst[h:h + 1, :] = jnp.broadcast_to(m_new, (1, 128))
        cst[...] = keep_x * c_old + sfac_x * kwv
        nst[0:1, :] = keep_x * n_old + sfac_x * ksum
        houts.append((f_x * num_loc + w_inter_x * q_c) * inv_den_x)

    for c in range(n_chunks):
        r0 = c * CHUNK
        hh = houts[c] * _sigmoid(c_ref[r0:r0 + CHUNK, 3 * GROUP_W:4 * GROUP_W])
        sq = hh * hh
        sq_hi = sq.astype(BF16)
        sq_lo = (sq - sq_hi.astype(F32)).astype(BF16)
        ms = (jnp.dot(sq_hi, segm_b, preferred_element_type=F32)
              + jnp.dot(sq_lo, segm_b, preferred_element_type=F32))
        o_ref[r0:r0 + CHUNK, :] = hh * lax.rsqrt(ms + EPS) * ng_ref[...]


def _mlstm(main, cw, cb, igb, fgb, ng, tri, seg, segm, bmask, bsz, seq, ls=512):
    t = main.shape[0]
    ns = seq // ls

    def full(shape):
        return pl.BlockSpec(shape, lambda b, s: tuple(0 for _ in shape))

    return pl.pallas_call(
        functools.partial(_mlstm_body, ls=ls),
        grid=(bsz, ns),
        in_specs=[
            pl.BlockSpec((ls, 1024), lambda b, s: (b * ns + s, COL_ML // 1024)),
            pl.BlockSpec((ls, 128), lambda b, s: (b * ns + s, COL_SMALL // 128)),
            full((CONV_K, 2 * GROUP_W)), full((1, 2 * GROUP_W)), full((1, 128)), full((1, 128)),
            full((1, GROUP_W)), full((CHUNK, CHUNK)), full((GROUP_W, 128)), full((GROUP_W, GROUP_W)),
            full((GROUP_W, GROUP_W)),
        ],
        out_specs=pl.BlockSpec((ls, GROUP_W), lambda b, s: (b * ns + s, 0)),
        out_shape=jax.ShapeDtypeStruct((t, GROUP_W), F32),
        scratch_shapes=[
            pltpu.VMEM((ls + 8, 2 * GROUP_W), F32),
            pltpu.VMEM((GROUP_W, GROUP_W), F32),
            pltpu.VMEM((8, GROUP_W), F32),
            pltpu.VMEM((8, 128), F32),
        ],
        compiler_params=_cparams(("arbitrary", "arbitrary")),
        name="mlstm",
    )(main, main, cw, cb, igb, fgb, ng, tri, seg, segm, bmask)


def _outproj_body(x_ref, ya_ref, yb_ref, yc_ref, yd_ref, wo_ref, g2_ref, wr_ref, br_ref,
                  xn_ref, h2_ref, cb_ref):
    acc = x_ref[...]
    for gi, y_ref in enumerate((ya_ref, yb_ref, yc_ref, yd_ref)):
        acc = acc + jnp.dot(y_ref[...].astype(BF16), wo_ref[gi * GROUP_W:(gi + 1) * GROUP_W, :],
                            preferred_element_type=F32)
    xn_ref[...] = acc
    h2 = _rms(acc, g2_ref[...])
    h_hi = h2.astype(BF16)
    h2_ref[...] = h_hi
    h_lo = (h2 - h_hi.astype(F32)).astype(BF16)
    hh = jnp.dot(h_hi, wr_ref[...], preferred_element_type=F32)
    lh = jnp.dot(h_lo, wr_ref[...], preferred_element_type=F32)
    logits = hh[:, :128] + hh[:, 128:] + lh[:, :128] + br_ref[...]
    lane = lax.broadcasted_iota(jnp.int32, logits.shape, 1)
    big = jnp.int32(1 << 20)
    n_grp = N_EXPERTS // EXPERTS_PER_GROUP
    gl = jnp.where(jnp.logical_and(lane >= N_EXPERTS, lane < N_EXPERTS + n_grp), logits, -jnp.inf)
    gmax = jnp.max(gl, axis=1, keepdims=True)
    g_w = 1.0 / jnp.sum(jnp.exp(gl - gmax), axis=1, keepdims=True)
    gidx = jnp.min(jnp.where(gl == gmax, lane, big), axis=1, keepdims=True) - N_EXPERTS
    in_grp = jnp.logical_and(lane < N_EXPERTS,
                             lax.shift_right_logical(lane, int(math.log2(EXPERTS_PER_GROUP))) == gidx)
    el = jnp.where(in_grp, logits, -jnp.inf)
    emax = jnp.max(el, axis=1, keepdims=True)
    esum = jnp.sum(jnp.exp(el - emax), axis=1, keepdims=True)
    i1 = jnp.min(jnp.where(el == emax, lane, big), axis=1, keepdims=True)
    el2 = jnp.where(lane == i1, -jnp.inf, el)
    emax2 = jnp.max(el2, axis=1, keepdims=True)
    i2 = jnp.min(jnp.where(el2 == emax2, lane, big), axis=1, keepdims=True)
    p1 = 1.0 / esum
    p2 = jnp.exp(emax2 - emax) / esum
    w1 = p1 / (p1 + p2)
    w2 = p2 / (p1 + p2)
    comb = g_w * (jnp.where(lane == i1, w1, 0.0) + jnp.where(lane == i2, w2, 0.0))
    cb_ref[...] = jnp.where(lane == GID_LANE, gidx.astype(F32), comb)


def _outproj(x2, ya, yb, yc, yd, wo, g2, wr, br, tm=512):
    t = x2.shape[0]
    row = pl.BlockSpec((tm, D_MODEL), lambda i: (i, 0))
    grp = pl.BlockSpec((tm, GROUP_W), lambda i: (i, 0))

    def full(shape):
        return pl.BlockSpec(shape, lambda i: tuple(0 for _ in shape))

    return pl.pallas_call(
        _outproj_body,
        grid=(t // tm,),
        in_specs=[row, grp, grp, grp, grp, full((D_MODEL, D_MODEL)), full((1, D_MODEL)),
                  full((D_MODEL, 256)), full((1, 128))],
        out_specs=[row, row, pl.BlockSpec((tm, 128), lambda i: (i, 0))],
        out_shape=[
            jax.ShapeDtypeStruct((t, D_MODEL), F32),
            jax.ShapeDtypeStruct((t, D_MODEL), BF16),
            jax.ShapeDtypeStruct((t, 128), F32),
        ],
        compiler_params=_cparams(("arbitrary",)),
        name="outproj_router",
    )(x2, ya, yb, yc, yd, wo, g2, wr, br)


MOE_EB = EXPERTS_PER_GROUP
MOE_RB = 128


def _moe_body(h_ref, cb_ref, xn_ref, tri_ref, wgu_ref, wd_ref, gf_ref, o_ref,
              perm_t, hs, cs, ys, seg_smem, *, final_norm):
    g = pl.program_id(1)
    tm = h_ref.shape[0]
    n_grp = N_EXPERTS // EXPERTS_PER_GROUP
    lane = lax.broadcasted_iota(jnp.int32, (tm, 128), 1)

    @pl.when(g == 0)
    def _():
        cb = cb_ref[...]
        gid = jnp.sum(jnp.where(lane == GID_LANE, cb, 0.0), axis=1, keepdims=True).astype(jnp.int32)
        onehot = jnp.where(lane == gid, 1.0, 0.0)
        oh16 = onehot.astype(BF16)
        tot = jnp.zeros((1, 128), F32)
        parts = []
        for r0 in range(0, tm, CHUNK):
            part = jnp.dot(tri_ref[...], oh16[r0:r0 + CHUNK], preferred_element_type=F32) + tot
            parts.append(part)
            tot = part[CHUNK - 1:CHUNK, :]
        cum = jnp.concatenate(parts, axis=0)
        lane1 = lane[0:1, :]
        base = jnp.zeros((1, 128), F32)
        run = jnp.zeros((1, 1), F32)
        for k in range(n_grp):
            base = base + jnp.where(lane1 == k, run, 0.0)
            seg_smem[k] = run[0, 0].astype(jnp.int32)
            seg_smem[n_grp + k] = tot[0, k].astype(jnp.int32)
            run = run + tot[:, k:k + 1]
        pos = jnp.sum(onehot * (base + cum - 1.0), axis=1, keepdims=True).astype(jnp.int32)
        pos_row = jnp.broadcast_to(pos.astype(F32), (tm, 128)).T[0:1, :].astype(jnp.int32)
        perm_t[...] = jnp.where(lax.broadcasted_iota(jnp.int32, (tm, tm), 1) == pos, 1.0, 0.0).astype(BF16)
        p = jnp.where(lax.broadcasted_iota(jnp.int32, (tm, tm), 0) == pos_row, 1.0, 0.0).astype(BF16)
        hs[...] = jnp.dot(p, h_ref[...], preferred_element_type=F32).astype(BF16)
        cb_hi = cb.astype(BF16)
        cb_lo = (cb - cb_hi.astype(F32)).astype(BF16)
        cs2 = jnp.dot(p, jnp.concatenate([cb_hi, cb_lo], axis=1), preferred_element_type=F32)
        cs[...] = cs2[:, :128] + cs2[:, 128:]
        ys[...] = jnp.zeros(ys.shape, F32)

    seg_lo = seg_smem[g]
    seg_hi = seg_lo + seg_smem[n_grp + g]
    lane_rb = lax.broadcasted_iota(jnp.int32, (MOE_RB, 128), 1)

    def run_block(lo):
        x = hs[lo:lo + MOE_RB, :]
        cw = cs[lo:lo + MOE_RB, :]

        def gate_up(j):
            return jnp.dot(x, wgu_ref[j], preferred_element_type=F32)

        def down(j, gu):
            col = jnp.sum(jnp.where(lane_rb == g * MOE_EB + j, cw, 0.0), axis=1,
                          keepdims=True)
            gt = gu[:, :D_EXPERT]
            hid = (gt * _sigmoid(gt)) * gu[:, D_EXPERT:] * col
            return jnp.dot(hid.astype(BF16), wd_ref[j], preferred_element_type=F32)

        total = jnp.zeros((MOE_RB, D_MODEL), F32)
        gu = gate_up(0)
        for j in range(MOE_EB):
            gu_next = gate_up(j + 1) if j + 1 < MOE_EB else None
            total = total + down(j, gu)
            gu = gu_next
        ys[lo:lo + MOE_RB, :] += total

    for b in range(tm // MOE_RB):
        lo = b * MOE_RB
        pl.when(jnp.logical_and(seg_lo < lo + MOE_RB, seg_hi > lo))(functools.partial(run_block, lo))

    @pl.when(g == n_grp - 1)
    def _():
        y = xn_ref[...] + jnp.dot(perm_t[...], ys[...].astype(BF16), preferred_element_type=F32)
        if final_norm:
            y = _rms(y, gf_ref[...])
        o_ref[...] = y


def _moe(h2, comb, xn, tri, wgu, wd, gf, final_norm, tm=1024):
    t = h2.shape[0]
    return pl.pallas_call(
        functools.partial(_moe_body, final_norm=final_norm),
        grid=(t // tm, N_EXPERTS // MOE_EB),
        in_specs=[
            pl.BlockSpec((tm, D_MODEL), lambda i, e: (i, 0)),
            pl.BlockSpec((tm, 128), lambda i, e: (i, 0)),
            pl.BlockSpec((tm, D_MODEL), lambda i, e: (i, 0)),
            pl.BlockSpec((CHUNK, CHUNK), lambda i, e: (0, 0)),
            pl.BlockSpec((MOE_EB, D_MODEL, 2 * D_EXPERT), lambda i, e: (e, 0, 0)),
            pl.BlockSpec((MOE_EB, D_EXPERT, D_MODEL), lambda i, e: (e, 0, 0)),
            pl.BlockSpec((1, D_MODEL), lambda i, e: (0, 0)),
        ],
        out_specs=pl.BlockSpec((tm, D_MODEL), lambda i, e: (i, 0)),
        out_shape=jax.ShapeDtypeStruct((t, D_MODEL), F32),
        scratch_shapes=[
            pltpu.VMEM((tm, tm), BF16),
            pltpu.VMEM((tm, D_MODEL), BF16),
            pltpu.VMEM((tm, 128), F32),
            pltpu.VMEM((tm, D_MODEL), F32),
            pltpu.SMEM((2 * (N_EXPERTS // MOE_EB),), jnp.int32),
        ],
        compiler_params=_cparams(("arbitrary", "arbitrary")),
        name="moe",
    )(h2, comb, xn, tri, wgu, wd, gf)


def _spread_heads(w, width, slab, total):
    rows = w.shape[0]
    pieces = []
    for h in range(HEADS):
        pieces.append(w[:, h * width:(h + 1) * width])
        pieces.append(jnp.zeros((rows, slab - width), w.dtype))
    if total > HEADS * slab:
        pieces.append(jnp.zeros((rows, total - HEADS * slab), w.dtype))
    return jnp.concatenate(pieces, axis=1)


def _pad_heads(w, width):
    return _spread_heads(w, width, HP, HEADS * HP)


def _v_cols(w):
    return _spread_heads(w, HD, VR, V_COLS)


def _lane_row(vals, offset, width=128):
    return jnp.zeros((1, width), F32).at[0, offset:offset + vals.shape[0]].set(vals.astype(F32))


def _split_w_in(w):
    sizes = (3 * GROUP_W, GROUP_W, SSM_XBC, HEADS, 2 * GROUP_W, GROUP_W, HEADS, HEADS, GROUP_W,
             256, 128, MLA_ROPE)
    offs = np.cumsum((0,) + sizes)
    return [w[:, int(offs[i]):int(offs[i + 1])] for i in range(len(sizes))]


def _layer_weights(l, p):
    a_qkv, s_z, s_xbc, s_dt, m_qk, m_v, m_i, m_f, m_o, d_cq, d_ckv, d_kr = _split_w_in(p["w_in_bf16"][l])
    aq, ak, av = a_qkv[:, :GROUP_W], a_qkv[:, GROUP_W:2 * GROUP_W], a_qkv[:, 2 * GROUP_W:]
    half = MLA_ROPE // 2
    small = jnp.concatenate([s_dt, m_i, m_f, jnp.zeros((D_MODEL, 64 - 12), BF16),
                             d_kr, d_kr[:, half:], d_kr[:, :half]], axis=1)
    w_all = jnp.concatenate([s_z, s_xbc, m_qk, m_v, m_o, small, d_cq, d_ckv, _pad_heads(ak, HD),
                             _pad_heads(aq, HD), _v_cols(av)], axis=1)
    w = {"w_all": w_all,
         "g_mix": p["norm_mix_g"][l][None, :]}
    w["ssm_cw"] = p["ssm_conv_w"][l]
    w["ssm_cb"] = p["ssm_conv_b"][l][None, :]
    w["ssm_dtb"] = _lane_row(p["ssm_dt_bias"][l], 0)
    w["ssm_arow"] = _lane_row(-jnp.exp(p["ssm_a_log"][l].astype(F32)), 0)
    w["ssm_dexp"] = jnp.repeat(p["ssm_d"][l].astype(F32), HD)[None, :]
    w["ssm_ng"] = p["ssm_norm_g"][l][None, :]
    w["ml_cw"] = p["ml_conv_w"][l]
    w["ml_cb"] = p["ml_conv_b"][l][None, :]
    w["ml_igb"] = _lane_row(p["ml_ig_bias"][l], 4)
    w["ml_fgb"] = _lane_row(p["ml_fg_bias"][l], 8)
    w["ml_ng"] = p["ml_norm_g"][l][None, :]
    wuq = p["mla_w_uq"][l]
    qd = MLA_NOPE + MLA_ROPE
    swapped = []
    for h in range(HEADS):
        c0 = h * qd + MLA_NOPE
        swapped += [jnp.zeros((256, MLA_NOPE), F32), wuq[:, c0 + half:c0 + 2 * half], wuq[:, c0:c0 + half]]
    wq = _spread_heads(wuq, qd, HP, HEADS * HP)
    wqs = _spread_heads(jnp.concatenate(swapped, axis=1), qd, HP, HEADS * HP)
    w["mla_wq2"] = jnp.concatenate([wq, wqs], axis=1).astype(BF16)
    wukv = p["mla_w_ukv"][l]
    wk_cols = jnp.concatenate([wukv[:, h * 2 * HD:h * 2 * HD + MLA_NOPE] for h in range(HEADS)], axis=1)
    wv_cols = jnp.concatenate([wukv[:, h * 2 * HD + MLA_NOPE:(h + 1) * 2 * HD] for h in range(HEADS)], axis=1)
    w["mla_wk"] = _pad_heads(wk_cols, MLA_NOPE).astype(BF16)
    w["mla_wv"] = _v_cols(wv_cols).astype(BF16)
    w["mla_gq"] = p["mla_q_norm_g"][l][None, :]
    w["mla_gkv"] = p["mla_kv_norm_g"][l][None, :]
    w["w_out"] = p["w_out"][l].astype(BF16)
    w["g_ffn"] = p["norm_ffn_g"][l][None, :]
    wr = jnp.zeros((D_MODEL, 128), F32)
    wr = wr.at[:, :N_EXPERTS].set(p["moe_w_router"][l]).at[:, N_EXPERTS:N_EXPERTS + 4].set(p["moe_w_group"][l])
    wr_hi = wr.astype(BF16)
    w["w_router"] = jnp.concatenate([wr_hi, (wr - wr_hi.astype(F32)).astype(BF16)], axis=1)
    w["b_router"] = _lane_row(p["moe_b_router"][l], 0) + _lane_row(p["moe_b_group"][l], N_EXPERTS)
    w["w_gu"] = jnp.concatenate([p["moe_w_gate"][l], p["moe_w_up"][l]], axis=2).astype(BF16)
    w["w_down"] = p["moe_w_down"][l].astype(BF16)
    return w


def kernel(x, positions, norm_mix_g, w_in, ssm_conv_w, ssm_conv_b, ssm_dt_bias, ssm_a_log, ssm_d, ssm_norm_g, ml_conv_w, ml_conv_b, ml_ig_bias, ml_fg_bias, ml_norm_g, mla_q_norm_g, mla_w_uq, mla_kv_norm_g, mla_w_ukv, w_out, norm_ffn_g, moe_w_group, moe_b_group, moe_w_router, moe_b_router, moe_w_gate, moe_w_up, moe_w_down, final_norm_g):
    p = dict(norm_mix_g=norm_mix_g, w_in_bf16=_cast_bf16(w_in), ssm_conv_w=ssm_conv_w, ssm_conv_b=ssm_conv_b,
             ssm_dt_bias=ssm_dt_bias, ssm_a_log=ssm_a_log, ssm_d=ssm_d, ssm_norm_g=ssm_norm_g,
             ml_conv_w=ml_conv_w, ml_conv_b=ml_conv_b, ml_ig_bias=ml_ig_bias, ml_fg_bias=ml_fg_bias,
             ml_norm_g=ml_norm_g, mla_q_norm_g=mla_q_norm_g, mla_w_uq=mla_w_uq,
             mla_kv_norm_g=mla_kv_norm_g, mla_w_ukv=mla_w_ukv, w_out=w_out, norm_ffn_g=norm_ffn_g,
             moe_w_group=moe_w_group, moe_b_group=moe_b_group, moe_w_router=moe_w_router,
             moe_b_router=moe_b_router, moe_w_gate=moe_w_gate, moe_w_up=moe_w_up, moe_w_down=moe_w_down)
    bsz, seq, _ = x.shape
    depth = w_in.shape[0]
    x2 = x.reshape(bsz * seq, D_MODEL)

    half = MLA_ROPE // 2
    inv = ROPE_THETA ** (-jnp.arange(0, MLA_ROPE, 2, dtype=F32) / MLA_ROPE)
    inv_full = _lane_row(jnp.concatenate([inv, inv]), MLA_NOPE)
    sgn_full = _lane_row(jnp.concatenate([-jnp.ones((half,), F32), jnp.ones((half,), F32)]), MLA_NOPE)
    pe_mask = _lane_row(jnp.ones((MLA_ROPE,), F32), MLA_NOPE)
    c_tok, s_tok, c_tr, s_tr = _rope_tables(positions.reshape(bsz * seq, 1), inv_full, sgn_full, bsz, seq)
    tri = jnp.tril(jnp.ones((CHUNK, CHUNK), F32))
    tri_moe = tri.astype(BF16)
    head_of = jnp.arange(GROUP_W) // HD
    seg = (head_of[:, None] == jnp.arange(128)[None, :]).astype(F32)
    bmask = (head_of[:, None] == head_of[None, :]).astype(F32)
    segm = bmask / HD

    for l in range(depth):
        w = _layer_weights(l, p)
        main, k_a, qT_a, vT_a, qT_d, k_d, vT_d = _in_proj(
            x2, w["g_mix"], w["w_all"], c_tok, s_tok, c_tr, s_tr, w["mla_gq"], w["mla_wq2"], w["mla_gkv"],
            w["mla_wk"], w["mla_wv"], pe_mask, bsz, seq)
        y_a = _attention(qT_a, k_a, vT_a, bsz, seq, "attn_moba")
        y_b = _ssd(main, w["ssm_cw"], w["ssm_cb"], w["ssm_dtb"], w["ssm_arow"], w["ssm_dexp"], w["ssm_ng"],
                   tri, bsz, seq)
        y_c = _mlstm(main, w["ml_cw"], w["ml_cb"], w["ml_igb"], w["ml_fgb"], w["ml_ng"], tri, seg, segm,
                     bmask, bsz, seq)
        y_d = _attention(qT_d, k_d, vT_d, bsz, seq, "attn_mla")
        xn, h2, comb = _outproj(x2, y_a, y_b, y_c, y_d, w["w_out"], w["g_ffn"], w["w_router"], w["b_router"])
        x2 = _moe(h2, comb, xn, tri_moe, w["w_gu"], w["w_down"], final_norm_g[None, :],
                  final_norm=(l == depth - 1), tm=MOE_TM)
    return x2.reshape(bsz, seq, D_MODEL)
```

```python
import functools
import math

import numpy as np
import jax
import jax.numpy as jnp
from jax import lax
from jax.experimental import pallas as pl
from jax.experimental.pallas import tpu as pltpu

F32 = jnp.float32
BF16 = jnp.bfloat16
HIGHEST = lax.Precision.HIGHEST

EPS = 1e-6
NEG = -1e30
LOG2E = 1.4426950408889634

D_MODEL = 1024
GROUP_W = 256
HEADS = 4
HD = 64
HP = 128
MOBA_BLOCK = 256
MOBA_TOPK = 3
SSM_STATE = 128
SSM_XBC = 768
CONV_K = 4
CHUNK = 128
MLA_NOPE = 64
MLA_ROPE = 32
ROPE_THETA = 10000.0
N_EXPERTS = 16
EXPERTS_PER_GROUP = 4
D_EXPERT = 256
GID_LANE = 20
MOE_TM = 1024

COL_ZX = 0
COL_ML = 1024
COL_SMALL = 2048
N_MAIN = 2176
VR = 80
V_COLS = 384
N_ALL = N_MAIN + 256 + 128 + 2 * HEADS * HP + V_COLS
ATT_UQ = 256
ATT_UK = 256

V7X_VMEM_LIMIT = 56 * 1024 * 1024

NT_DIMS = (((1,), (1,)), ((), ()))


def _cparams(sem):
    return pltpu.CompilerParams(dimension_semantics=sem, vmem_limit_bytes=V7X_VMEM_LIMIT)


def _rms(x, g):
    ms = jnp.mean(x * x, axis=-1, keepdims=True)
    return x * lax.rsqrt(ms + EPS) * g


def _sigmoid(x):
    return 0.5 * jnp.tanh(0.5 * x) + 0.5


def _softplus(x):
    return jnp.maximum(x, 0.0) + jnp.log1p(jnp.exp(-jnp.abs(x)))


def _tril(n):
    r = lax.broadcasted_iota(jnp.int32, (n, n), 0)
    c = lax.broadcasted_iota(jnp.int32, (n, n), 1)
    return c <= r


def _cast_body(w_ref, o_ref):
    o_ref[...] = w_ref[...].astype(BF16)


def _cast_bf16(w, rows=256):
    nl, r, c = w.shape
    return pl.pallas_call(
        _cast_body,
        grid=(nl, r // rows),
        in_specs=[pl.BlockSpec((1, rows, c), lambda l, i: (l, i, 0))],
        out_specs=pl.BlockSpec((1, rows, c), lambda l, i: (l, i, 0)),
        out_shape=jax.ShapeDtypeStruct(w.shape, BF16),
        compiler_params=_cparams(("arbitrary", "arbitrary")),
        name="cast_w_in",
    )(w)


def _ones_rows(shape):
    row = lax.broadcasted_iota(jnp.int32, shape, 0)
    hit = row == HD
    for h in range(1, HEADS):
        hit = jnp.logical_or(hit, row == h * VR + HD)
    return jnp.where(hit, 1.0, 0.0)


def _in_proj_body(x_ref, g_ref, w_ref, c_ref, s_ref, ct_ref, st_ref, gq_ref, wq_ref, gkv_ref, wk_ref,
                  wv_ref, msk_ref, om_ref, ka_ref, qa_ref, va_ref, qd_ref, kd_ref, vd_ref, km_sc,
                  *, nb, nst, qscale_a, qscale_d):
    s_idx = pl.program_id(0) % nst
    tm = x_ref.shape[0]

    @pl.when(s_idx == 0)
    def _():
        km_sc[...] = jnp.zeros(km_sc.shape, F32)

    h = _rms(x_ref[...], g_ref[...]).astype(BF16)
    for c in range(0, COL_SMALL, 512):
        om_ref[:, c:c + 512] = jnp.dot(h, w_ref[:, c:c + 512], preferred_element_type=F32)
    rest = jnp.dot(h, w_ref[:, COL_SMALL:], preferred_element_type=F32)
    sm = rest[:, 0:128]
    om_ref[:, COL_SMALL:COL_SMALL + 128] = sm
    cq = rest[:, 128:384]
    ckv = rest[:, 384:512]
    kp = rest[:, 512:1024]
    qp = rest[:, 1024:1536]
    vc = rest[:, 1536:1536 + V_COLS]

    nblk = tm // MOBA_BLOCK
    lane_k = lax.broadcasted_iota(jnp.int32, (MOBA_BLOCK, HEADS * HP), 1)
    for j in range(nblk):
        kj = kp[j * MOBA_BLOCK:(j + 1) * MOBA_BLOCK]
        blk_id = s_idx * nblk + j
        km_sc[pl.ds(blk_id, 1), :] = jnp.mean(kj, axis=0, keepdims=True)
        ka_ref[j * MOBA_BLOCK:(j + 1) * MOBA_BLOCK, :] = jnp.where(
            (lane_k & (HP - 1)) == HD + blk_id, 1.0, kj).astype(BF16)

    qT = qp.T
    km = km_sc[...]
    pos = s_idx * tm + lax.broadcasted_iota(jnp.int32, (nb, tm), 1)
    cur = lax.shift_right_logical(pos, int(math.log2(MOBA_BLOCK)))
    blk = lax.broadcasted_iota(jnp.int32, (nb, tm), 0)
    past = blk < cur
    for hd in range(HEADS):
        qh = qT[hd * HP:(hd + 1) * HP, :]
        gate = jnp.dot(km[:, hd * HP:(hd + 1) * HP], qh, precision=HIGHEST,
                       preferred_element_type=F32)
        gate = jnp.where(past, gate, -jnp.inf)
        bias = jnp.where(blk == cur, 0.0, NEG)
        for _ in range(MOBA_TOPK):
            m = jnp.max(gate, axis=0, keepdims=True)
            cand = jnp.logical_and(gate == m, gate > -jnp.inf)
            idx = jnp.min(jnp.where(cand, blk, nb), axis=0, keepdims=True)
            pick = blk == idx
            bias = jnp.where(pick, 0.0, bias)
            gate = jnp.where(pick, -jnp.inf, gate)
        rows = [qh[:HD] * qscale_a, bias]
        if HP - HD - nb:
            rows.append(jnp.zeros((HP - HD - nb, tm), F32))
        qa_ref[0, hd * HP:(hd + 1) * HP, :] = jnp.concatenate(rows, axis=0).astype(BF16)
    vt = vc.T[:HEADS * VR]
    va_ref[0] = (vt + _ones_rows(vt.shape)).astype(BF16)

    qn = _rms(cq, gq_ref[...]).astype(BF16)
    q2 = jnp.dot(qn, wq_ref[...], preferred_element_type=F32).T
    nq = HEADS * HP
    ct4 = jnp.concatenate([ct_ref[0]] * HEADS, axis=0)
    st4 = jnp.concatenate([st_ref[0]] * HEADS, axis=0)
    qd_ref[0] = ((q2[:nq] * ct4 + q2[nq:] * st4) * qscale_d).astype(BF16)
    kvn = _rms(ckv, gkv_ref[...]).astype(BF16)
    kn = jnp.dot(kvn, wk_ref[...], preferred_element_type=F32)
    pe = (sm * c_ref[...] + pltpu.roll(sm, HP - MLA_ROPE, 1) * s_ref[...]) * msk_ref[...]
    kd_ref[...] = (kn + jnp.concatenate([pe] * HEADS, axis=1)).astype(BF16)
    vtd = jnp.dot(kvn, wv_ref[...], preferred_element_type=F32).T[:HEADS * VR]
    vd_ref[0] = (vtd + _ones_rows(vtd.shape)).astype(BF16)


def _in_proj(x2, g, w_all, c_tok, s_tok, c_tr, s_tr, gq, wq2, gkv, wk, wv, pe_mask, bsz, seq, tm=512):
    t = x2.shape[0]
    nst = seq // tm
    nb = seq // MOBA_BLOCK
    assert nb <= HP - HD
    tok128 = pl.BlockSpec((tm, HP), lambda i: (i, 0))
    tr128 = pl.BlockSpec((1, HP, tm), lambda i: (i // nst, 0, i % nst))
    tr_q = pl.BlockSpec((1, HEADS * HP, tm), lambda i: (i // nst, 0, i % nst))
    tr_v = pl.BlockSpec((1, HEADS * VR, tm), lambda i: (i // nst, 0, i % nst))
    tok_k = pl.BlockSpec((tm, HEADS * HP), lambda i: (i, 0))

    def full(shape):
        return pl.BlockSpec(shape, lambda i: tuple(0 for _ in shape))

    q_shape = jax.ShapeDtypeStruct((bsz, HEADS * HP, seq), BF16)
    k_shape = jax.ShapeDtypeStruct((t, HEADS * HP), BF16)
    v_shape = jax.ShapeDtypeStruct((bsz, HEADS * VR, seq), BF16)
    return pl.pallas_call(
        functools.partial(_in_proj_body, nb=nb, nst=nst, qscale_a=(HD ** -0.5) * LOG2E,
                          qscale_d=((MLA_NOPE + MLA_ROPE) ** -0.5) * LOG2E),
        grid=(t // tm,),
        in_specs=[
            pl.BlockSpec((tm, D_MODEL), lambda i: (i, 0)),
            full((1, D_MODEL)), full((D_MODEL, N_ALL)),
            tok128, tok128, tr128, tr128,
            full((1, 256)), full((256, 2 * HEADS * HP)), full((1, 128)),
            full((128, HEADS * HP)), full((128, V_COLS)), full((1, HP)),
        ],
        out_specs=[pl.BlockSpec((tm, N_MAIN), lambda i: (i, 0)), tok_k, tr_q, tr_v, tr_q, tok_k, tr_v],
        out_shape=[jax.ShapeDtypeStruct((t, N_MAIN), F32), k_shape, q_shape, v_shape,
                   q_shape, k_shape, v_shape],
        scratch_shapes=[pltpu.VMEM((nb, HEADS * HP), F32)],
        compiler_params=_cparams(("arbitrary",)),
        name="in_proj",
    )(x2, g, w_all, c_tok, s_tok, c_tr, s_tr, gq, wq2, gkv, wk, wv, pe_mask)


def _rope_body(pos_ref, inv_ref, sgn_ref, c_ref, s_ref, ct_ref, st_ref):
    ts = pos_ref.shape[2]
    ang = inv_ref[...] * pos_ref[0].astype(F32)
    c32 = jnp.cos(ang)
    s32 = jnp.sin(ang) * sgn_ref[...]
    ct = jnp.concatenate([jnp.ones((MLA_NOPE, ts), F32), c32,
                          jnp.ones((HP - MLA_NOPE - MLA_ROPE, ts), F32)], axis=0)
    st = jnp.concatenate([jnp.zeros((MLA_NOPE, ts), F32), s32,
                          jnp.zeros((HP - MLA_NOPE - MLA_ROPE, ts), F32)], axis=0)
    ct_ref[0] = ct
    st_ref[0] = st
    c_ref[...] = ct.T
    s_ref[...] = st.T


def _rope_tables(pos3, inv_col, sgn_col, bsz, seq, ts=512):
    t = bsz * seq
    nst = seq // ts
    tok = pl.BlockSpec((ts, HP), lambda i: (i, 0))
    tr = pl.BlockSpec((1, HP, ts), lambda i: (i // nst, 0, i % nst))
    col = pl.BlockSpec((MLA_ROPE, 1), lambda i: (0, 0))
    return pl.pallas_call(
        _rope_body,
        grid=(t // ts,),
        in_specs=[pl.BlockSpec((1, 1, ts), lambda i: (i // nst, 0, i % nst)), col, col],
        out_specs=[tok, tok, tr, tr],
        out_shape=[
            jax.ShapeDtypeStruct((t, HP), F32),
            jax.ShapeDtypeStruct((t, HP), F32),
            jax.ShapeDtypeStruct((bsz, HP, seq), F32),
            jax.ShapeDtypeStruct((bsz, HP, seq), F32),
        ],
        compiler_params=_cparams(("arbitrary",)),
        name="rope_tables",
    )(pos3, inv_col, sgn_col)


def _attn_body(ii_ref, jj_ref, qT_ref, k_ref, vT_ref, o_ref, m_sc, acc_sc, *, tq, tk, ahead):
    p = pl.program_id(1)
    i = ii_ref[p]
    j = jj_ref[p]

    @pl.when(j == 0)
    def _():
        m_sc[...] = jnp.full(m_sc.shape, NEG, F32)
        acc_sc[...] = jnp.zeros(acc_sc.shape, F32)

    def scores(u):
        h, qc, kb, _ = u
        return jnp.dot(k_ref[kb * ATT_UK:(kb + 1) * ATT_UK, h * HP:(h + 1) * HP],
                       qT_ref[0, h * HP:(h + 1) * HP, qc * ATT_UQ:(qc + 1) * ATT_UQ],
                       preferred_element_type=F32)

    def update(u, s):
        h, qc, kb, masked = u
        if masked:
            kpos = kb * ATT_UK + lax.broadcasted_iota(jnp.int32, s.shape, 0)
            qpos = qc * ATT_UQ + lax.broadcasted_iota(jnp.int32, s.shape, 1)
            s = jnp.where(kpos <= qpos, s, NEG)
        alphas, pms = [], []
        for c in range(ATT_UQ // 128):
            ql = slice(qc * ATT_UQ + c * 128, qc * ATT_UQ + (c + 1) * 128)
            sc = s[:, c * 128:(c + 1) * 128]
            m_old = m_sc[h:h + 1, ql]
            m_new = jnp.maximum(m_old, jnp.max(sc, axis=0, keepdims=True))
            alphas.append(jnp.exp2(m_old - m_new))
            pms.append(jnp.exp2(sc - m_new).astype(BF16))
            m_sc[h:h + 1, ql] = m_new
        ql = slice(qc * ATT_UQ, (qc + 1) * ATT_UQ)
        rows = slice(h * VR, (h + 1) * VR)
        acc_sc[rows, ql] = jnp.concatenate(alphas, axis=1) * acc_sc[rows, ql] + jnp.dot(
            vT_ref[0, rows, kb * ATT_UK:(kb + 1) * ATT_UK], jnp.concatenate(pms, axis=1),
            preferred_element_type=F32)

    def run(units):
        pending = [scores(u) for u in units[:ahead]]
        for n, u in enumerate(units):
            s = pending.pop(0)
            if n + ahead < len(units):
                pending.append(scores(units[n + ahead]))
            update(u, s)

    def unit_list(diag):
        units = []
        for kb in range(tk // ATT_UK):
            for h in range(HEADS):
                for qc in range(tq // ATT_UQ):
                    k_lo, k_hi = kb * ATT_UK, (kb + 1) * ATT_UK - 1
                    q_lo, q_hi = qc * ATT_UQ, (qc + 1) * ATT_UQ - 1
                    if diag and k_lo > q_hi:
                        continue
                    units.append((h, qc, kb, diag and k_hi > q_lo))
        return units

    @pl.when(j < i)
    def _():
        run(unit_list(False))

    @pl.when(j == i)
    def _():
        run(unit_list(True))
        parts = []
        for h in range(HEADS):
            den = acc_sc[h * VR + HD:h * VR + HD + 1, :]
            parts.append(acc_sc[h * VR:h * VR + HD, :] * (1.0 / den))
        o_ref[...] = jnp.concatenate(parts, axis=0).T


def _attention(qT, k, vT, bsz, seq, name, tq=1024, ahead=8):
    t = k.shape[0]
    nq = seq // tq
    ii = np.concatenate([np.full(i + 1, i, np.int32) for i in range(nq)])
    jj = np.concatenate([np.arange(i + 1, dtype=np.int32) for i in range(nq)])
    grid_spec = pltpu.PrefetchScalarGridSpec(
        num_scalar_prefetch=2,
        grid=(bsz, len(ii)),
        in_specs=[
            pl.BlockSpec((1, HEADS * HP, tq), lambda b, p, ii, jj: (b, 0, ii[p])),
            pl.BlockSpec((tq, HEADS * HP), lambda b, p, ii, jj: (b * nq + jj[p], 0)),
            pl.BlockSpec((1, HEADS * VR, tq), lambda b, p, ii, jj: (b, 0, jj[p])),
        ],
        out_specs=pl.BlockSpec((tq, GROUP_W), lambda b, p, ii, jj: (b * nq + ii[p], 0)),
        scratch_shapes=[
            pltpu.VMEM((8, tq), F32),
            pltpu.VMEM((HEADS * VR, tq), F32),
        ],
    )
    return pl.pallas_call(
        functools.partial(_attn_body, tq=tq, tk=tq, ahead=ahead),
        grid_spec=grid_spec,
        out_shape=jax.ShapeDtypeStruct((t, GROUP_W), F32),
        compiler_params=_cparams(("arbitrary", "arbitrary")),
        name=name,
    )(jnp.asarray(ii), jnp.asarray(jj), qT, k, vT)


def _conv_silu(xpad, src, cw_ref, cb_ref, first, ls):
    width = xpad.shape[1]

    @pl.when(first)
    def _():
        xpad[0:8, :] = jnp.zeros((8, width), F32)

    @pl.when(jnp.logical_not(first))
    def _():
        xpad[0:8, :] = xpad[ls:ls + 8, :]

    xpad[8:ls + 8, :] = src
    acc = cb_ref[...] + cw_ref[0:1, :] * xpad[pl.ds(8 - CONV_K + 1, ls), :]
    for kk in range(1, CONV_K):
        acc = acc + cw_ref[kk:kk + 1, :] * xpad[pl.ds(8 - CONV_K + 1 + kk, ls), :]
    return acc * _sigmoid(acc)


def _ssd_body(zx_ref, sm_ref, cw_ref, cb_ref, dtb_ref, arow_ref, dexp_ref, ng_ref, tri_ref, o_ref,
              xpad, state, *, ls):
    first = pl.program_id(1) == 0

    @pl.when(first)
    def _():
        state[...] = jnp.zeros(state.shape, F32)

    xc = _conv_silu(xpad, zx_ref[:, GROUP_W:GROUP_W + SSM_XBC], cw_ref, cb_ref, first, ls)
    dt_all = _softplus(sm_ref[...] + dtb_ref[...])
    da_all = dt_all * arow_ref[...]
    lane = lax.broadcasted_iota(jnp.int32, (1, 128), 1)
    half = lane < HD
    tril = _tril(CHUNK)
    for c in range(ls // CHUNK):
        r0 = c * CHUNK
        dt_c = dt_all[r0:r0 + CHUNK]
        acum = jnp.dot(tri_ref[...], da_all[r0:r0 + CHUNK], precision=HIGHEST,
                       preferred_element_type=F32)
        acum_t = acum.T
        alast = acum[CHUNK - 1:CHUNK, :]
        for g in range(2):
            h0, h1 = 2 * g, 2 * g + 1
            xg = xc[r0:r0 + CHUNK, g * 128:(g + 1) * 128]
            bg = xc[r0:r0 + CHUNK, GROUP_W + g * 128:GROUP_W + (g + 1) * 128]
            cg = xc[r0:r0 + CHUNK, 2 * GROUP_W + g * 128:2 * GROUP_W + (g + 1) * 128]
            cgb = cg.astype(BF16)
            gram = lax.dot_general(cgb, bg.astype(BF16), NT_DIMS, preferred_element_type=F32)
            a0 = acum[:, h0:h0 + 1]
            a1 = acum[:, h1:h1 + 1]
            xdt = xg * jnp.where(half, dt_c[:, h0:h0 + 1], dt_c[:, h1:h1 + 1])
            ydiag = jnp.zeros((CHUNK, 128), F32)
            for hh, col, msk in ((h0, a0, half), (h1, a1, jnp.logical_not(half))):
                dec = jnp.exp(jnp.where(tril, col - acum_t[hh:hh + 1, :], -jnp.inf))
                ydiag = ydiag + jnp.dot((gram * dec).astype(BF16),
                                        jnp.where(msk, xdt, 0.0).astype(BF16),
                                        preferred_element_type=F32)
            st = state[g]
            yoff = jnp.dot(cgb, st.astype(BF16), preferred_element_type=F32) * jnp.where(
                half, jnp.exp(a0), jnp.exp(a1))
            l0 = alast[:, h0:h0 + 1]
            l1 = alast[:, h1:h1 + 1]
            dout = jnp.where(half, jnp.exp(l0 - a0), jnp.exp(l1 - a1))
            state[g] = (jnp.where(half, jnp.exp(l0), jnp.exp(l1)) * st
                        + jnp.dot(bg.T.astype(BF16), (xdt * dout).astype(BF16), preferred_element_type=F32))
            y = ydiag + yoff + xg * dexp_ref[:, g * 128:(g + 1) * 128]
            zg = zx_ref[r0:r0 + CHUNK, g * 128:(g + 1) * 128]
            y = y * (zg * _sigmoid(zg))
            o_ref[r0:r0 + CHUNK, g * 128:(g + 1) * 128] = _rms(y, ng_ref[:, g * 128:(g + 1) * 128])


def _ssd(main, cw, cb, dtb, arow, dexp, ng, tri, bsz, seq, ls=512):
    t = main.shape[0]
    ns = seq // ls

    def full(shape):
        return pl.BlockSpec(shape, lambda b, s: tuple(0 for _ in shape))

    return pl.pallas_call(
        functools.partial(_ssd_body, ls=ls),
        grid=(bsz, ns),
        in_specs=[
            pl.BlockSpec((ls, 1024), lambda b, s: (b * ns + s, COL_ZX // 1024)),
            pl.BlockSpec((ls, 128), lambda b, s: (b * ns + s, COL_SMALL // 128)),
            full((CONV_K, SSM_XBC)), full((1, SSM_XBC)), full((1, 128)), full((1, 128)),
            full((1, GROUP_W)), full((1, GROUP_W)), full((CHUNK, CHUNK)),
        ],
        out_specs=pl.BlockSpec((ls, GROUP_W), lambda b, s: (b * ns + s, 0)),
        out_shape=jax.ShapeDtypeStruct((t, GROUP_W), F32),
        scratch_shapes=[pltpu.VMEM((ls + 8, SSM_XBC), F32), pltpu.VMEM((2, SSM_STATE, 128), F32)],
        compiler_params=_cparams(("arbitrary", "arbitrary")),
        name="ssd",
    )(main, main, cw, cb, dtb, arow, dexp, ng, tri)


def _mlstm_body(c_ref, sm_ref, cw_ref, cb_ref, igb_ref, fgb_ref, ng_ref, tri_ref, seg_ref, segm_ref,
                bmask_ref, o_ref, xpad, cst, nst, mst, *, ls):
    first = pl.program_id(1) == 0

    @pl.when(first)
    def _():
        cst[...] = jnp.zeros(cst.shape, F32)
        nst[...] = jnp.zeros(nst.shape, F32)
        mst[...] = jnp.zeros(mst.shape, F32)

    qk = _conv_silu(xpad, c_ref[:, 0:2 * GROUP_W], cw_ref, cb_ref, first, ls)
    sm = sm_ref[...]
    ig_all = sm + igb_ref[...]
    fx = sm + fgb_ref[...]
    lf_all = jnp.minimum(fx, 0.0) - jnp.log1p(jnp.exp(-jnp.abs(fx)))
    lane = lax.broadcasted_iota(jnp.int32, (1, GROUP_W), 1)
    hms = [jnp.logical_and(lane >= h * HD, lane < (h + 1) * HD) for h in range(HEADS)]
    tril = _tril(CHUNK)
    segb = seg_ref[...].astype(BF16)
    segm_b = segm_ref[...].astype(BF16)
    n_chunks = ls // CHUNK

    local = []
    for c in range(n_chunks):
        r0 = c * CHUNK
        q = qk[r0:r0 + CHUNK, 0:GROUP_W]
        k = qk[r0:r0 + CHUNK, GROUP_W:2 * GROUP_W] * (HD ** -0.5)
        v = c_ref[r0:r0 + CHUNK, 2 * GROUP_W:3 * GROUP_W]
        ig = ig_all[r0:r0 + CHUNK]
        bcs = jnp.dot(tri_ref[...], lf_all[r0:r0 + CHUNK], precision=HIGHEST,
                      preferred_element_type=F32)
        bcs_t = bcs.T
        ig_t = ig.T
        kb = k.astype(BF16)
        num_loc = jnp.zeros((CHUNK, GROUP_W), F32)
        w_loc_x = jnp.zeros((CHUNK, GROUP_W), F32)
        heads = []
        for h in range(HEADS):
            bcol = bcs[:, 8 + h:9 + h]
            dmat = jnp.where(tril, bcol - bcs_t[8 + h:9 + h, :] + ig_t[4 + h:5 + h, :], -jnp.inf)
            m_loc = jnp.max(dmat, axis=1, keepdims=True)
            s_qk = lax.dot_general(jnp.where(hms[h], q, 0.0).astype(BF16), kb, NT_DIMS,
                                   preferred_element_type=F32)
            sc = s_qk * jnp.exp(dmat - m_loc)
            den_loc = jnp.sum(sc, axis=1, keepdims=True)
            num_loc = num_loc + jnp.dot(sc.astype(BF16), jnp.where(hms[h], v, 0.0).astype(BF16),
                                        preferred_element_type=F32)
            blast = bcs[CHUNK - 1:CHUNK, 8 + h:9 + h]
            gcol = blast - bcol + ig[:, 4 + h:5 + h]
            gmax = jnp.max(gcol, axis=0, keepdims=True)
            w_loc_x = w_loc_x + jnp.where(hms[h], jnp.exp(gcol - gmax), 0.0)
            heads.append((bcol, m_loc, den_loc, blast, gmax))
        kw = k * w_loc_x
        kwv = bmask_ref[...] * jnp.dot(kw.T.astype(BF16), v.astype(BF16), preferred_element_type=F32)
        local.append((q, num_loc, heads, kwv, jnp.sum(kw, axis=0, keepdims=True)))

    houts = []
    for c in range(n_chunks):
        q, num_loc, heads, kwv, ksum = local[c]
        c_old = cst[...]
        n_old = nst[0:1, :]
        q_c = jnp.dot(q.astype(BF16), c_old.astype(BF16), preferred_element_type=F32)
        q_n = jnp.dot((q * n_old).astype(BF16), segb, preferred_element_type=F32)
        f_x = jnp.zeros((CHUNK, GROUP_W), F32)
        w_inter_x = jnp.zeros((CHUNK, GROUP_W), F32)
        inv_den_x = jnp.zeros((CHUNK, GROUP_W), F32)
        keep_x = jnp.zeros((1, GROUP_W), F32)
        sfac_x = jnp.zeros((1, GROUP_W), F32)
        for h in range(HEADS):
            bcol, m_loc, den_loc, blast, gmax = heads[h]
            m_prev = mst[h:h + 1, 0:1]
            inter = bcol + m_prev
            m_t = jnp.maximum(inter, m_loc)
            f = jnp.exp(m_loc - m_t)
            w_inter = jnp.exp(inter - m_t)
            den = f * den_loc + w_inter * q_n[:, h:h + 1]
            inv_den = 1.0 / jnp.maximum(jnp.abs(den), jnp.exp(-m_t))
            f_x = f_x + jnp.where(hms[h], f, 0.0)
            w_inter_x = w_inter_x + jnp.where(hms[h], w_inter, 0.0)
            inv_den_x = inv_den_x + jnp.where(hms[h], inv_den, 0.0)
            m_new = jnp.maximum(blast + m_prev, gmax)
            sfac_x = sfac_x + jnp.where(hms[h], jnp.exp(gmax - m_new), 0.0)
            keep_x = keep_x + jnp.where(hms[h], jnp.exp(blast + m_prev - m_new), 0.0)
            mst[h:h + 1, :] = jnp.broadcast_to(m_new, (1, 128))
        cst[...] = keep_x * c_old + sfac_x * kwv
        nst[0:1, :] = keep_x * n_old + sfac_x * ksum
        houts.append((f_x * num_loc + w_inter_x * q_c) * inv_den_x)

    for c in range(n_chunks):
        r0 = c * CHUNK
        hh = houts[c] * _sigmoid(c_ref[r0:r0 + CHUNK, 3 * GROUP_W:4 * GROUP_W])
        sq = hh * hh
        sq_hi = sq.astype(BF16)
        sq_lo = (sq - sq_hi.astype(F32)).astype(BF16)
        ms = (jnp.dot(sq_hi, segm_b, preferred_element_type=F32)
              + jnp.dot(sq_lo, segm_b, preferred_element_type=F32))
        o_ref[r0:r0 + CHUNK, :] = hh * lax.rsqrt(ms + EPS) * ng_ref[...]


def _mlstm(main, cw, cb, igb, fgb, ng, tri, seg, segm, bmask, bsz, seq, ls=512):
    t = main.shape[0]
    ns = seq // ls

    def full(shape):
        return pl.BlockSpec(shape, lambda b, s: tuple(0 for _ in shape))

    return pl.pallas_call(
        functools.partial(_mlstm_body, ls=ls),
        grid=(bsz, ns),
        in_specs=[
            pl.BlockSpec((ls, 1024), lambda b, s: (b * ns + s, COL_ML // 1024)),
            pl.BlockSpec((ls, 128), lambda b, s: (b * ns + s, COL_SMALL // 128)),
            full((CONV_K, 2 * GROUP_W)), full((1, 2 * GROUP_W)), full((1, 128)), full((1, 128)),
            full((1, GROUP_W)), full((CHUNK, CHUNK)), full((GROUP_W, 128)), full((GROUP_W, GROUP_W)),
            full((GROUP_W, GROUP_W)),
        ],
        out_specs=pl.BlockSpec((ls, GROUP_W), lambda b, s: (b * ns + s, 0)),
        out_shape=jax.ShapeDtypeStruct((t, GROUP_W), F32),
        scratch_shapes=[
            pltpu.VMEM((ls + 8, 2 * GROUP_W), F32),
            pltpu.VMEM((GROUP_W, GROUP_W), F32),
            pltpu.VMEM((8, GROUP_W), F32),
            pltpu.VMEM((8, 128), F32),
        ],
        compiler_params=_cparams(("arbitrary", "arbitrary")),
        name="mlstm",
    )(main, main, cw, cb, igb, fgb, ng, tri, seg, segm, bmask)


def _outproj_body(x_ref, ya_ref, yb_ref, yc_ref, yd_ref, wo_ref, g2_ref, wr_ref, br_ref,
                  xn_ref, h2_ref, cb_ref):
    acc = x_ref[...]
    for gi, y_ref in enumerate((ya_ref, yb_ref, yc_ref, yd_ref)):
        acc = acc + jnp.dot(y_ref[...].astype(BF16), wo_ref[gi * GROUP_W:(gi + 1) * GROUP_W, :],
                            preferred_element_type=F32)
    xn_ref[...] = acc
    h2 = _rms(acc, g2_ref[...])
    h_hi = h2.astype(BF16)
    h2_ref[...] = h_hi
    h_lo = (h2 - h_hi.astype(F32)).astype(BF16)
    hh = jnp.dot(h_hi, wr_ref[...], preferred_element_type=F32)
    lh = jnp.dot(h_lo, wr_ref[...], preferred_element_type=F32)
    logits = hh[:, :128] + hh[:, 128:] + lh[:, :128] + br_ref[...]
    lane = lax.broadcasted_iota(jnp.int32, logits.shape, 1)
    big = jnp.int32(1 << 20)
    n_grp = N_EXPERTS // EXPERTS_PER_GROUP
    gl = jnp.where(jnp.logical_and(lane >= N_EXPERTS, lane < N_EXPERTS + n_grp), logits, -jnp.inf)
    gmax = jnp.max(gl, axis=1, keepdims=True)
    g_w = 1.0 / jnp.sum(jnp.exp(gl - gmax), axis=1, keepdims=True)
    gidx = jnp.min(jnp.where(gl == gmax, lane, big), axis=1, keepdims=True) - N_EXPERTS
    in_grp = jnp.logical_and(lane < N_EXPERTS,
                             lax.shift_right_logical(lane, int(math.log2(EXPERTS_PER_GROUP))) == gidx)
    el = jnp.where(in_grp, logits, -jnp.inf)
    emax = jnp.max(el, axis=1, keepdims=True)
    esum = jnp.sum(jnp.exp(el - emax), axis=1, keepdims=True)
    i1 = jnp.min(jnp.where(el == emax, lane, big), axis=1, keepdims=True)
    el2 = jnp.where(lane == i1, -jnp.inf, el)
    emax2 = jnp.max(el2, axis=1, keepdims=True)
    i2 = jnp.min(jnp.where(el2 == emax2, lane, big), axis=1, keepdims=True)
    p1 = 1.0 / esum
    p2 = jnp.exp(emax2 - emax) / esum
    w1 = p1 / (p1 + p2)
    w2 = p2 / (p1 + p2)
    comb = g_w * (jnp.where(lane == i1, w1, 0.0) + jnp.where(lane == i2, w2, 0.0))
    cb_ref[...] = jnp.where(lane == GID_LANE, gidx.astype(F32), comb)


def _outproj(x2, ya, yb, yc, yd, wo, g2, wr, br, tm=512):
    t = x2.shape[0]
    row = pl.BlockSpec((tm, D_MODEL), lambda i: (i, 0))
    grp = pl.BlockSpec((tm, GROUP_W), lambda i: (i, 0))

    def full(shape):
        return pl.BlockSpec(shape, lambda i: tuple(0 for _ in shape))

    return pl.pallas_call(
        _outproj_body,
        grid=(t // tm,),
        in_specs=[row, grp, grp, grp, grp, full((D_MODEL, D_MODEL)), full((1, D_MODEL)),
                  full((D_MODEL, 256)), full((1, 128))],
        out_specs=[row, row, pl.BlockSpec((tm, 128), lambda i: (i, 0))],
        out_shape=[
            jax.ShapeDtypeStruct((t, D_MODEL), F32),
            jax.ShapeDtypeStruct((t, D_MODEL), BF16),
            jax.ShapeDtypeStruct((t, 128), F32),
        ],
        compiler_params=_cparams(("arbitrary",)),
        name="outproj_router",
    )(x2, ya, yb, yc, yd, wo, g2, wr, br)


MOE_EB = EXPERTS_PER_GROUP
MOE_RB = 128


def _moe_body(h_ref, cb_ref, xn_ref, tri_ref, wgu_ref, wd_ref, gf_ref, o_ref,
              perm_t, hs, cs, ys, seg_smem, *, final_norm, rb):
    g = pl.program_id(1)
    tm = h_ref.shape[0]
    n_grp = N_EXPERTS // EXPERTS_PER_GROUP
    lane = lax.broadcasted_iota(jnp.int32, (tm, 128), 1)

    @pl.when(g == 0)
    def _():
        cb = cb_ref[...]
        gid = jnp.sum(jnp.where(lane == GID_LANE, cb, 0.0), axis=1, keepdims=True).astype(jnp.int32)
        onehot = jnp.where(lane == gid, 1.0, 0.0)
        oh16 = onehot.astype(BF16)
        tot = jnp.zeros((1, 128), F32)
        parts = []
        for r0 in range(0, tm, CHUNK):
            part = jnp.dot(tri_ref[...], oh16[r0:r0 + CHUNK], preferred_element_type=F32) + tot
            parts.append(part)
            tot = part[CHUNK - 1:CHUNK, :]
        cum = jnp.concatenate(parts, axis=0)
        lane1 = lane[0:1, :]
        base = jnp.zeros((1, 128), F32)
        run = jnp.zeros((1, 1), F32)
        for k in range(n_grp):
            base = base + jnp.where(lane1 == k, run, 0.0)
            seg_smem[k] = run[0, 0].astype(jnp.int32)
            seg_smem[n_grp + k] = tot[0, k].astype(jnp.int32)
            run = run + tot[:, k:k + 1]
        pos = jnp.sum(onehot * (base + cum - 1.0), axis=1, keepdims=True).astype(jnp.int32)
        pos_row = jnp.broadcast_to(pos.astype(F32), (tm, 128)).T[0:1, :].astype(jnp.int32)
        perm_t[...] = jnp.where(lax.broadcasted_iota(jnp.int32, (tm, tm), 1) == pos, 1.0, 0.0).astype(BF16)
        p = jnp.where(lax.broadcasted_iota(jnp.int32, (tm, tm), 0) == pos_row, 1.0, 0.0).astype(BF16)
        hs[...] = jnp.dot(p, h_ref[...], preferred_element_type=F32).astype(BF16)
        cb_hi = cb.astype(BF16)
        cb_lo = (cb - cb_hi.astype(F32)).astype(BF16)
        cs2 = jnp.dot(p, jnp.concatenate([cb_hi, cb_lo], axis=1), preferred_element_type=F32)
        cs[...] = cs2[:, :128] + cs2[:, 128:]
        ys[...] = jnp.zeros(ys.shape, F32)

    seg_lo = seg_smem[g]
    seg_hi = seg_lo + seg_smem[n_grp + g]
    lane_rb = lax.broadcasted_iota(jnp.int32, (rb, 128), 1)

    def run_block(lo):
        x = hs[lo:lo + rb, :]
        cw = cs[lo:lo + rb, :]

        def gate_up(j):
            return jnp.dot(x, wgu_ref[j], preferred_element_type=F32)

        def down(j, gu):
            col = jnp.sum(jnp.where(lane_rb == g * MOE_EB + j, cw, 0.0), axis=1,
                          keepdims=True)
            gt = gu[:, :D_EXPERT]
            hid = (gt * _sigmoid(gt)) * gu[:, D_EXPERT:] * col
            return jnp.dot(hid.astype(BF16), wd_ref[j], preferred_element_type=F32)

        total = jnp.zeros((rb, D_MODEL), F32)
        gu = gate_up(0)
        for j in range(MOE_EB):
            gu_next = gate_up(j + 1) if j + 1 < MOE_EB else None
            total = total + down(j, gu)
            gu = gu_next
        ys[lo:lo + rb, :] += total

    for b in range(tm // rb):
        lo = b * rb
        pl.when(jnp.logical_and(seg_lo < lo + rb, seg_hi > lo))(functools.partial(run_block, lo))

    @pl.when(g == n_grp - 1)
    def _():
        y = xn_ref[...] + jnp.dot(perm_t[...], ys[...].astype(BF16), preferred_element_type=F32)
        if final_norm:
            y = _rms(y, gf_ref[...])
        o_ref[...] = y


def _moe(h2, comb, xn, tri, wgu, wd, gf, final_norm, tm=1024, rb=MOE_RB):
    t = h2.shape[0]
    return pl.pallas_call(
        functools.partial(_moe_body, final_norm=final_norm, rb=rb),
        grid=(t // tm, N_EXPERTS // MOE_EB),
        in_specs=[
            pl.BlockSpec((tm, D_MODEL), lambda i, e: (i, 0)),
            pl.BlockSpec((tm, 128), lambda i, e: (i, 0)),
            pl.BlockSpec((tm, D_MODEL), lambda i, e: (i, 0)),
            pl.BlockSpec((CHUNK, CHUNK), lambda i, e: (0, 0)),
            pl.BlockSpec((MOE_EB, D_MODEL, 2 * D_EXPERT), lambda i, e: (e, 0, 0)),
            pl.BlockSpec((MOE_EB, D_EXPERT, D_MODEL), lambda i, e: (e, 0, 0)),
            pl.BlockSpec((1, D_MODEL), lambda i, e: (0, 0)),
        ],
        out_specs=pl.BlockSpec((tm, D_MODEL), lambda i, e: (i, 0)),
        out_shape=jax.ShapeDtypeStruct((t, D_MODEL), F32),
        scratch_shapes=[
            pltpu.VMEM((tm, tm), BF16),
            pltpu.VMEM((tm, D_MODEL), BF16),
            pltpu.VMEM((tm, 128), F32),
            pltpu.VMEM((tm, D_MODEL), F32),
            pltpu.SMEM((2 * (N_EXPERTS // MOE_EB),), jnp.int32),
        ],
        compiler_params=_cparams(("arbitrary", "arbitrary")),
        name="moe",
    )(h2, comb, xn, tri, wgu, wd, gf)


def _spread_heads(w, width, slab, total):
    rows = w.shape[0]
    pieces = []
    for h in range(HEADS):
        pieces.append(w[:, h * width:(h + 1) * width])
        pieces.append(jnp.zeros((rows, slab - width), w.dtype))
    if total > HEADS * slab:
        pieces.append(jnp.zeros((rows, total - HEADS * slab), w.dtype))
    return jnp.concatenate(pieces, axis=1)


def _pad_heads(w, width):
    return _spread_heads(w, width, HP, HEADS * HP)


def _v_cols(w):
    return _spread_heads(w, HD, VR, V_COLS)


def _lane_row(vals, offset, width=128):
    return jnp.zeros((1, width), F32).at[0, offset:offset + vals.shape[0]].set(vals.astype(F32))


def _split_w_in(w):
    sizes = (3 * GROUP_W, GROUP_W, SSM_XBC, HEADS, 2 * GROUP_W, GROUP_W, HEADS, HEADS, GROUP_W,
             256, 128, MLA_ROPE)
    offs = np.cumsum((0,) + sizes)
    return [w[:, int(offs[i]):int(offs[i + 1])] for i in range(len(sizes))]


def _layer_weights(l, p):
    a_qkv, s_z, s_xbc, s_dt, m_qk, m_v, m_i, m_f, m_o, d_cq, d_ckv, d_kr = _split_w_in(p["w_in_bf16"][l])
    aq, ak, av = a_qkv[:, :GROUP_W], a_qkv[:, GROUP_W:2 * GROUP_W], a_qkv[:, 2 * GROUP_W:]
    half = MLA_ROPE // 2
    small = jnp.concatenate([s_dt, m_i, m_f, jnp.zeros((D_MODEL, 64 - 12), BF16),
                             d_kr, d_kr[:, half:], d_kr[:, :half]], axis=1)
    w_all = jnp.concatenate([s_z, s_xbc, m_qk, m_v, m_o, small, d_cq, d_ckv, _pad_heads(ak, HD),
                             _pad_heads(aq, HD), _v_cols(av)], axis=1)
    w = {"w_all": w_all,
         "g_mix": p["norm_mix_g"][l][None, :]}
    w["ssm_cw"] = p["ssm_conv_w"][l]
    w["ssm_cb"] = p["ssm_conv_b"][l][None, :]
    w["ssm_dtb"] = _lane_row(p["ssm_dt_bias"][l], 0)
    w["ssm_arow"] = _lane_row(-jnp.exp(p["ssm_a_log"][l].astype(F32)), 0)
    w["ssm_dexp"] = jnp.repeat(p["ssm_d"][l].astype(F32), HD)[None, :]
    w["ssm_ng"] = p["ssm_norm_g"][l][None, :]
    w["ml_cw"] = p["ml_conv_w"][l]
    w["ml_cb"] = p["ml_conv_b"][l][None, :]
    w["ml_igb"] = _lane_row(p["ml_ig_bias"][l], 4)
    w["ml_fgb"] = _lane_row(p["ml_fg_bias"][l], 8)
    w["ml_ng"] = p["ml_norm_g"][l][None, :]
    wuq = p["mla_w_uq"][l]
    qd = MLA_NOPE + MLA_ROPE
    swapped = []
    for h in range(HEADS):
        c0 = h * qd + MLA_NOPE
        swapped += [jnp.zeros((256, MLA_NOPE), F32), wuq[:, c0 + half:c0 + 2 * half], wuq[:, c0:c0 + half]]
    wq = _spread_heads(wuq, qd, HP, HEADS * HP)
    wqs = _spread_heads(jnp.concatenate(swapped, axis=1), qd, HP, HEADS * HP)
    w["mla_wq2"] = jnp.concatenate([wq, wqs], axis=1).astype(BF16)
    wukv = p["mla_w_ukv"][l]
    wk_cols = jnp.concatenate([wukv[:, h * 2 * HD:h * 2 * HD + MLA_NOPE] for h in range(HEADS)], axis=1)
    wv_cols = jnp.concatenate([wukv[:, h * 2 * HD + MLA_NOPE:(h + 1) * 2 * HD] for h in range(HEADS)], axis=1)
    w["mla_wk"] = _pad_heads(wk_cols, MLA_NOPE).astype(BF16)
    w["mla_wv"] = _v_cols(wv_cols).astype(BF16)
    w["mla_gq"] = p["mla_q_norm_g"][l][None, :]
    w["mla_gkv"] = p["mla_kv_norm_g"][l][None, :]
    w["w_out"] = p["w_out"][l].astype(BF16)
    w["g_ffn"] = p["norm_ffn_g"][l][None, :]
    wr = jnp.zeros((D_MODEL, 128), F32)
    wr = wr.at[:, :N_EXPERTS].set(p["moe_w_router"][l]).at[:, N_EXPERTS:N_EXPERTS + 4].set(p["moe_w_group"][l])
    wr_hi = wr.astype(BF16)
    w["w_router"] = jnp.concatenate([wr_hi, (wr - wr_hi.astype(F32)).astype(BF16)], axis=1)
    w["b_router"] = _lane_row(p["moe_b_router"][l], 0) + _lane_row(p["moe_b_group"][l], N_EXPERTS)
    w["w_gu"] = jnp.concatenate([p["moe_w_gate"][l], p["moe_w_up"][l]], axis=2).astype(BF16)
    w["w_down"] = p["moe_w_down"][l].astype(BF16)
    return w


def kernel(x, positions, norm_mix_g, w_in, ssm_conv_w, ssm_conv_b, ssm_dt_bias, ssm_a_log, ssm_d, ssm_norm_g, ml_conv_w, ml_conv_b, ml_ig_bias, ml_fg_bias, ml_norm_g, mla_q_norm_g, mla_w_uq, mla_kv_norm_g, mla_w_ukv, w_out, norm_ffn_g, moe_w_group, moe_b_group, moe_w_router, moe_b_router, moe_w_gate, moe_w_up, moe_w_down, final_norm_g):
    p = dict(norm_mix_g=norm_mix_g, w_in_bf16=_cast_bf16(w_in), ssm_conv_w=ssm_conv_w, ssm_conv_b=ssm_conv_b,
             ssm_dt_bias=ssm_dt_bias, ssm_a_log=ssm_a_log, ssm_d=ssm_d, ssm_norm_g=ssm_norm_g,
             ml_conv_w=ml_conv_w, ml_conv_b=ml_conv_b, ml_ig_bias=ml_ig_bias, ml_fg_bias=ml_fg_bias,
             ml_norm_g=ml_norm_g, mla_q_norm_g=mla_q_norm_g, mla_w_uq=mla_w_uq,
             mla_kv_norm_g=mla_kv_norm_g, mla_w_ukv=mla_w_ukv, w_out=w_out, norm_ffn_g=norm_ffn_g,
             moe_w_group=moe_w_group, moe_b_group=moe_b_group, moe_w_router=moe_w_router,
             moe_b_router=moe_b_router, moe_w_gate=moe_w_gate, moe_w_up=moe_w_up, moe_w_down=moe_w_down)
    bsz, seq, _ = x.shape
    depth = w_in.shape[0]
    x2 = x.reshape(bsz * seq, D_MODEL)

    half = MLA_ROPE // 2
    inv = ROPE_THETA ** (-jnp.arange(0, MLA_ROPE, 2, dtype=F32) / MLA_ROPE)
    inv_col = jnp.concatenate([inv, inv])[:, None]
    sgn_col = jnp.concatenate([-jnp.ones((half,), F32), jnp.ones((half,), F32)])[:, None]
    pe_mask = _lane_row(jnp.ones((MLA_ROPE,), F32), MLA_NOPE)
    c_tok, s_tok, c_tr, s_tr = _rope_tables(positions.reshape(bsz, 1, seq), inv_col, sgn_col, bsz, seq)
    tri = jnp.tril(jnp.ones((CHUNK, CHUNK), F32))
    tri_moe = tri.astype(BF16)
    head_of = jnp.arange(GROUP_W) // HD
    seg = (head_of[:, None] == jnp.arange(128)[None, :]).astype(F32)
    bmask = (head_of[:, None] == head_of[None, :]).astype(F32)
    segm = bmask / HD

    for l in range(depth):
        w = _layer_weights(l, p)
        main, k_a, qT_a, vT_a, qT_d, k_d, vT_d = _in_proj(
            x2, w["g_mix"], w["w_all"], c_tok, s_tok, c_tr, s_tr, w["mla_gq"], w["mla_wq2"], w["mla_gkv"],
            w["mla_wk"], w["mla_wv"], pe_mask, bsz, seq)
        y_a = _attention(qT_a, k_a, vT_a, bsz, seq, "attn_moba", ahead=5)
        y_b = _ssd(main, w["ssm_cw"], w["ssm_cb"], w["ssm_dtb"], w["ssm_arow"], w["ssm_dexp"], w["ssm_ng"],
                   tri, bsz, seq, ls=(512, 1024)[l % 2])
        y_c = _mlstm(main, w["ml_cw"], w["ml_cb"], w["ml_igb"], w["ml_fgb"], w["ml_ng"], tri, seg, segm,
                     bmask, bsz, seq, ls=(512, 256)[l % 2])
        y_d = _attention(qT_d, k_d, vT_d, bsz, seq, "attn_mla")
        xn, h2, comb = _outproj(x2, y_a, y_b, y_c, y_d, w["w_out"], w["g_ffn"], w["w_router"], w["b_router"])
        x2 = _moe(h2, comb, xn, tri_moe, w["w_gu"], w["w_down"], final_norm_g[None, :],
                  final_norm=(l == depth - 1), tm=MOE_TM, rb=(128, 256)[l % 2])
    return x2.reshape(bsz, seq, D_MODEL)
```

```python
import functools
import math

import numpy as np
import jax
import jax.numpy as jnp
from jax import lax
from jax.experimental import pallas as pl
from jax.experimental.pallas import tpu as pltpu

F32 = jnp.float32
BF16 = jnp.bfloat16
HIGHEST = lax.Precision.HIGHEST

EPS = 1e-6
NEG = -1e30
LOG2E = 1.4426950408889634

D_MODEL = 1024
GROUP_W = 256
HEADS = 4
HD = 64
HP = 128
MOBA_BLOCK = 256
MOBA_TOPK = 3
SSM_STATE = 128
SSM_XBC = 768
CONV_K = 4
CHUNK = 128
MLA_NOPE = 64
MLA_ROPE = 32
ROPE_THETA = 10000.0
N_EXPERTS = 16
EXPERTS_PER_GROUP = 4
D_EXPERT = 256
GID_LANE = 20
MOE_TM = 1024

COL_ZX = 0
COL_ML = 1024
COL_SMALL = 2048
N_MAIN = 2176
VR = 80
V_COLS = 384
N_ALL = N_MAIN + 256 + 128 + 2 * HEADS * HP + V_COLS
ATT_UQ = 256
ATT_UK = 256

V7X_VMEM_LIMIT = 56 * 1024 * 1024

NT_DIMS = (((1,), (1,)), ((), ()))


def _cparams(sem):
    return pltpu.CompilerParams(dimension_semantics=sem, vmem_limit_bytes=V7X_VMEM_LIMIT)


def _rms(x, g):
    ms = jnp.mean(x * x, axis=-1, keepdims=True)
    return x * lax.rsqrt(ms + EPS) * g


def _sigmoid(x):
    return 0.5 * jnp.tanh(0.5 * x) + 0.5


def _softplus(x):
    return jnp.maximum(x, 0.0) + jnp.log1p(jnp.exp(-jnp.abs(x)))


def _tril(n):
    r = lax.broadcasted_iota(jnp.int32, (n, n), 0)
    c = lax.broadcasted_iota(jnp.int32, (n, n), 1)
    return c <= r


def _cast_body(w_ref, o_ref):
    o_ref[...] = w_ref[...].astype(BF16)


def _cast_bf16(w, rows=256):
    nl, r, c = w.shape
    return pl.pallas_call(
        _cast_body,
        grid=(nl, r // rows),
        in_specs=[pl.BlockSpec((1, rows, c), lambda l, i: (l, i, 0))],
        out_specs=pl.BlockSpec((1, rows, c), lambda l, i: (l, i, 0)),
        out_shape=jax.ShapeDtypeStruct(w.shape, BF16),
        compiler_params=_cparams(("arbitrary", "arbitrary")),
        name="cast_w_in",
    )(w)


def _ones_rows(shape):
    row = lax.broadcasted_iota(jnp.int32, shape, 0)
    hit = row == HD
    for h in range(1, HEADS):
        hit = jnp.logical_or(hit, row == h * VR + HD)
    return jnp.where(hit, 1.0, 0.0)


def _in_proj_body(x_ref, g_ref, w_ref, c_ref, s_ref, ct_ref, st_ref, gq_ref, wq_ref, gkv_ref, wk_ref,
                  wv_ref, msk_ref, om_ref, ka_ref, qa_ref, va_ref, qd_ref, kd_ref, vd_ref, km_sc,
                  *, nb, nst, qscale_a, qscale_d):
    s_idx = pl.program_id(0) % nst
    tm = x_ref.shape[0]

    @pl.when(s_idx == 0)
    def _():
        km_sc[...] = jnp.zeros(km_sc.shape, F32)

    h = _rms(x_ref[...], g_ref[...]).astype(BF16)
    for c in range(0, COL_SMALL, 512):
        om_ref[:, c:c + 512] = jnp.dot(h, w_ref[:, c:c + 512], preferred_element_type=F32)
    rest = jnp.dot(h, w_ref[:, COL_SMALL:], preferred_element_type=F32)
    sm = rest[:, 0:128]
    om_ref[:, COL_SMALL:COL_SMALL + 128] = sm
    cq = rest[:, 128:384]
    ckv = rest[:, 384:512]
    kp = rest[:, 512:1024]
    qp = rest[:, 1024:1536]
    vc = rest[:, 1536:1536 + V_COLS]

    nblk = tm // MOBA_BLOCK
    lane_k = lax.broadcasted_iota(jnp.int32, (MOBA_BLOCK, HEADS * HP), 1)
    for j in range(nblk):
        kj = kp[j * MOBA_BLOCK:(j + 1) * MOBA_BLOCK]
        blk_id = s_idx * nblk + j
        km_sc[pl.ds(blk_id, 1), :] = jnp.mean(kj, axis=0, keepdims=True)
        ka_ref[j * MOBA_BLOCK:(j + 1) * MOBA_BLOCK, :] = jnp.where(
            (lane_k & (HP - 1)) == HD + blk_id, 1.0, kj).astype(BF16)

    qT = qp.T
    km = km_sc[...]
    pos = s_idx * tm + lax.broadcasted_iota(jnp.int32, (nb, tm), 1)
    cur = lax.shift_right_logical(pos, int(math.log2(MOBA_BLOCK)))
    blk = lax.broadcasted_iota(jnp.int32, (nb, tm), 0)
    past = blk < cur
    for hd in range(HEADS):
        qh = qT[hd * HP:(hd + 1) * HP, :]
        gate = jnp.dot(km[:, hd * HP:(hd + 1) * HP], qh, precision=HIGHEST,
                       preferred_element_type=F32)
        gate = jnp.where(past, gate, -jnp.inf)
        bias = jnp.where(blk == cur, 0.0, NEG)
        for _ in range(MOBA_TOPK):
            m = jnp.max(gate, axis=0, keepdims=True)
            cand = jnp.logical_and(gate == m, gate > -jnp.inf)
            idx = jnp.min(jnp.where(cand, blk, nb), axis=0, keepdims=True)
            pick = blk == idx
            bias = jnp.where(pick, 0.0, bias)
            gate = jnp.where(pick, -jnp.inf, gate)
        rows = [qh[:HD] * qscale_a, bias]
        if HP - HD - nb:
            rows.append(jnp.zeros((HP - HD - nb, tm), F32))
        qa_ref[0, hd * HP:(hd + 1) * HP, :] = jnp.concatenate(rows, axis=0).astype(BF16)
    vt = vc.T[:HEADS * VR]
    va_ref[0] = (vt + _ones_rows(vt.shape)).astype(BF16)

    qn = _rms(cq, gq_ref[...]).astype(BF16)
    q2 = jnp.dot(qn, wq_ref[...], preferred_element_type=F32).T
    nq = HEADS * HP
    ct4 = jnp.concatenate([ct_ref[0]] * HEADS, axis=0)
    st4 = jnp.concatenate([st_ref[0]] * HEADS, axis=0)
    qd_ref[0] = ((q2[:nq] * ct4 + q2[nq:] * st4) * qscale_d).astype(BF16)
    kvn = _rms(ckv, gkv_ref[...]).astype(BF16)
    kn = jnp.dot(kvn, wk_ref[...], preferred_element_type=F32)
    pe = (sm * c_ref[...] + pltpu.roll(sm, HP - MLA_ROPE, 1) * s_ref[...]) * msk_ref[...]
    kd_ref[...] = (kn + jnp.concatenate([pe] * HEADS, axis=1)).astype(BF16)
    vtd = jnp.dot(kvn, wv_ref[...], preferred_element_type=F32).T[:HEADS * VR]
    vd_ref[0] = (vtd + _ones_rows(vtd.shape)).astype(BF16)


def _in_proj(x2, g, w_all, c_tok, s_tok, c_tr, s_tr, gq, wq2, gkv, wk, wv, pe_mask, bsz, seq, tm=512):
    t = x2.shape[0]
    nst = seq // tm
    nb = seq // MOBA_BLOCK
    assert nb <= HP - HD
    tok128 = pl.BlockSpec((tm, HP), lambda i: (i, 0))
    tr128 = pl.BlockSpec((1, HP, tm), lambda i: (i // nst, 0, i % nst))
    tr_q = pl.BlockSpec((1, HEADS * HP, tm), lambda i: (i // nst, 0, i % nst))
    tr_v = pl.BlockSpec((1, HEADS * VR, tm), lambda i: (i // nst, 0, i % nst))
    tok_k = pl.BlockSpec((tm, HEADS * HP), lambda i: (i, 0))

    def full(shape):
        return pl.BlockSpec(shape, lambda i: tuple(0 for _ in shape))

    q_shape = jax.ShapeDtypeStruct((bsz, HEADS * HP, seq), BF16)
    k_shape = jax.ShapeDtypeStruct((t, HEADS * HP), BF16)
    v_shape = jax.ShapeDtypeStruct((bsz, HEADS * VR, seq), BF16)
    return pl.pallas_call(
        functools.partial(_in_proj_body, nb=nb, nst=nst, qscale_a=(HD ** -0.5) * LOG2E,
                          qscale_d=((MLA_NOPE + MLA_ROPE) ** -0.5) * LOG2E),
        grid=(t // tm,),
        in_specs=[
            pl.BlockSpec((tm, D_MODEL), lambda i: (i, 0)),
            full((1, D_MODEL)), full((D_MODEL, N_ALL)),
            tok128, tok128, tr128, tr128,
            full((1, 256)), full((256, 2 * HEADS * HP)), full((1, 128)),
            full((128, HEADS * HP)), full((128, V_COLS)), full((1, HP)),
        ],
        out_specs=[pl.BlockSpec((tm, N_MAIN), lambda i: (i, 0)), tok_k, tr_q, tr_v, tr_q, tok_k, tr_v],
        out_shape=[jax.ShapeDtypeStruct((t, N_MAIN), F32), k_shape, q_shape, v_shape,
                   q_shape, k_shape, v_shape],
        scratch_shapes=[pltpu.VMEM((nb, HEADS * HP), F32)],
        compiler_params=_cparams(("arbitrary",)),
        name="in_proj",
    )(x2, g, w_all, c_tok, s_tok, c_tr, s_tr, gq, wq2, gkv, wk, wv, pe_mask)


def _rope_body(pos_ref, inv_ref, sgn_ref, c_ref, s_ref, ct_ref, st_ref):
    ts = pos_ref.shape[2]
    ang = inv_ref[...] * pos_ref[0].astype(F32)
    c32 = jnp.cos(ang)
    s32 = jnp.sin(ang) * sgn_ref[...]
    ct = jnp.concatenate([jnp.ones((MLA_NOPE, ts), F32), c32,
                          jnp.ones((HP - MLA_NOPE - MLA_ROPE, ts), F32)], axis=0)
    st = jnp.concatenate([jnp.zeros((MLA_NOPE, ts), F32), s32,
                          jnp.zeros((HP - MLA_NOPE - MLA_ROPE, ts), F32)], axis=0)
    ct_ref[0] = ct
    st_ref[0] = st
    c_ref[...] = ct.T
    s_ref[...] = st.T


def _rope_tables(pos3, inv_col, sgn_col, bsz, seq, ts=512):
    t = bsz * seq
    nst = seq // ts
    tok = pl.BlockSpec((ts, HP), lambda i: (i, 0))
    tr = pl.BlockSpec((1, HP, ts), lambda i: (i // nst, 0, i % nst))
    col = pl.BlockSpec((MLA_ROPE, 1), lambda i: (0, 0))
    return pl.pallas_call(
        _rope_body,
        grid=(t // ts,),
        in_specs=[pl.BlockSpec((1, 1, ts), lambda i: (i // nst, 0, i % nst)), col, col],
        out_specs=[tok, tok, tr, tr],
        out_shape=[
            jax.ShapeDtypeStruct((t, HP), F32),
            jax.ShapeDtypeStruct((t, HP), F32),
            jax.ShapeDtypeStruct((bsz, HP, seq), F32),
            jax.ShapeDtypeStruct((bsz, HP, seq), F32),
        ],
        compiler_params=_cparams(("arbitrary",)),
        name="rope_tables",
    )(pos3, inv_col, sgn_col)


def _attn_body(ii_ref, jj_ref, qT_ref, k_ref, vT_ref, o_ref, m_sc, acc_sc, *, tq, tk, ahead):
    p = pl.program_id(1)
    i = ii_ref[p]
    j = jj_ref[p]

    @pl.when(j == 0)
    def _():
        m_sc[...] = jnp.full(m_sc.shape, NEG, F32)
        acc_sc[...] = jnp.zeros(acc_sc.shape, F32)

    def scores(u):
        h, qc, kb, _ = u
        return jnp.dot(k_ref[kb * ATT_UK:(kb + 1) * ATT_UK, h * HP:(h + 1) * HP],
                       qT_ref[0, h * HP:(h + 1) * HP, qc * ATT_UQ:(qc + 1) * ATT_UQ],
                       preferred_element_type=F32)

    def update(u, s):
        h, qc, kb, masked = u
        if masked:
            kpos = kb * ATT_UK + lax.broadcasted_iota(jnp.int32, s.shape, 0)
            qpos = qc * ATT_UQ + lax.broadcasted_iota(jnp.int32, s.shape, 1)
            s = jnp.where(kpos <= qpos, s, NEG)
        alphas, pms = [], []
        for c in range(ATT_UQ // 128):
            ql = slice(qc * ATT_UQ + c * 128, qc * ATT_UQ + (c + 1) * 128)
            sc = s[:, c * 128:(c + 1) * 128]
            m_old = m_sc[h:h + 1, ql]
            m_new = jnp.maximum(m_old, jnp.max(sc, axis=0, keepdims=True))
            alphas.append(jnp.exp2(m_old - m_new))
            pms.append(jnp.exp2(sc - m_new).astype(BF16))
            m_sc[h:h + 1, ql] = m_new
        ql = slice(qc * ATT_UQ, (qc + 1) * ATT_UQ)
        rows = slice(h * VR, (h + 1) * VR)
        acc_sc[rows, ql] = jnp.concatenate(alphas, axis=1) * acc_sc[rows, ql] + jnp.dot(
            vT_ref[0, rows, kb * ATT_UK:(kb + 1) * ATT_UK], jnp.concatenate(pms, axis=1),
            preferred_element_type=F32)

    def run(units):
        pending = [scores(u) for u in units[:ahead]]
        for n, u in enumerate(units):
            s = pending.pop(0)
            if n + ahead < len(units):
                pending.append(scores(units[n + ahead]))
            update(u, s)

    def unit_list(diag):
        units = []
        for kb in range(tk // ATT_UK):
            for h in range(HEADS):
                for qc in range(tq // ATT_UQ):
                    k_lo, k_hi = kb * ATT_UK, (kb + 1) * ATT_UK - 1
                    q_lo, q_hi = qc * ATT_UQ, (qc + 1) * ATT_UQ - 1
                    if diag and k_lo > q_hi:
                        continue
                    units.append((h, qc, kb, diag and k_hi > q_lo))
        return units

    @pl.when(j < i)
    def _():
        run(unit_list(False))

    @pl.when(j == i)
    def _():
        run(unit_list(True))
        parts = []
        for h in range(HEADS):
            den = acc_sc[h * VR + HD:h * VR + HD + 1, :]
            parts.append(acc_sc[h * VR:h * VR + HD, :] * (1.0 / den))
        o_ref[...] = jnp.concatenate(parts, axis=0).T


def _attention(qT, k, vT, bsz, seq, name, tq=1024, ahead=5):
    t = k.shape[0]
    nq = seq // tq
    ii = np.concatenate([np.full(i + 1, i, np.int32) for i in range(nq)])
    jj = np.concatenate([np.arange(i + 1, dtype=np.int32) for i in range(nq)])
    grid_spec = pltpu.PrefetchScalarGridSpec(
        num_scalar_prefetch=2,
        grid=(bsz, len(ii)),
        in_specs=[
            pl.BlockSpec((1, HEADS * HP, tq), lambda b, p, ii, jj: (b, 0, ii[p])),
            pl.BlockSpec((tq, HEADS * HP), lambda b, p, ii, jj: (b * nq + jj[p], 0)),
            pl.BlockSpec((1, HEADS * VR, tq), lambda b, p, ii, jj: (b, 0, jj[p])),
        ],
        out_specs=pl.BlockSpec((tq, GROUP_W), lambda b, p, ii, jj: (b * nq + ii[p], 0)),
        scratch_shapes=[
            pltpu.VMEM((8, tq), F32),
            pltpu.VMEM((HEADS * VR, tq), F32),
        ],
    )
    return pl.pallas_call(
        functools.partial(_attn_body, tq=tq, tk=tq, ahead=ahead),
        grid_spec=grid_spec,
        out_shape=jax.ShapeDtypeStruct((t, GROUP_W), F32),
        compiler_params=_cparams(("arbitrary", "arbitrary")),
        name=name,
    )(jnp.asarray(ii), jnp.asarray(jj), qT, k, vT)


def _conv_silu(xpad, src, cw_ref, cb_ref, first, ls):
    width = xpad.shape[1]

    @pl.when(first)
    def _():
        xpad[0:8, :] = jnp.zeros((8, width), F32)

    @pl.when(jnp.logical_not(first))
    def _():
        xpad[0:8, :] = xpad[ls:ls + 8, :]

    xpad[8:ls + 8, :] = src
    acc = cb_ref[...] + cw_ref[0:1, :] * xpad[pl.ds(8 - CONV_K + 1, ls), :]
    for kk in range(1, CONV_K):
        acc = acc + cw_ref[kk:kk + 1, :] * xpad[pl.ds(8 - CONV_K + 1 + kk, ls), :]
    return acc * _sigmoid(acc)


def _ssd_body(zx_ref, sm_ref, cw_ref, cb_ref, dtb_ref, arow_ref, dexp_ref, ng_ref, tri_ref, o_ref,
              xpad, state, *, ls):
    first = pl.program_id(1) == 0

    @pl.when(first)
    def _():
        state[...] = jnp.zeros(state.shape, F32)

    xc = _conv_silu(xpad, zx_ref[:, GROUP_W:GROUP_W + SSM_XBC], cw_ref, cb_ref, first, ls)
    dt_all = _softplus(sm_ref[...] + dtb_ref[...])
    da_all = dt_all * arow_ref[...]
    lane = lax.broadcasted_iota(jnp.int32, (1, 128), 1)
    half = lane < HD
    tril = _tril(CHUNK)
    for c in range(ls // CHUNK):
        r0 = c * CHUNK
        dt_c = dt_all[r0:r0 + CHUNK]
        acum = jnp.dot(tri_ref[...], da_all[r0:r0 + CHUNK], precision=HIGHEST,
                       preferred_element_type=F32)
        acum_t = acum.T
        alast = acum[CHUNK - 1:CHUNK, :]
        for g in range(2):
            h0, h1 = 2 * g, 2 * g + 1
            xg = xc[r0:r0 + CHUNK, g * 128:(g + 1) * 128]
            bg = xc[r0:r0 + CHUNK, GROUP_W + g * 128:GROUP_W + (g + 1) * 128]
            cg = xc[r0:r0 + CHUNK, 2 * GROUP_W + g * 128:2 * GROUP_W + (g + 1) * 128]
            cgb = cg.astype(BF16)
            gram = lax.dot_general(cgb, bg.astype(BF16), NT_DIMS, preferred_element_type=F32)
            a0 = acum[:, h0:h0 + 1]
            a1 = acum[:, h1:h1 + 1]
            xdt = xg * jnp.where(half, dt_c[:, h0:h0 + 1], dt_c[:, h1:h1 + 1])
            ydiag = jnp.zeros((CHUNK, 128), F32)
            for hh, col, msk in ((h0, a0, half), (h1, a1, jnp.logical_not(half))):
                dec = jnp.exp(jnp.where(tril, col - acum_t[hh:hh + 1, :], -jnp.inf))
                ydiag = ydiag + jnp.dot((gram * dec).astype(BF16),
                                        jnp.where(msk, xdt, 0.0).astype(BF16),
                                        preferred_element_type=F32)
            st = state[g]
            yoff = jnp.dot(cgb, st.astype(BF16), preferred_element_type=F32) * jnp.where(
                half, jnp.exp(a0), jnp.exp(a1))
            l0 = alast[:, h0:h0 + 1]
            l1 = alast[:, h1:h1 + 1]
            dout = jnp.where(half, jnp.exp(l0 - a0), jnp.exp(l1 - a1))
            state[g] = (jnp.where(half, jnp.exp(l0), jnp.exp(l1)) * st
                        + jnp.dot(bg.T.astype(BF16), (xdt * dout).astype(BF16), preferred_element_type=F32))
            y = ydiag + yoff + xg * dexp_ref[:, g * 128:(g + 1) * 128]
            zg = zx_ref[r0:r0 + CHUNK, g * 128:(g + 1) * 128]
            y = y * (zg * _sigmoid(zg))
            o_ref[r0:r0 + CHUNK, g * 128:(g + 1) * 128] = _rms(y, ng_ref[:, g * 128:(g + 1) * 128])


def _ssd(main, cw, cb, dtb, arow, dexp, ng, tri, bsz, seq, ls=512):
    t = main.shape[0]
    ns = seq // ls

    def full(shape):
        return pl.BlockSpec(shape, lambda b, s: tuple(0 for _ in shape))

    return pl.pallas_call(
        functools.partial(_ssd_body, ls=ls),
        grid=(bsz, ns),
        in_specs=[
            pl.BlockSpec((ls, 1024), lambda b, s: (b * ns + s, COL_ZX // 1024)),
            pl.BlockSpec((ls, 128), lambda b, s: (b * ns + s, COL_SMALL // 128)),
            full((CONV_K, SSM_XBC)), full((1, SSM_XBC)), full((1, 128)), full((1, 128)),
            full((1, GROUP_W)), full((1, GROUP_W)), full((CHUNK, CHUNK)),
        ],
        out_specs=pl.BlockSpec((ls, GROUP_W), lambda b, s: (b * ns + s, 0)),
        out_shape=jax.ShapeDtypeStruct((t, GROUP_W), F32),
        scratch_shapes=[pltpu.VMEM((ls + 8, SSM_XBC), F32), pltpu.VMEM((2, SSM_STATE, 128), F32)],
        compiler_params=_cparams(("arbitrary", "arbitrary")),
        name="ssd",
    )(main, main, cw, cb, dtb, arow, dexp, ng, tri)


def _mlstm_body(c_ref, sm_ref, cw_ref, cb_ref, igb_ref, fgb_ref, ng_ref, tri_ref, seg_ref, segm_ref,
                bmask_ref, o_ref, xpad, cst, nst, mst, *, ls):
    first = pl.program_id(1) == 0

    @pl.when(first)
    def _():
        cst[...] = jnp.zeros(cst.shape, F32)
        nst[...] = jnp.zeros(nst.shape, F32)
        mst[...] = jnp.zeros(mst.shape, F32)

    qk = _conv_silu(xpad, c_ref[:, 0:2 * GROUP_W], cw_ref, cb_ref, first, ls)
    sm = sm_ref[...]
    ig_all = sm + igb_ref[...]
    fx = sm + fgb_ref[...]
    lf_all = jnp.minimum(fx, 0.0) - jnp.log1p(jnp.exp(-jnp.abs(fx)))
    lane = lax.broadcasted_iota(jnp.int32, (1, GROUP_W), 1)
    hms = [jnp.logical_and(lane >= h * HD, lane < (h + 1) * HD) for h in range(HEADS)]
    tril = _tril(CHUNK)
    segb = seg_ref[...].astype(BF16)
    segm_b = segm_ref[...].astype(BF16)
    n_chunks = ls // CHUNK

    local = []
    for c in range(n_chunks):
        r0 = c * CHUNK
        q = qk[r0:r0 + CHUNK, 0:GROUP_W]
        k = qk[r0:r0 + CHUNK, GROUP_W:2 * GROUP_W] * (HD ** -0.5)
        v = c_ref[r0:r0 + CHUNK, 2 * GROUP_W:3 * GROUP_W]
        ig = ig_all[r0:r0 + CHUNK]
        bcs = jnp.dot(tri_ref[...], lf_all[r0:r0 + CHUNK], precision=HIGHEST,
                      preferred_element_type=F32)
        bcs_t = bcs.T
        ig_t = ig.T
        kb = k.astype(BF16)
        num_loc = jnp.zeros((CHUNK, GROUP_W), F32)
        w_loc_x = jnp.zeros((CHUNK, GROUP_W), F32)
        heads = []
        for h in range(HEADS):
            bcol = bcs[:, 8 + h:9 + h]
            dmat = jnp.where(tril, bcol - bcs_t[8 + h:9 + h, :] + ig_t[4 + h:5 + h, :], -jnp.inf)
            m_loc = jnp.max(dmat, axis=1, keepdims=True)
            s_qk = lax.dot_general(jnp.where(hms[h], q, 0.0).astype(BF16), kb, NT_DIMS,
                                   preferred_element_type=F32)
            sc = s_qk * jnp.exp(dmat - m_loc)
            den_loc = jnp.sum(sc, axis=1, keepdims=True)
            num_loc = num_loc + jnp.dot(sc.astype(BF16), jnp.where(hms[h], v, 0.0).astype(BF16),
                                        preferred_element_type=F32)
            blast = bcs[CHUNK - 1:CHUNK, 8 + h:9 + h]
            gcol = blast - bcol + ig[:, 4 + h:5 + h]
            gmax = jnp.max(gcol, axis=0, keepdims=True)
            w_loc_x = w_loc_x + jnp.where(hms[h], jnp.exp(gcol - gmax), 0.0)
            heads.append((bcol, m_loc, den_loc, blast, gmax))
        kw = k * w_loc_x
        kwv = bmask_ref[...] * jnp.dot(kw.T.astype(BF16), v.astype(BF16), preferred_element_type=F32)
        local.append((q, num_loc, heads, kwv, jnp.sum(kw, axis=0, keepdims=True)))

    houts = []
    for c in range(n_chunks):
        q, num_loc, heads, kwv, ksum = local[c]
        c_old = cst[...]
        n_old = nst[0:1, :]
        q_c = jnp.dot(q.astype(BF16), c_old.astype(BF16), preferred_element_type=F32)
        q_n = jnp.dot((q * n_old).astype(BF16), segb, preferred_element_type=F32)
        f_x = jnp.zeros((CHUNK, GROUP_W), F32)
        w_inter_x = jnp.zeros((CHUNK, GROUP_W), F32)
        inv_den_x = jnp.zeros((CHUNK, GROUP_W), F32)
        keep_x = jnp.zeros((1, GROUP_W), F32)
        sfac_x = jnp.zeros((1, GROUP_W), F32)
        for h in range(HEADS):
            bcol, m_loc, den_loc, blast, gmax = heads[h]
            m_prev = mst[h:h + 1, 0:1]
            inter = bcol + m_prev
            m_t = jnp.maximum(inter, m_loc)
            f = jnp.exp(m_loc - m_t)
            w_inter = jnp.exp(inter - m_t)
            den = f * den_loc + w_inter * q_n[:, h:h + 1]
            inv_den = 1.0 / jnp.maximum(jnp.abs(den), jnp.exp(-m_t))
            f_x = f_x + jnp.where(hms[h], f, 0.0)
            w_inter_x = w_inter_x + jnp.where(hms[h], w_inter, 0.0)
            inv_den_x = inv_den_x + jnp.where(hms[h], inv_den, 0.0)
            m_new = jnp.maximum(blast + m_prev, gmax)
            sfac_x = sfac_x + jnp.where(hms[h], jnp.exp(gmax - m_new), 0.0)
            keep_x = keep_x + jnp.where(hms[h], jnp.exp(blast + m_prev - m_new), 0.0)
            mst[h:h + 1, :] = jnp.broadcast_to(m_new, (1, 128))
        cst[...] = keep_x * c_old + sfac_x * kwv
        nst[0:1, :] = keep_x * n_old + sfac_x * ksum
        houts.append((f_x * num_loc + w_inter_x * q_c) * inv_den_x)

    for c in range(n_chunks):
        r0 = c * CHUNK
        hh = houts[c] * _sigmoid(c_ref[r0:r0 + CHUNK, 3 * GROUP_W:4 * GROUP_W])
        sq = hh * hh
        sq_hi = sq.astype(BF16)
        sq_lo = (sq - sq_hi.astype(F32)).astype(BF16)
        ms = (jnp.dot(sq_hi, segm_b, preferred_element_type=F32)
              + jnp.dot(sq_lo, segm_b, preferred_element_type=F32))
        o_ref[r0:r0 + CHUNK, :] = hh * lax.rsqrt(ms + EPS) * ng_ref[...]


def _mlstm(main, cw, cb, igb, fgb, ng, tri, seg, segm, bmask, bsz, seq, ls=256):
    t = main.shape[0]
    ns = seq // ls

    def full(shape):
        return pl.BlockSpec(shape, lambda b, s: tuple(0 for _ in shape))

    return pl.pallas_call(
        functools.partial(_mlstm_body, ls=ls),
        grid=(bsz, ns),
        in_specs=[
            pl.BlockSpec((ls, 1024), lambda b, s: (b * ns + s, COL_ML // 1024)),
            pl.BlockSpec((ls, 128), lambda b, s: (b * ns + s, COL_SMALL // 128)),
            full((CONV_K, 2 * GROUP_W)), full((1, 2 * GROUP_W)), full((1, 128)), full((1, 128)),
            full((1, GROUP_W)), full((CHUNK, CHUNK)), full((GROUP_W, 128)), full((GROUP_W, GROUP_W)),
            full((GROUP_W, GROUP_W)),
        ],
        out_specs=pl.BlockSpec((ls, GROUP_W), lambda b, s: (b * ns + s, 0)),
        out_shape=jax.ShapeDtypeStruct((t, GROUP_W), F32),
        scratch_shapes=[
            pltpu.VMEM((ls + 8, 2 * GROUP_W), F32),
            pltpu.VMEM((GROUP_W, GROUP_W), F32),
            pltpu.VMEM((8, GROUP_W), F32),
            pltpu.VMEM((8, 128), F32),
        ],
        compiler_params=_cparams(("arbitrary", "arbitrary")),
        name="mlstm",
    )(main, main, cw, cb, igb, fgb, ng, tri, seg, segm, bmask)


def _outproj_body(x_ref, ya_ref, yb_ref, yc_ref, yd_ref, wo_ref, g2_ref, wr_ref, br_ref,
                  xn_ref, h2_ref, cb_ref):
    acc = x_ref[...]
    for gi, y_ref in enumerate((ya_ref, yb_ref, yc_ref, yd_ref)):
        acc = acc + jnp.dot(y_ref[...].astype(BF16), wo_ref[gi * GROUP_W:(gi + 1) * GROUP_W, :],
                            preferred_element_type=F32)
    xn_ref[...] = acc
    h2 = _rms(acc, g2_ref[...])
    h_hi = h2.astype(BF16)
    h2_ref[...] = h_hi
    h_lo = (h2 - h_hi.astype(F32)).astype(BF16)
    hh = jnp.dot(h_hi, wr_ref[...], preferred_element_type=F32)
    lh = jnp.dot(h_lo, wr_ref[...], preferred_element_type=F32)
    logits = hh[:, :128] + hh[:, 128:] + lh[:, :128] + br_ref[...]
    lane = lax.broadcasted_iota(jnp.int32, logits.shape, 1)
    big = jnp.int32(1 << 20)
    n_grp = N_EXPERTS // EXPERTS_PER_GROUP
    gl = jnp.where(jnp.logical_and(lane >= N_EXPERTS, lane < N_EXPERTS + n_grp), logits, -jnp.inf)
    gmax = jnp.max(gl, axis=1, keepdims=True)
    g_w = 1.0 / jnp.sum(jnp.exp(gl - gmax), axis=1, keepdims=True)
    gidx = jnp.min(jnp.where(gl == gmax, lane, big), axis=1, keepdims=True) - N_EXPERTS
    in_grp = jnp.logical_and(lane < N_EXPERTS,
                             lax.shift_right_logical(lane, int(math.log2(EXPERTS_PER_GROUP))) == gidx)
    el = jnp.where(in_grp, logits, -jnp.inf)
    emax = jnp.max(el, axis=1, keepdims=True)
    esum = jnp.sum(jnp.exp(el - emax), axis=1, keepdims=True)
    i1 = jnp.min(jnp.where(el == emax, lane, big), axis=1, keepdims=True)
    el2 = jnp.where(lane == i1, -jnp.inf, el)
    emax2 = jnp.max(el2, axis=1, keepdims=True)
    i2 = jnp.min(jnp.where(el2 == emax2, lane, big), axis=1, keepdims=True)
    p1 = 1.0 / esum
    p2 = jnp.exp(emax2 - emax) / esum
    w1 = p1 / (p1 + p2)
    w2 = p2 / (p1 + p2)
    comb = g_w * (jnp.where(lane == i1, w1, 0.0) + jnp.where(lane == i2, w2, 0.0))
    cb_ref[...] = jnp.where(lane == GID_LANE, gidx.astype(F32), comb)


def _outproj(x2, ya, yb, yc, yd, wo, g2, wr, br, tm=512):
    t = x2.shape[0]
    row = pl.BlockSpec((tm, D_MODEL), lambda i: (i, 0))
    grp = pl.BlockSpec((tm, GROUP_W), lambda i: (i, 0))

    def full(shape):
        return pl.BlockSpec(shape, lambda i: tuple(0 for _ in shape))

    return pl.pallas_call(
        _outproj_body,
        grid=(t // tm,),
        in_specs=[row, grp, grp, grp, grp, full((D_MODEL, D_MODEL)), full((1, D_MODEL)),
                  full((D_MODEL, 256)), full((1, 128))],
        out_specs=[row, row, pl.BlockSpec((tm, 128), lambda i: (i, 0))],
        out_shape=[
            jax.ShapeDtypeStruct((t, D_MODEL), F32),
            jax.ShapeDtypeStruct((t, D_MODEL), BF16),
            jax.ShapeDtypeStruct((t, 128), F32),
        ],
        compiler_params=_cparams(("arbitrary",)),
        name="outproj_router",
    )(x2, ya, yb, yc, yd, wo, g2, wr, br)


MOE_EB = EXPERTS_PER_GROUP
MOE_RB = 128


def _moe_body(h_ref, cb_ref, xn_ref, tri_ref, wgu_ref, wd_ref, gf_ref, o_ref,
              perm_t, hs, cs, ys, seg_smem, *, final_norm, rb):
    g = pl.program_id(1)
    tm = h_ref.shape[0]
    n_grp = N_EXPERTS // EXPERTS_PER_GROUP
    lane = lax.broadcasted_iota(jnp.int32, (tm, 128), 1)

    @pl.when(g == 0)
    def _():
        cb = cb_ref[...]
        gid = jnp.sum(jnp.where(lane == GID_LANE, cb, 0.0), axis=1, keepdims=True).astype(jnp.int32)
        onehot = jnp.where(lane == gid, 1.0, 0.0)
        oh16 = onehot.astype(BF16)
        tot = jnp.zeros((1, 128), F32)
        parts = []
        for r0 in range(0, tm, CHUNK):
            part = jnp.dot(tri_ref[...], oh16[r0:r0 + CHUNK], preferred_element_type=F32) + tot
            parts.append(part)
            tot = part[CHUNK - 1:CHUNK, :]
        cum = jnp.concatenate(parts, axis=0)
        lane1 = lane[0:1, :]
        base = jnp.zeros((1, 128), F32)
        run = jnp.zeros((1, 1), F32)
        for k in range(n_grp):
            base = base + jnp.where(lane1 == k, run, 0.0)
            seg_smem[k] = run[0, 0].astype(jnp.int32)
            seg_smem[n_grp + k] = tot[0, k].astype(jnp.int32)
            run = run + tot[:, k:k + 1]
        pos = jnp.sum(onehot * (base + cum - 1.0), axis=1, keepdims=True).astype(jnp.int32)
        pos_row = jnp.broadcast_to(pos.astype(F32), (tm, 128)).T[0:1, :].astype(jnp.int32)
        perm_t[...] = jnp.where(lax.broadcasted_iota(jnp.int32, (tm, tm), 1) == pos, 1.0, 0.0).astype(BF16)
        p = jnp.where(lax.broadcasted_iota(jnp.int32, (tm, tm), 0) == pos_row, 1.0, 0.0).astype(BF16)
        hs[...] = jnp.dot(p, h_ref[...], preferred_element_type=F32).astype(BF16)
        cb_hi = cb.astype(BF16)
        cb_lo = (cb - cb_hi.astype(F32)).astype(BF16)
        cs2 = jnp.dot(p, jnp.concatenate([cb_hi, cb_lo], axis=1), preferred_element_type=F32)
        cs[...] = cs2[:, :128] + cs2[:, 128:]
        ys[...] = jnp.zeros(ys.shape, F32)

    seg_lo = seg_smem[g]
    seg_hi = seg_lo + seg_smem[n_grp + g]
    lane_rb = lax.broadcasted_iota(jnp.int32, (rb, 128), 1)

    def run_block(lo):
        x = hs[lo:lo + rb, :]
        cw = cs[lo:lo + rb, :]

        def gate_up(j):
            return jnp.dot(x, wgu_ref[j], preferred_element_type=F32)

        def down(j, gu):
            col = jnp.sum(jnp.where(lane_rb == g * MOE_EB + j, cw, 0.0), axis=1,
                          keepdims=True)
            gt = gu[:, :D_EXPERT]
            hid = (gt * _sigmoid(gt)) * gu[:, D_EXPERT:] * col
            return jnp.dot(hid.astype(BF16), wd_ref[j], preferred_element_type=F32)

        total = jnp.zeros((rb, D_MODEL), F32)
        gu = gate_up(0)
        for j in range(MOE_EB):
            gu_next = gate_up(j + 1) if j + 1 < MOE_EB else None
            total = total + down(j, gu)
            gu = gu_next
        ys[lo:lo + rb, :] += total

    for b in range(tm // rb):
        lo = b * rb
        pl.when(jnp.logical_and(seg_lo < lo + rb, seg_hi > lo))(functools.partial(run_block, lo))

    @pl.when(g == n_grp - 1)
    def _():
        y = xn_ref[...] + jnp.dot(perm_t[...], ys[...].astype(BF16), preferred_element_type=F32)
        if final_norm:
            y = _rms(y, gf_ref[...])
        o_ref[...] = y


def _moe(h2, comb, xn, tri, wgu, wd, gf, final_norm, tm=1024, rb=MOE_RB):
    t = h2.shape[0]
    return pl.pallas_call(
        functools.partial(_moe_body, final_norm=final_norm, rb=rb),
        grid=(t // tm, N_EXPERTS // MOE_EB),
        in_specs=[
            pl.BlockSpec((tm, D_MODEL), lambda i, e: (i, 0)),
            pl.BlockSpec((tm, 128), lambda i, e: (i, 0)),
            pl.BlockSpec((tm, D_MODEL), lambda i, e: (i, 0)),
            pl.BlockSpec((CHUNK, CHUNK), lambda i, e: (0, 0)),
            pl.BlockSpec((MOE_EB, D_MODEL, 2 * D_EXPERT), lambda i, e: (e, 0, 0)),
            pl.BlockSpec((MOE_EB, D_EXPERT, D_MODEL), lambda i, e: (e, 0, 0)),
            pl.BlockSpec((1, D_MODEL), lambda i, e: (0, 0)),
        ],
        out_specs=pl.BlockSpec((tm, D_MODEL), lambda i, e: (i, 0)),
        out_shape=jax.ShapeDtypeStruct((t, D_MODEL), F32),
        scratch_shapes=[
            pltpu.VMEM((tm, tm), BF16),
            pltpu.VMEM((tm, D_MODEL), BF16),
            pltpu.VMEM((tm, 128), F32),
            pltpu.VMEM((tm, D_MODEL), F32),
            pltpu.SMEM((2 * (N_EXPERTS // MOE_EB),), jnp.int32),
        ],
        compiler_params=_cparams(("arbitrary", "arbitrary")),
        name="moe",
    )(h2, comb, xn, tri, wgu, wd, gf)


def _spread_heads(w, width, slab, total):
    rows = w.shape[0]
    pieces = []
    for h in range(HEADS):
        pieces.append(w[:, h * width:(h + 1) * width])
        pieces.append(jnp.zeros((rows, slab - width), w.dtype))
    if total > HEADS * slab:
        pieces.append(jnp.zeros((rows, total - HEADS * slab), w.dtype))
    return jnp.concatenate(pieces, axis=1)


def _pad_heads(w, width):
    return _spread_heads(w, width, HP, HEADS * HP)


def _v_cols(w):
    return _spread_heads(w, HD, VR, V_COLS)


def _lane_row(vals, offset, width=128):
    return jnp.zeros((1, width), F32).at[0, offset:offset + vals.shape[0]].set(vals.astype(F32))


def _split_w_in(w):
    sizes = (3 * GROUP_W, GROUP_W, SSM_XBC, HEADS, 2 * GROUP_W, GROUP_W, HEADS, HEADS, GROUP_W,
             256, 128, MLA_ROPE)
    offs = np.cumsum((0,) + sizes)
    return [w[:, int(offs[i]):int(offs[i + 1])] for i in range(len(sizes))]


def _layer_weights(l, p):
    a_qkv, s_z, s_xbc, s_dt, m_qk, m_v, m_i, m_f, m_o, d_cq, d_ckv, d_kr = _split_w_in(p["w_in_bf16"][l])
    aq, ak, av = a_qkv[:, :GROUP_W], a_qkv[:, GROUP_W:2 * GROUP_W], a_qkv[:, 2 * GROUP_W:]
    half = MLA_ROPE // 2
    small = jnp.concatenate([s_dt, m_i, m_f, jnp.zeros((D_MODEL, 64 - 12), BF16),
                             d_kr, d_kr[:, half:], d_kr[:, :half]], axis=1)
    w_all = jnp.concatenate([s_z, s_xbc, m_qk, m_v, m_o, small, d_cq, d_ckv, _pad_heads(ak, HD),
                             _pad_heads(aq, HD), _v_cols(av)], axis=1)
    w = {"w_all": w_all,
         "g_mix": p["norm_mix_g"][l][None, :]}
    w["ssm_cw"] = p["ssm_conv_w"][l]
    w["ssm_cb"] = p["ssm_conv_b"][l][None, :]
    w["ssm_dtb"] = _lane_row(p["ssm_dt_bias"][l], 0)
    w["ssm_arow"] = _lane_row(-jnp.exp(p["ssm_a_log"][l].astype(F32)), 0)
    w["ssm_dexp"] = jnp.repeat(p["ssm_d"][l].astype(F32), HD)[None, :]
    w["ssm_ng"] = p["ssm_norm_g"][l][None, :]
    w["ml_cw"] = p["ml_conv_w"][l]
    w["ml_cb"] = p["ml_conv_b"][l][None, :]
    w["ml_igb"] = _lane_row(p["ml_ig_bias"][l], 4)
    w["ml_fgb"] = _lane_row(p["ml_fg_bias"][l], 8)
    w["ml_ng"] = p["ml_norm_g"][l][None, :]
    wuq = p["mla_w_uq"][l]
    qd = MLA_NOPE + MLA_ROPE
    swapped = []
    for h in range(HEADS):
        c0 = h * qd + MLA_NOPE
        swapped += [jnp.zeros((256, MLA_NOPE), F32), wuq[:, c0 + half:c0 + 2 * half], wuq[:, c0:c0 + half]]
    wq = _spread_heads(wuq, qd, HP, HEADS * HP)
    wqs = _spread_heads(jnp.concatenate(swapped, axis=1), qd, HP, HEADS * HP)
    w["mla_wq2"] = jnp.concatenate([wq, wqs], axis=1).astype(BF16)
    wukv = p["mla_w_ukv"][l]
    wk_cols = jnp.concatenate([wukv[:, h * 2 * HD:h * 2 * HD + MLA_NOPE] for h in range(HEADS)], axis=1)
    wv_cols = jnp.concatenate([wukv[:, h * 2 * HD + MLA_NOPE:(h + 1) * 2 * HD] for h in range(HEADS)], axis=1)
    w["mla_wk"] = _pad_heads(wk_cols, MLA_NOPE).astype(BF16)
    w["mla_wv"] = _v_cols(wv_cols).astype(BF16)
    w["mla_gq"] = p["mla_q_norm_g"][l][None, :]
    w["mla_gkv"] = p["mla_kv_norm_g"][l][None, :]
    w["w_out"] = p["w_out"][l].astype(BF16)
    w["g_ffn"] = p["norm_ffn_g"][l][None, :]
    wr = jnp.zeros((D_MODEL, 128), F32)
    wr = wr.at[:, :N_EXPERTS].set(p["moe_w_router"][l]).at[:, N_EXPERTS:N_EXPERTS + 4].set(p["moe_w_group"][l])
    wr_hi = wr.astype(BF16)
    w["w_router"] = jnp.concatenate([wr_hi, (wr - wr_hi.astype(F32)).astype(BF16)], axis=1)
    w["b_router"] = _lane_row(p["moe_b_router"][l], 0) + _lane_row(p["moe_b_group"][l], N_EXPERTS)
    w["w_gu"] = jnp.concatenate([p["moe_w_gate"][l], p["moe_w_up"][l]], axis=2).astype(BF16)
    w["w_down"] = p["moe_w_down"][l].astype(BF16)
    return w


def kernel(x, positions, norm_mix_g, w_in, ssm_conv_w, ssm_conv_b, ssm_dt_bias, ssm_a_log, ssm_d, ssm_norm_g, ml_conv_w, ml_conv_b, ml_ig_bias, ml_fg_bias, ml_norm_g, mla_q_norm_g, mla_w_uq, mla_kv_norm_g, mla_w_ukv, w_out, norm_ffn_g, moe_w_group, moe_b_group, moe_w_router, moe_b_router, moe_w_gate, moe_w_up, moe_w_down, final_norm_g):
    p = dict(norm_mix_g=norm_mix_g, w_in_bf16=_cast_bf16(w_in), ssm_conv_w=ssm_conv_w, ssm_conv_b=ssm_conv_b,
             ssm_dt_bias=ssm_dt_bias, ssm_a_log=ssm_a_log, ssm_d=ssm_d, ssm_norm_g=ssm_norm_g,
             ml_conv_w=ml_conv_w, ml_conv_b=ml_conv_b, ml_ig_bias=ml_ig_bias, ml_fg_bias=ml_fg_bias,
             ml_norm_g=ml_norm_g, mla_q_norm_g=mla_q_norm_g, mla_w_uq=mla_w_uq,
             mla_kv_norm_g=mla_kv_norm_g, mla_w_ukv=mla_w_ukv, w_out=w_out, norm_ffn_g=norm_ffn_g,
             moe_w_group=moe_w_group, moe_b_group=moe_b_group, moe_w_router=moe_w_router,
             moe_b_router=moe_b_router, moe_w_gate=moe_w_gate, moe_w_up=moe_w_up, moe_w_down=moe_w_down)
    bsz, seq, _ = x.shape
    depth = w_in.shape[0]
    x2 = x.reshape(bsz * seq, D_MODEL)

    half = MLA_ROPE // 2
    inv = ROPE_THETA ** (-jnp.arange(0, MLA_ROPE, 2, dtype=F32) / MLA_ROPE)
    inv_col = jnp.concatenate([inv, inv])[:, None]
    sgn_col = jnp.concatenate([-jnp.ones((half,), F32), jnp.ones((half,), F32)])[:, None]
    pe_mask = _lane_row(jnp.ones((MLA_ROPE,), F32), MLA_NOPE)
    c_tok, s_tok, c_tr, s_tr = _rope_tables(positions.reshape(bsz, 1, seq), inv_col, sgn_col, bsz, seq)
    tri = jnp.tril(jnp.ones((CHUNK, CHUNK), F32))
    tri_moe = tri.astype(BF16)
    head_of = jnp.arange(GROUP_W) // HD
    seg = (head_of[:, None] == jnp.arange(128)[None, :]).astype(F32)
    bmask = (head_of[:, None] == head_of[None, :]).astype(F32)
    segm = bmask / HD

    for l in range(depth):
        w = _layer_weights(l, p)
        main, k_a, qT_a, vT_a, qT_d, k_d, vT_d = _in_proj(
            x2, w["g_mix"], w["w_all"], c_tok, s_tok, c_tr, s_tr, w["mla_gq"], w["mla_wq2"], w["mla_gkv"],
            w["mla_wk"], w["mla_wv"], pe_mask, bsz, seq)
        y_a = _attention(qT_a, k_a, vT_a, bsz, seq, "attn_moba")
        y_b = _ssd(main, w["ssm_cw"], w["ssm_cb"], w["ssm_dtb"], w["ssm_arow"], w["ssm_dexp"], w["ssm_ng"],
                   tri, bsz, seq)
        y_c = _mlstm(main, w["ml_cw"], w["ml_cb"], w["ml_igb"], w["ml_fgb"], w["ml_ng"], tri, seg, segm,
                     bmask, bsz, seq)
        y_d = _attention(qT_d, k_d, vT_d, bsz, seq, "attn_mla")
        xn, h2, comb = _outproj(x2, y_a, y_b, y_c, y_d, w["w_out"], w["g_ffn"], w["w_router"], w["b_router"])
        x2 = _moe(h2, comb, xn, tri_moe, w["w_gu"], w["w_down"], final_norm_g[None, :],
                  final_norm=(l == depth - 1), tm=MOE_TM)
    return x2.reshape(bsz, seq, D_MODEL)
```

```python
import functools
import math

import numpy as np
import jax
import jax.numpy as jnp
from jax import lax
from jax.experimental import pallas as pl
from jax.experimental.pallas import tpu as pltpu

F32 = jnp.float32
BF16 = jnp.bfloat16
HIGHEST = lax.Precision.HIGHEST

EPS = 1e-6
NEG = -1e30
LOG2E = 1.4426950408889634

D_MODEL = 1024
GROUP_W = 256
HEADS = 4
HD = 64
HP = 128
MOBA_BLOCK = 256
MOBA_TOPK = 3
SSM_STATE = 128
SSM_XBC = 768
CONV_K = 4
CHUNK = 128
MLA_NOPE = 64
MLA_ROPE = 32
ROPE_THETA = 10000.0
N_EXPERTS = 16
EXPERTS_PER_GROUP = 4
D_EXPERT = 256
GATE_LANE = 8
GID_LANE = 20
MOE_TM = 1024

COL_ZX = 0
COL_ML = 1024
COL_SMALL = 2048
N_MAIN = 2176
VR = 80
V_COLS = 384
N_ALL = N_MAIN + 256 + 128 + 2 * HEADS * HP + V_COLS
ATT_UQ = 256
ATT_UK = 256

V7X_VMEM_LIMIT = 56 * 1024 * 1024

NT_DIMS = (((1,), (1,)), ((), ()))


def _cparams(sem):
    return pltpu.CompilerParams(dimension_semantics=sem, vmem_limit_bytes=V7X_VMEM_LIMIT)


def _rms(x, g):
    ms = jnp.mean(x * x, axis=-1, keepdims=True)
    return x * lax.rsqrt(ms + EPS) * g


def _sigmoid(x):
    return 0.5 * jnp.tanh(0.5 * x) + 0.5


def _softplus(x):
    return jnp.maximum(x, 0.0) + jnp.log1p(jnp.exp(-jnp.abs(x)))


def _tril(n):
    r = lax.broadcasted_iota(jnp.int32, (n, n), 0)
    c = lax.broadcasted_iota(jnp.int32, (n, n), 1)
    return c <= r


def _cast_body(w_ref, o_ref):
    o_ref[...] = w_ref[...].astype(BF16)


def _cast_bf16(w, rows=256):
    nl, r, c = w.shape
    return pl.pallas_call(
        _cast_body,
        grid=(nl, r // rows),
        in_specs=[pl.BlockSpec((1, rows, c), lambda l, i: (l, i, 0))],
        out_specs=pl.BlockSpec((1, rows, c), lambda l, i: (l, i, 0)),
        out_shape=jax.ShapeDtypeStruct(w.shape, BF16),
        compiler_params=_cparams(("arbitrary", "arbitrary")),
        name="cast_w_in",
    )(w)


def _ones_rows(shape):
    row = lax.broadcasted_iota(jnp.int32, shape, 0)
    hit = row == HD
    for h in range(1, HEADS):
        hit = jnp.logical_or(hit, row == h * VR + HD)
    return jnp.where(hit, 1.0, 0.0)


def _in_proj_body(x_ref, g_ref, w_ref, c_ref, s_ref, ct_ref, st_ref, gq_ref, wq_ref, gkv_ref, wk_ref,
                  wv_ref, msk_ref, om_ref, ka_ref, qa_ref, va_ref, qd_ref, kd_ref, vd_ref, km_sc,
                  *, nb, nst, qscale_a, qscale_d):
    s_idx = pl.program_id(0) % nst
    tm = x_ref.shape[0]

    @pl.when(s_idx == 0)
    def _():
        km_sc[...] = jnp.zeros(km_sc.shape, F32)

    h = _rms(x_ref[...], g_ref[...]).astype(BF16)
    for c in range(0, COL_SMALL, 512):
        om_ref[:, c:c + 512] = jnp.dot(h, w_ref[:, c:c + 512], preferred_element_type=F32)
    rest = jnp.dot(h, w_ref[:, COL_SMALL:], preferred_element_type=F32)
    sm = rest[:, 0:128]
    om_ref[:, COL_SMALL:COL_SMALL + 128] = sm
    cq = rest[:, 128:384]
    ckv = rest[:, 384:512]
    kp = rest[:, 512:1024]
    qp = rest[:, 1024:1536]
    vc = rest[:, 1536:1536 + V_COLS]

    nblk = tm // MOBA_BLOCK
    lane_k = lax.broadcasted_iota(jnp.int32, (MOBA_BLOCK, HEADS * HP), 1)
    for j in range(nblk):
        kj = kp[j * MOBA_BLOCK:(j + 1) * MOBA_BLOCK]
        blk_id = s_idx * nblk + j
        km_sc[pl.ds(blk_id, 1), :] = jnp.mean(kj, axis=0, keepdims=True)
        ka_ref[j * MOBA_BLOCK:(j + 1) * MOBA_BLOCK, :] = jnp.where(
            (lane_k & (HP - 1)) == HD + blk_id, 1.0, kj).astype(BF16)

    qT = qp.T
    km = km_sc[...]
    pos = s_idx * tm + lax.broadcasted_iota(jnp.int32, (nb, tm), 1)
    cur = lax.shift_right_logical(pos, int(math.log2(MOBA_BLOCK)))
    blk = lax.broadcasted_iota(jnp.int32, (nb, tm), 0)
    past = blk < cur
    for hd in range(HEADS):
        qh = qT[hd * HP:(hd + 1) * HP, :]
        gate = jnp.dot(km[:, hd * HP:(hd + 1) * HP], qh, precision=HIGHEST,
                       preferred_element_type=F32)
        gate = jnp.where(past, gate, -jnp.inf)
        bias = jnp.where(blk == cur, 0.0, NEG)
        for _ in range(MOBA_TOPK):
            m = jnp.max(gate, axis=0, keepdims=True)
            cand = jnp.logical_and(gate == m, gate > -jnp.inf)
            idx = jnp.min(jnp.where(cand, blk, nb), axis=0, keepdims=True)
            pick = blk == idx
            bias = jnp.where(pick, 0.0, bias)
            gate = jnp.where(pick, -jnp.inf, gate)
        rows = [qh[:HD] * qscale_a, bias]
        if HP - HD - nb:
            rows.append(jnp.zeros((HP - HD - nb, tm), F32))
        qa_ref[0, hd * HP:(hd + 1) * HP, :] = jnp.concatenate(rows, axis=0).astype(BF16)
    vt = vc.T[:HEADS * VR]
    va_ref[0] = (vt + _ones_rows(vt.shape)).astype(BF16)

    qn = _rms(cq, gq_ref[...]).astype(BF16)
    q2 = jnp.dot(qn, wq_ref[...], preferred_element_type=F32).T
    nq = HEADS * HP
    ct4 = jnp.concatenate([ct_ref[0]] * HEADS, axis=0)
    st4 = jnp.concatenate([st_ref[0]] * HEADS, axis=0)
    qd_ref[0] = ((q2[:nq] * ct4 + q2[nq:] * st4) * qscale_d).astype(BF16)
    kvn = _rms(ckv, gkv_ref[...]).astype(BF16)
    kn = jnp.dot(kvn, wk_ref[...], preferred_element_type=F32)
    pe = (sm * c_ref[...] + pltpu.roll(sm, HP - MLA_ROPE, 1) * s_ref[...]) * msk_ref[...]
    kd_ref[...] = (kn + jnp.concatenate([pe] * HEADS, axis=1)).astype(BF16)
    vtd = jnp.dot(kvn, wv_ref[...], preferred_element_type=F32).T[:HEADS * VR]
    vd_ref[0] = (vtd + _ones_rows(vtd.shape)).astype(BF16)


def _in_proj(x2, g, w_all, c_tok, s_tok, c_tr, s_tr, gq, wq2, gkv, wk, wv, pe_mask, bsz, seq, tm=512):
    t = x2.shape[0]
    nst = seq // tm
    nb = seq // MOBA_BLOCK
    assert nb <= HP - HD
    tok128 = pl.BlockSpec((tm, HP), lambda i: (i, 0))
    tr128 = pl.BlockSpec((1, HP, tm), lambda i: (i // nst, 0, i % nst))
    tr_q = pl.BlockSpec((1, HEADS * HP, tm), lambda i: (i // nst, 0, i % nst))
    tr_v = pl.BlockSpec((1, HEADS * VR, tm), lambda i: (i // nst, 0, i % nst))
    tok_k = pl.BlockSpec((tm, HEADS * HP), lambda i: (i, 0))

    def full(shape):
        return pl.BlockSpec(shape, lambda i: tuple(0 for _ in shape))

    q_shape = jax.ShapeDtypeStruct((bsz, HEADS * HP, seq), BF16)
    k_shape = jax.ShapeDtypeStruct((t, HEADS * HP), BF16)
    v_shape = jax.ShapeDtypeStruct((bsz, HEADS * VR, seq), BF16)
    return pl.pallas_call(
        functools.partial(_in_proj_body, nb=nb, nst=nst, qscale_a=(HD ** -0.5) * LOG2E,
                          qscale_d=((MLA_NOPE + MLA_ROPE) ** -0.5) * LOG2E),
        grid=(t // tm,),
        in_specs=[
            pl.BlockSpec((tm, D_MODEL), lambda i: (i, 0)),
            full((1, D_MODEL)), full((D_MODEL, N_ALL)),
            tok128, tok128, tr128, tr128,
            full((1, 256)), full((256, 2 * HEADS * HP)), full((1, 128)),
            full((128, HEADS * HP)), full((128, V_COLS)), full((1, HP)),
        ],
        out_specs=[pl.BlockSpec((tm, N_MAIN), lambda i: (i, 0)), tok_k, tr_q, tr_v, tr_q, tok_k, tr_v],
        out_shape=[jax.ShapeDtypeStruct((t, N_MAIN), F32), k_shape, q_shape, v_shape,
                   q_shape, k_shape, v_shape],
        scratch_shapes=[pltpu.VMEM((nb, HEADS * HP), F32)],
        compiler_params=_cparams(("arbitrary",)),
        name="in_proj",
    )(x2, g, w_all, c_tok, s_tok, c_tr, s_tr, gq, wq2, gkv, wk, wv, pe_mask)


def _rope_body(pos_ref, inv_ref, sgn_ref, c_ref, s_ref, ct_ref, st_ref):
    ts = pos_ref.shape[2]
    ang = inv_ref[...] * pos_ref[0].astype(F32)
    c32 = jnp.cos(ang)
    s32 = jnp.sin(ang) * sgn_ref[...]
    ct = jnp.concatenate([jnp.ones((MLA_NOPE, ts), F32), c32,
                          jnp.ones((HP - MLA_NOPE - MLA_ROPE, ts), F32)], axis=0)
    st = jnp.concatenate([jnp.zeros((MLA_NOPE, ts), F32), s32,
                          jnp.zeros((HP - MLA_NOPE - MLA_ROPE, ts), F32)], axis=0)
    ct_ref[0] = ct
    st_ref[0] = st
    c_ref[...] = ct.T
    s_ref[...] = st.T


def _rope_tables(pos3, inv_col, sgn_col, bsz, seq, ts=512):
    t = bsz * seq
    nst = seq // ts
    tok = pl.BlockSpec((ts, HP), lambda i: (i, 0))
    tr = pl.BlockSpec((1, HP, ts), lambda i: (i // nst, 0, i % nst))
    col = pl.BlockSpec((MLA_ROPE, 1), lambda i: (0, 0))
    return pl.pallas_call(
        _rope_body,
        grid=(t // ts,),
        in_specs=[pl.BlockSpec((1, 1, ts), lambda i: (i // nst, 0, i % nst)), col, col],
        out_specs=[tok, tok, tr, tr],
        out_shape=[
            jax.ShapeDtypeStruct((t, HP), F32),
            jax.ShapeDtypeStruct((t, HP), F32),
            jax.ShapeDtypeStruct((bsz, HP, seq), F32),
            jax.ShapeDtypeStruct((bsz, HP, seq), F32),
        ],
        compiler_params=_cparams(("arbitrary",)),
        name="rope_tables",
    )(pos3, inv_col, sgn_col)


def _attn_body(ii_ref, jj_ref, qT_ref, k_ref, vT_ref, o_ref, m_sc, acc_sc, *, tq, tk, ahead):
    p = pl.program_id(1)
    i = ii_ref[p]
    j = jj_ref[p]

    @pl.when(j == 0)
    def _():
        m_sc[...] = jnp.full(m_sc.shape, NEG, F32)
        acc_sc[...] = jnp.zeros(acc_sc.shape, F32)

    def scores(u):
        h, qc, kb, _ = u
        return jnp.dot(k_ref[kb * ATT_UK:(kb + 1) * ATT_UK, h * HP:(h + 1) * HP],
                       qT_ref[0, h * HP:(h + 1) * HP, qc * ATT_UQ:(qc + 1) * ATT_UQ],
                       preferred_element_type=F32)

    def update(u, s):
        h, qc, kb, masked = u
        if masked:
            kpos = kb * ATT_UK + lax.broadcasted_iota(jnp.int32, s.shape, 0)
            qpos = qc * ATT_UQ + lax.broadcasted_iota(jnp.int32, s.shape, 1)
            s = jnp.where(kpos <= qpos, s, NEG)
        alphas, pms = [], []
        for c in range(ATT_UQ // 128):
            ql = slice(qc * ATT_UQ + c * 128, qc * ATT_UQ + (c + 1) * 128)
            sc = s[:, c * 128:(c + 1) * 128]
            m_old = m_sc[h:h + 1, ql]
            m_new = jnp.maximum(m_old, jnp.max(sc, axis=0, keepdims=True))
            alphas.append(jnp.exp2(m_old - m_new))
            pms.append(jnp.exp2(sc - m_new).astype(BF16))
            m_sc[h:h + 1, ql] = m_new
        ql = slice(qc * ATT_UQ, (qc + 1) * ATT_UQ)
        rows = slice(h * VR, (h + 1) * VR)
        acc_sc[rows, ql] = jnp.concatenate(alphas, axis=1) * acc_sc[rows, ql] + jnp.dot(
            vT_ref[0, rows, kb * ATT_UK:(kb + 1) * ATT_UK], jnp.concatenate(pms, axis=1),
            preferred_element_type=F32)

    def run(units):
        pending = [scores(u) for u in units[:ahead]]
        for n, u in enumerate(units):
            s = pending.pop(0)
            if n + ahead < len(units):
                pending.append(scores(units[n + ahead]))
            update(u, s)

    def unit_list(diag):
        units = []
        for kb in range(tk // ATT_UK):
            for h in range(HEADS):
                for qc in range(tq // ATT_UQ):
                    k_lo, k_hi = kb * ATT_UK, (kb + 1) * ATT_UK - 1
                    q_lo, q_hi = qc * ATT_UQ, (qc + 1) * ATT_UQ - 1
                    if diag and k_lo > q_hi:
                        continue
                    units.append((h, qc, kb, diag and k_hi > q_lo))
        return units

    @pl.when(j < i)
    def _():
        run(unit_list(False))

    @pl.when(j == i)
    def _():
        run(unit_list(True))
        parts = []
        for h in range(HEADS):
            den = acc_sc[h * VR + HD:h * VR + HD + 1, :]
            parts.append(acc_sc[h * VR:h * VR + HD, :] * (1.0 / den))
        o_ref[...] = jnp.concatenate(parts, axis=0).T


def _attention(qT, k, vT, bsz, seq, name, tq=1024, ahead=5):
    t = k.shape[0]
    nq = seq // tq
    ii = np.concatenate([np.full(i + 1, i, np.int32) for i in range(nq)])
    jj = np.concatenate([np.arange(i + 1, dtype=np.int32) for i in range(nq)])
    grid_spec = pltpu.PrefetchScalarGridSpec(
        num_scalar_prefetch=2,
        grid=(bsz, len(ii)),
        in_specs=[
            pl.BlockSpec((1, HEADS * HP, tq), lambda b, p, ii, jj: (b, 0, ii[p])),
            pl.BlockSpec((tq, HEADS * HP), lambda b, p, ii, jj: (b * nq + jj[p], 0)),
            pl.BlockSpec((1, HEADS * VR, tq), lambda b, p, ii, jj: (b, 0, jj[p])),
        ],
        out_specs=pl.BlockSpec((tq, GROUP_W), lambda b, p, ii, jj: (b * nq + ii[p], 0)),
        scratch_shapes=[
            pltpu.VMEM((8, tq), F32),
            pltpu.VMEM((HEADS * VR, tq), F32),
        ],
    )
    return pl.pallas_call(
        functools.partial(_attn_body, tq=tq, tk=tq, ahead=ahead),
        grid_spec=grid_spec,
        out_shape=jax.ShapeDtypeStruct((t, GROUP_W), F32),
        compiler_params=_cparams(("arbitrary", "arbitrary")),
        name=name,
    )(jnp.asarray(ii), jnp.asarray(jj), qT, k, vT)


def _conv_silu(xpad, src, cw_ref, cb_ref, first, ls):
    width = xpad.shape[1]

    @pl.when(first)
    def _():
        xpad[0:8, :] = jnp.zeros((8, width), F32)

    @pl.when(jnp.logical_not(first))
    def _():
        xpad[0:8, :] = xpad[ls:ls + 8, :]

    xpad[8:ls + 8, :] = src
    acc = cb_ref[...] + cw_ref[0:1, :] * xpad[pl.ds(8 - CONV_K + 1, ls), :]
    for kk in range(1, CONV_K):
        acc = acc + cw_ref[kk:kk + 1, :] * xpad[pl.ds(8 - CONV_K + 1 + kk, ls), :]
    return acc * _sigmoid(acc)


def _ssd_body(zx_ref, sm_ref, cw_ref, cb_ref, dtb_ref, arow_ref, dexp_ref, ng_ref, tri_ref, hexp_ref,
              o_ref, xpad, state, *, ls):
    first = pl.program_id(1) == 0

    @pl.when(first)
    def _():
        state[...] = jnp.zeros(state.shape, F32)

    xc = _conv_silu(xpad, zx_ref[:, GROUP_W:GROUP_W + SSM_XBC], cw_ref, cb_ref, first, ls)
    dt_all = _softplus(sm_ref[...] + dtb_ref[...])
    da_all = dt_all * arow_ref[...]
    lane = lax.broadcasted_iota(jnp.int32, (1, 128), 1)
    half = lane < HD
    head_lanes = lane < HEADS
    hexp = hexp_ref[...].astype(BF16)
    tril = _tril(CHUNK)

    def expand(x):
        x = jnp.where(head_lanes, x, 0.0)
        hi = x.astype(BF16)
        lo = (x - hi.astype(F32)).astype(BF16)
        return (jnp.dot(hi, hexp, preferred_element_type=F32)
                + jnp.dot(lo, hexp, preferred_element_type=F32))

    for c in range(ls // CHUNK):
        r0 = c * CHUNK
        acum = jnp.dot(tri_ref[...], da_all[r0:r0 + CHUNK], precision=HIGHEST,
                       preferred_element_type=F32)
        acum_t = acum.T
        alast = acum[CHUNK - 1:CHUNK, :]
        ex = expand(jnp.concatenate([dt_all[r0:r0 + CHUNK], jnp.exp(acum), jnp.exp(alast - acum),
                                     jnp.exp(alast), jnp.zeros((7, 128), F32)], axis=0))
        dt_x = ex[0:CHUNK]
        ecol_x = ex[CHUNK:2 * CHUNK]
        dout_x = ex[2 * CHUNK:3 * CHUNK]
        cdec_x = ex[3 * CHUNK:3 * CHUNK + 1]
        for g in range(2):
            h0, h1 = 2 * g, 2 * g + 1
            gl = slice(g * 128, (g + 1) * 128)
            xg = xc[r0:r0 + CHUNK, gl]
            bg = xc[r0:r0 + CHUNK, GROUP_W + g * 128:GROUP_W + (g + 1) * 128]
            cg = xc[r0:r0 + CHUNK, 2 * GROUP_W + g * 128:2 * GROUP_W + (g + 1) * 128]
            cgb = cg.astype(BF16)
            gram = lax.dot_general(cgb, bg.astype(BF16), NT_DIMS, preferred_element_type=F32)
            xdt = xg * dt_x[:, gl]
            ydiag = jnp.zeros((CHUNK, 128), F32)
            for hh, msk in ((h0, half), (h1, jnp.logical_not(half))):
                dec = jnp.exp(jnp.where(tril, acum[:, hh:hh + 1] - acum_t[hh:hh + 1, :], -jnp.inf))
                ydiag = ydiag + jnp.dot((gram * dec).astype(BF16),
                                        jnp.where(msk, xdt, 0.0).astype(BF16),
                                        preferred_element_type=F32)
            st = state[g]
            yoff = jnp.dot(cgb, st.astype(BF16), preferred_element_type=F32) * ecol_x[:, gl]
            state[g] = (cdec_x[:, gl] * st
                        + jnp.dot(bg.T.astype(BF16), (xdt * dout_x[:, gl]).astype(BF16),
                                  preferred_element_type=F32))
            y = ydiag + yoff + xg * dexp_ref[:, g * 128:(g + 1) * 128]
            zg = zx_ref[r0:r0 + CHUNK, g * 128:(g + 1) * 128]
            y = y * (zg * _sigmoid(zg))
            o_ref[r0:r0 + CHUNK, g * 128:(g + 1) * 128] = _rms(y, ng_ref[:, g * 128:(g + 1) * 128])


def _ssd(main, cw, cb, dtb, arow, dexp, ng, tri, hexp, bsz, seq, ls=512):
    t = main.shape[0]
    ns = seq // ls

    def full(shape):
        return pl.BlockSpec(shape, lambda b, s: tuple(0 for _ in shape))

    return pl.pallas_call(
        functools.partial(_ssd_body, ls=ls),
        grid=(bsz, ns),
        in_specs=[
            pl.BlockSpec((ls, 1024), lambda b, s: (b * ns + s, COL_ZX // 1024)),
            pl.BlockSpec((ls, 128), lambda b, s: (b * ns + s, COL_SMALL // 128)),
            full((CONV_K, SSM_XBC)), full((1, SSM_XBC)), full((1, 128)), full((1, 128)),
            full((1, GROUP_W)), full((1, GROUP_W)), full((CHUNK, CHUNK)), full((128, GROUP_W)),
        ],
        out_specs=pl.BlockSpec((ls, GROUP_W), lambda b, s: (b * ns + s, 0)),
        out_shape=jax.ShapeDtypeStruct((t, GROUP_W), F32),
        scratch_shapes=[pltpu.VMEM((ls + 8, SSM_XBC), F32), pltpu.VMEM((2, SSM_STATE, 128), F32)],
        compiler_params=_cparams(("arbitrary", "arbitrary")),
        name="ssd",
    )(main, main, cw, cb, dtb, arow, dexp, ng, tri, hexp)


def _mlstm_body(c_ref, sm_ref, cw_ref, cb_ref, igb_ref, fgb_ref, ng_ref, tri_ref, seg_ref, segm_ref,
                bmask_ref, o_ref, xpad, cst, nst, mst, *, ls):
    first = pl.program_id(1) == 0

    @pl.when(first)
    def _():
        cst[...] = jnp.zeros(cst.shape, F32)
        nst[...] = jnp.zeros(nst.shape, F32)
        mst[...] = jnp.zeros(mst.shape, F32)

    qk = _conv_silu(xpad, c_ref[:, 0:2 * GROUP_W], cw_ref, cb_ref, first, ls)
    sm = sm_ref[...]
    ig_all = sm + igb_ref[...]
    fx = sm + fgb_ref[...]
    lf_all = jnp.minimum(fx, 0.0) - jnp.log1p(jnp.exp(-jnp.abs(fx)))
    lane = lax.broadcasted_iota(jnp.int32, (1, GROUP_W), 1)
    hms = [jnp.logical_and(lane >= h * HD, lane < (h + 1) * HD) for h in range(HEADS)]
    tril = _tril(CHUNK)
    lane_s = lax.broadcasted_iota(jnp.int32, (1, 128), 1)
    gate_lanes = jnp.logical_and(lane_s >= GATE_LANE, lane_s < GATE_LANE + HEADS)
    segb = seg_ref[...].astype(BF16)
    hexp = seg_ref[...].T.astype(BF16)
    segm_b = segm_ref[...].astype(BF16)
    zero_b = jnp.zeros((), BF16)
    n_chunks = ls // CHUNK

    def expand(x):
        x = jnp.where(gate_lanes, x, 0.0)
        hi = x.astype(BF16)
        lo = (x - hi.astype(F32)).astype(BF16)
        return (jnp.dot(hi, hexp, preferred_element_type=F32)
                + jnp.dot(lo, hexp, preferred_element_type=F32))

    local = []
    for c in range(n_chunks):
        r0 = c * CHUNK
        q = qk[r0:r0 + CHUNK, 0:GROUP_W]
        k = qk[r0:r0 + CHUNK, GROUP_W:2 * GROUP_W] * (HD ** -0.5)
        ig = ig_all[r0:r0 + CHUNK]
        bcs = jnp.dot(tri_ref[...], lf_all[r0:r0 + CHUNK], precision=HIGHEST,
                      preferred_element_type=F32)
        bcs_t = bcs.T
        ig_t = ig.T
        ig_al = pltpu.roll(ig, GATE_LANE - 4, 1)
        k_t = k.T.astype(BF16)
        qb = q.astype(BF16)
        vb = c_ref[r0:r0 + CHUNK, 2 * GROUP_W:3 * GROUP_W].astype(BF16)
        num_loc = jnp.zeros((CHUNK, GROUP_W), F32)
        m_loc_all = jnp.zeros((CHUNK, 128), F32)
        den_loc_all = jnp.zeros((CHUNK, 128), F32)
        for h in range(HEADS):
            bcol = bcs[:, GATE_LANE + h:GATE_LANE + h + 1]
            dmat = jnp.where(tril, bcol - bcs_t[GATE_LANE + h:GATE_LANE + h + 1, :] + ig_t[4 + h:5 + h, :],
                             -jnp.inf)
            m_loc = jnp.max(dmat, axis=1, keepdims=True)
            s_qk = jnp.dot(jnp.where(hms[h], qb, zero_b), k_t, preferred_element_type=F32)
            sc = s_qk * jnp.exp(dmat - m_loc)
            num_loc = num_loc + jnp.dot(sc.astype(BF16), jnp.where(hms[h], vb, zero_b),
                                        preferred_element_type=F32)
            on_lane = lane_s == GATE_LANE + h
            m_loc_all = jnp.where(on_lane, m_loc, m_loc_all)
            den_loc_all = jnp.where(on_lane, jnp.sum(sc, axis=1, keepdims=True), den_loc_all)
        blast = bcs[CHUNK - 1:CHUNK, :]
        gcol = blast - bcs + ig_al
        gmax = jnp.max(gcol, axis=0, keepdims=True)
        kw = k * expand(jnp.exp(gcol - gmax))
        kwv = bmask_ref[...] * jnp.dot(kw.T.astype(BF16), vb, preferred_element_type=F32)
        local.append((q, num_loc, bcs, m_loc_all, den_loc_all, blast, gmax, kwv,
                      jnp.sum(kw, axis=0, keepdims=True)))

    houts = []
    for c in range(n_chunks):
        q, num_loc, bcs, m_loc_all, den_loc_all, blast, gmax, kwv, ksum = local[c]
        c_old = cst[...]
        n_old = nst[0:1, :]
        m_prev = mst[0:1, :]
        q_c = jnp.dot(q.astype(BF16), c_old.astype(BF16), preferred_element_type=F32)
        q_n = jnp.dot((q * n_old).astype(BF16), segb, preferred_element_type=F32)
        inter = bcs + m_prev
        m_t = jnp.maximum(inter, m_loc_all)
        f = jnp.exp(m_loc_all - m_t)
        w_inter = jnp.exp(inter - m_t)
        den = f * den_loc_all + w_inter * q_n
        inv_den = 1.0 / jnp.maximum(jnp.abs(den), jnp.exp(-m_t))
        m_new = jnp.maximum(blast + m_prev, gmax)
        scal = jnp.concatenate([jnp.exp(gmax - m_new), jnp.exp(blast + m_prev - m_new),
                                jnp.zeros((6, 128), F32)], axis=0)
        ex = expand(jnp.concatenate([f, w_inter, inv_den, scal], axis=0))
        f_x = ex[0:CHUNK]
        w_inter_x = ex[CHUNK:2 * CHUNK]
        inv_den_x = ex[2 * CHUNK:3 * CHUNK]
        sfac_x = ex[3 * CHUNK:3 * CHUNK + 1]
        keep_x = ex[3 * CHUNK + 1:3 * CHUNK + 2]
        mst[0:1, :] = m_new
        cst[...] = keep_x * c_old + sfac_x * kwv
        nst[0:1, :] = keep_x * n_old + sfac_x * ksum
        houts.append((f_x * num_loc + w_inter_x * q_c) * inv_den_x)

    for c in range(n_chunks):
        r0 = c * CHUNK
        hh = houts[c] * _sigmoid(c_ref[r0:r0 + CHUNK, 3 * GROUP_W:4 * GROUP_W])
        sq = hh * hh
        sq_hi = sq.astype(BF16)
        sq_lo = (sq - sq_hi.astype(F32)).astype(BF16)
        ms = (jnp.dot(sq_hi, segm_b, preferred_element_type=F32)
              + jnp.dot(sq_lo, segm_b, preferred_element_type=F32))
        o_ref[r0:r0 + CHUNK, :] = hh * lax.rsqrt(ms + EPS) * ng_ref[...]


def _mlstm(main, cw, cb, igb, fgb, ng, tri, seg, segm, bmask, bsz, seq, ls=512):
    t = main.shape[0]
    ns = seq // ls

    def full(shape):
        return pl.BlockSpec(shape, lambda b, s: tuple(0 for _ in shape))

    return pl.pallas_call(
        functools.partial(_mlstm_body, ls=ls),
        grid=(bsz, ns),
        in_specs=[
            pl.BlockSpec((ls, 1024), lambda b, s: (b * ns + s, COL_ML // 1024)),
            pl.BlockSpec((ls, 128), lambda b, s: (b * ns + s, COL_SMALL // 128)),
            full((CONV_K, 2 * GROUP_W)), full((1, 2 * GROUP_W)), full((1, 128)), full((1, 128)),
            full((1, GROUP_W)), full((CHUNK, CHUNK)), full((GROUP_W, 128)), full((GROUP_W, GROUP_W)),
            full((GROUP_W, GROUP_W)),
        ],
        out_specs=pl.BlockSpec((ls, GROUP_W), lambda b, s: (b * ns + s, 0)),
        out_shape=jax.ShapeDtypeStruct((t, GROUP_W), F32),
        scratch_shapes=[
            pltpu.VMEM((ls + 8, 2 * GROUP_W), F32),
            pltpu.VMEM((GROUP_W, GROUP_W), F32),
            pltpu.VMEM((8, GROUP_W), F32),
            pltpu.VMEM((8, 128), F32),
        ],
        compiler_params=_cparams(("arbitrary", "arbitrary")),
        name="mlstm",
    )(main, main, cw, cb, igb, fgb, ng, tri, seg, segm, bmask)


def _outproj_body(x_ref, ya_ref, yb_ref, yc_ref, yd_ref, wo_ref, g2_ref, wr_ref, br_ref,
                  xn_ref, h2_ref, cb_ref):
    acc = x_ref[...]
    for gi, y_ref in enumerate((ya_ref, yb_ref, yc_ref, yd_ref)):
        acc = acc + jnp.dot(y_ref[...].astype(BF16), wo_ref[gi * GROUP_W:(gi + 1) * GROUP_W, :],
                            preferred_element_type=F32)
    xn_ref[...] = acc
    h2 = _rms(acc, g2_ref[...])
    h_hi = h2.astype(BF16)
    h2_ref[...] = h_hi
    h_lo = (h2 - h_hi.astype(F32)).astype(BF16)
    hh = jnp.dot(h_hi, wr_ref[...], preferred_element_type=F32)
    lh = jnp.dot(h_lo, wr_ref[...], preferred_element_type=F32)
    logits = hh[:, :128] + hh[:, 128:] + lh[:, :128] + br_ref[...]
    lane = lax.broadcasted_iota(jnp.int32, logits.shape, 1)
    big = jnp.int32(1 << 20)
    n_grp = N_EXPERTS // EXPERTS_PER_GROUP
    gl = jnp.where(jnp.logical_and(lane >= N_EXPERTS, lane < N_EXPERTS + n_grp), logits, -jnp.inf)
    gmax = jnp.max(gl, axis=1, keepdims=True)
    g_w = 1.0 / jnp.sum(jnp.exp(gl - gmax), axis=1, keepdims=True)
    gidx = jnp.min(jnp.where(gl == gmax, lane, big), axis=1, keepdims=True) - N_EXPERTS
    in_grp = jnp.logical_and(lane < N_EXPERTS,
                             lax.shift_right_logical(lane, int(math.log2(EXPERTS_PER_GROUP))) == gidx)
    el = jnp.where(in_grp, logits, -jnp.inf)
    emax = jnp.max(el, axis=1, keepdims=True)
    esum = jnp.sum(jnp.exp(el - emax), axis=1, keepdims=True)
    i1 = jnp.min(jnp.where(el == emax, lane, big), axis=1, keepdims=True)
    el2 = jnp.where(lane == i1, -jnp.inf, el)
    emax2 = jnp.max(el2, axis=1, keepdims=True)
    i2 = jnp.min(jnp.where(el2 == emax2, lane, big), axis=1, keepdims=True)
    p1 = 1.0 / esum
    p2 = jnp.exp(emax2 - emax) / esum
    w1 = p1 / (p1 + p2)
    w2 = p2 / (p1 + p2)
    comb = g_w * (jnp.where(lane == i1, w1, 0.0) + jnp.where(lane == i2, w2, 0.0))
    cb_ref[...] = jnp.where(lane == GID_LANE, gidx.astype(F32), comb)


def _outproj(x2, ya, yb, yc, yd, wo, g2, wr, br, tm=512):
    t = x2.shape[0]
    row = pl.BlockSpec((tm, D_MODEL), lambda i: (i, 0))
    grp = pl.BlockSpec((tm, GROUP_W), lambda i: (i, 0))

    def full(shape):
        return pl.BlockSpec(shape, lambda i: tuple(0 for _ in shape))

    return pl.pallas_call(
        _outproj_body,
        grid=(t // tm,),
        in_specs=[row, grp, grp, grp, grp, full((D_MODEL, D_MODEL)), full((1, D_MODEL)),
                  full((D_MODEL, 256)), full((1, 128))],
        out_specs=[row, row, pl.BlockSpec((tm, 128), lambda i: (i, 0))],
        out_shape=[
            jax.ShapeDtypeStruct((t, D_MODEL), F32),
            jax.ShapeDtypeStruct((t, D_MODEL), BF16),
            jax.ShapeDtypeStruct((t, 128), F32),
        ],
        compiler_params=_cparams(("arbitrary",)),
        name="outproj_router",
    )(x2, ya, yb, yc, yd, wo, g2, wr, br)


MOE_EB = EXPERTS_PER_GROUP
MOE_RB = 128


def _moe_body(h_ref, cb_ref, xn_ref, tri_ref, wgu_ref, wd_ref, gf_ref, o_ref,
              perm_t, hs, cs, ys, seg_smem, *, final_norm, rb):
    g = pl.program_id(1)
    tm = h_ref.shape[0]
    n_grp = N_EXPERTS // EXPERTS_PER_GROUP
    lane = lax.broadcasted_iota(jnp.int32, (tm, 128), 1)

    @pl.when(g == 0)
    def _():
        cb = cb_ref[...]
        gid = jnp.sum(jnp.where(lane == GID_LANE, cb, 0.0), axis=1, keepdims=True).astype(jnp.int32)
        onehot = jnp.where(lane == gid, 1.0, 0.0)
        oh16 = onehot.astype(BF16)
        tot = jnp.zeros((1, 128), F32)
        parts = []
        for r0 in range(0, tm, CHUNK):
            part = jnp.dot(tri_ref[...], oh16[r0:r0 + CHUNK], preferred_element_type=F32) + tot
            parts.append(part)
            tot = part[CHUNK - 1:CHUNK, :]
        cum = jnp.concatenate(parts, axis=0)
        lane1 = lane[0:1, :]
        base = jnp.zeros((1, 128), F32)
        run = jnp.zeros((1, 1), F32)
        for k in range(n_grp):
            base = base + jnp.where(lane1 == k, run, 0.0)
            seg_smem[k] = run[0, 0].astype(jnp.int32)
            seg_smem[n_grp + k] = tot[0, k].astype(jnp.int32)
            run = run + tot[:, k:k + 1]
        pos = jnp.sum(onehot * (base + cum - 1.0), axis=1, keepdims=True).astype(jnp.int32)
        pos_row = jnp.broadcast_to(pos.astype(F32), (tm, 128)).T[0:1, :].astype(jnp.int32)
        perm_t[...] = jnp.where(lax.broadcasted_iota(jnp.int32, (tm, tm), 1) == pos, 1.0, 0.0).astype(BF16)
        p = jnp.where(lax.broadcasted_iota(jnp.int32, (tm, tm), 0) == pos_row, 1.0, 0.0).astype(BF16)
        hs[...] = jnp.dot(p, h_ref[...], preferred_element_type=F32).astype(BF16)
        cb_hi = cb.astype(BF16)
        cb_lo = (cb - cb_hi.astype(F32)).astype(BF16)
        cs2 = jnp.dot(p, jnp.concatenate([cb_hi, cb_lo], axis=1), preferred_element_type=F32)
        cs[...] = cs2[:, :128] + cs2[:, 128:]
        ys[...] = jnp.zeros(ys.shape, F32)

    seg_lo = seg_smem[g]
    seg_hi = seg_lo + seg_smem[n_grp + g]
    lane_rb = lax.broadcasted_iota(jnp.int32, (rb, 128), 1)

    def run_block(lo):
        x = hs[lo:lo + rb, :]
        cw = cs[lo:lo + rb, :]

        def gate_up(j):
            return jnp.dot(x, wgu_ref[j], preferred_element_type=F32)

        def down(j, gu):
            col = jnp.sum(jnp.where(lane_rb == g * MOE_EB + j, cw, 0.0), axis=1,
                          keepdims=True)
            gt = gu[:, :D_EXPERT]
            hid = (gt * _sigmoid(gt)) * gu[:, D_EXPERT:] * col
            return jnp.dot(hid.astype(BF16), wd_ref[j], preferred_element_type=F32)

        total = jnp.zeros((rb, D_MODEL), F32)
        gu = gate_up(0)
        for j in range(MOE_EB):
            gu_next = gate_up(j + 1) if j + 1 < MOE_EB else None
            total = total + down(j, gu)
            gu = gu_next
        ys[lo:lo + rb, :] += total

    for b in range(tm // rb):
        lo = b * rb
        pl.when(jnp.logical_and(seg_lo < lo + rb, seg_hi > lo))(functools.partial(run_block, lo))

    @pl.when(g == n_grp - 1)
    def _():
        y = xn_ref[...] + jnp.dot(perm_t[...], ys[...].astype(BF16), preferred_element_type=F32)
        if final_norm:
            y = _rms(y, gf_ref[...])
        o_ref[...] = y


def _moe(h2, comb, xn, tri, wgu, wd, gf, final_norm, tm=1024, rb=MOE_RB):
    t = h2.shape[0]
    return pl.pallas_call(
        functools.partial(_moe_body, final_norm=final_norm, rb=rb),
        grid=(t // tm, N_EXPERTS // MOE_EB),
        in_specs=[
            pl.BlockSpec((tm, D_MODEL), lambda i, e: (i, 0)),
            pl.BlockSpec((tm, 128), lambda i, e: (i, 0)),
            pl.BlockSpec((tm, D_MODEL), lambda i, e: (i, 0)),
            pl.BlockSpec((CHUNK, CHUNK), lambda i, e: (0, 0)),
            pl.BlockSpec((MOE_EB, D_MODEL, 2 * D_EXPERT), lambda i, e: (e, 0, 0)),
            pl.BlockSpec((MOE_EB, D_EXPERT, D_MODEL), lambda i, e: (e, 0, 0)),
            pl.BlockSpec((1, D_MODEL), lambda i, e: (0, 0)),
        ],
        out_specs=pl.BlockSpec((tm, D_MODEL), lambda i, e: (i, 0)),
        out_shape=jax.ShapeDtypeStruct((t, D_MODEL), F32),
        scratch_shapes=[
            pltpu.VMEM((tm, tm), BF16),
            pltpu.VMEM((tm, D_MODEL), BF16),
            pltpu.VMEM((tm, 128), F32),
            pltpu.VMEM((tm, D_MODEL), F32),
            pltpu.SMEM((2 * (N_EXPERTS // MOE_EB),), jnp.int32),
        ],
        compiler_params=_cparams(("arbitrary", "arbitrary")),
        name="moe",
    )(h2, comb, xn, tri, wgu, wd, gf)


def _spread_heads(w, width, slab, total):
    rows = w.shape[0]
    pieces = []
    for h in range(HEADS):
        pieces.append(w[:, h * width:(h + 1) * width])
        pieces.append(jnp.zeros((rows, slab - width), w.dtype))
    if total > HEADS * slab:
        pieces.append(jnp.zeros((rows, total - HEADS * slab), w.dtype))
    return jnp.concatenate(pieces, axis=1)


def _pad_heads(w, width):
    return _spread_heads(w, width, HP, HEADS * HP)


def _v_cols(w):
    return _spread_heads(w, HD, VR, V_COLS)


def _lane_row(vals, offset, width=128):
    return jnp.zeros((1, width), F32).at[0, offset:offset + vals.shape[0]].set(vals.astype(F32))


def _split_w_in(w):
    sizes = (3 * GROUP_W, GROUP_W, SSM_XBC, HEADS, 2 * GROUP_W, GROUP_W, HEADS, HEADS, GROUP_W,
             256, 128, MLA_ROPE)
    offs = np.cumsum((0,) + sizes)
    return [w[:, int(offs[i]):int(offs[i + 1])] for i in range(len(sizes))]


def _layer_weights(l, p):
    a_qkv, s_z, s_xbc, s_dt, m_qk, m_v, m_i, m_f, m_o, d_cq, d_ckv, d_kr = _split_w_in(p["w_in_bf16"][l])
    aq, ak, av = a_qkv[:, :GROUP_W], a_qkv[:, GROUP_W:2 * GROUP_W], a_qkv[:, 2 * GROUP_W:]
    half = MLA_ROPE // 2
    small = jnp.concatenate([s_dt, m_i, m_f, jnp.zeros((D_MODEL, 64 - 12), BF16),
                             d_kr, d_kr[:, half:], d_kr[:, :half]], axis=1)
    w_all = jnp.concatenate([s_z, s_xbc, m_qk, m_v, m_o, small, d_cq, d_ckv, _pad_heads(ak, HD),
                             _pad_heads(aq, HD), _v_cols(av)], axis=1)
    w = {"w_all": w_all,
         "g_mix": p["norm_mix_g"][l][None, :]}
    w["ssm_cw"] = p["ssm_conv_w"][l]
    w["ssm_cb"] = p["ssm_conv_b"][l][None, :]
    w["ssm_dtb"] = _lane_row(p["ssm_dt_bias"][l], 0)
    w["ssm_arow"] = _lane_row(-jnp.exp(p["ssm_a_log"][l].astype(F32)), 0)
    w["ssm_dexp"] = jnp.repeat(p["ssm_d"][l].astype(F32), HD)[None, :]
    w["ssm_ng"] = p["ssm_norm_g"][l][None, :]
    w["ml_cw"] = p["ml_conv_w"][l]
    w["ml_cb"] = p["ml_conv_b"][l][None, :]
    w["ml_igb"] = _lane_row(p["ml_ig_bias"][l], 4)
    w["ml_fgb"] = _lane_row(p["ml_fg_bias"][l], 8)
    w["ml_ng"] = p["ml_norm_g"][l][None, :]
    wuq = p["mla_w_uq"][l]
    qd = MLA_NOPE + MLA_ROPE
    swapped = []
    for h in range(HEADS):
        c0 = h * qd + MLA_NOPE
        swapped += [jnp.zeros((256, MLA_NOPE), F32), wuq[:, c0 + half:c0 + 2 * half], wuq[:, c0:c0 + half]]
    wq = _spread_heads(wuq, qd, HP, HEADS * HP)
    wqs = _spread_heads(jnp.concatenate(swapped, axis=1), qd, HP, HEADS * HP)
    w["mla_wq2"] = jnp.concatenate([wq, wqs], axis=1).astype(BF16)
    wukv = p["mla_w_ukv"][l]
    wk_cols = jnp.concatenate([wukv[:, h * 2 * HD:h * 2 * HD + MLA_NOPE] for h in range(HEADS)], axis=1)
    wv_cols = jnp.concatenate([wukv[:, h * 2 * HD + MLA_NOPE:(h + 1) * 2 * HD] for h in range(HEADS)], axis=1)
    w["mla_wk"] = _pad_heads(wk_cols, MLA_NOPE).astype(BF16)
    w["mla_wv"] = _v_cols(wv_cols).astype(BF16)
    w["mla_gq"] = p["mla_q_norm_g"][l][None, :]
    w["mla_gkv"] = p["mla_kv_norm_g"][l][None, :]
    w["w_out"] = p["w_out"][l].astype(BF16)
    w["g_ffn"] = p["norm_ffn_g"][l][None, :]
    wr = jnp.zeros((D_MODEL, 128), F32)
    wr = wr.at[:, :N_EXPERTS].set(p["moe_w_router"][l]).at[:, N_EXPERTS:N_EXPERTS + 4].set(p["moe_w_group"][l])
    wr_hi = wr.astype(BF16)
    w["w_router"] = jnp.concatenate([wr_hi, (wr - wr_hi.astype(F32)).astype(BF16)], axis=1)
    w["b_router"] = _lane_row(p["moe_b_router"][l], 0) + _lane_row(p["moe_b_group"][l], N_EXPERTS)
    w["w_gu"] = jnp.concatenate([p["moe_w_gate"][l], p["moe_w_up"][l]], axis=2).astype(BF16)
    w["w_down"] = p["moe_w_down"][l].astype(BF16)
    return w


def kernel(x, positions, norm_mix_g, w_in, ssm_conv_w, ssm_conv_b, ssm_dt_bias, ssm_a_log, ssm_d, ssm_norm_g, ml_conv_w, ml_conv_b, ml_ig_bias, ml_fg_bias, ml_norm_g, mla_q_norm_g, mla_w_uq, mla_kv_norm_g, mla_w_ukv, w_out, norm_ffn_g, moe_w_group, moe_b_group, moe_w_router, moe_b_router, moe_w_gate, moe_w_up, moe_w_down, final_norm_g):
    p = dict(norm_mix_g=norm_mix_g, w_in_bf16=_cast_bf16(w_in), ssm_conv_w=ssm_conv_w, ssm_conv_b=ssm_conv_b,
             ssm_dt_bias=ssm_dt_bias, ssm_a_log=ssm_a_log, ssm_d=ssm_d, ssm_norm_g=ssm_norm_g,
             ml_conv_w=ml_conv_w, ml_conv_b=ml_conv_b, ml_ig_bias=ml_ig_bias, ml_fg_bias=ml_fg_bias,
             ml_norm_g=ml_norm_g, mla_q_norm_g=mla_q_norm_g, mla_w_uq=mla_w_uq,
             mla_kv_norm_g=mla_kv_norm_g, mla_w_ukv=mla_w_ukv, w_out=w_out, norm_ffn_g=norm_ffn_g,
             moe_w_group=moe_w_group, moe_b_group=moe_b_group, moe_w_router=moe_w_router,
             moe_b_router=moe_b_router, moe_w_gate=moe_w_gate, moe_w_up=moe_w_up, moe_w_down=moe_w_down)
    bsz, seq, _ = x.shape
    depth = w_in.shape[0]
    x2 = x.reshape(bsz * seq, D_MODEL)

    half = MLA_ROPE // 2
    inv = ROPE_THETA ** (-jnp.arange(0, MLA_ROPE, 2, dtype=F32) / MLA_ROPE)
    inv_col = jnp.concatenate([inv, inv])[:, None]
    sgn_col = jnp.concatenate([-jnp.ones((half,), F32), jnp.ones((half,), F32)])[:, None]
    pe_mask = _lane_row(jnp.ones((MLA_ROPE,), F32), MLA_NOPE)
    c_tok, s_tok, c_tr, s_tr = _rope_tables(positions.reshape(bsz, 1, seq), inv_col, sgn_col, bsz, seq)
    tri = jnp.tril(jnp.ones((CHUNK, CHUNK), F32))
    tri_moe = tri.astype(BF16)
    head_of = jnp.arange(GROUP_W) // HD
    seg = (head_of[:, None] + GATE_LANE == jnp.arange(128)[None, :]).astype(F32)
    ssd_hexp = (jnp.arange(128)[:, None] == head_of[None, :]).astype(F32)
    bmask = (head_of[:, None] == head_of[None, :]).astype(F32)
    segm = bmask / HD

    for l in range(depth):
        w = _layer_weights(l, p)
        main, k_a, qT_a, vT_a, qT_d, k_d, vT_d = _in_proj(
            x2, w["g_mix"], w["w_all"], c_tok, s_tok, c_tr, s_tr, w["mla_gq"], w["mla_wq2"], w["mla_gkv"],
            w["mla_wk"], w["mla_wv"], pe_mask, bsz, seq)
        y_a = _attention(qT_a, k_a, vT_a, bsz, seq, "attn_moba")
        y_b = _ssd(main, w["ssm_cw"], w["ssm_cb"], w["ssm_dtb"], w["ssm_arow"], w["ssm_dexp"], w["ssm_ng"],
                   tri, ssd_hexp, bsz, seq)
        y_c = _mlstm(main, w["ml_cw"], w["ml_cb"], w["ml_igb"], w["ml_fgb"], w["ml_ng"], tri, seg, segm,
                     bmask, bsz, seq)
        y_d = _attention(qT_d, k_d, vT_d, bsz, seq, "attn_mla")
        xn, h2, comb = _outproj(x2, y_a, y_b, y_c, y_d, w["w_out"], w["g_ffn"], w["w_router"], w["b_router"])
        x2 = _moe(h2, comb, xn, tri_moe, w["w_gu"], w["w_down"], final_norm_g[None, :],
                  final_norm=(l == depth - 1), tm=MOE_TM)
    return x2.reshape(bsz, seq, D_MODEL)
```

```python
import functools
import math

import numpy as np
import jax
import jax.numpy as jnp
from jax import lax
from jax.experimental import pallas as pl
from jax.experimental.pallas import tpu as pltpu

F32 = jnp.float32
BF16 = jnp.bfloat16
HIGHEST = lax.Precision.HIGHEST

EPS = 1e-6
NEG = -1e30
LOG2E = 1.4426950408889634

D_MODEL = 1024
GROUP_W = 256
HEADS = 4
HD = 64
HP = 128
MOBA_BLOCK = 256
MOBA_TOPK = 3
SSM_STATE = 128
SSM_XBC = 768
CONV_K = 4
CHUNK = 128
MLA_NOPE = 64
MLA_ROPE = 32
ROPE_THETA = 10000.0
N_EXPERTS = 16
EXPERTS_PER_GROUP = 4
D_EXPERT = 256
GATE_LANE = 8
GID_LANE = 20
MOE_TM = 1024

COL_ZX = 0
COL_ML = 1024
COL_SMALL = 2048
N_MAIN = 2176
VR = 80
V_COLS = 384
N_ALL = N_MAIN + 256 + 128 + 2 * HEADS * HP + V_COLS
ATT_UQ = 256
ATT_UK = 256

V7X_VMEM_LIMIT = 56 * 1024 * 1024

NT_DIMS = (((1,), (1,)), ((), ()))


def _cparams(sem):
    return pltpu.CompilerParams(dimension_semantics=sem, vmem_limit_bytes=V7X_VMEM_LIMIT)


def _rms(x, g):
    ms = jnp.mean(x * x, axis=-1, keepdims=True)
    return x * lax.rsqrt(ms + EPS) * g


def _sigmoid(x):
    return 0.5 * jnp.tanh(0.5 * x) + 0.5


def _softplus(x):
    return jnp.maximum(x, 0.0) + jnp.log1p(jnp.exp(-jnp.abs(x)))


def _tril(n):
    r = lax.broadcasted_iota(jnp.int32, (n, n), 0)
    c = lax.broadcasted_iota(jnp.int32, (n, n), 1)
    return c <= r


W_IN_SIZES = (3 * GROUP_W, GROUP_W, SSM_XBC, HEADS, 2 * GROUP_W, GROUP_W, HEADS, HEADS, GROUP_W, 256, 128,
              MLA_ROPE)


def _relayout_body(w_ref, o_ref):
    w = w_ref[0]
    rows = w.shape[0]
    offs = np.cumsum((0,) + W_IN_SIZES)
    a_qkv, s_z, s_xbc, s_dt, m_qk, m_v, m_i, m_f, m_o, d_cq, d_ckv, d_kr = [
        w[:, int(offs[i]):int(offs[i + 1])] for i in range(len(W_IN_SIZES))]
    half = MLA_ROPE // 2

    def zeros(n):
        return jnp.zeros((rows, n), F32)

    def spread(x, slab, total):
        pieces = []
        for h in range(HEADS):
            pieces += [x[:, h * HD:(h + 1) * HD], zeros(slab - HD)]
        if total > HEADS * slab:
            pieces.append(zeros(total - HEADS * slab))
        return pieces

    pieces = [s_z, s_xbc, m_qk, m_v, m_o,
              s_dt, m_i, m_f, zeros(64 - 3 * HEADS), d_kr, d_kr[:, half:], d_kr[:, :half],
              d_cq, d_ckv]
    pieces += spread(a_qkv[:, GROUP_W:2 * GROUP_W], HP, HEADS * HP)
    pieces += spread(a_qkv[:, :GROUP_W], HP, HEADS * HP)
    pieces += spread(a_qkv[:, 2 * GROUP_W:], VR, V_COLS)
    o_ref[0] = jnp.concatenate(pieces, axis=1).astype(BF16)


def _relayout_w_in(w, rows=256):
    nl, r, c = w.shape
    return pl.pallas_call(
        _relayout_body,
        grid=(nl, r // rows),
        in_specs=[pl.BlockSpec((1, rows, c), lambda l, i: (l, i, 0))],
        out_specs=pl.BlockSpec((1, rows, N_ALL), lambda l, i: (l, i, 0)),
        out_shape=jax.ShapeDtypeStruct((nl, r, N_ALL), BF16),
        compiler_params=_cparams(("arbitrary", "arbitrary")),
        name="relayout_w_in",
    )(w)


def _ones_rows(shape):
    row = lax.broadcasted_iota(jnp.int32, shape, 0)
    hit = row == HD
    for h in range(1, HEADS):
        hit = jnp.logical_or(hit, row == h * VR + HD)
    return jnp.where(hit, 1.0, 0.0)


def _in_proj_body(x_ref, g_ref, w_ref, c_ref, s_ref, ct_ref, st_ref, gq_ref, wq_ref, gkv_ref, wk_ref,
                  wv_ref, msk_ref, om_ref, ka_ref, qa_ref, va_ref, qd_ref, kd_ref, vd_ref, km_sc,
                  *, nb, nst, qscale_a, qscale_d):
    s_idx = pl.program_id(0) % nst
    tm = x_ref.shape[0]

    @pl.when(s_idx == 0)
    def _():
        km_sc[...] = jnp.zeros(km_sc.shape, F32)

    h = _rms(x_ref[...], g_ref[...]).astype(BF16)
    rest = jnp.dot(h, w_ref[0, :, COL_SMALL:], preferred_element_type=F32)
    for c in range(0, COL_SMALL, 512):
        om_ref[:, c:c + 512] = jnp.dot(h, w_ref[0, :, c:c + 512], preferred_element_type=F32)
    sm = rest[:, 0:128]
    om_ref[:, COL_SMALL:COL_SMALL + 128] = sm
    cq = rest[:, 128:384]
    ckv = rest[:, 384:512]
    kp = rest[:, 512:1024]
    qp = rest[:, 1024:1536]
    vc = rest[:, 1536:1536 + V_COLS]

    nblk = tm // MOBA_BLOCK
    lane_k = lax.broadcasted_iota(jnp.int32, (MOBA_BLOCK, HEADS * HP), 1)
    for j in range(nblk):
        kj = kp[j * MOBA_BLOCK:(j + 1) * MOBA_BLOCK]
        blk_id = s_idx * nblk + j
        km_sc[pl.ds(blk_id, 1), :] = jnp.mean(kj, axis=0, keepdims=True)
        ka_ref[j * MOBA_BLOCK:(j + 1) * MOBA_BLOCK, :] = jnp.where(
            (lane_k & (HP - 1)) == HD + blk_id, 1.0, kj).astype(BF16)

    qT = qp.T
    km = km_sc[...]
    pos = s_idx * tm + lax.broadcasted_iota(jnp.int32, (nb, tm), 1)
    cur = lax.shift_right_logical(pos, int(math.log2(MOBA_BLOCK)))
    blk = lax.broadcasted_iota(jnp.int32, (nb, tm), 0)
    past = blk < cur
    for hd in range(HEADS):
        qh = qT[hd * HP:(hd + 1) * HP, :]
        gate = jnp.dot(km[:, hd * HP:(hd + 1) * HP], qh, precision=HIGHEST,
                       preferred_element_type=F32)
        gate = jnp.where(past, gate, -jnp.inf)
        bias = jnp.where(blk == cur, 0.0, NEG)
        for _ in range(MOBA_TOPK):
            m = jnp.max(gate, axis=0, keepdims=True)
            cand = jnp.logical_and(gate == m, gate > -jnp.inf)
            idx = jnp.min(jnp.where(cand, blk, nb), axis=0, keepdims=True)
            pick = blk == idx
            bias = jnp.where(pick, 0.0, bias)
            gate = jnp.where(pick, -jnp.inf, gate)
        rows = [qh[:HD] * qscale_a, bias]
        if HP - HD - nb:
            rows.append(jnp.zeros((HP - HD - nb, tm), F32))
        qa_ref[0, hd * HP:(hd + 1) * HP, :] = jnp.concatenate(rows, axis=0).astype(BF16)
    vt = vc.T[:HEADS * VR]
    va_ref[0] = (vt + _ones_rows(vt.shape)).astype(BF16)

    qn = _rms(cq, gq_ref[...]).astype(BF16)
    q2 = jnp.dot(qn, wq_ref[...], preferred_element_type=F32).T
    nq = HEADS * HP
    ct4 = jnp.concatenate([ct_ref[0]] * HEADS, axis=0)
    st4 = jnp.concatenate([st_ref[0]] * HEADS, axis=0)
    qd_ref[0] = ((q2[:nq] * ct4 + q2[nq:] * st4) * qscale_d).astype(BF16)
    kvn = _rms(ckv, gkv_ref[...]).astype(BF16)
    kn = jnp.dot(kvn, wk_ref[...], preferred_element_type=F32)
    pe = (sm * c_ref[...] + pltpu.roll(sm, HP - MLA_ROPE, 1) * s_ref[...]) * msk_ref[...]
    kd_ref[...] = (kn + jnp.concatenate([pe] * HEADS, axis=1)).astype(BF16)
    vtd = jnp.dot(kvn, wv_ref[...], preferred_element_type=F32).T[:HEADS * VR]
    vd_ref[0] = (vtd + _ones_rows(vtd.shape)).astype(BF16)


def _in_proj(x2, g, w_all, layer, c_tok, s_tok, c_tr, s_tr, gq, wq2, gkv, wk, wv, pe_mask, bsz, seq, tm=512):
    t = x2.shape[0]
    nst = seq // tm
    nb = seq // MOBA_BLOCK
    assert nb <= HP - HD
    tok128 = pl.BlockSpec((tm, HP), lambda i: (i, 0))
    tr128 = pl.BlockSpec((1, HP, tm), lambda i: (i // nst, 0, i % nst))
    tr_q = pl.BlockSpec((1, HEADS * HP, tm), lambda i: (i // nst, 0, i % nst))
    tr_v = pl.BlockSpec((1, HEADS * VR, tm), lambda i: (i // nst, 0, i % nst))
    tok_k = pl.BlockSpec((tm, HEADS * HP), lambda i: (i, 0))

    def full(shape):
        return pl.BlockSpec(shape, lambda i: tuple(0 for _ in shape))

    q_shape = jax.ShapeDtypeStruct((bsz, HEADS * HP, seq), BF16)
    k_shape = jax.ShapeDtypeStruct((t, HEADS * HP), BF16)
    v_shape = jax.ShapeDtypeStruct((bsz, HEADS * VR, seq), BF16)
    return pl.pallas_call(
        functools.partial(_in_proj_body, nb=nb, nst=nst, qscale_a=(HD ** -0.5) * LOG2E,
                          qscale_d=((MLA_NOPE + MLA_ROPE) ** -0.5) * LOG2E),
        grid=(t // tm,),
        in_specs=[
            pl.BlockSpec((tm, D_MODEL), lambda i: (i, 0)),
            full((1, D_MODEL)), pl.BlockSpec((1, D_MODEL, N_ALL), lambda i: (layer, 0, 0)),
            tok128, tok128, tr128, tr128,
            full((1, 256)), full((256, 2 * HEADS * HP)), full((1, 128)),
            full((128, HEADS * HP)), full((128, V_COLS)), full((1, HP)),
        ],
        out_specs=[pl.BlockSpec((tm, N_MAIN), lambda i: (i, 0)), tok_k, tr_q, tr_v, tr_q, tok_k, tr_v],
        out_shape=[jax.ShapeDtypeStruct((t, N_MAIN), F32), k_shape, q_shape, v_shape,
                   q_shape, k_shape, v_shape],
        scratch_shapes=[pltpu.VMEM((nb, HEADS * HP), F32)],
        compiler_params=_cparams(("arbitrary",)),
        name="in_proj",
    )(x2, g, w_all, c_tok, s_tok, c_tr, s_tr, gq, wq2, gkv, wk, wv, pe_mask)


def _rope_body(pos_ref, inv_ref, sgn_ref, c_ref, s_ref, ct_ref, st_ref):
    ts = pos_ref.shape[2]
    ang = inv_ref[...] * pos_ref[0].astype(F32)
    c32 = jnp.cos(ang)
    s32 = jnp.sin(ang) * sgn_ref[...]
    ct = jnp.concatenate([jnp.ones((MLA_NOPE, ts), F32), c32,
                          jnp.ones((HP - MLA_NOPE - MLA_ROPE, ts), F32)], axis=0)
    st = jnp.concatenate([jnp.zeros((MLA_NOPE, ts), F32), s32,
                          jnp.zeros((HP - MLA_NOPE - MLA_ROPE, ts), F32)], axis=0)
    ct_ref[0] = ct
    st_ref[0] = st
    c_ref[...] = ct.T
    s_ref[...] = st.T


def _rope_tables(pos3, inv_col, sgn_col, bsz, seq, ts=512):
    t = bsz * seq
    nst = seq // ts
    tok = pl.BlockSpec((ts, HP), lambda i: (i, 0))
    tr = pl.BlockSpec((1, HP, ts), lambda i: (i // nst, 0, i % nst))
    col = pl.BlockSpec((MLA_ROPE, 1), lambda i: (0, 0))
    return pl.pallas_call(
        _rope_body,
        grid=(t // ts,),
        in_specs=[pl.BlockSpec((1, 1, ts), lambda i: (i // nst, 0, i % nst)), col, col],
        out_specs=[tok, tok, tr, tr],
        out_shape=[
            jax.ShapeDtypeStruct((t, HP), F32),
            jax.ShapeDtypeStruct((t, HP), F32),
            jax.ShapeDtypeStruct((bsz, HP, seq), F32),
            jax.ShapeDtypeStruct((bsz, HP, seq), F32),
        ],
        compiler_params=_cparams(("arbitrary",)),
        name="rope_tables",
    )(pos3, inv_col, sgn_col)


def _attn_body(ii_ref, jj_ref, qT_ref, k_ref, vT_ref, o_ref, m_sc, acc_sc, *, tq, tk, ahead):
    p = pl.program_id(1)
    i = ii_ref[p]
    j = jj_ref[p]

    @pl.when(j == 0)
    def _():
        m_sc[...] = jnp.full(m_sc.shape, NEG, F32)
        acc_sc[...] = jnp.zeros(acc_sc.shape, F32)

    def scores(u):
        h, qc, kb, _ = u
        return jnp.dot(k_ref[kb * ATT_UK:(kb + 1) * ATT_UK, h * HP:(h + 1) * HP],
                       qT_ref[0, h * HP:(h + 1) * HP, qc * ATT_UQ:(qc + 1) * ATT_UQ],
                       preferred_element_type=F32)

    def update(u, s):
        h, qc, kb, masked = u
        if masked:
            kpos = kb * ATT_UK + lax.broadcasted_iota(jnp.int32, s.shape, 0)
            qpos = qc * ATT_UQ + lax.broadcasted_iota(jnp.int32, s.shape, 1)
            s = jnp.where(kpos <= qpos, s, NEG)
        alphas, pms = [], []
        for c in range(ATT_UQ // 128):
            ql = slice(qc * ATT_UQ + c * 128, qc * ATT_UQ + (c + 1) * 128)
            sc = s[:, c * 128:(c + 1) * 128]
            m_old = m_sc[h:h + 1, ql]
            m_new = jnp.maximum(m_old, jnp.max(sc, axis=0, keepdims=True))
            alphas.append(jnp.exp2(m_old - m_new))
            pms.append(jnp.exp2(sc - m_new).astype(BF16))
            m_sc[h:h + 1, ql] = m_new
        ql = slice(qc * ATT_UQ, (qc + 1) * ATT_UQ)
        rows = slice(h * VR, (h + 1) * VR)
        acc_sc[rows, ql] = jnp.concatenate(alphas, axis=1) * acc_sc[rows, ql] + jnp.dot(
            vT_ref[0, rows, kb * ATT_UK:(kb + 1) * ATT_UK], jnp.concatenate(pms, axis=1),
            preferred_element_type=F32)

    def run(units):
        pending = [scores(u) for u in units[:ahead]]
        for n, u in enumerate(units):
            s = pending.pop(0)
            if n + ahead < len(units):
                pending.append(scores(units[n + ahead]))
            update(u, s)

    def unit_list(diag):
        units = []
        for kb in range(tk // ATT_UK):
            for h in range(HEADS):
                for qc in range(tq // ATT_UQ):
                    k_lo, k_hi = kb * ATT_UK, (kb + 1) * ATT_UK - 1
                    q_lo, q_hi = qc * ATT_UQ, (qc + 1) * ATT_UQ - 1
                    if diag and k_lo > q_hi:
                        continue
                    units.append((h, qc, kb, diag and k_hi > q_lo))
        return units

    @pl.when(j < i)
    def _():
        run(unit_list(False))

    @pl.when(j == i)
    def _():
        run(unit_list(True))
        parts = []
        for h in range(HEADS):
            den = acc_sc[h * VR + HD:h * VR + HD + 1, :]
            parts.append(acc_sc[h * VR:h * VR + HD, :] * (1.0 / den))
        o_ref[...] = jnp.concatenate(parts, axis=0).T


def _attention(qT, k, vT, bsz, seq, name, tq=1024, ahead=5):
    t = k.shape[0]
    nq = seq // tq
    ii = np.concatenate([np.full(i + 1, i, np.int32) for i in range(nq)])
    jj = np.concatenate([np.arange(i + 1, dtype=np.int32) for i in range(nq)])
    grid_spec = pltpu.PrefetchScalarGridSpec(
        num_scalar_prefetch=2,
        grid=(bsz, len(ii)),
        in_specs=[
            pl.BlockSpec((1, HEADS * HP, tq), lambda b, p, ii, jj: (b, 0, ii[p])),
            pl.BlockSpec((tq, HEADS * HP), lambda b, p, ii, jj: (b * nq + jj[p], 0)),
            pl.BlockSpec((1, HEADS * VR, tq), lambda b, p, ii, jj: (b, 0, jj[p])),
        ],
        out_specs=pl.BlockSpec((tq, GROUP_W), lambda b, p, ii, jj: (b * nq + ii[p], 0)),
        scratch_shapes=[
            pltpu.VMEM((8, tq), F32),
            pltpu.VMEM((HEADS * VR, tq), F32),
        ],
    )
    return pl.pallas_call(
        functools.partial(_attn_body, tq=tq, tk=tq, ahead=ahead),
        grid_spec=grid_spec,
        out_shape=jax.ShapeDtypeStruct((t, GROUP_W), F32),
        compiler_params=_cparams(("arbitrary", "arbitrary")),
        name=name,
    )(jnp.asarray(ii), jnp.asarray(jj), qT, k, vT)


def _conv_silu(xpad, src, cw_ref, cb_ref, first, ls):
    width = xpad.shape[1]

    @pl.when(first)
    def _():
        xpad[0:8, :] = jnp.zeros((8, width), F32)

    @pl.when(jnp.logical_not(first))
    def _():
        xpad[0:8, :] = xpad[ls:ls + 8, :]

    xpad[8:ls + 8, :] = src
    acc = cb_ref[...] + cw_ref[0:1, :] * xpad[pl.ds(8 - CONV_K + 1, ls), :]
    for kk in range(1, CONV_K):
        acc = acc + cw_ref[kk:kk + 1, :] * xpad[pl.ds(8 - CONV_K + 1 + kk, ls), :]
    return acc * _sigmoid(acc)


def _ssd_body(zx_ref, sm_ref, cw_ref, cb_ref, dtb_ref, arow_ref, dexp_ref, ng_ref, tri_ref, hexp_ref,
              o_ref, xpad, state, *, ls):
    first = pl.program_id(1) == 0

    @pl.when(first)
    def _():
        state[...] = jnp.zeros(state.shape, F32)

    xc = _conv_silu(xpad, zx_ref[:, GROUP_W:GROUP_W + SSM_XBC], cw_ref, cb_ref, first, ls)
    dt_all = _softplus(sm_ref[...] + dtb_ref[...])
    da_all = dt_all * arow_ref[...]
    lane = lax.broadcasted_iota(jnp.int32, (1, 128), 1)
    half = lane < HD
    head_lanes = lane < HEADS
    hexp = hexp_ref[...].astype(BF16)
    tril = _tril(CHUNK)

    def expand(x):
        x = jnp.where(head_lanes, x, 0.0)
        hi = x.astype(BF16)
        lo = (x - hi.astype(F32)).astype(BF16)
        return (jnp.dot(hi, hexp, preferred_element_type=F32)
                + jnp.dot(lo, hexp, preferred_element_type=F32))

    for c in range(ls // CHUNK):
        r0 = c * CHUNK
        acum = jnp.dot(tri_ref[...], da_all[r0:r0 + CHUNK], precision=HIGHEST,
                       preferred_element_type=F32)
        acum_t = acum.T
        alast = acum[CHUNK - 1:CHUNK, :]
        ex = expand(jnp.concatenate([dt_all[r0:r0 + CHUNK], jnp.exp(acum), jnp.exp(alast - acum),
                                     jnp.exp(alast), jnp.zeros((7, 128), F32)], axis=0))
        dt_x = ex[0:CHUNK]
        ecol_x = ex[CHUNK:2 * CHUNK]
        dout_x = ex[2 * CHUNK:3 * CHUNK]
        cdec_x = ex[3 * CHUNK:3 * CHUNK + 1]
        for g in range(2):
            h0, h1 = 2 * g, 2 * g + 1
            gl = slice(g * 128, (g + 1) * 128)
            xg = xc[r0:r0 + CHUNK, gl]
            bg = xc[r0:r0 + CHUNK, GROUP_W + g * 128:GROUP_W + (g + 1) * 128]
            cg = xc[r0:r0 + CHUNK, 2 * GROUP_W + g * 128:2 * GROUP_W + (g + 1) * 128]
            cgb = cg.astype(BF16)
            gram = lax.dot_general(cgb, bg.astype(BF16), NT_DIMS, preferred_element_type=F32)
            xdt = xg * dt_x[:, gl]
            ydiag = jnp.zeros((CHUNK, 128), F32)
            for hh, msk in ((h0, half), (h1, jnp.logical_not(half))):
                dec = jnp.exp(jnp.where(tril, acum[:, hh:hh + 1] - acum_t[hh:hh + 1, :], -jnp.inf))
                ydiag = ydiag + jnp.dot((gram * dec).astype(BF16),
                                        jnp.where(msk, xdt, 0.0).astype(BF16),
                                        preferred_element_type=F32)
            st = state[g]
            yoff = jnp.dot(cgb, st.astype(BF16), preferred_element_type=F32) * ecol_x[:, gl]
            state[g] = (cdec_x[:, gl] * st
                        + jnp.dot(bg.T.astype(BF16), (xdt * dout_x[:, gl]).astype(BF16),
                                  preferred_element_type=F32))
            y = ydiag + yoff + xg * dexp_ref[:, g * 128:(g + 1) * 128]
            zg = zx_ref[r0:r0 + CHUNK, g * 128:(g + 1) * 128]
            y = y * (zg * _sigmoid(zg))
            o_ref[r0:r0 + CHUNK, g * 128:(g + 1) * 128] = _rms(y, ng_ref[:, g * 128:(g + 1) * 128])


def _ssd(main, cw, cb, dtb, arow, dexp, ng, tri, hexp, bsz, seq, ls=512):
    t = main.shape[0]
    ns = seq // ls

    def full(shape):
        return pl.BlockSpec(shape, lambda b, s: tuple(0 for _ in shape))

    return pl.pallas_call(
        functools.partial(_ssd_body, ls=ls),
        grid=(bsz, ns),
        in_specs=[
            pl.BlockSpec((ls, 1024), lambda b, s: (b * ns + s, COL_ZX // 1024)),
            pl.BlockSpec((ls, 128), lambda b, s: (b * ns + s, COL_SMALL // 128)),
            full((CONV_K, SSM_XBC)), full((1, SSM_XBC)), full((1, 128)), full((1, 128)),
            full((1, GROUP_W)), full((1, GROUP_W)), full((CHUNK, CHUNK)), full((128, GROUP_W)),
        ],
        out_specs=pl.BlockSpec((ls, GROUP_W), lambda b, s: (b * ns + s, 0)),
        out_shape=jax.ShapeDtypeStruct((t, GROUP_W), F32),
        scratch_shapes=[pltpu.VMEM((ls + 8, SSM_XBC), F32), pltpu.VMEM((2, SSM_STATE, 128), F32)],
        compiler_params=_cparams(("arbitrary", "arbitrary")),
        name="ssd",
    )(main, main, cw, cb, dtb, arow, dexp, ng, tri, hexp)


def _mlstm_body(c_ref, sm_ref, cw_ref, cb_ref, igb_ref, fgb_ref, ng_ref, tri_ref, seg_ref, segm_ref,
                bmask_ref, o_ref, xpad, cst, nst, mst, *, ls):
    first = pl.program_id(1) == 0

    @pl.when(first)
    def _():
        cst[...] = jnp.zeros(cst.shape, F32)
        nst[...] = jnp.zeros(nst.shape, F32)
        mst[...] = jnp.zeros(mst.shape, F32)

    qk = _conv_silu(xpad, c_ref[:, 0:2 * GROUP_W], cw_ref, cb_ref, first, ls)
    sm = sm_ref[...]
    ig_all = sm + igb_ref[...]
    fx = sm + fgb_ref[...]
    lf_all = jnp.minimum(fx, 0.0) - jnp.log1p(jnp.exp(-jnp.abs(fx)))
    lane = lax.broadcasted_iota(jnp.int32, (1, GROUP_W), 1)
    hms = [jnp.logical_and(lane >= h * HD, lane < (h + 1) * HD) for h in range(HEADS)]
    tril = _tril(CHUNK)
    lane_s = lax.broadcasted_iota(jnp.int32, (1, 128), 1)
    gate_lanes = jnp.logical_and(lane_s >= GATE_LANE, lane_s < GATE_LANE + HEADS)
    segb = seg_ref[...].astype(BF16)
    hexp = seg_ref[...].T.astype(BF16)
    segm_b = segm_ref[...].astype(BF16)
    zero_b = jnp.zeros((), BF16)
    n_chunks = ls // CHUNK

    def expand(x):
        x = jnp.where(gate_lanes, x, 0.0)
        hi = x.astype(BF16)
        lo = (x - hi.astype(F32)).astype(BF16)
        return (jnp.dot(hi, hexp, preferred_element_type=F32)
                + jnp.dot(lo, hexp, preferred_element_type=F32))

    local = []
    for c in range(n_chunks):
        r0 = c * CHUNK
        q = qk[r0:r0 + CHUNK, 0:GROUP_W]
        k = qk[r0:r0 + CHUNK, GROUP_W:2 * GROUP_W] * (HD ** -0.5)
        ig = ig_all[r0:r0 + CHUNK]
        bcs = jnp.dot(tri_ref[...], lf_all[r0:r0 + CHUNK], precision=HIGHEST,
                      preferred_element_type=F32)
        bcs_t = bcs.T
        ig_t = ig.T
        ig_al = pltpu.roll(ig, GATE_LANE - 4, 1)
        k_t = k.T.astype(BF16)
        qb = q.astype(BF16)
        vb = c_ref[r0:r0 + CHUNK, 2 * GROUP_W:3 * GROUP_W].astype(BF16)
        num_loc = jnp.zeros((CHUNK, GROUP_W), F32)
        m_loc_all = jnp.zeros((CHUNK, 128), F32)
        den_loc_all = jnp.zeros((CHUNK, 128), F32)
        for h in range(HEADS):
            bcol = bcs[:, GATE_LANE + h:GATE_LANE + h + 1]
            dmat = jnp.where(tril, bcol - bcs_t[GATE_LANE + h:GATE_LANE + h + 1, :] + ig_t[4 + h:5 + h, :],
                             -jnp.inf)
            m_loc = jnp.max(dmat, axis=1, keepdims=True)
            s_qk = jnp.dot(jnp.where(hms[h], qb, zero_b), k_t, preferred_element_type=F32)
            sc = s_qk * jnp.exp(dmat - m_loc)
            num_loc = num_loc + jnp.dot(sc.astype(BF16), jnp.where(hms[h], vb, zero_b),
                                        preferred_element_type=F32)
            on_lane = lane_s == GATE_LANE + h
            m_loc_all = jnp.where(on_lane, m_loc, m_loc_all)
            den_loc_all = jnp.where(on_lane, jnp.sum(sc, axis=1, keepdims=True), den_loc_all)
        blast = bcs[CHUNK - 1:CHUNK, :]
        gcol = blast - bcs + ig_al
        gmax = jnp.max(gcol, axis=0, keepdims=True)
        kw = k * expand(jnp.exp(gcol - gmax))
        kwv = bmask_ref[...] * jnp.dot(kw.T.astype(BF16), vb, preferred_element_type=F32)
        local.append((q, num_loc, bcs, m_loc_all, den_loc_all, blast, gmax, kwv,
                      jnp.sum(kw, axis=0, keepdims=True)))

    houts = []
    for c in range(n_chunks):
        q, num_loc, bcs, m_loc_all, den_loc_all, blast, gmax, kwv, ksum = local[c]
        c_old = cst[...]
        n_old = nst[0:1, :]
        m_prev = mst[0:1, :]
        q_c = jnp.dot(q.astype(BF16), c_old.astype(BF16), preferred_element_type=F32)
        q_n = jnp.dot((q * n_old).astype(BF16), segb, preferred_element_type=F32)
        inter = bcs + m_prev
        m_t = jnp.maximum(inter, m_loc_all)
        f = jnp.exp(m_loc_all - m_t)
        w_inter = jnp.exp(inter - m_t)
        den = f * den_loc_all + w_inter * q_n
        inv_den = 1.0 / jnp.maximum(jnp.abs(den), jnp.exp(-m_t))
        m_new = jnp.maximum(blast + m_prev, gmax)
        scal = jnp.concatenate([jnp.exp(gmax - m_new), jnp.exp(blast + m_prev - m_new),
                                jnp.zeros((6, 128), F32)], axis=0)
        ex = expand(jnp.concatenate([f, w_inter, inv_den, scal], axis=0))
        f_x = ex[0:CHUNK]
        w_inter_x = ex[CHUNK:2 * CHUNK]
        inv_den_x = ex[2 * CHUNK:3 * CHUNK]
        sfac_x = ex[3 * CHUNK:3 * CHUNK + 1]
        keep_x = ex[3 * CHUNK + 1:3 * CHUNK + 2]
        mst[0:1, :] = m_new
        cst[...] = keep_x * c_old + sfac_x * kwv
        nst[0:1, :] = keep_x * n_old + sfac_x * ksum
        houts.append((f_x * num_loc + w_inter_x * q_c) * inv_den_x)

    for c in range(n_chunks):
        r0 = c * CHUNK
        hh = houts[c] * _sigmoid(c_ref[r0:r0 + CHUNK, 3 * GROUP_W:4 * GROUP_W])
        sq = hh * hh
        sq_hi = sq.astype(BF16)
        sq_lo = (sq - sq_hi.astype(F32)).astype(BF16)
        ms = (jnp.dot(sq_hi, segm_b, preferred_element_type=F32)
              + jnp.dot(sq_lo, segm_b, preferred_element_type=F32))
        o_ref[r0:r0 + CHUNK, :] = hh * lax.rsqrt(ms + EPS) * ng_ref[...]


def _mlstm(main, cw, cb, igb, fgb, ng, tri, seg, segm, bmask, bsz, seq, ls=512):
    t = main.shape[0]
    ns = seq // ls

    def full(shape):
        return pl.BlockSpec(shape, lambda b, s: tuple(0 for _ in shape))

    return pl.pallas_call(
        functools.partial(_mlstm_body, ls=ls),
        grid=(bsz, ns),
        in_specs=[
            pl.BlockSpec((ls, 1024), lambda b, s: (b * ns + s, COL_ML // 1024)),
            pl.BlockSpec((ls, 128), lambda b, s: (b * ns + s, COL_SMALL // 128)),
            full((CONV_K, 2 * GROUP_W)), full((1, 2 * GROUP_W)), full((1, 128)), full((1, 128)),
            full((1, GROUP_W)), full((CHUNK, CHUNK)), full((GROUP_W, 128)), full((GROUP_W, GROUP_W)),
            full((GROUP_W, GROUP_W)),
        ],
        out_specs=pl.BlockSpec((ls, GROUP_W), lambda b, s: (b * ns + s, 0)),
        out_shape=jax.ShapeDtypeStruct((t, GROUP_W), F32),
        scratch_shapes=[
            pltpu.VMEM((ls + 8, 2 * GROUP_W), F32),
            pltpu.VMEM((GROUP_W, GROUP_W), F32),
            pltpu.VMEM((8, GROUP_W), F32),
            pltpu.VMEM((8, 128), F32),
        ],
        compiler_params=_cparams(("arbitrary", "arbitrary")),
        name="mlstm",
    )(main, main, cw, cb, igb, fgb, ng, tri, seg, segm, bmask)


def _outproj_body(x_ref, ya_ref, yb_ref, yc_ref, yd_ref, wo_ref, g2_ref, wr_ref, br_ref,
                  xn_ref, h2_ref, cb_ref):
    acc = x_ref[...]
    for gi, y_ref in enumerate((ya_ref, yb_ref, yc_ref, yd_ref)):
        acc = acc + jnp.dot(y_ref[...].astype(BF16), wo_ref[gi * GROUP_W:(gi + 1) * GROUP_W, :],
                            preferred_element_type=F32)
    xn_ref[...] = acc
    h2 = _rms(acc, g2_ref[...])
    h_hi = h2.astype(BF16)
    h2_ref[...] = h_hi
    h_lo = (h2 - h_hi.astype(F32)).astype(BF16)
    hh = jnp.dot(h_hi, wr_ref[...], preferred_element_type=F32)
    lh = jnp.dot(h_lo, wr_ref[...], preferred_element_type=F32)
    logits = hh[:, :128] + hh[:, 128:] + lh[:, :128] + br_ref[...]
    lane = lax.broadcasted_iota(jnp.int32, logits.shape, 1)
    big = jnp.int32(1 << 20)
    n_grp = N_EXPERTS // EXPERTS_PER_GROUP
    gl = jnp.where(jnp.logical_and(lane >= N_EXPERTS, lane < N_EXPERTS + n_grp), logits, -jnp.inf)
    gmax = jnp.max(gl, axis=1, keepdims=True)
    g_w = 1.0 / jnp.sum(jnp.exp(gl - gmax), axis=1, keepdims=True)
    gidx = jnp.min(jnp.where(gl == gmax, lane, big), axis=1, keepdims=True) - N_EXPERTS
    in_grp = jnp.logical_and(lane < N_EXPERTS,
                             lax.shift_right_logical(lane, int(math.log2(EXPERTS_PER_GROUP))) == gidx)
    el = jnp.where(in_grp, logits, -jnp.inf)
    emax = jnp.max(el, axis=1, keepdims=True)
    esum = jnp.sum(jnp.exp(el - emax), axis=1, keepdims=True)
    i1 = jnp.min(jnp.where(el == emax, lane, big), axis=1, keepdims=True)
    el2 = jnp.where(lane == i1, -jnp.inf, el)
    emax2 = jnp.max(el2, axis=1, keepdims=True)
    i2 = jnp.min(jnp.where(el2 == emax2, lane, big), axis=1, keepdims=True)
    p1 = 1.0 / esum
    p2 = jnp.exp(emax2 - emax) / esum
    w1 = p1 / (p1 + p2)
    w2 = p2 / (p1 + p2)
    comb = g_w * (jnp.where(lane == i1, w1, 0.0) + jnp.where(lane == i2, w2, 0.0))
    cb_ref[...] = jnp.where(lane == GID_LANE, gidx.astype(F32), comb)


def _outproj(x2, ya, yb, yc, yd, wo, g2, wr, br, tm=512):
    t = x2.shape[0]
    row = pl.BlockSpec((tm, D_MODEL), lambda i: (i, 0))
    grp = pl.BlockSpec((tm, GROUP_W), lambda i: (i, 0))

    def full(shape):
        return pl.BlockSpec(shape, lambda i: tuple(0 for _ in shape))

    return pl.pallas_call(
        _outproj_body,
        grid=(t // tm,),
        in_specs=[row, grp, grp, grp, grp, full((D_MODEL, D_MODEL)), full((1, D_MODEL)),
                  full((D_MODEL, 256)), full((1, 128))],
        out_specs=[row, row, pl.BlockSpec((tm, 128), lambda i: (i, 0))],
        out_shape=[
            jax.ShapeDtypeStruct((t, D_MODEL), F32),
            jax.ShapeDtypeStruct((t, D_MODEL), BF16),
            jax.ShapeDtypeStruct((t, 128), F32),
        ],
        compiler_params=_cparams(("arbitrary",)),
        name="outproj_router",
    )(x2, ya, yb, yc, yd, wo, g2, wr, br)


MOE_EB = EXPERTS_PER_GROUP
MOE_RB = 128


def _moe_body(h_ref, cb_ref, xn_ref, tri_ref, wg_ref, wu_ref, wd_ref, gf_ref, o_ref,
              perm_t, hs, cs, ys, seg_smem, *, final_norm, rb):
    g = pl.program_id(1)
    tm = h_ref.shape[0]
    n_grp = N_EXPERTS // EXPERTS_PER_GROUP
    lane = lax.broadcasted_iota(jnp.int32, (tm, 128), 1)

    @pl.when(g == 0)
    def _():
        cb = cb_ref[...]
        gid = jnp.sum(jnp.where(lane == GID_LANE, cb, 0.0), axis=1, keepdims=True).astype(jnp.int32)
        onehot = jnp.where(lane == gid, 1.0, 0.0)
        oh16 = onehot.astype(BF16)
        tot = jnp.zeros((1, 128), F32)
        parts = []
        for r0 in range(0, tm, CHUNK):
            part = jnp.dot(tri_ref[...], oh16[r0:r0 + CHUNK], preferred_element_type=F32) + tot
            parts.append(part)
            tot = part[CHUNK - 1:CHUNK, :]
        cum = jnp.concatenate(parts, axis=0)
        lane1 = lane[0:1, :]
        base = jnp.zeros((1, 128), F32)
        run = jnp.zeros((1, 1), F32)
        for k in range(n_grp):
            base = base + jnp.where(lane1 == k, run, 0.0)
            seg_smem[k] = run[0, 0].astype(jnp.int32)
            seg_smem[n_grp + k] = tot[0, k].astype(jnp.int32)
            run = run + tot[:, k:k + 1]
        pos = jnp.sum(onehot * (base + cum - 1.0), axis=1, keepdims=True).astype(jnp.int32)
        pos_row = jnp.broadcast_to(pos.astype(F32), (tm, 128)).T[0:1, :].astype(jnp.int32)
        perm_t[...] = jnp.where(lax.broadcasted_iota(jnp.int32, (tm, tm), 1) == pos, 1.0, 0.0).astype(BF16)
        p = jnp.where(lax.broadcasted_iota(jnp.int32, (tm, tm), 0) == pos_row, 1.0, 0.0).astype(BF16)
        hs[...] = jnp.dot(p, h_ref[...], preferred_element_type=F32).astype(BF16)
        cb_hi = cb.astype(BF16)
        cb_lo = (cb - cb_hi.astype(F32)).astype(BF16)
        cs2 = jnp.dot(p, jnp.concatenate([cb_hi, cb_lo], axis=1), preferred_element_type=F32)
        cs[...] = cs2[:, :128] + cs2[:, 128:]
        ys[...] = jnp.zeros(ys.shape, F32)

    seg_lo = seg_smem[g]
    seg_hi = seg_lo + seg_smem[n_grp + g]
    lane_rb = lax.broadcasted_iota(jnp.int32, (rb, 128), 1)

    def run_block(lo):
        x = hs[lo:lo + rb, :]
        cw = cs[lo:lo + rb, :]

        def gate_up(j):
            return (jnp.dot(x, wg_ref[j], preferred_element_type=F32),
                    jnp.dot(x, wu_ref[j], preferred_element_type=F32))

        def down(j, gu):
            col = jnp.sum(jnp.where(lane_rb == g * MOE_EB + j, cw, 0.0), axis=1,
                          keepdims=True)
            gt, up = gu
            hid = (gt * _sigmoid(gt)) * up * col
            return jnp.dot(hid.astype(BF16), wd_ref[j], preferred_element_type=F32)

        total = jnp.zeros((rb, D_MODEL), F32)
        gu = gate_up(0)
        for j in range(MOE_EB):
            gu_next = gate_up(j + 1) if j + 1 < MOE_EB else None
            total = total + down(j, gu)
            gu = gu_next
        ys[lo:lo + rb, :] += total

    for b in range(tm // rb):
        lo = b * rb
        pl.when(jnp.logical_and(seg_lo < lo + rb, seg_hi > lo))(functools.partial(run_block, lo))

    @pl.when(g == n_grp - 1)
    def _():
        y = xn_ref[...] + jnp.dot(perm_t[...], ys[...].astype(BF16), preferred_element_type=F32)
        if final_norm:
            y = _rms(y, gf_ref[...])
        o_ref[...] = y


def _moe(h2, comb, xn, tri, wg, wu, wd, gf, final_norm, tm=1024, rb=MOE_RB):
    t = h2.shape[0]
    return pl.pallas_call(
        functools.partial(_moe_body, final_norm=final_norm, rb=rb),
        grid=(t // tm, N_EXPERTS // MOE_EB),
        in_specs=[
            pl.BlockSpec((tm, D_MODEL), lambda i, e: (i, 0)),
            pl.BlockSpec((tm, 128), lambda i, e: (i, 0)),
            pl.BlockSpec((tm, D_MODEL), lambda i, e: (i, 0)),
            pl.BlockSpec((CHUNK, CHUNK), lambda i, e: (0, 0)),
            pl.BlockSpec((MOE_EB, D_MODEL, D_EXPERT), lambda i, e: (e, 0, 0)),
            pl.BlockSpec((MOE_EB, D_MODEL, D_EXPERT), lambda i, e: (e, 0, 0)),
            pl.BlockSpec((MOE_EB, D_EXPERT, D_MODEL), lambda i, e: (e, 0, 0)),
            pl.BlockSpec((1, D_MODEL), lambda i, e: (0, 0)),
        ],
        out_specs=pl.BlockSpec((tm, D_MODEL), lambda i, e: (i, 0)),
        out_shape=jax.ShapeDtypeStruct((t, D_MODEL), F32),
        scratch_shapes=[
            pltpu.VMEM((tm, tm), BF16),
            pltpu.VMEM((tm, D_MODEL), BF16),
            pltpu.VMEM((tm, 128), F32),
            pltpu.VMEM((tm, D_MODEL), F32),
            pltpu.SMEM((2 * (N_EXPERTS // MOE_EB),), jnp.int32),
        ],
        compiler_params=_cparams(("arbitrary", "arbitrary")),
        name="moe",
    )(h2, comb, xn, tri, wg, wu, wd, gf)


def _spread_heads(w, width, slab, total):
    rows = w.shape[0]
    pieces = []
    for h in range(HEADS):
        pieces.append(w[:, h * width:(h + 1) * width])
        pieces.append(jnp.zeros((rows, slab - width), w.dtype))
    if total > HEADS * slab:
        pieces.append(jnp.zeros((rows, total - HEADS * slab), w.dtype))
    return jnp.concatenate(pieces, axis=1)


def _pad_heads(w, width):
    return _spread_heads(w, width, HP, HEADS * HP)


def _v_cols(w):
    return _spread_heads(w, HD, VR, V_COLS)


def _lane_row(vals, offset, width=128):
    return jnp.zeros((1, width), F32).at[0, offset:offset + vals.shape[0]].set(vals.astype(F32))


def _layer_weights(l, p):
    half = MLA_ROPE // 2
    w = {"g_mix": p["norm_mix_g"][l][None, :]}
    w["ssm_cw"] = p["ssm_conv_w"][l]
    w["ssm_cb"] = p["ssm_conv_b"][l][None, :]
    w["ssm_dtb"] = _lane_row(p["ssm_dt_bias"][l], 0)
    w["ssm_arow"] = _lane_row(-jnp.exp(p["ssm_a_log"][l].astype(F32)), 0)
    w["ssm_dexp"] = jnp.repeat(p["ssm_d"][l].astype(F32), HD)[None, :]
    w["ssm_ng"] = p["ssm_norm_g"][l][None, :]
    w["ml_cw"] = p["ml_conv_w"][l]
    w["ml_cb"] = p["ml_conv_b"][l][None, :]
    w["ml_igb"] = _lane_row(p["ml_ig_bias"][l], 4)
    w["ml_fgb"] = _lane_row(p["ml_fg_bias"][l], 8)
    w["ml_ng"] = p["ml_norm_g"][l][None, :]
    wuq = p["mla_w_uq"][l]
    qd = MLA_NOPE + MLA_ROPE
    swapped = []
    for h in range(HEADS):
        c0 = h * qd + MLA_NOPE
        swapped += [jnp.zeros((256, MLA_NOPE), F32), wuq[:, c0 + half:c0 + 2 * half], wuq[:, c0:c0 + half]]
    wq = _spread_heads(wuq, qd, HP, HEADS * HP)
    wqs = _spread_heads(jnp.concatenate(swapped, axis=1), qd, HP, HEADS * HP)
    w["mla_wq2"] = jnp.concatenate([wq, wqs], axis=1).astype(BF16)
    wukv = p["mla_w_ukv"][l]
    wk_cols = jnp.concatenate([wukv[:, h * 2 * HD:h * 2 * HD + MLA_NOPE] for h in range(HEADS)], axis=1)
    wv_cols = jnp.concatenate([wukv[:, h * 2 * HD + MLA_NOPE:(h + 1) * 2 * HD] for h in range(HEADS)], axis=1)
    w["mla_wk"] = _pad_heads(wk_cols, MLA_NOPE).astype(BF16)
    w["mla_wv"] = _v_cols(wv_cols).astype(BF16)
    w["mla_gq"] = p["mla_q_norm_g"][l][None, :]
    w["mla_gkv"] = p["mla_kv_norm_g"][l][None, :]
    w["w_out"] = p["w_out"][l].astype(BF16)
    w["g_ffn"] = p["norm_ffn_g"][l][None, :]
    wr = jnp.zeros((D_MODEL, 128), F32)
    wr = wr.at[:, :N_EXPERTS].set(p["moe_w_router"][l]).at[:, N_EXPERTS:N_EXPERTS + 4].set(p["moe_w_group"][l])
    wr_hi = wr.astype(BF16)
    w["w_router"] = jnp.concatenate([wr_hi, (wr - wr_hi.astype(F32)).astype(BF16)], axis=1)
    w["b_router"] = _lane_row(p["moe_b_router"][l], 0) + _lane_row(p["moe_b_group"][l], N_EXPERTS)
    w["w_gate"] = p["moe_w_gate"][l].astype(BF16)
    w["w_up"] = p["moe_w_up"][l].astype(BF16)
    w["w_down"] = p["moe_w_down"][l].astype(BF16)
    return w


def kernel(x, positions, norm_mix_g, w_in, ssm_conv_w, ssm_conv_b, ssm_dt_bias, ssm_a_log, ssm_d, ssm_norm_g, ml_conv_w, ml_conv_b, ml_ig_bias, ml_fg_bias, ml_norm_g, mla_q_norm_g, mla_w_uq, mla_kv_norm_g, mla_w_ukv, w_out, norm_ffn_g, moe_w_group, moe_b_group, moe_w_router, moe_b_router, moe_w_gate, moe_w_up, moe_w_down, final_norm_g):
    p = dict(norm_mix_g=norm_mix_g, w_all=_relayout_w_in(w_in), ssm_conv_w=ssm_conv_w, ssm_conv_b=ssm_conv_b,
             ssm_dt_bias=ssm_dt_bias, ssm_a_log=ssm_a_log, ssm_d=ssm_d, ssm_norm_g=ssm_norm_g,
             ml_conv_w=ml_conv_w, ml_conv_b=ml_conv_b, ml_ig_bias=ml_ig_bias, ml_fg_bias=ml_fg_bias,
             ml_norm_g=ml_norm_g, mla_q_norm_g=mla_q_norm_g, mla_w_uq=mla_w_uq,
             mla_kv_norm_g=mla_kv_norm_g, mla_w_ukv=mla_w_ukv, w_out=w_out, norm_ffn_g=norm_ffn_g,
             moe_w_group=moe_w_group, moe_b_group=moe_b_group, moe_w_router=moe_w_router,
             moe_b_router=moe_b_router, moe_w_gate=moe_w_gate, moe_w_up=moe_w_up, moe_w_down=moe_w_down)
    bsz, seq, _ = x.shape
    depth = w_in.shape[0]
    x2 = x.reshape(bsz * seq, D_MODEL)

    half = MLA_ROPE // 2
    inv = ROPE_THETA ** (-jnp.arange(0, MLA_ROPE, 2, dtype=F32) / MLA_ROPE)
    inv_col = jnp.concatenate([inv, inv])[:, None]
    sgn_col = jnp.concatenate([-jnp.ones((half,), F32), jnp.ones((half,), F32)])[:, None]
    pe_mask = _lane_row(jnp.ones((MLA_ROPE,), F32), MLA_NOPE)
    c_tok, s_tok, c_tr, s_tr = _rope_tables(positions.reshape(bsz, 1, seq), inv_col, sgn_col, bsz, seq)
    tri = jnp.tril(jnp.ones((CHUNK, CHUNK), F32))
    tri_moe = tri.astype(BF16)
    head_of = jnp.arange(GROUP_W) // HD
    seg = (head_of[:, None] + GATE_LANE == jnp.arange(128)[None, :]).astype(F32)
    ssd_hexp = (jnp.arange(128)[:, None] == head_of[None, :]).astype(F32)
    bmask = (head_of[:, None] == head_of[None, :]).astype(F32)
    segm = bmask / HD

    for l in range(depth):
        w = _layer_weights(l, p)
        main, k_a, qT_a, vT_a, qT_d, k_d, vT_d = _in_proj(
            x2, w["g_mix"], p["w_all"], l, c_tok, s_tok, c_tr, s_tr, w["mla_gq"], w["mla_wq2"], w["mla_gkv"],
            w["mla_wk"], w["mla_wv"], pe_mask, bsz, seq)
        y_a = _attention(qT_a, k_a, vT_a, bsz, seq, "attn_moba")
        y_b = _ssd(main, w["ssm_cw"], w["ssm_cb"], w["ssm_dtb"], w["ssm_arow"], w["ssm_dexp"], w["ssm_ng"],
                   tri, ssd_hexp, bsz, seq)
        y_c = _mlstm(main, w["ml_cw"], w["ml_cb"], w["ml_igb"], w["ml_fgb"], w["ml_ng"], tri, seg, segm,
                     bmask, bsz, seq)
        y_d = _attention(qT_d, k_d, vT_d, bsz, seq, "attn_mla")
        xn, h2, comb = _outproj(x2, y_a, y_b, y_c, y_d, w["w_out"], w["g_ffn"], w["w_router"], w["b_router"])
        x2 = _moe(h2, comb, xn, tri_moe, w["w_gate"], w["w_up"], w["w_down"], final_norm_g[None, :],
                  final_norm=(l == depth - 1), tm=MOE_TM)
    return x2.reshape(bsz, seq, D_MODEL)
```

```python
import functools
import math

import numpy as np
import jax
import jax.numpy as jnp
from jax import lax
from jax.experimental import pallas as pl
from jax.experimental.pallas import tpu as pltpu

F32 = jnp.float32
BF16 = jnp.bfloat16
HIGHEST = lax.Precision.HIGHEST

EPS = 1e-6
NEG = -1e30
LOG2E = 1.4426950408889634

D_MODEL = 1024
GROUP_W = 256
HEADS = 4
HD = 64
HP = 128
MOBA_BLOCK = 256
MOBA_TOPK = 3
SSM_STATE = 128
SSM_XBC = 768
CONV_K = 4
CHUNK = 128
MLA_NOPE = 64
MLA_ROPE = 32
ROPE_THETA = 10000.0
N_EXPERTS = 16
EXPERTS_PER_GROUP = 4
D_EXPERT = 256
GATE_LANE = 8
GID_LANE = 20
MOE_TM = 1024

COL_ZX = 0
COL_ML = 1024
COL_SMALL = 2048
N_MAIN = 2176
VR = 80
V_COLS = 384
N_ALL = N_MAIN + 256 + 128 + 3 * GROUP_W
ATT_UQ = 256
ATT_UK = 256

V7X_VMEM_LIMIT = 56 * 1024 * 1024

NT_DIMS = (((1,), (1,)), ((), ()))


def _cparams(sem):
    return pltpu.CompilerParams(dimension_semantics=sem, vmem_limit_bytes=V7X_VMEM_LIMIT)


def _rms(x, g):
    ms = jnp.mean(x * x, axis=-1, keepdims=True)
    return x * lax.rsqrt(ms + EPS) * g


def _sigmoid(x):
    return 0.5 * jnp.tanh(0.5 * x) + 0.5


def _softplus(x):
    return jnp.maximum(x, 0.0) + jnp.log1p(jnp.exp(-jnp.abs(x)))


def _tril(n):
    r = lax.broadcasted_iota(jnp.int32, (n, n), 0)
    c = lax.broadcasted_iota(jnp.int32, (n, n), 1)
    return c <= r


W_IN_SIZES = (3 * GROUP_W, GROUP_W, SSM_XBC, HEADS, 2 * GROUP_W, GROUP_W, HEADS, HEADS, GROUP_W, 256, 128,
              MLA_ROPE)


def _relayout_body(w_ref, o_ref):
    w = w_ref[0]
    rows = w.shape[0]
    offs = np.cumsum((0,) + W_IN_SIZES)
    a_qkv, s_z, s_xbc, s_dt, m_qk, m_v, m_i, m_f, m_o, d_cq, d_ckv, d_kr = [
        w[:, int(offs[i]):int(offs[i + 1])] for i in range(len(W_IN_SIZES))]
    half = MLA_ROPE // 2

    def zeros(n):
        return jnp.zeros((rows, n), F32)

    pieces = [s_z, s_xbc, m_qk, m_v, m_o,
              s_dt, m_i, m_f, zeros(64 - 3 * HEADS), d_kr, d_kr[:, half:], d_kr[:, :half],
              d_cq, d_ckv,
              a_qkv[:, GROUP_W:2 * GROUP_W], a_qkv[:, :GROUP_W], a_qkv[:, 2 * GROUP_W:]]
    o_ref[0] = jnp.concatenate(pieces, axis=1).astype(BF16)


def _relayout_w_in(w, rows=256):
    nl, r, c = w.shape
    return pl.pallas_call(
        _relayout_body,
        grid=(nl, r // rows),
        in_specs=[pl.BlockSpec((1, rows, c), lambda l, i: (l, i, 0))],
        out_specs=pl.BlockSpec((1, rows, N_ALL), lambda l, i: (l, i, 0)),
        out_shape=jax.ShapeDtypeStruct((nl, r, N_ALL), BF16),
        compiler_params=_cparams(("arbitrary", "arbitrary")),
        name="relayout_w_in",
    )(w)


def _ones_rows(shape):
    row = lax.broadcasted_iota(jnp.int32, shape, 0)
    hit = row == HD
    for h in range(1, HEADS):
        hit = jnp.logical_or(hit, row == h * VR + HD)
    return jnp.where(hit, 1.0, 0.0)


def _in_proj_body(x_ref, g_ref, w_ref, c_ref, s_ref, ct_ref, st_ref, gq_ref, wq_ref, gkv_ref, wk_ref,
                  wv_ref, msk_ref, om_ref, ka_ref, qa_ref, va_ref, qd_ref, kd_ref, vd_ref, km_sc,
                  *, nb, nst, qscale_a, qscale_d):
    s_idx = pl.program_id(0) % nst
    tm = x_ref.shape[0]

    @pl.when(s_idx == 0)
    def _():
        km_sc[...] = jnp.zeros(km_sc.shape, F32)

    h = _rms(x_ref[...], g_ref[...]).astype(BF16)
    rest = jnp.dot(h, w_ref[0, :, COL_SMALL:], preferred_element_type=F32)
    for c in range(0, COL_SMALL, 512):
        om_ref[:, c:c + 512] = jnp.dot(h, w_ref[0, :, c:c + 512], preferred_element_type=F32)
    sm = rest[:, 0:128]
    om_ref[:, COL_SMALL:COL_SMALL + 128] = sm
    cq = rest[:, 128:384]
    ckv = rest[:, 384:512]
    kp = _spread_heads(rest[:, 512:768], HD, HP, HEADS * HP)
    qp = _spread_heads(rest[:, 768:1024], HD, HP, HEADS * HP)
    vc = _spread_heads(rest[:, 1024:1280], HD, VR, V_COLS)

    nblk = tm // MOBA_BLOCK
    lane_k = lax.broadcasted_iota(jnp.int32, (MOBA_BLOCK, HEADS * HP), 1)
    for j in range(nblk):
        kj = kp[j * MOBA_BLOCK:(j + 1) * MOBA_BLOCK]
        blk_id = s_idx * nblk + j
        km_sc[pl.ds(blk_id, 1), :] = jnp.mean(kj, axis=0, keepdims=True)
        ka_ref[j * MOBA_BLOCK:(j + 1) * MOBA_BLOCK, :] = jnp.where(
            (lane_k & (HP - 1)) == HD + blk_id, 1.0, kj).astype(BF16)

    qT = qp.T
    km = km_sc[...]
    pos = s_idx * tm + lax.broadcasted_iota(jnp.int32, (nb, tm), 1)
    cur = lax.shift_right_logical(pos, int(math.log2(MOBA_BLOCK)))
    blk = lax.broadcasted_iota(jnp.int32, (nb, tm), 0)
    past = blk < cur
    for hd in range(HEADS):
        qh = qT[hd * HP:(hd + 1) * HP, :]
        gate = jnp.dot(km[:, hd * HP:(hd + 1) * HP], qh, precision=HIGHEST,
                       preferred_element_type=F32)
        gate = jnp.where(past, gate, -jnp.inf)
        bias = jnp.where(blk == cur, 0.0, NEG)
        for _ in range(MOBA_TOPK):
            m = jnp.max(gate, axis=0, keepdims=True)
            cand = jnp.logical_and(gate == m, gate > -jnp.inf)
            idx = jnp.min(jnp.where(cand, blk, nb), axis=0, keepdims=True)
            pick = blk == idx
            bias = jnp.where(pick, 0.0, bias)
            gate = jnp.where(pick, -jnp.inf, gate)
        rows = [qh[:HD] * qscale_a, bias]
        if HP - HD - nb:
            rows.append(jnp.zeros((HP - HD - nb, tm), F32))
        qa_ref[0, hd * HP:(hd + 1) * HP, :] = jnp.concatenate(rows, axis=0).astype(BF16)
    vt = vc.T[:HEADS * VR]
    va_ref[0] = (vt + _ones_rows(vt.shape)).astype(BF16)

    qn = _rms(cq, gq_ref[...]).astype(BF16)
    q2 = jnp.dot(qn, wq_ref[...], preferred_element_type=F32).T
    nq = HEADS * HP
    ct4 = jnp.concatenate([ct_ref[0]] * HEADS, axis=0)
    st4 = jnp.concatenate([st_ref[0]] * HEADS, axis=0)
    qd_ref[0] = ((q2[:nq] * ct4 + q2[nq:] * st4) * qscale_d).astype(BF16)
    kvn = _rms(ckv, gkv_ref[...]).astype(BF16)
    kn = jnp.dot(kvn, wk_ref[...], preferred_element_type=F32)
    pe = (sm * c_ref[...] + pltpu.roll(sm, HP - MLA_ROPE, 1) * s_ref[...]) * msk_ref[...]
    kd_ref[...] = (kn + jnp.concatenate([pe] * HEADS, axis=1)).astype(BF16)
    vtd = jnp.dot(kvn, wv_ref[...], preferred_element_type=F32).T[:HEADS * VR]
    vd_ref[0] = (vtd + _ones_rows(vtd.shape)).astype(BF16)


def _in_proj(x2, g, w_all, layer, c_tok, s_tok, c_tr, s_tr, gq, wq2, gkv, wk, wv, pe_mask, bsz, seq, tm=512):
    t = x2.shape[0]
    nst = seq // tm
    nb = seq // MOBA_BLOCK
    assert nb <= HP - HD
    tok128 = pl.BlockSpec((tm, HP), lambda i: (i, 0))
    tr128 = pl.BlockSpec((1, HP, tm), lambda i: (i // nst, 0, i % nst))
    tr_q = pl.BlockSpec((1, HEADS * HP, tm), lambda i: (i // nst, 0, i % nst))
    tr_v = pl.BlockSpec((1, HEADS * VR, tm), lambda i: (i // nst, 0, i % nst))
    tok_k = pl.BlockSpec((tm, HEADS * HP), lambda i: (i, 0))

    def full(shape):
        return pl.BlockSpec(shape, lambda i: tuple(0 for _ in shape))

    q_shape = jax.ShapeDtypeStruct((bsz, HEADS * HP, seq), BF16)
    k_shape = jax.ShapeDtypeStruct((t, HEADS * HP), BF16)
    v_shape = jax.ShapeDtypeStruct((bsz, HEADS * VR, seq), BF16)
    return pl.pallas_call(
        functools.partial(_in_proj_body, nb=nb, nst=nst, qscale_a=(HD ** -0.5) * LOG2E,
                          qscale_d=((MLA_NOPE + MLA_ROPE) ** -0.5) * LOG2E),
        grid=(t // tm,),
        in_specs=[
            pl.BlockSpec((tm, D_MODEL), lambda i: (i, 0)),
            full((1, D_MODEL)), pl.BlockSpec((1, D_MODEL, N_ALL), lambda i: (layer, 0, 0)),
            tok128, tok128, tr128, tr128,
            full((1, 256)), full((256, 2 * HEADS * HP)), full((1, 128)),
            full((128, HEADS * HP)), full((128, V_COLS)), full((1, HP)),
        ],
        out_specs=[pl.BlockSpec((tm, N_MAIN), lambda i: (i, 0)), tok_k, tr_q, tr_v, tr_q, tok_k, tr_v],
        out_shape=[jax.ShapeDtypeStruct((t, N_MAIN), F32), k_shape, q_shape, v_shape,
                   q_shape, k_shape, v_shape],
        scratch_shapes=[pltpu.VMEM((nb, HEADS * HP), F32)],
        compiler_params=_cparams(("arbitrary",)),
        name="in_proj",
    )(x2, g, w_all, c_tok, s_tok, c_tr, s_tr, gq, wq2, gkv, wk, wv, pe_mask)


def _rope_body(pos_ref, inv_ref, sgn_ref, c_ref, s_ref, ct_ref, st_ref):
    ts = pos_ref.shape[2]
    ang = inv_ref[...] * pos_ref[0].astype(F32)
    c32 = jnp.cos(ang)
    s32 = jnp.sin(ang) * sgn_ref[...]
    ct = jnp.concatenate([jnp.ones((MLA_NOPE, ts), F32), c32,
                          jnp.ones((HP - MLA_NOPE - MLA_ROPE, ts), F32)], axis=0)
    st = jnp.concatenate([jnp.zeros((MLA_NOPE, ts), F32), s32,
                          jnp.zeros((HP - MLA_NOPE - MLA_ROPE, ts), F32)], axis=0)
    ct_ref[0] = ct
    st_ref[0] = st
    c_ref[...] = ct.T
    s_ref[...] = st.T


def _rope_tables(pos3, inv_col, sgn_col, bsz, seq, ts=512):
    t = bsz * seq
    nst = seq // ts
    tok = pl.BlockSpec((ts, HP), lambda i: (i, 0))
    tr = pl.BlockSpec((1, HP, ts), lambda i: (i // nst, 0, i % nst))
    col = pl.BlockSpec((MLA_ROPE, 1), lambda i: (0, 0))
    return pl.pallas_call(
        _rope_body,
        grid=(t // ts,),
        in_specs=[pl.BlockSpec((1, 1, ts), lambda i: (i // nst, 0, i % nst)), col, col],
        out_specs=[tok, tok, tr, tr],
        out_shape=[
            jax.ShapeDtypeStruct((t, HP), F32),
            jax.ShapeDtypeStruct((t, HP), F32),
            jax.ShapeDtypeStruct((bsz, HP, seq), F32),
            jax.ShapeDtypeStruct((bsz, HP, seq), F32),
        ],
        compiler_params=_cparams(("arbitrary",)),
        name="rope_tables",
    )(pos3, inv_col, sgn_col)


def _attn_body(ii_ref, jj_ref, qT_ref, k_ref, vT_ref, o_ref, m_sc, acc_sc, *, tq, tk, ahead, exp_bf16):
    p = pl.program_id(1)
    i = ii_ref[p]
    j = jj_ref[p]

    @pl.when(j == 0)
    def _():
        m_sc[...] = jnp.full(m_sc.shape, NEG, F32)
        acc_sc[...] = jnp.zeros(acc_sc.shape, F32)

    def scores(u):
        h, qc, kb, _ = u
        return jnp.dot(k_ref[kb * ATT_UK:(kb + 1) * ATT_UK, h * HP:(h + 1) * HP],
                       qT_ref[0, h * HP:(h + 1) * HP, qc * ATT_UQ:(qc + 1) * ATT_UQ],
                       preferred_element_type=F32)

    def update(u, s):
        h, qc, kb, masked = u
        if masked:
            kpos = kb * ATT_UK + lax.broadcasted_iota(jnp.int32, s.shape, 0)
            qpos = qc * ATT_UQ + lax.broadcasted_iota(jnp.int32, s.shape, 1)
            s = jnp.where(kpos <= qpos, s, NEG)
        alphas, pms = [], []
        for c in range(ATT_UQ // 128):
            ql = slice(qc * ATT_UQ + c * 128, qc * ATT_UQ + (c + 1) * 128)
            sc = s[:, c * 128:(c + 1) * 128]
            m_old = m_sc[h:h + 1, ql]
            m_new = jnp.maximum(m_old, jnp.max(sc, axis=0, keepdims=True))
            alphas.append(jnp.exp2(m_old - m_new))
            if exp_bf16:
                pms.append(jnp.exp2((sc - m_new).astype(BF16)))
            else:
                pms.append(jnp.exp2(sc - m_new).astype(BF16))
            m_sc[h:h + 1, ql] = m_new
        ql = slice(qc * ATT_UQ, (qc + 1) * ATT_UQ)
        rows = slice(h * VR, (h + 1) * VR)
        acc_sc[rows, ql] = jnp.concatenate(alphas, axis=1) * acc_sc[rows, ql] + jnp.dot(
            vT_ref[0, rows, kb * ATT_UK:(kb + 1) * ATT_UK], jnp.concatenate(pms, axis=1),
            preferred_element_type=F32)

    def run(units):
        pending = [scores(u) for u in units[:ahead]]
        for n, u in enumerate(units):
            s = pending.pop(0)
            if n + ahead < len(units):
                pending.append(scores(units[n + ahead]))
            update(u, s)

    def unit_list(diag):
        units = []
        for kb in range(tk // ATT_UK):
            for h in range(HEADS):
                for qc in range(tq // ATT_UQ):
                    k_lo, k_hi = kb * ATT_UK, (kb + 1) * ATT_UK - 1
                    q_lo, q_hi = qc * ATT_UQ, (qc + 1) * ATT_UQ - 1
                    if diag and k_lo > q_hi:
                        continue
                    units.append((h, qc, kb, diag and k_hi > q_lo))
        return units

    @pl.when(j < i)
    def _():
        run(unit_list(False))

    @pl.when(j == i)
    def _():
        run(unit_list(True))
        parts = []
        for h in range(HEADS):
            den = acc_sc[h * VR + HD:h * VR + HD + 1, :]
            parts.append(acc_sc[h * VR:h * VR + HD, :] * (1.0 / den))
        o_ref[...] = jnp.concatenate(parts, axis=0).T


def _attention(qT, k, vT, bsz, seq, name, tq=1024, ahead=5, exp_bf16=False):
    t = k.shape[0]
    nq = seq // tq
    ii = np.concatenate([np.full(i + 1, i, np.int32) for i in range(nq)])
    jj = np.concatenate([np.arange(i + 1, dtype=np.int32) for i in range(nq)])
    grid_spec = pltpu.PrefetchScalarGridSpec(
        num_scalar_prefetch=2,
        grid=(bsz, len(ii)),
        in_specs=[
            pl.BlockSpec((1, HEADS * HP, tq), lambda b, p, ii, jj: (b, 0, ii[p])),
            pl.BlockSpec((tq, HEADS * HP), lambda b, p, ii, jj: (b * nq + jj[p], 0)),
            pl.BlockSpec((1, HEADS * VR, tq), lambda b, p, ii, jj: (b, 0, jj[p])),
        ],
        out_specs=pl.BlockSpec((tq, GROUP_W), lambda b, p, ii, jj: (b * nq + ii[p], 0)),
        scratch_shapes=[
            pltpu.VMEM((8, tq), F32),
            pltpu.VMEM((HEADS * VR, tq), F32),
        ],
    )
    return pl.pallas_call(
        functools.partial(_attn_body, tq=tq, tk=tq, ahead=ahead, exp_bf16=exp_bf16),
        grid_spec=grid_spec,
        out_shape=jax.ShapeDtypeStruct((t, GROUP_W), F32),
        compiler_params=_cparams(("arbitrary", "arbitrary")),
        name=name,
    )(jnp.asarray(ii), jnp.asarray(jj), qT, k, vT)


def _conv_silu(xpad, src, cw_ref, cb_ref, first, ls):
    width = xpad.shape[1]

    @pl.when(first)
    def _():
        xpad[0:8, :] = jnp.zeros((8, width), F32)

    @pl.when(jnp.logical_not(first))
    def _():
        xpad[0:8, :] = xpad[ls:ls + 8, :]

    xpad[8:ls + 8, :] = src
    acc = cb_ref[...] + cw_ref[0:1, :] * xpad[pl.ds(8 - CONV_K + 1, ls), :]
    for kk in range(1, CONV_K):
        acc = acc + cw_ref[kk:kk + 1, :] * xpad[pl.ds(8 - CONV_K + 1 + kk, ls), :]
    return acc * _sigmoid(acc)


def _ssd_body(zx_ref, sm_ref, cw_ref, cb_ref, dtb_ref, arow_ref, dexp_ref, ng_ref, tri_ref, hexp_ref,
              o_ref, xpad, state, *, ls):
    first = pl.program_id(1) == 0

    @pl.when(first)
    def _():
        state[...] = jnp.zeros(state.shape, F32)

    xc = _conv_silu(xpad, zx_ref[:, GROUP_W:GROUP_W + SSM_XBC], cw_ref, cb_ref, first, ls)
    dt_all = _softplus(sm_ref[...] + dtb_ref[...])
    da_all = dt_all * arow_ref[...]
    lane = lax.broadcasted_iota(jnp.int32, (1, 128), 1)
    half = lane < HD
    head_lanes = lane < HEADS
    hexp = hexp_ref[...].astype(BF16)
    tril = _tril(CHUNK)

    def expand(x):
        x = jnp.where(head_lanes, x, 0.0)
        hi = x.astype(BF16)
        lo = (x - hi.astype(F32)).astype(BF16)
        return (jnp.dot(hi, hexp, preferred_element_type=F32)
                + jnp.dot(lo, hexp, preferred_element_type=F32))

    for c in range(ls // CHUNK):
        r0 = c * CHUNK
        acum = jnp.dot(tri_ref[...], da_all[r0:r0 + CHUNK], precision=HIGHEST,
                       preferred_element_type=F32)
        acum_t = acum.T
        alast = acum[CHUNK - 1:CHUNK, :]
        ex = expand(jnp.concatenate([dt_all[r0:r0 + CHUNK], jnp.exp(acum), jnp.exp(alast - acum),
                                     jnp.exp(alast), jnp.zeros((7, 128), F32)], axis=0))
        dt_x = ex[0:CHUNK]
        ecol_x = ex[CHUNK:2 * CHUNK]
        dout_x = ex[2 * CHUNK:3 * CHUNK]
        cdec_x = ex[3 * CHUNK:3 * CHUNK + 1]
        for g in range(2):
            h0, h1 = 2 * g, 2 * g + 1
            gl = slice(g * 128, (g + 1) * 128)
            xg = xc[r0:r0 + CHUNK, gl]
            bg = xc[r0:r0 + CHUNK, GROUP_W + g * 128:GROUP_W + (g + 1) * 128]
            cg = xc[r0:r0 + CHUNK, 2 * GROUP_W + g * 128:2 * GROUP_W + (g + 1) * 128]
            cgb = cg.astype(BF16)
            gram = lax.dot_general(cgb, bg.astype(BF16), NT_DIMS, preferred_element_type=F32)
            xdt = xg * dt_x[:, gl]
            ydiag = jnp.zeros((CHUNK, 128), F32)
            for hh, msk in ((h0, half), (h1, jnp.logical_not(half))):
                dec = jnp.exp(jnp.where(tril, acum[:, hh:hh + 1] - acum_t[hh:hh + 1, :], -jnp.inf))
                ydiag = ydiag + jnp.dot((gram * dec).astype(BF16),
                                        jnp.where(msk, xdt, 0.0).astype(BF16),
                                        preferred_element_type=F32)
            st = state[g]
            yoff = jnp.dot(cgb, st.astype(BF16), preferred_element_type=F32) * ecol_x[:, gl]
            state[g] = (cdec_x[:, gl] * st
                        + jnp.dot(bg.T.astype(BF16), (xdt * dout_x[:, gl]).astype(BF16),
                                  preferred_element_type=F32))
            y = ydiag + yoff + xg * dexp_ref[:, g * 128:(g + 1) * 128]
            zg = zx_ref[r0:r0 + CHUNK, g * 128:(g + 1) * 128]
            y = y * (zg * _sigmoid(zg))
            o_ref[r0:r0 + CHUNK, g * 128:(g + 1) * 128] = _rms(y, ng_ref[:, g * 128:(g + 1) * 128])


def _ssd(main, cw, cb, dtb, arow, dexp, ng, tri, hexp, bsz, seq, ls=512):
    t = main.shape[0]
    ns = seq // ls

    def full(shape):
        return pl.BlockSpec(shape, lambda b, s: tuple(0 for _ in shape))

    return pl.pallas_call(
        functools.partial(_ssd_body, ls=ls),
        grid=(bsz, ns),
        in_specs=[
            pl.BlockSpec((ls, 1024), lambda b, s: (b * ns + s, COL_ZX // 1024)),
            pl.BlockSpec((ls, 128), lambda b, s: (b * ns + s, COL_SMALL // 128)),
            full((CONV_K, SSM_XBC)), full((1, SSM_XBC)), full((1, 128)), full((1, 128)),
            full((1, GROUP_W)), full((1, GROUP_W)), full((CHUNK, CHUNK)), full((128, GROUP_W)),
        ],
        out_specs=pl.BlockSpec((ls, GROUP_W), lambda b, s: (b * ns + s, 0)),
        out_shape=jax.ShapeDtypeStruct((t, GROUP_W), F32),
        scratch_shapes=[pltpu.VMEM((ls + 8, SSM_XBC), F32), pltpu.VMEM((2, SSM_STATE, 128), F32)],
        compiler_params=_cparams(("arbitrary", "arbitrary")),
        name="ssd",
    )(main, main, cw, cb, dtb, arow, dexp, ng, tri, hexp)


def _mlstm_body(c_ref, sm_ref, cw_ref, cb_ref, igb_ref, fgb_ref, ng_ref, tri_ref, seg_ref, segm_ref,
                bmask_ref, o_ref, xpad, cst, nst, mst, *, ls):
    first = pl.program_id(1) == 0

    @pl.when(first)
    def _():
        cst[...] = jnp.zeros(cst.shape, F32)
        nst[...] = jnp.zeros(nst.shape, F32)
        mst[...] = jnp.zeros(mst.shape, F32)

    qk = _conv_silu(xpad, c_ref[:, 0:2 * GROUP_W], cw_ref, cb_ref, first, ls)
    sm = sm_ref[...]
    ig_all = sm + igb_ref[...]
    fx = sm + fgb_ref[...]
    lf_all = jnp.minimum(fx, 0.0) - jnp.log1p(jnp.exp(-jnp.abs(fx)))
    lane = lax.broadcasted_iota(jnp.int32, (1, GROUP_W), 1)
    hms = [jnp.logical_and(lane >= h * HD, lane < (h + 1) * HD) for h in range(HEADS)]
    tril = _tril(CHUNK)
    lane_s = lax.broadcasted_iota(jnp.int32, (1, 128), 1)
    gate_lanes = jnp.logical_and(lane_s >= GATE_LANE, lane_s < GATE_LANE + HEADS)
    segb = seg_ref[...].astype(BF16)
    hexp = seg_ref[...].T.astype(BF16)
    segm_b = segm_ref[...].astype(BF16)
    zero_b = jnp.zeros((), BF16)
    n_chunks = ls // CHUNK

    def expand(x):
        x = jnp.where(gate_lanes, x, 0.0)
        hi = x.astype(BF16)
        lo = (x - hi.astype(F32)).astype(BF16)
        return (jnp.dot(hi, hexp, preferred_element_type=F32)
                + jnp.dot(lo, hexp, preferred_element_type=F32))

    local = []
    for c in range(n_chunks):
        r0 = c * CHUNK
        q = qk[r0:r0 + CHUNK, 0:GROUP_W]
        k = qk[r0:r0 + CHUNK, GROUP_W:2 * GROUP_W] * (HD ** -0.5)
        ig = ig_all[r0:r0 + CHUNK]
        bcs = jnp.dot(tri_ref[...], lf_all[r0:r0 + CHUNK], precision=HIGHEST,
                      preferred_element_type=F32)
        bcs_t = bcs.T
        ig_t = ig.T
        ig_al = pltpu.roll(ig, GATE_LANE - 4, 1)
        k_t = k.T.astype(BF16)
        qb = q.astype(BF16)
        vb = c_ref[r0:r0 + CHUNK, 2 * GROUP_W:3 * GROUP_W].astype(BF16)
        num_loc = jnp.zeros((CHUNK, GROUP_W), F32)
        m_loc_all = jnp.zeros((CHUNK, 128), F32)
        den_loc_all = jnp.zeros((CHUNK, 128), F32)
        for h in range(HEADS):
            bcol = bcs[:, GATE_LANE + h:GATE_LANE + h + 1]
            dmat = jnp.where(tril, bcol - bcs_t[GATE_LANE + h:GATE_LANE + h + 1, :] + ig_t[4 + h:5 + h, :],
                             -jnp.inf)
            m_loc = jnp.max(dmat, axis=1, keepdims=True)
            s_qk = jnp.dot(jnp.where(hms[h], qb, zero_b), k_t, preferred_element_type=F32)
            sc = s_qk * jnp.exp(dmat - m_loc)
            num_loc = num_loc + jnp.dot(sc.astype(BF16), jnp.where(hms[h], vb, zero_b),
                                        preferred_element_type=F32)
            on_lane = lane_s == GATE_LANE + h
            m_loc_all = jnp.where(on_lane, m_loc, m_loc_all)
            den_loc_all = jnp.where(on_lane, jnp.sum(sc, axis=1, keepdims=True), den_loc_all)
        blast = bcs[CHUNK - 1:CHUNK, :]
        gcol = blast - bcs + ig_al
        gmax = jnp.max(gcol, axis=0, keepdims=True)
        kw = k * expand(jnp.exp(gcol - gmax))
        kwv = bmask_ref[...] * jnp.dot(kw.T.astype(BF16), vb, preferred_element_type=F32)
        local.append((q, num_loc, bcs, m_loc_all, den_loc_all, blast, gmax, kwv,
                      jnp.sum(kw, axis=0, keepdims=True)))

    houts = []
    for c in range(n_chunks):
        q, num_loc, bcs, m_loc_all, den_loc_all, blast, gmax, kwv, ksum = local[c]
        c_old = cst[...]
        n_old = nst[0:1, :]
        m_prev = mst[0:1, :]
        q_c = jnp.dot(q.astype(BF16), c_old.astype(BF16), preferred_element_type=F32)
        q_n = jnp.dot((q * n_old).astype(BF16), segb, preferred_element_type=F32)
        inter = bcs + m_prev
        m_t = jnp.maximum(inter, m_loc_all)
        f = jnp.exp(m_loc_all - m_t)
        w_inter = jnp.exp(inter - m_t)
        den = f * den_loc_all + w_inter * q_n
        inv_den = 1.0 / jnp.maximum(jnp.abs(den), jnp.exp(-m_t))
        m_new = jnp.maximum(blast + m_prev, gmax)
        scal = jnp.concatenate([jnp.exp(gmax - m_new), jnp.exp(blast + m_prev - m_new),
                                jnp.zeros((6, 128), F32)], axis=0)
        ex = expand(jnp.concatenate([f, w_inter, inv_den, scal], axis=0))
        f_x = ex[0:CHUNK]
        w_inter_x = ex[CHUNK:2 * CHUNK]
        inv_den_x = ex[2 * CHUNK:3 * CHUNK]
        sfac_x = ex[3 * CHUNK:3 * CHUNK + 1]
        keep_x = ex[3 * CHUNK + 1:3 * CHUNK + 2]
        mst[0:1, :] = m_new
        cst[...] = keep_x * c_old + sfac_x * kwv
        nst[0:1, :] = keep_x * n_old + sfac_x * ksum
        houts.append((f_x * num_loc + w_inter_x * q_c) * inv_den_x)

    for c in range(n_chunks):
        r0 = c * CHUNK
        hh = houts[c] * _sigmoid(c_ref[r0:r0 + CHUNK, 3 * GROUP_W:4 * GROUP_W])
        sq = hh * hh
        sq_hi = sq.astype(BF16)
        sq_lo = (sq - sq_hi.astype(F32)).astype(BF16)
        ms = (jnp.dot(sq_hi, segm_b, preferred_element_type=F32)
              + jnp.dot(sq_lo, segm_b, preferred_element_type=F32))
        o_ref[r0:r0 + CHUNK, :] = hh * lax.rsqrt(ms + EPS) * ng_ref[...]


def _mlstm(main, cw, cb, igb, fgb, ng, tri, seg, segm, bmask, bsz, seq, ls=512):
    t = main.shape[0]
    ns = seq // ls

    def full(shape):
        return pl.BlockSpec(shape, lambda b, s: tuple(0 for _ in shape))

    return pl.pallas_call(
        functools.partial(_mlstm_body, ls=ls),
        grid=(bsz, ns),
        in_specs=[
            pl.BlockSpec((ls, 1024), lambda b, s: (b * ns + s, COL_ML // 1024)),
            pl.BlockSpec((ls, 128), lambda b, s: (b * ns + s, COL_SMALL // 128)),
            full((CONV_K, 2 * GROUP_W)), full((1, 2 * GROUP_W)), full((1, 128)), full((1, 128)),
            full((1, GROUP_W)), full((CHUNK, CHUNK)), full((GROUP_W, 128)), full((GROUP_W, GROUP_W)),
            full((GROUP_W, GROUP_W)),
        ],
        out_specs=pl.BlockSpec((ls, GROUP_W), lambda b, s: (b * ns + s, 0)),
        out_shape=jax.ShapeDtypeStruct((t, GROUP_W), F32),
        scratch_shapes=[
            pltpu.VMEM((ls + 8, 2 * GROUP_W), F32),
            pltpu.VMEM((GROUP_W, GROUP_W), F32),
            pltpu.VMEM((8, GROUP_W), F32),
            pltpu.VMEM((8, 128), F32),
        ],
        compiler_params=_cparams(("arbitrary", "arbitrary")),
        name="mlstm",
    )(main, main, cw, cb, igb, fgb, ng, tri, seg, segm, bmask)


def _outproj_body(x_ref, ya_ref, yb_ref, yc_ref, yd_ref, wo_ref, g2_ref, wr_ref, br_ref,
                  xn_ref, h2_ref, cb_ref):
    sub = x_ref.shape[0] // 2
    accs = []
    for s in range(2):
        rows = slice(s * sub, (s + 1) * sub)
        acc = x_ref[rows, :]
        for gi, y_ref in enumerate((ya_ref, yb_ref, yc_ref, yd_ref)):
            acc = acc + jnp.dot(y_ref[rows, :].astype(BF16), wo_ref[gi * GROUP_W:(gi + 1) * GROUP_W, :],
                                preferred_element_type=F32)
        accs.append(acc)
    for s in range(2):
        _route(accs[s], slice(s * sub, (s + 1) * sub), g2_ref, wr_ref, br_ref, xn_ref, h2_ref, cb_ref)


def _route(acc, rows, g2_ref, wr_ref, br_ref, xn_ref, h2_ref, cb_ref):
    xn_ref[rows, :] = acc
    h2 = _rms(acc, g2_ref[...])
    h_hi = h2.astype(BF16)
    h2_ref[rows, :] = h_hi
    h_lo = (h2 - h_hi.astype(F32)).astype(BF16)
    hh = jnp.dot(h_hi, wr_ref[...], preferred_element_type=F32)
    lh = jnp.dot(h_lo, wr_ref[...], preferred_element_type=F32)
    logits = hh[:, :128] + hh[:, 128:] + lh[:, :128] + br_ref[...]
    lane = lax.broadcasted_iota(jnp.int32, logits.shape, 1)
    big = jnp.int32(1 << 20)
    n_grp = N_EXPERTS // EXPERTS_PER_GROUP
    gl = jnp.where(jnp.logical_and(lane >= N_EXPERTS, lane < N_EXPERTS + n_grp), logits, -jnp.inf)
    gmax = jnp.max(gl, axis=1, keepdims=True)
    g_w = 1.0 / jnp.sum(jnp.exp(gl - gmax), axis=1, keepdims=True)
    gidx = jnp.min(jnp.where(gl == gmax, lane, big), axis=1, keepdims=True) - N_EXPERTS
    in_grp = jnp.logical_and(lane < N_EXPERTS,
                             lax.shift_right_logical(lane, int(math.log2(EXPERTS_PER_GROUP))) == gidx)
    el = jnp.where(in_grp, logits, -jnp.inf)
    emax = jnp.max(el, axis=1, keepdims=True)
    esum = jnp.sum(jnp.exp(el - emax), axis=1, keepdims=True)
    i1 = jnp.min(jnp.where(el == emax, lane, big), axis=1, keepdims=True)
    el2 = jnp.where(lane == i1, -jnp.inf, el)
    emax2 = jnp.max(el2, axis=1, keepdims=True)
    i2 = jnp.min(jnp.where(el2 == emax2, lane, big), axis=1, keepdims=True)
    p1 = 1.0 / esum
    p2 = jnp.exp(emax2 - emax) / esum
    w1 = p1 / (p1 + p2)
    w2 = p2 / (p1 + p2)
    comb = g_w * (jnp.where(lane == i1, w1, 0.0) + jnp.where(lane == i2, w2, 0.0))
    cb_ref[rows, :] = jnp.where(lane == GID_LANE, gidx.astype(F32), comb)


def _outproj(x2, ya, yb, yc, yd, wo, g2, wr, br, tm=512):
    t = x2.shape[0]
    row = pl.BlockSpec((tm, D_MODEL), lambda i: (i, 0))
    grp = pl.BlockSpec((tm, GROUP_W), lambda i: (i, 0))

    def full(shape):
        return pl.BlockSpec(shape, lambda i: tuple(0 for _ in shape))

    return pl.pallas_call(
        _outproj_body,
        grid=(t // tm,),
        in_specs=[row, grp, grp, grp, grp, full((D_MODEL, D_MODEL)), full((1, D_MODEL)),
                  full((D_MODEL, 256)), full((1, 128))],
        out_specs=[row, row, pl.BlockSpec((tm, 128), lambda i: (i, 0))],
        out_shape=[
            jax.ShapeDtypeStruct((t, D_MODEL), F32),
            jax.ShapeDtypeStruct((t, D_MODEL), BF16),
            jax.ShapeDtypeStruct((t, 128), F32),
        ],
        compiler_params=_cparams(("arbitrary",)),
        name="outproj_router",
    )(x2, ya, yb, yc, yd, wo, g2, wr, br)


MOE_EB = EXPERTS_PER_GROUP
MOE_RB = 128


def _moe_body(h_ref, cb_ref, xn_ref, tri_ref, wg_ref, wu_ref, wd_ref, gf_ref, o_ref,
              perm_t, hs, cs, ys, seg_smem, *, final_norm, rb):
    g = pl.program_id(1)
    tm = h_ref.shape[0]
    n_grp = N_EXPERTS // EXPERTS_PER_GROUP
    lane = lax.broadcasted_iota(jnp.int32, (tm, 128), 1)

    @pl.when(g == 0)
    def _():
        cb = cb_ref[...]
        gid = jnp.sum(jnp.where(lane == GID_LANE, cb, 0.0), axis=1, keepdims=True).astype(jnp.int32)
        onehot = jnp.where(lane == gid, 1.0, 0.0)
        oh16 = onehot.astype(BF16)
        tot = jnp.zeros((1, 128), F32)
        parts = []
        for r0 in range(0, tm, CHUNK):
            part = jnp.dot(tri_ref[...], oh16[r0:r0 + CHUNK], preferred_element_type=F32) + tot
            parts.append(part)
            tot = part[CHUNK - 1:CHUNK, :]
        cum = jnp.concatenate(parts, axis=0)
        lane1 = lane[0:1, :]
        base = jnp.zeros((1, 128), F32)
        run = jnp.zeros((1, 1), F32)
        for k in range(n_grp):
            base = base + jnp.where(lane1 == k, run, 0.0)
            seg_smem[k] = run[0, 0].astype(jnp.int32)
            seg_smem[n_grp + k] = tot[0, k].astype(jnp.int32)
            run = run + tot[:, k:k + 1]
        pos = jnp.sum(onehot * (base + cum - 1.0), axis=1, keepdims=True).astype(jnp.int32)
        pos_row = jnp.broadcast_to(pos.astype(F32), (tm, 128)).T[0:1, :].astype(jnp.int32)
        perm_t[...] = jnp.where(lax.broadcasted_iota(jnp.int32, (tm, tm), 1) == pos, 1.0, 0.0).astype(BF16)
        p = jnp.where(lax.broadcasted_iota(jnp.int32, (tm, tm), 0) == pos_row, 1.0, 0.0).astype(BF16)
        hs[...] = jnp.dot(p, h_ref[...], preferred_element_type=F32).astype(BF16)
        cb_hi = cb.astype(BF16)
        cb_lo = (cb - cb_hi.astype(F32)).astype(BF16)
        cs2 = jnp.dot(p, jnp.concatenate([cb_hi, cb_lo], axis=1), preferred_element_type=F32)
        cs[...] = cs2[:, :128] + cs2[:, 128:]
        ys[...] = jnp.zeros(ys.shape, F32)

    seg_lo = seg_smem[g]
    seg_hi = seg_lo + seg_smem[n_grp + g]
    lane_rb = lax.broadcasted_iota(jnp.int32, (rb, 128), 1)

    def run_block(lo):
        x = hs[lo:lo + rb, :]
        cw = cs[lo:lo + rb, :]

        def gate_up(j):
            return (jnp.dot(x, wg_ref[j], preferred_element_type=F32),
                    jnp.dot(x, wu_ref[j], preferred_element_type=F32))

        def down(j, gu):
            col = jnp.sum(jnp.where(lane_rb == g * MOE_EB + j, cw, 0.0), axis=1,
                          keepdims=True)
            gt, up = gu
            hid = (gt * _sigmoid(gt)) * up * col
            return jnp.dot(hid.astype(BF16), wd_ref[j], preferred_element_type=F32)

        total = jnp.zeros((rb, D_MODEL), F32)
        gu = gate_up(0)
        for j in range(MOE_EB):
            gu_next = gate_up(j + 1) if j + 1 < MOE_EB else None
            total = total + down(j, gu)
            gu = gu_next
        ys[lo:lo + rb, :] += total

    for b in range(tm // rb):
        lo = b * rb
        pl.when(jnp.logical_and(seg_lo < lo + rb, seg_hi > lo))(functools.partial(run_block, lo))

    @pl.when(g == n_grp - 1)
    def _():
        y = xn_ref[...] + jnp.dot(perm_t[...], ys[...].astype(BF16), preferred_element_type=F32)
        if final_norm:
            y = _rms(y, gf_ref[...])
        o_ref[...] = y


def _moe(h2, comb, xn, tri, wg, wu, wd, gf, final_norm, tm=1024, rb=MOE_RB):
    t = h2.shape[0]
    return pl.pallas_call(
        functools.partial(_moe_body, final_norm=final_norm, rb=rb),
        grid=(t // tm, N_EXPERTS // MOE_EB),
        in_specs=[
            pl.BlockSpec((tm, D_MODEL), lambda i, e: (i, 0)),
            pl.BlockSpec((tm, 128), lambda i, e: (i, 0)),
            pl.BlockSpec((tm, D_MODEL), lambda i, e: (i, 0)),
            pl.BlockSpec((CHUNK, CHUNK), lambda i, e: (0, 0)),
            pl.BlockSpec((MOE_EB, D_MODEL, D_EXPERT), lambda i, e: (e, 0, 0)),
            pl.BlockSpec((MOE_EB, D_MODEL, D_EXPERT), lambda i, e: (e, 0, 0)),
            pl.BlockSpec((MOE_EB, D_EXPERT, D_MODEL), lambda i, e: (e, 0, 0)),
            pl.BlockSpec((1, D_MODEL), lambda i, e: (0, 0)),
        ],
        out_specs=pl.BlockSpec((tm, D_MODEL), lambda i, e: (i, 0)),
        out_shape=jax.ShapeDtypeStruct((t, D_MODEL), F32),
        scratch_shapes=[
            pltpu.VMEM((tm, tm), BF16),
            pltpu.VMEM((tm, D_MODEL), BF16),
            pltpu.VMEM((tm, 128), F32),
            pltpu.VMEM((tm, D_MODEL), F32),
            pltpu.SMEM((2 * (N_EXPERTS // MOE_EB),), jnp.int32),
        ],
        compiler_params=_cparams(("arbitrary", "arbitrary")),
        name="moe",
    )(h2, comb, xn, tri, wg, wu, wd, gf)


def _spread_heads(w, width, slab, total):
    rows = w.shape[0]
    pieces = []
    for h in range(HEADS):
        pieces.append(w[:, h * width:(h + 1) * width])
        pieces.append(jnp.zeros((rows, slab - width), w.dtype))
    if total > HEADS * slab:
        pieces.append(jnp.zeros((rows, total - HEADS * slab), w.dtype))
    return jnp.concatenate(pieces, axis=1)


def _pad_heads(w, width):
    return _spread_heads(w, width, HP, HEADS * HP)


def _v_cols(w):
    return _spread_heads(w, HD, VR, V_COLS)


def _lane_row(vals, offset, width=128):
    return jnp.zeros((1, width), F32).at[0, offset:offset + vals.shape[0]].set(vals.astype(F32))


def _layer_weights(l, p):
    half = MLA_ROPE // 2
    w = {"g_mix": p["norm_mix_g"][l][None, :]}
    w["ssm_cw"] = p["ssm_conv_w"][l]
    w["ssm_cb"] = p["ssm_conv_b"][l][None, :]
    w["ssm_dtb"] = _lane_row(p["ssm_dt_bias"][l], 0)
    w["ssm_arow"] = _lane_row(-jnp.exp(p["ssm_a_log"][l].astype(F32)), 0)
    w["ssm_dexp"] = jnp.repeat(p["ssm_d"][l].astype(F32), HD)[None, :]
    w["ssm_ng"] = p["ssm_norm_g"][l][None, :]
    w["ml_cw"] = p["ml_conv_w"][l]
    w["ml_cb"] = p["ml_conv_b"][l][None, :]
    w["ml_igb"] = _lane_row(p["ml_ig_bias"][l], 4)
    w["ml_fgb"] = _lane_row(p["ml_fg_bias"][l], 8)
    w["ml_ng"] = p["ml_norm_g"][l][None, :]
    wuq = p["mla_w_uq"][l]
    qd = MLA_NOPE + MLA_ROPE
    swapped = []
    for h in range(HEADS):
        c0 = h * qd + MLA_NOPE
        swapped += [jnp.zeros((256, MLA_NOPE), F32), wuq[:, c0 + half:c0 + 2 * half], wuq[:, c0:c0 + half]]
    wq = _spread_heads(wuq, qd, HP, HEADS * HP)
    wqs = _spread_heads(jnp.concatenate(swapped, axis=1), qd, HP, HEADS * HP)
    w["mla_wq2"] = jnp.concatenate([wq, wqs], axis=1).astype(BF16)
    wukv = p["mla_w_ukv"][l]
    wk_cols = jnp.concatenate([wukv[:, h * 2 * HD:h * 2 * HD + MLA_NOPE] for h in range(HEADS)], axis=1)
    wv_cols = jnp.concatenate([wukv[:, h * 2 * HD + MLA_NOPE:(h + 1) * 2 * HD] for h in range(HEADS)], axis=1)
    w["mla_wk"] = _pad_heads(wk_cols, MLA_NOPE).astype(BF16)
    w["mla_wv"] = _v_cols(wv_cols).astype(BF16)
    w["mla_gq"] = p["mla_q_norm_g"][l][None, :]
    w["mla_gkv"] = p["mla_kv_norm_g"][l][None, :]
    w["w_out"] = p["w_out"][l].astype(BF16)
    w["g_ffn"] = p["norm_ffn_g"][l][None, :]
    wr = jnp.zeros((D_MODEL, 128), F32)
    wr = wr.at[:, :N_EXPERTS].set(p["moe_w_router"][l]).at[:, N_EXPERTS:N_EXPERTS + 4].set(p["moe_w_group"][l])
    wr_hi = wr.astype(BF16)
    w["w_router"] = jnp.concatenate([wr_hi, (wr - wr_hi.astype(F32)).astype(BF16)], axis=1)
    w["b_router"] = _lane_row(p["moe_b_router"][l], 0) + _lane_row(p["moe_b_group"][l], N_EXPERTS)
    w["w_gate"] = p["moe_w_gate"][l].astype(BF16)
    w["w_up"] = p["moe_w_up"][l].astype(BF16)
    w["w_down"] = p["moe_w_down"][l].astype(BF16)
    return w


def kernel(x, positions, norm_mix_g, w_in, ssm_conv_w, ssm_conv_b, ssm_dt_bias, ssm_a_log, ssm_d, ssm_norm_g, ml_conv_w, ml_conv_b, ml_ig_bias, ml_fg_bias, ml_norm_g, mla_q_norm_g, mla_w_uq, mla_kv_norm_g, mla_w_ukv, w_out, norm_ffn_g, moe_w_group, moe_b_group, moe_w_router, moe_b_router, moe_w_gate, moe_w_up, moe_w_down, final_norm_g):
    p = dict(norm_mix_g=norm_mix_g, w_all=_relayout_w_in(w_in), ssm_conv_w=ssm_conv_w, ssm_conv_b=ssm_conv_b,
             ssm_dt_bias=ssm_dt_bias, ssm_a_log=ssm_a_log, ssm_d=ssm_d, ssm_norm_g=ssm_norm_g,
             ml_conv_w=ml_conv_w, ml_conv_b=ml_conv_b, ml_ig_bias=ml_ig_bias, ml_fg_bias=ml_fg_bias,
             ml_norm_g=ml_norm_g, mla_q_norm_g=mla_q_norm_g, mla_w_uq=mla_w_uq,
             mla_kv_norm_g=mla_kv_norm_g, mla_w_ukv=mla_w_ukv, w_out=w_out, norm_ffn_g=norm_ffn_g,
             moe_w_group=moe_w_group, moe_b_group=moe_b_group, moe_w_router=moe_w_router,
             moe_b_router=moe_b_router, moe_w_gate=moe_w_gate, moe_w_up=moe_w_up, moe_w_down=moe_w_down)
    bsz, seq, _ = x.shape
    depth = w_in.shape[0]
    x2 = x.reshape(bsz * seq, D_MODEL)

    half = MLA_ROPE // 2
    inv = ROPE_THETA ** (-jnp.arange(0, MLA_ROPE, 2, dtype=F32) / MLA_ROPE)
    inv_col = jnp.concatenate([inv, inv])[:, None]
    sgn_col = jnp.concatenate([-jnp.ones((half,), F32), jnp.ones((half,), F32)])[:, None]
    pe_mask = _lane_row(jnp.ones((MLA_ROPE,), F32), MLA_NOPE)
    c_tok, s_tok, c_tr, s_tr = _rope_tables(positions.reshape(bsz, 1, seq), inv_col, sgn_col, bsz, seq)
    tri = jnp.tril(jnp.ones((CHUNK, CHUNK), F32))
    tri_moe = tri.astype(BF16)
    head_of = jnp.arange(GROUP_W) // HD
    seg = (head_of[:, None] + GATE_LANE == jnp.arange(128)[None, :]).astype(F32)
    ssd_hexp = (jnp.arange(128)[:, None] == head_of[None, :]).astype(F32)
    bmask = (head_of[:, None] == head_of[None, :]).astype(F32)
    segm = bmask / HD

    for l in range(depth):
        w = _layer_weights(l, p)
        main, k_a, qT_a, vT_a, qT_d, k_d, vT_d = _in_proj(
            x2, w["g_mix"], p["w_all"], l, c_tok, s_tok, c_tr, s_tr, w["mla_gq"], w["mla_wq2"], w["mla_gkv"],
            w["mla_wk"], w["mla_wv"], pe_mask, bsz, seq)
        y_a = _attention(qT_a, k_a, vT_a, bsz, seq, "attn_moba", exp_bf16=True)
        y_b = _ssd(main, w["ssm_cw"], w["ssm_cb"], w["ssm_dtb"], w["ssm_arow"], w["ssm_dexp"], w["ssm_ng"],
                   tri, ssd_hexp, bsz, seq)
        y_c = _mlstm(main, w["ml_cw"], w["ml_cb"], w["ml_igb"], w["ml_fgb"], w["ml_ng"], tri, seg, segm,
                     bmask, bsz, seq)
        y_d = _attention(qT_d, k_d, vT_d, bsz, seq, "attn_mla")
        xn, h2, comb = _outproj(x2, y_a, y_b, y_c, y_d, w["w_out"], w["g_ffn"], w["w_router"], w["b_router"])
        x2 = _moe(h2, comb, xn, tri_moe, w["w_gate"], w["w_up"], w["w_down"], final_norm_g[None, :],
                  final_norm=(l == depth - 1), tm=MOE_TM)
    return x2.reshape(bsz, seq, D_MODEL)
```

```python
import functools
import math

import numpy as np
import jax
import jax.numpy as jnp
from jax import lax
from jax.experimental import pallas as pl
from jax.experimental.pallas import tpu as pltpu

F32 = jnp.float32
BF16 = jnp.bfloat16
HIGHEST = lax.Precision.HIGHEST

EPS = 1e-6
NEG = -1e30
LOG2E = 1.4426950408889634

D_MODEL = 1024
GROUP_W = 256
HEADS = 4
HD = 64
HP = 128
MOBA_BLOCK = 256
MOBA_TOPK = 3
SSM_STATE = 128
SSM_XBC = 768
CONV_K = 4
CHUNK = 128
MLA_NOPE = 64
MLA_ROPE = 32
ROPE_THETA = 10000.0
N_EXPERTS = 16
EXPERTS_PER_GROUP = 4
D_EXPERT = 256
GATE_LANE = 8
GID_LANE = 20
MOE_TM = 1024

COL_ZX = 0
COL_ML = 1024
COL_SMALL = 2048
N_MAIN = 2176
VR = 80
V_COLS = 384
N_ALL = N_MAIN + 256 + 128 + 3 * GROUP_W
ATT_UQ = 256
ATT_UK = 256

V7X_VMEM_LIMIT = 56 * 1024 * 1024

NT_DIMS = (((1,), (1,)), ((), ()))


def _cparams(sem):
    return pltpu.CompilerParams(dimension_semantics=sem, vmem_limit_bytes=V7X_VMEM_LIMIT)


def _rms(x, g):
    ms = jnp.mean(x * x, axis=-1, keepdims=True)
    return x * lax.rsqrt(ms + EPS) * g


def _sigmoid(x):
    return 0.5 * jnp.tanh(0.5 * x) + 0.5


def _softplus(x):
    return jnp.maximum(x, 0.0) + jnp.log1p(jnp.exp(-jnp.abs(x)))


def _tril(n):
    r = lax.broadcasted_iota(jnp.int32, (n, n), 0)
    c = lax.broadcasted_iota(jnp.int32, (n, n), 1)
    return c <= r


W_IN_SIZES = (3 * GROUP_W, GROUP_W, SSM_XBC, HEADS, 2 * GROUP_W, GROUP_W, HEADS, HEADS, GROUP_W, 256, 128,
              MLA_ROPE)


def _relayout_body(w_ref, o_ref):
    w = w_ref[0]
    rows = w.shape[0]
    offs = np.cumsum((0,) + W_IN_SIZES)
    a_qkv, s_z, s_xbc, s_dt, m_qk, m_v, m_i, m_f, m_o, d_cq, d_ckv, d_kr = [
        w[:, int(offs[i]):int(offs[i + 1])] for i in range(len(W_IN_SIZES))]
    half = MLA_ROPE // 2

    def zeros(n):
        return jnp.zeros((rows, n), F32)

    pieces = [s_z, s_xbc, m_qk, m_v, m_o,
              s_dt, m_i, m_f, zeros(64 - 3 * HEADS), d_kr, d_kr[:, half:], d_kr[:, :half],
              d_cq, d_ckv,
              a_qkv[:, GROUP_W:2 * GROUP_W], a_qkv[:, :GROUP_W], a_qkv[:, 2 * GROUP_W:]]
    o_ref[0] = jnp.concatenate(pieces, axis=1).astype(BF16)


def _relayout_w_in(w, rows=256):
    nl, r, c = w.shape
    return pl.pallas_call(
        _relayout_body,
        grid=(nl, r // rows),
        in_specs=[pl.BlockSpec((1, rows, c), lambda l, i: (l, i, 0))],
        out_specs=pl.BlockSpec((1, rows, N_ALL), lambda l, i: (l, i, 0)),
        out_shape=jax.ShapeDtypeStruct((nl, r, N_ALL), BF16),
        compiler_params=_cparams(("arbitrary", "arbitrary")),
        name="relayout_w_in",
    )(w)


def _ones_rows(shape):
    row = lax.broadcasted_iota(jnp.int32, shape, 0)
    hit = row == HD
    for h in range(1, HEADS):
        hit = jnp.logical_or(hit, row == h * VR + HD)
    return jnp.where(hit, 1.0, 0.0)


def _in_proj_body(x_ref, g_ref, w_ref, c_ref, s_ref, ct_ref, st_ref, gq_ref, wq_ref, gkv_ref, wk_ref,
                  wv_ref, msk_ref, om_ref, ka_ref, qa_ref, va_ref, qd_ref, kd_ref, vd_ref, km_sc,
                  *, nb, nst, qscale_a, qscale_d):
    s_idx = pl.program_id(0) % nst
    tm = x_ref.shape[0]

    @pl.when(s_idx == 0)
    def _():
        km_sc[...] = jnp.zeros(km_sc.shape, F32)

    h = _rms(x_ref[...], g_ref[...]).astype(BF16)
    rest = jnp.dot(h, w_ref[0, :, COL_SMALL:], preferred_element_type=F32)
    for c in range(0, COL_SMALL, 512):
        om_ref[:, c:c + 512] = jnp.dot(h, w_ref[0, :, c:c + 512], preferred_element_type=F32)
    sm = rest[:, 0:128]
    om_ref[:, COL_SMALL:COL_SMALL + 128] = sm
    cq = rest[:, 128:384]
    ckv = rest[:, 384:512]
    kp = _spread_heads(rest[:, 512:768], HD, HP, HEADS * HP)
    qp = _spread_heads(rest[:, 768:1024], HD, HP, HEADS * HP)
    vc = _spread_heads(rest[:, 1024:1280], HD, VR, V_COLS)

    nblk = tm // MOBA_BLOCK
    lane_k = lax.broadcasted_iota(jnp.int32, (MOBA_BLOCK, HEADS * HP), 1)
    for j in range(nblk):
        kj = kp[j * MOBA_BLOCK:(j + 1) * MOBA_BLOCK]
        blk_id = s_idx * nblk + j
        km_sc[pl.ds(blk_id, 1), :] = jnp.mean(kj, axis=0, keepdims=True)
        ka_ref[j * MOBA_BLOCK:(j + 1) * MOBA_BLOCK, :] = jnp.where(
            (lane_k & (HP - 1)) == HD + blk_id, 1.0, kj).astype(BF16)

    qT = qp.T
    km = km_sc[...]
    pos = s_idx * tm + lax.broadcasted_iota(jnp.int32, (nb, tm), 1)
    cur = lax.shift_right_logical(pos, int(math.log2(MOBA_BLOCK)))
    blk = lax.broadcasted_iota(jnp.int32, (nb, tm), 0)
    past = blk < cur
    for hd in range(HEADS):
        qh = qT[hd * HP:(hd + 1) * HP, :]
        gate = jnp.dot(km[:, hd * HP:(hd + 1) * HP], qh, precision=HIGHEST,
                       preferred_element_type=F32)
        gate = jnp.where(past, gate, -jnp.inf)
        bias = jnp.where(blk == cur, 0.0, NEG)
        for _ in range(MOBA_TOPK):
            m = jnp.max(gate, axis=0, keepdims=True)
            cand = jnp.logical_and(gate == m, gate > -jnp.inf)
            idx = jnp.min(jnp.where(cand, blk, nb), axis=0, keepdims=True)
            pick = blk == idx
            bias = jnp.where(pick, 0.0, bias)
            gate = jnp.where(pick, -jnp.inf, gate)
        rows = [qh[:HD] * qscale_a, bias]
        if HP - HD - nb:
            rows.append(jnp.zeros((HP - HD - nb, tm), F32))
        qa_ref[0, hd * HP:(hd + 1) * HP, :] = jnp.concatenate(rows, axis=0).astype(BF16)
    vt = vc.T[:HEADS * VR]
    va_ref[0] = (vt + _ones_rows(vt.shape)).astype(BF16)

    qn = _rms(cq, gq_ref[...]).astype(BF16)
    q2 = jnp.dot(qn, wq_ref[...], preferred_element_type=F32).T
    nq = HEADS * HP
    ct4 = jnp.concatenate([ct_ref[0]] * HEADS, axis=0)
    st4 = jnp.concatenate([st_ref[0]] * HEADS, axis=0)
    qd_ref[0] = ((q2[:nq] * ct4 + q2[nq:] * st4) * qscale_d).astype(BF16)
    kvn = _rms(ckv, gkv_ref[...]).astype(BF16)
    kn = jnp.dot(kvn, wk_ref[...], preferred_element_type=F32)
    pe = (sm * c_ref[...] + pltpu.roll(sm, HP - MLA_ROPE, 1) * s_ref[...]) * msk_ref[...]
    kd_ref[...] = (kn + jnp.concatenate([pe] * HEADS, axis=1)).astype(BF16)
    vtd = jnp.dot(kvn, wv_ref[...], preferred_element_type=F32).T[:HEADS * VR]
    vd_ref[0] = (vtd + _ones_rows(vtd.shape)).astype(BF16)


def _in_proj(x2, g, w_all, layer, c_tok, s_tok, c_tr, s_tr, gq, wq2, gkv, wk, wv, pe_mask, bsz, seq, tm=512):
    t = x2.shape[0]
    nst = seq // tm
    nb = seq // MOBA_BLOCK
    assert nb <= HP - HD
    tok128 = pl.BlockSpec((tm, HP), lambda i: (i, 0))
    tr128 = pl.BlockSpec((1, HP, tm), lambda i: (i // nst, 0, i % nst))
    tr_q = pl.BlockSpec((1, HEADS * HP, tm), lambda i: (i // nst, 0, i % nst))
    tr_v = pl.BlockSpec((1, HEADS * VR, tm), lambda i: (i // nst, 0, i % nst))
    tok_k = pl.BlockSpec((tm, HEADS * HP), lambda i: (i, 0))

    def full(shape):
        return pl.BlockSpec(shape, lambda i: tuple(0 for _ in shape))

    q_shape = jax.ShapeDtypeStruct((bsz, HEADS * HP, seq), BF16)
    k_shape = jax.ShapeDtypeStruct((t, HEADS * HP), BF16)
    v_shape = jax.ShapeDtypeStruct((bsz, HEADS * VR, seq), BF16)
    return pl.pallas_call(
        functools.partial(_in_proj_body, nb=nb, nst=nst, qscale_a=(HD ** -0.5) * LOG2E,
                          qscale_d=((MLA_NOPE + MLA_ROPE) ** -0.5) * LOG2E),
        grid=(t // tm,),
        in_specs=[
            pl.BlockSpec((tm, D_MODEL), lambda i: (i, 0)),
            full((1, D_MODEL)), pl.BlockSpec((1, D_MODEL, N_ALL), lambda i: (layer, 0, 0)),
            tok128, tok128, tr128, tr128,
            full((1, 256)), full((256, 2 * HEADS * HP)), full((1, 128)),
            full((128, HEADS * HP)), full((128, V_COLS)), full((1, HP)),
        ],
        out_specs=[pl.BlockSpec((tm, N_MAIN), lambda i: (i, 0)), tok_k, tr_q, tr_v, tr_q, tok_k, tr_v],
        out_shape=[jax.ShapeDtypeStruct((t, N_MAIN), F32), k_shape, q_shape, v_shape,
                   q_shape, k_shape, v_shape],
        scratch_shapes=[pltpu.VMEM((nb, HEADS * HP), F32)],
        compiler_params=_cparams(("arbitrary",)),
        name="in_proj",
    )(x2, g, w_all, c_tok, s_tok, c_tr, s_tr, gq, wq2, gkv, wk, wv, pe_mask)


def _rope_body(pos_ref, inv_ref, sgn_ref, c_ref, s_ref, ct_ref, st_ref):
    ts = pos_ref.shape[2]
    ang = inv_ref[...] * pos_ref[0].astype(F32)
    c32 = jnp.cos(ang)
    s32 = jnp.sin(ang) * sgn_ref[...]
    ct = jnp.concatenate([jnp.ones((MLA_NOPE, ts), F32), c32,
                          jnp.ones((HP - MLA_NOPE - MLA_ROPE, ts), F32)], axis=0)
    st = jnp.concatenate([jnp.zeros((MLA_NOPE, ts), F32), s32,
                          jnp.zeros((HP - MLA_NOPE - MLA_ROPE, ts), F32)], axis=0)
    ct_ref[0] = ct
    st_ref[0] = st
    c_ref[...] = ct.T
    s_ref[...] = st.T


def _rope_tables(pos3, inv_col, sgn_col, bsz, seq, ts=512):
    t = bsz * seq
    nst = seq // ts
    tok = pl.BlockSpec((ts, HP), lambda i: (i, 0))
    tr = pl.BlockSpec((1, HP, ts), lambda i: (i // nst, 0, i % nst))
    col = pl.BlockSpec((MLA_ROPE, 1), lambda i: (0, 0))
    return pl.pallas_call(
        _rope_body,
        grid=(t // ts,),
        in_specs=[pl.BlockSpec((1, 1, ts), lambda i: (i // nst, 0, i % nst)), col, col],
        out_specs=[tok, tok, tr, tr],
        out_shape=[
            jax.ShapeDtypeStruct((t, HP), F32),
            jax.ShapeDtypeStruct((t, HP), F32),
            jax.ShapeDtypeStruct((bsz, HP, seq), F32),
            jax.ShapeDtypeStruct((bsz, HP, seq), F32),
        ],
        compiler_params=_cparams(("arbitrary",)),
        name="rope_tables",
    )(pos3, inv_col, sgn_col)


def _attn_body(ii_ref, jj_ref, qT_ref, k_ref, vT_ref, o_ref, m_sc, acc_sc, *, tq, tk, ahead):
    p = pl.program_id(1)
    i = ii_ref[p]
    j = jj_ref[p]

    @pl.when(j == 0)
    def _():
        m_sc[...] = jnp.full(m_sc.shape, NEG, F32)
        acc_sc[...] = jnp.zeros(acc_sc.shape, F32)

    def scores(u):
        h, qc, kb, _ = u
        return jnp.dot(k_ref[kb * ATT_UK:(kb + 1) * ATT_UK, h * HP:(h + 1) * HP],
                       qT_ref[0, h * HP:(h + 1) * HP, qc * ATT_UQ:(qc + 1) * ATT_UQ],
                       preferred_element_type=F32)

    def update(u, s):
        h, qc, kb, masked = u
        if masked:
            kpos = kb * ATT_UK + lax.broadcasted_iota(jnp.int32, s.shape, 0)
            qpos = qc * ATT_UQ + lax.broadcasted_iota(jnp.int32, s.shape, 1)
            s = jnp.where(kpos <= qpos, s, NEG)
        alphas, pms = [], []
        for c in range(ATT_UQ // 128):
            ql = slice(qc * ATT_UQ + c * 128, qc * ATT_UQ + (c + 1) * 128)
            sc = s[:, c * 128:(c + 1) * 128]
            m_old = m_sc[h:h + 1, ql]
            m_new = jnp.maximum(m_old, jnp.max(sc, axis=0, keepdims=True))
            alphas.append(jnp.exp2(m_old - m_new))
            pms.append(jnp.exp2(sc - m_new).astype(BF16))
            m_sc[h:h + 1, ql] = m_new
        ql = slice(qc * ATT_UQ, (qc + 1) * ATT_UQ)
        rows = slice(h * VR, (h + 1) * VR)
        acc_sc[rows, ql] = jnp.concatenate(alphas, axis=1) * acc_sc[rows, ql] + jnp.dot(
            vT_ref[0, rows, kb * ATT_UK:(kb + 1) * ATT_UK], jnp.concatenate(pms, axis=1),
            preferred_element_type=F32)

    def run(units):
        pending = [scores(u) for u in units[:ahead]]
        for n, u in enumerate(units):
            s = pending.pop(0)
            if n + ahead < len(units):
                pending.append(scores(units[n + ahead]))
            update(u, s)

    def unit_list(diag):
        units = []
        for kb in range(tk // ATT_UK):
            for h in range(HEADS):
                for qc in range(tq // ATT_UQ):
                    k_lo, k_hi = kb * ATT_UK, (kb + 1) * ATT_UK - 1
                    q_lo, q_hi = qc * ATT_UQ, (qc + 1) * ATT_UQ - 1
                    if diag and k_lo > q_hi:
                        continue
                    units.append((h, qc, kb, diag and k_hi > q_lo))
        return units

    @pl.when(j < i)
    def _():
        run(unit_list(False))

    @pl.when(j == i)
    def _():
        run(unit_list(True))
        parts = []
        for h in range(HEADS):
            den = acc_sc[h * VR + HD:h * VR + HD + 1, :]
            parts.append(acc_sc[h * VR:h * VR + HD, :] * (1.0 / den))
        o_ref[...] = jnp.concatenate(parts, axis=0).T


def _attention(qT, k, vT, bsz, seq, name, tq=1024, ahead=5):
    t = k.shape[0]
    nq = seq // tq
    ii = np.concatenate([np.full(i + 1, i, np.int32) for i in range(nq)])
    jj = np.concatenate([np.arange(i + 1, dtype=np.int32) for i in range(nq)])
    grid_spec = pltpu.PrefetchScalarGridSpec(
        num_scalar_prefetch=2,
        grid=(bsz, len(ii)),
        in_specs=[
            pl.BlockSpec((1, HEADS * HP, tq), lambda b, p, ii, jj: (b, 0, ii[p])),
            pl.BlockSpec((tq, HEADS * HP), lambda b, p, ii, jj: (b * nq + jj[p], 0)),
            pl.BlockSpec((1, HEADS * VR, tq), lambda b, p, ii, jj: (b, 0, jj[p])),
        ],
        out_specs=pl.BlockSpec((tq, GROUP_W), lambda b, p, ii, jj: (b * nq + ii[p], 0)),
        scratch_shapes=[
            pltpu.VMEM((8, tq), F32),
            pltpu.VMEM((HEADS * VR, tq), F32),
        ],
    )
    return pl.pallas_call(
        functools.partial(_attn_body, tq=tq, tk=tq, ahead=ahead),
        grid_spec=grid_spec,
        out_shape=jax.ShapeDtypeStruct((t, GROUP_W), F32),
        compiler_params=_cparams(("arbitrary", "arbitrary")),
        name=name,
    )(jnp.asarray(ii), jnp.asarray(jj), qT, k, vT)


def _conv_silu(xpad, src, cw_ref, cb_ref, first, ls):
    width = xpad.shape[1]

    @pl.when(first)
    def _():
        xpad[0:8, :] = jnp.zeros((8, width), F32)

    @pl.when(jnp.logical_not(first))
    def _():
        xpad[0:8, :] = xpad[ls:ls + 8, :]

    xpad[8:ls + 8, :] = src
    acc = cb_ref[...] + cw_ref[0:1, :] * xpad[pl.ds(8 - CONV_K + 1, ls), :]
    for kk in range(1, CONV_K):
        acc = acc + cw_ref[kk:kk + 1, :] * xpad[pl.ds(8 - CONV_K + 1 + kk, ls), :]
    return acc * _sigmoid(acc)


def _ssd_body(zx_ref, sm_ref, cw_ref, cb_ref, dtb_ref, arow_ref, dexp_ref, ng_ref, tri_ref, hexp_ref,
              o_ref, xpad, state, *, ls):
    first = pl.program_id(1) == 0

    @pl.when(first)
    def _():
        state[...] = jnp.zeros(state.shape, F32)

    xc = _conv_silu(xpad, zx_ref[:, GROUP_W:GROUP_W + SSM_XBC], cw_ref, cb_ref, first, ls)
    dt_all = _softplus(sm_ref[...] + dtb_ref[...])
    da_all = dt_all * arow_ref[...]
    lane = lax.broadcasted_iota(jnp.int32, (1, 128), 1)
    half = lane < HD
    head_lanes = lane < HEADS
    hexp = hexp_ref[...].astype(BF16)
    tril = _tril(CHUNK)

    def expand(x):
        x = jnp.where(head_lanes, x, 0.0)
        hi = x.astype(BF16)
        lo = (x - hi.astype(F32)).astype(BF16)
        return (jnp.dot(hi, hexp, preferred_element_type=F32)
                + jnp.dot(lo, hexp, preferred_element_type=F32))

    for c in range(ls // CHUNK):
        r0 = c * CHUNK
        acum = jnp.dot(tri_ref[...], da_all[r0:r0 + CHUNK], precision=HIGHEST,
                       preferred_element_type=F32)
        acum_t = acum.T
        alast = acum[CHUNK - 1:CHUNK, :]
        ex = expand(jnp.concatenate([dt_all[r0:r0 + CHUNK], jnp.exp(acum), jnp.exp(alast - acum),
                                     jnp.exp(alast), jnp.zeros((7, 128), F32)], axis=0))
        dt_x = ex[0:CHUNK]
        ecol_x = ex[CHUNK:2 * CHUNK]
        dout_x = ex[2 * CHUNK:3 * CHUNK]
        cdec_x = ex[3 * CHUNK:3 * CHUNK + 1]
        for g in range(2):
            h0, h1 = 2 * g, 2 * g + 1
            gl = slice(g * 128, (g + 1) * 128)
            xg = xc[r0:r0 + CHUNK, gl]
            bg = xc[r0:r0 + CHUNK, GROUP_W + g * 128:GROUP_W + (g + 1) * 128]
            cg = xc[r0:r0 + CHUNK, 2 * GROUP_W + g * 128:2 * GROUP_W + (g + 1) * 128]
            cgb = cg.astype(BF16)
            gram = lax.dot_general(cgb, bg.astype(BF16), NT_DIMS, preferred_element_type=F32)
            xdt = xg * dt_x[:, gl]
            ydiag = jnp.zeros((CHUNK, 128), F32)
            for hh, msk in ((h0, half), (h1, jnp.logical_not(half))):
                dec = jnp.exp(jnp.where(tril, acum[:, hh:hh + 1] - acum_t[hh:hh + 1, :], -jnp.inf))
                ydiag = ydiag + jnp.dot((gram * dec).astype(BF16),
                                        jnp.where(msk, xdt, 0.0).astype(BF16),
                                        preferred_element_type=F32)
            st = state[g]
            yoff = jnp.dot(cgb, st.astype(BF16), preferred_element_type=F32) * ecol_x[:, gl]
            state[g] = (cdec_x[:, gl] * st
                        + jnp.dot(bg.T.astype(BF16), (xdt * dout_x[:, gl]).astype(BF16),
                                  preferred_element_type=F32))
            y = ydiag + yoff + xg * dexp_ref[:, g * 128:(g + 1) * 128]
            zg = zx_ref[r0:r0 + CHUNK, g * 128:(g + 1) * 128]
            y = y * (zg * _sigmoid(zg))
            o_ref[r0:r0 + CHUNK, g * 128:(g + 1) * 128] = _rms(y, ng_ref[:, g * 128:(g + 1) * 128])


def _ssd(main, cw, cb, dtb, arow, dexp, ng, tri, hexp, bsz, seq, ls=512):
    t = main.shape[0]
    ns = seq // ls

    def full(shape):
        return pl.BlockSpec(shape, lambda b, s: tuple(0 for _ in shape))

    return pl.pallas_call(
        functools.partial(_ssd_body, ls=ls),
        grid=(bsz, ns),
        in_specs=[
            pl.BlockSpec((ls, 1024), lambda b, s: (b * ns + s, COL_ZX // 1024)),
            pl.BlockSpec((ls, 128), lambda b, s: (b * ns + s, COL_SMALL // 128)),
            full((CONV_K, SSM_XBC)), full((1, SSM_XBC)), full((1, 128)), full((1, 128)),
            full((1, GROUP_W)), full((1, GROUP_W)), full((CHUNK, CHUNK)), full((128, GROUP_W)),
        ],
        out_specs=pl.BlockSpec((ls, GROUP_W), lambda b, s: (b * ns + s, 0)),
        out_shape=jax.ShapeDtypeStruct((t, GROUP_W), F32),
        scratch_shapes=[pltpu.VMEM((ls + 8, SSM_XBC), F32), pltpu.VMEM((2, SSM_STATE, 128), F32)],
        compiler_params=_cparams(("arbitrary", "arbitrary")),
        name="ssd",
    )(main, main, cw, cb, dtb, arow, dexp, ng, tri, hexp)


def _mlstm_body(c_ref, sm_ref, cw_ref, cb_ref, igb_ref, fgb_ref, ng_ref, tri_ref, seg_ref, segm_ref,
                bmask_ref, o_ref, xpad, cst, nst, mst, *, ls):
    first = pl.program_id(1) == 0

    @pl.when(first)
    def _():
        cst[...] = jnp.zeros(cst.shape, F32)
        nst[...] = jnp.zeros(nst.shape, F32)
        mst[...] = jnp.zeros(mst.shape, F32)

    qk = _conv_silu(xpad, c_ref[:, 0:2 * GROUP_W], cw_ref, cb_ref, first, ls)
    sm = sm_ref[...]
    ig_all = sm + igb_ref[...]
    fx = sm + fgb_ref[...]
    lf_all = jnp.minimum(fx, 0.0) - jnp.log1p(jnp.exp(-jnp.abs(fx)))
    lane = lax.broadcasted_iota(jnp.int32, (1, GROUP_W), 1)
    hms = [jnp.logical_and(lane >= h * HD, lane < (h + 1) * HD) for h in range(HEADS)]
    tril = _tril(CHUNK)
    lane_s = lax.broadcasted_iota(jnp.int32, (1, 128), 1)
    gate_lanes = jnp.logical_and(lane_s >= GATE_LANE, lane_s < GATE_LANE + HEADS)
    segb = seg_ref[...].astype(BF16)
    hexp = seg_ref[...].T.astype(BF16)
    segm_b = segm_ref[...].astype(BF16)
    zero_b = jnp.zeros((), BF16)
    n_chunks = ls // CHUNK

    def expand(x):
        x = jnp.where(gate_lanes, x, 0.0)
        hi = x.astype(BF16)
        lo = (x - hi.astype(F32)).astype(BF16)
        return (jnp.dot(hi, hexp, preferred_element_type=F32)
                + jnp.dot(lo, hexp, preferred_element_type=F32))

    local = []
    for c in range(n_chunks):
        r0 = c * CHUNK
        q = qk[r0:r0 + CHUNK, 0:GROUP_W]
        k = qk[r0:r0 + CHUNK, GROUP_W:2 * GROUP_W] * (HD ** -0.5)
        ig = ig_all[r0:r0 + CHUNK]
        bcs = jnp.dot(tri_ref[...], lf_all[r0:r0 + CHUNK], precision=HIGHEST,
                      preferred_element_type=F32)
        bcs_t = bcs.T
        ig_t = ig.T
        ig_al = pltpu.roll(ig, GATE_LANE - 4, 1)
        k_t = k.T.astype(BF16)
        qb = q.astype(BF16)
        vb = c_ref[r0:r0 + CHUNK, 2 * GROUP_W:3 * GROUP_W].astype(BF16)
        num_loc = jnp.zeros((CHUNK, GROUP_W), F32)
        m_loc_all = jnp.zeros((CHUNK, 128), F32)
        den_loc_all = jnp.zeros((CHUNK, 128), F32)
        for h in range(HEADS):
            bcol = bcs[:, GATE_LANE + h:GATE_LANE + h + 1]
            dmat = jnp.where(tril, bcol - bcs_t[GATE_LANE + h:GATE_LANE + h + 1, :] + ig_t[4 + h:5 + h, :],
                             -jnp.inf)
            m_loc = jnp.max(dmat, axis=1, keepdims=True)
            s_qk = jnp.dot(jnp.where(hms[h], qb, zero_b), k_t, preferred_element_type=F32)
            sc = s_qk * jnp.exp(dmat - m_loc)
            num_loc = num_loc + jnp.dot(sc.astype(BF16), jnp.where(hms[h], vb, zero_b),
                                        preferred_element_type=F32)
            on_lane = lane_s == GATE_LANE + h
            m_loc_all = jnp.where(on_lane, m_loc, m_loc_all)
            den_loc_all = jnp.where(on_lane, jnp.sum(sc, axis=1, keepdims=True), den_loc_all)
        blast = bcs[CHUNK - 1:CHUNK, :]
        gcol = blast - bcs + ig_al
        gmax = jnp.max(gcol, axis=0, keepdims=True)
        kw = k * expand(jnp.exp(gcol - gmax))
        kwv = bmask_ref[...] * jnp.dot(kw.T.astype(BF16), vb, preferred_element_type=F32)
        local.append((q, num_loc, bcs, m_loc_all, den_loc_all, blast, gmax, kwv,
                      jnp.sum(kw, axis=0, keepdims=True)))

    houts = []
    for c in range(n_chunks):
        q, num_loc, bcs, m_loc_all, den_loc_all, blast, gmax, kwv, ksum = local[c]
        c_old = cst[...]
        n_old = nst[0:1, :]
        m_prev = mst[0:1, :]
        q_c = jnp.dot(q.astype(BF16), c_old.astype(BF16), preferred_element_type=F32)
        q_n = jnp.dot((q * n_old).astype(BF16), segb, preferred_element_type=F32)
        inter = bcs + m_prev
        m_t = jnp.maximum(inter, m_loc_all)
        f = jnp.exp(m_loc_all - m_t)
        w_inter = jnp.exp(inter - m_t)
        den = f * den_loc_all + w_inter * q_n
        inv_den = 1.0 / jnp.maximum(jnp.abs(den), jnp.exp(-m_t))
        m_new = jnp.maximum(blast + m_prev, gmax)
        scal = jnp.concatenate([jnp.exp(gmax - m_new), jnp.exp(blast + m_prev - m_new),
                                jnp.zeros((6, 128), F32)], axis=0)
        ex = expand(jnp.concatenate([f, w_inter, inv_den, scal], axis=0))
        f_x = ex[0:CHUNK]
        w_inter_x = ex[CHUNK:2 * CHUNK]
        inv_den_x = ex[2 * CHUNK:3 * CHUNK]
        sfac_x = ex[3 * CHUNK:3 * CHUNK + 1]
        keep_x = ex[3 * CHUNK + 1:3 * CHUNK + 2]
        mst[0:1, :] = m_new
        cst[...] = keep_x * c_old + sfac_x * kwv
        nst[0:1, :] = keep_x * n_old + sfac_x * ksum
        houts.append((f_x * num_loc + w_inter_x * q_c) * inv_den_x)

    for c in range(n_chunks):
        r0 = c * CHUNK
        hh = houts[c] * _sigmoid(c_ref[r0:r0 + CHUNK, 3 * GROUP_W:4 * GROUP_W])
        sq = hh * hh
        sq_hi = sq.astype(BF16)
        sq_lo = (sq - sq_hi.astype(F32)).astype(BF16)
        ms = (jnp.dot(sq_hi, segm_b, preferred_element_type=F32)
              + jnp.dot(sq_lo, segm_b, preferred_element_type=F32))
        o_ref[r0:r0 + CHUNK, :] = hh * lax.rsqrt(ms + EPS) * ng_ref[...]


def _mlstm(main, cw, cb, igb, fgb, ng, tri, seg, segm, bmask, bsz, seq, ls=512):
    t = main.shape[0]
    ns = seq // ls

    def full(shape):
        return pl.BlockSpec(shape, lambda b, s: tuple(0 for _ in shape))

    return pl.pallas_call(
        functools.partial(_mlstm_body, ls=ls),
        grid=(bsz, ns),
        in_specs=[
            pl.BlockSpec((ls, 1024), lambda b, s: (b * ns + s, COL_ML // 1024)),
            pl.BlockSpec((ls, 128), lambda b, s: (b * ns + s, COL_SMALL // 128)),
            full((CONV_K, 2 * GROUP_W)), full((1, 2 * GROUP_W)), full((1, 128)), full((1, 128)),
            full((1, GROUP_W)), full((CHUNK, CHUNK)), full((GROUP_W, 128)), full((GROUP_W, GROUP_W)),
            full((GROUP_W, GROUP_W)),
        ],
        out_specs=pl.BlockSpec((ls, GROUP_W), lambda b, s: (b * ns + s, 0)),
        out_shape=jax.ShapeDtypeStruct((t, GROUP_W), F32),
        scratch_shapes=[
            pltpu.VMEM((ls + 8, 2 * GROUP_W), F32),
            pltpu.VMEM((GROUP_W, GROUP_W), F32),
            pltpu.VMEM((8, GROUP_W), F32),
            pltpu.VMEM((8, 128), F32),
        ],
        compiler_params=_cparams(("arbitrary", "arbitrary")),
        name="mlstm",
    )(main, main, cw, cb, igb, fgb, ng, tri, seg, segm, bmask)


def _outproj_body(x_ref, ya_ref, yb_ref, yc_ref, yd_ref, wo_ref, g2_ref, wr_ref, br_ref,
                  xn_ref, h2_ref, cb_ref):
    sub = x_ref.shape[0] // 2
    accs = []
    for s in range(2):
        rows = slice(s * sub, (s + 1) * sub)
        acc = x_ref[rows, :]
        for gi, y_ref in enumerate((ya_ref, yb_ref, yc_ref, yd_ref)):
            acc = acc + jnp.dot(y_ref[rows, :].astype(BF16), wo_ref[gi * GROUP_W:(gi + 1) * GROUP_W, :],
                                preferred_element_type=F32)
        accs.append(acc)
    for s in range(2):
        _route(accs[s], slice(s * sub, (s + 1) * sub), g2_ref, wr_ref, br_ref, xn_ref, h2_ref, cb_ref)


def _route(acc, rows, g2_ref, wr_ref, br_ref, xn_ref, h2_ref, cb_ref):
    xn_ref[rows, :] = acc
    h2 = _rms(acc, g2_ref[...])
    h_hi = h2.astype(BF16)
    h2_ref[rows, :] = h_hi
    h_lo = (h2 - h_hi.astype(F32)).astype(BF16)
    hh = jnp.dot(h_hi, wr_ref[...], preferred_element_type=F32)
    lh = jnp.dot(h_lo, wr_ref[...], preferred_element_type=F32)
    logits = hh[:, :128] + hh[:, 128:] + lh[:, :128] + br_ref[...]
    lane = lax.broadcasted_iota(jnp.int32, logits.shape, 1)
    big = jnp.int32(1 << 20)
    n_grp = N_EXPERTS // EXPERTS_PER_GROUP
    gl = jnp.where(jnp.logical_and(lane >= N_EXPERTS, lane < N_EXPERTS + n_grp), logits, -jnp.inf)
    gmax = jnp.max(gl, axis=1, keepdims=True)
    g_w = 1.0 / jnp.sum(jnp.exp(gl - gmax), axis=1, keepdims=True)
    gidx = jnp.min(jnp.where(gl == gmax, lane, big), axis=1, keepdims=True) - N_EXPERTS
    in_grp = jnp.logical_and(lane < N_EXPERTS,
                             lax.shift_right_logical(lane, int(math.log2(EXPERTS_PER_GROUP))) == gidx)
    el = jnp.where(in_grp, logits, -jnp.inf)
    emax = jnp.max(el, axis=1, keepdims=True)
    esum = jnp.sum(jnp.exp(el - emax), axis=1, keepdims=True)
    i1 = jnp.min(jnp.where(el == emax, lane, big), axis=1, keepdims=True)
    el2 = jnp.where(lane == i1, -jnp.inf, el)
    emax2 = jnp.max(el2, axis=1, keepdims=True)
    i2 = jnp.min(jnp.where(el2 == emax2, lane, big), axis=1, keepdims=True)
    p1 = 1.0 / esum
    p2 = jnp.exp(emax2 - emax) / esum
    w1 = p1 / (p1 + p2)
    w2 = p2 / (p1 + p2)
    comb = g_w * (jnp.where(lane == i1, w1, 0.0) + jnp.where(lane == i2, w2, 0.0))
    cb_ref[rows, :] = jnp.where(lane == GID_LANE, gidx.astype(F32), comb)


def _outproj(x2, ya, yb, yc, yd, wo, g2, wr, br, tm=512):
    t = x2.shape[0]
    row = pl.BlockSpec((tm, D_MODEL), lambda i: (i, 0))
    grp = pl.BlockSpec((tm, GROUP_W), lambda i: (i, 0))

    def full(shape):
        return pl.BlockSpec(shape, lambda i: tuple(0 for _ in shape))

    return pl.pallas_call(
        _outproj_body,
        grid=(t // tm,),
        in_specs=[row, grp, grp, grp, grp, full((D_MODEL, D_MODEL)), full((1, D_MODEL)),
                  full((D_MODEL, 256)), full((1, 128))],
        out_specs=[row, row, pl.BlockSpec((tm, 128), lambda i: (i, 0))],
        out_shape=[
            jax.ShapeDtypeStruct((t, D_MODEL), F32),
            jax.ShapeDtypeStruct((t, D_MODEL), BF16),
            jax.ShapeDtypeStruct((t, 128), F32),
        ],
        compiler_params=_cparams(("arbitrary",)),
        name="outproj_router",
    )(x2, ya, yb, yc, yd, wo, g2, wr, br)


MOE_EB = EXPERTS_PER_GROUP
MOE_RB = 128


def _moe_body(h_ref, cb_ref, xn_ref, tri_ref, wg_ref, wu_ref, wd_ref, gf_ref, o_ref,
              perm_t, hs, cs, ys, seg_smem, *, final_norm, rb):
    g = pl.program_id(1)
    tm = h_ref.shape[0]
    n_grp = N_EXPERTS // EXPERTS_PER_GROUP
    lane = lax.broadcasted_iota(jnp.int32, (tm, 128), 1)

    @pl.when(g == 0)
    def _():
        cb = cb_ref[...]
        gid = jnp.sum(jnp.where(lane == GID_LANE, cb, 0.0), axis=1, keepdims=True).astype(jnp.int32)
        onehot = jnp.where(lane == gid, 1.0, 0.0)
        oh16 = onehot.astype(BF16)
        tot = jnp.zeros((1, 128), F32)
        parts = []
        for r0 in range(0, tm, CHUNK):
            part = jnp.dot(tri_ref[...], oh16[r0:r0 + CHUNK], preferred_element_type=F32) + tot
            parts.append(part)
            tot = part[CHUNK - 1:CHUNK, :]
        cum = jnp.concatenate(parts, axis=0)
        lane1 = lane[0:1, :]
        base = jnp.zeros((1, 128), F32)
        run = jnp.zeros((1, 1), F32)
        for k in range(n_grp):
            base = base + jnp.where(lane1 == k, run, 0.0)
            seg_smem[k] = run[0, 0].astype(jnp.int32)
            seg_smem[n_grp + k] = tot[0, k].astype(jnp.int32)
            run = run + tot[:, k:k + 1]
        pos = jnp.sum(onehot * (base + cum - 1.0), axis=1, keepdims=True).astype(jnp.int32)
        pos_row = jnp.broadcast_to(pos.astype(F32), (tm, 128)).T[0:1, :].astype(jnp.int32)
        perm_t[...] = jnp.where(lax.broadcasted_iota(jnp.int32, (tm, tm), 1) == pos, 1.0, 0.0).astype(BF16)
        p = jnp.where(lax.broadcasted_iota(jnp.int32, (tm, tm), 0) == pos_row, 1.0, 0.0).astype(BF16)
        hs[...] = jnp.dot(p, h_ref[...], preferred_element_type=F32).astype(BF16)
        cb_hi = cb.astype(BF16)
        cb_lo = (cb - cb_hi.astype(F32)).astype(BF16)
        cs2 = jnp.dot(p, jnp.concatenate([cb_hi, cb_lo], axis=1), preferred_element_type=F32)
        cs[...] = cs2[:, :128] + cs2[:, 128:]
        ys[...] = jnp.zeros(ys.shape, F32)

    seg_lo = seg_smem[g]
    seg_hi = seg_lo + seg_smem[n_grp + g]
    lane_rb = lax.broadcasted_iota(jnp.int32, (rb, 128), 1)

    def run_block(lo):
        x = hs[lo:lo + rb, :]
        cw = cs[lo:lo + rb, :]

        def gate_up(j):
            return (jnp.dot(x, wg_ref[j], preferred_element_type=F32),
                    jnp.dot(x, wu_ref[j], preferred_element_type=F32))

        def down(j, gu):
            col = jnp.sum(jnp.where(lane_rb == g * MOE_EB + j, cw, 0.0), axis=1,
                          keepdims=True)
            gt, up = gu
            hid = (gt * _sigmoid(gt)) * up * col
            return jnp.dot(hid.astype(BF16), wd_ref[j], preferred_element_type=F32)

        total = jnp.zeros((rb, D_MODEL), F32)
        gu = gate_up(0)
        for j in range(MOE_EB):
            gu_next = gate_up(j + 1) if j + 1 < MOE_EB else None
            total = total + down(j, gu)
            gu = gu_next
        ys[lo:lo + rb, :] += total

    for b in range(tm // rb):
        lo = b * rb
        pl.when(jnp.logical_and(seg_lo < lo + rb, seg_hi > lo))(functools.partial(run_block, lo))

    @pl.when(g == n_grp - 1)
    def _():
        y = xn_ref[...] + jnp.dot(perm_t[...], ys[...].astype(BF16), preferred_element_type=F32)
        if final_norm:
            y = _rms(y, gf_ref[...])
        o_ref[...] = y


def _moe(h2, comb, xn, tri, wg, wu, wd, gf, final_norm, tm=1024, rb=MOE_RB):
    t = h2.shape[0]
    return pl.pallas_call(
        functools.partial(_moe_body, final_norm=final_norm, rb=rb),
        grid=(t // tm, N_EXPERTS // MOE_EB),
        in_specs=[
            pl.BlockSpec((tm, D_MODEL), lambda i, e: (i, 0)),
            pl.BlockSpec((tm, 128), lambda i, e: (i, 0)),
            pl.BlockSpec((tm, D_MODEL), lambda i, e: (i, 0)),
            pl.BlockSpec((CHUNK, CHUNK), lambda i, e: (0, 0)),
            pl.BlockSpec((MOE_EB, D_MODEL, D_EXPERT), lambda i, e: (e, 0, 0)),
            pl.BlockSpec((MOE_EB, D_MODEL, D_EXPERT), lambda i, e: (e, 0, 0)),
            pl.BlockSpec((MOE_EB, D_EXPERT, D_MODEL), lambda i, e: (e, 0, 0)),
            pl.BlockSpec((1, D_MODEL), lambda i, e: (0, 0)),
        ],
        out_specs=pl.BlockSpec((tm, D_MODEL), lambda i, e: (i, 0)),
        out_shape=jax.ShapeDtypeStruct((t, D_MODEL), F32),
        scratch_shapes=[
            pltpu.VMEM((tm, tm), BF16),
            pltpu.VMEM((tm, D_MODEL), BF16),
            pltpu.VMEM((tm, 128), F32),
            pltpu.VMEM((tm, D_MODEL), F32),
            pltpu.SMEM((2 * (N_EXPERTS // MOE_EB),), jnp.int32),
        ],
        compiler_params=_cparams(("arbitrary", "arbitrary")),
        name="moe",
    )(h2, comb, xn, tri, wg, wu, wd, gf)


def _spread_heads(w, width, slab, total):
    rows = w.shape[0]
    pieces = []
    for h in range(HEADS):
        pieces.append(w[:, h * width:(h + 1) * width])
        pieces.append(jnp.zeros((rows, slab - width), w.dtype))
    if total > HEADS * slab:
        pieces.append(jnp.zeros((rows, total - HEADS * slab), w.dtype))
    return jnp.concatenate(pieces, axis=1)


def _pad_heads(w, width):
    return _spread_heads(w, width, HP, HEADS * HP)


def _v_cols(w):
    return _spread_heads(w, HD, VR, V_COLS)


def _lane_row(vals, offset, width=128):
    return jnp.zeros((1, width), F32).at[0, offset:offset + vals.shape[0]].set(vals.astype(F32))


def _layer_weights(l, p):
    half = MLA_ROPE // 2
    w = {"g_mix": p["norm_mix_g"][l][None, :]}
    w["ssm_cw"] = p["ssm_conv_w"][l]
    w["ssm_cb"] = p["ssm_conv_b"][l][None, :]
    w["ssm_dtb"] = _lane_row(p["ssm_dt_bias"][l], 0)
    w["ssm_arow"] = _lane_row(-jnp.exp(p["ssm_a_log"][l].astype(F32)), 0)
    w["ssm_dexp"] = jnp.repeat(p["ssm_d"][l].astype(F32), HD)[None, :]
    w["ssm_ng"] = p["ssm_norm_g"][l][None, :]
    w["ml_cw"] = p["ml_conv_w"][l]
    w["ml_cb"] = p["ml_conv_b"][l][None, :]
    w["ml_igb"] = _lane_row(p["ml_ig_bias"][l], 4)
    w["ml_fgb"] = _lane_row(p["ml_fg_bias"][l], 8)
    w["ml_ng"] = p["ml_norm_g"][l][None, :]
    wuq = p["mla_w_uq"][l]
    qd = MLA_NOPE + MLA_ROPE
    swapped = []
    for h in range(HEADS):
        c0 = h * qd + MLA_NOPE
        swapped += [jnp.zeros((256, MLA_NOPE), F32), wuq[:, c0 + half:c0 + 2 * half], wuq[:, c0:c0 + half]]
    wq = _spread_heads(wuq, qd, HP, HEADS * HP)
    wqs = _spread_heads(jnp.concatenate(swapped, axis=1), qd, HP, HEADS * HP)
    w["mla_wq2"] = jnp.concatenate([wq, wqs], axis=1).astype(BF16)
    wukv = p["mla_w_ukv"][l]
    wk_cols = jnp.concatenate([wukv[:, h * 2 * HD:h * 2 * HD + MLA_NOPE] for h in range(HEADS)], axis=1)
    wv_cols = jnp.concatenate([wukv[:, h * 2 * HD + MLA_NOPE:(h + 1) * 2 * HD] for h in range(HEADS)], axis=1)
    w["mla_wk"] = _pad_heads(wk_cols, MLA_NOPE).astype(BF16)
    w["mla_wv"] = _v_cols(wv_cols).astype(BF16)
    w["mla_gq"] = p["mla_q_norm_g"][l][None, :]
    w["mla_gkv"] = p["mla_kv_norm_g"][l][None, :]
    w["w_out"] = p["w_out"][l].astype(BF16)
    w["g_ffn"] = p["norm_ffn_g"][l][None, :]
    wr = jnp.zeros((D_MODEL, 128), F32)
    wr = wr.at[:, :N_EXPERTS].set(p["moe_w_router"][l]).at[:, N_EXPERTS:N_EXPERTS + 4].set(p["moe_w_group"][l])
    wr_hi = wr.astype(BF16)
    w["w_router"] = jnp.concatenate([wr_hi, (wr - wr_hi.astype(F32)).astype(BF16)], axis=1)
    w["b_router"] = _lane_row(p["moe_b_router"][l], 0) + _lane_row(p["moe_b_group"][l], N_EXPERTS)
    w["w_gate"] = p["moe_w_gate"][l].astype(BF16)
    w["w_up"] = p["moe_w_up"][l].astype(BF16)
    w["w_down"] = p["moe_w_down"][l].astype(BF16)
    return w


def kernel(x, positions, norm_mix_g, w_in, ssm_conv_w, ssm_conv_b, ssm_dt_bias, ssm_a_log, ssm_d, ssm_norm_g, ml_conv_w, ml_conv_b, ml_ig_bias, ml_fg_bias, ml_norm_g, mla_q_norm_g, mla_w_uq, mla_kv_norm_g, mla_w_ukv, w_out, norm_ffn_g, moe_w_group, moe_b_group, moe_w_router, moe_b_router, moe_w_gate, moe_w_up, moe_w_down, final_norm_g):
    p = dict(norm_mix_g=norm_mix_g, w_all=_relayout_w_in(w_in), ssm_conv_w=ssm_conv_w, ssm_conv_b=ssm_conv_b,
             ssm_dt_bias=ssm_dt_bias, ssm_a_log=ssm_a_log, ssm_d=ssm_d, ssm_norm_g=ssm_norm_g,
             ml_conv_w=ml_conv_w, ml_conv_b=ml_conv_b, ml_ig_bias=ml_ig_bias, ml_fg_bias=ml_fg_bias,
             ml_norm_g=ml_norm_g, mla_q_norm_g=mla_q_norm_g, mla_w_uq=mla_w_uq,
             mla_kv_norm_g=mla_kv_norm_g, mla_w_ukv=mla_w_ukv, w_out=w_out, norm_ffn_g=norm_ffn_g,
             moe_w_group=moe_w_group, moe_b_group=moe_b_group, moe_w_router=moe_w_router,
             moe_b_router=moe_b_router, moe_w_gate=moe_w_gate, moe_w_up=moe_w_up, moe_w_down=moe_w_down)
    bsz, seq, _ = x.shape
    depth = w_in.shape[0]
    x2 = x.reshape(bsz * seq, D_MODEL)

    half = MLA_ROPE // 2
    inv = ROPE_THETA ** (-jnp.arange(0, MLA_ROPE, 2, dtype=F32) / MLA_ROPE)
    inv_col = jnp.concatenate([inv, inv])[:, None]
    sgn_col = jnp.concatenate([-jnp.ones((half,), F32), jnp.ones((half,), F32)])[:, None]
    pe_mask = _lane_row(jnp.ones((MLA_ROPE,), F32), MLA_NOPE)
    c_tok, s_tok, c_tr, s_tr = _rope_tables(positions.reshape(bsz, 1, seq), inv_col, sgn_col, bsz, seq)
    tri = jnp.tril(jnp.ones((CHUNK, CHUNK), F32))
    tri_moe = tri.astype(BF16)
    head_of = jnp.arange(GROUP_W) // HD
    seg = (head_of[:, None] + GATE_LANE == jnp.arange(128)[None, :]).astype(F32)
    ssd_hexp = (jnp.arange(128)[:, None] == head_of[None, :]).astype(F32)
    bmask = (head_of[:, None] == head_of[None, :]).astype(F32)
    segm = bmask / HD

    for l in range(depth):
        w = _layer_weights(l, p)
        main, k_a, qT_a, vT_a, qT_d, k_d, vT_d = _in_proj(
            x2, w["g_mix"], p["w_all"], l, c_tok, s_tok, c_tr, s_tr, w["mla_gq"], w["mla_wq2"], w["mla_gkv"],
            w["mla_wk"], w["mla_wv"], pe_mask, bsz, seq)
        y_a = _attention(qT_a, k_a, vT_a, bsz, seq, "attn_moba")
        y_b = _ssd(main, w["ssm_cw"], w["ssm_cb"], w["ssm_dtb"], w["ssm_arow"], w["ssm_dexp"], w["ssm_ng"],
                   tri, ssd_hexp, bsz, seq)
        y_c = _mlstm(main, w["ml_cw"], w["ml_cb"], w["ml_igb"], w["ml_fgb"], w["ml_ng"], tri, seg, segm,
                     bmask, bsz, seq)
        y_d = _attention(qT_d, k_d, vT_d, bsz, seq, "attn_mla")
        xn, h2, comb = _outproj(x2, y_a, y_b, y_c, y_d, w["w_out"], w["g_ffn"], w["w_router"], w["b_router"])
        x2 = _moe(h2, comb, xn, tri_moe, w["w_gate"], w["w_up"], w["w_down"], final_norm_g[None, :],
                  final_norm=(l == depth - 1), tm=MOE_TM)
    return x2.reshape(bsz, seq, D_MODEL)
```

```python
import functools
import math

import numpy as np
import jax
import jax.numpy as jnp
from jax import lax
from jax.experimental import pallas as pl
from jax.experimental.pallas import tpu as pltpu

F32 = jnp.float32
BF16 = jnp.bfloat16
HIGHEST = lax.Precision.HIGHEST

EPS = 1e-6
NEG = -1e30
LOG2E = 1.4426950408889634

D_MODEL = 1024
GROUP_W = 256
HEADS = 4
HD = 64
HP = 128
MOBA_BLOCK = 256
MOBA_TOPK = 3
SSM_STATE = 128
SSM_XBC = 768
CONV_K = 4
CHUNK = 128
MLA_NOPE = 64
MLA_ROPE = 32
ROPE_THETA = 10000.0
N_EXPERTS = 16
EXPERTS_PER_GROUP = 4
D_EXPERT = 256
GATE_LANE = 8
GID_LANE = 20
MOE_TM = 1024

COL_ZX = 0
COL_ML = 1024
COL_SMALL = 2048
N_MAIN = 2176
VR = 80
V_COLS = 384
N_ALL = N_MAIN + 256 + 128 + 3 * GROUP_W
ATT_UQ = 256
ATT_UK = 256

V7X_VMEM_LIMIT = 56 * 1024 * 1024

NT_DIMS = (((1,), (1,)), ((), ()))


def _cparams(sem):
    return pltpu.CompilerParams(dimension_semantics=sem, vmem_limit_bytes=V7X_VMEM_LIMIT)


def _rms(x, g):
    ms = jnp.mean(x * x, axis=-1, keepdims=True)
    return x * lax.rsqrt(ms + EPS) * g


def _sigmoid(x):
    return 0.5 * jnp.tanh(0.5 * x) + 0.5


def _softplus(x):
    return jnp.maximum(x, 0.0) + jnp.log1p(jnp.exp(-jnp.abs(x)))


def _tril(n):
    r = lax.broadcasted_iota(jnp.int32, (n, n), 0)
    c = lax.broadcasted_iota(jnp.int32, (n, n), 1)
    return c <= r


W_IN_SIZES = (3 * GROUP_W, GROUP_W, SSM_XBC, HEADS, 2 * GROUP_W, GROUP_W, HEADS, HEADS, GROUP_W, 256, 128,
              MLA_ROPE)


def _relayout_body(w_ref, o_ref):
    w = w_ref[0]
    rows = w.shape[0]
    offs = np.cumsum((0,) + W_IN_SIZES)
    a_qkv, s_z, s_xbc, s_dt, m_qk, m_v, m_i, m_f, m_o, d_cq, d_ckv, d_kr = [
        w[:, int(offs[i]):int(offs[i + 1])] for i in range(len(W_IN_SIZES))]
    half = MLA_ROPE // 2

    def zeros(n):
        return jnp.zeros((rows, n), F32)

    pieces = [s_z, s_xbc, m_qk, m_v, m_o,
              s_dt, m_i, m_f, zeros(64 - 3 * HEADS), d_kr, d_kr[:, half:], d_kr[:, :half],
              d_cq, d_ckv,
              a_qkv[:, GROUP_W:2 * GROUP_W], a_qkv[:, :GROUP_W], a_qkv[:, 2 * GROUP_W:]]
    o_ref[0] = jnp.concatenate(pieces, axis=1).astype(BF16)


def _relayout_w_in(w, rows=256):
    nl, r, c = w.shape
    return pl.pallas_call(
        _relayout_body,
        grid=(nl, r // rows),
        in_specs=[pl.BlockSpec((1, rows, c), lambda l, i: (l, i, 0))],
        out_specs=pl.BlockSpec((1, rows, N_ALL), lambda l, i: (l, i, 0)),
        out_shape=jax.ShapeDtypeStruct((nl, r, N_ALL), BF16),
        compiler_params=_cparams(("arbitrary", "arbitrary")),
        name="relayout_w_in",
    )(w)


def _ones_rows(shape):
    row = lax.broadcasted_iota(jnp.int32, shape, 0)
    hit = row == HD
    for h in range(1, HEADS):
        hit = jnp.logical_or(hit, row == h * VR + HD)
    return jnp.where(hit, 1.0, 0.0)


def _in_proj_body(x_ref, g_ref, w_ref, c_ref, s_ref, ct_ref, st_ref, gq_ref, wq_ref, gkv_ref, wk_ref,
                  wv_ref, msk_ref, om_ref, ka_ref, qa_ref, va_ref, qd_ref, kd_ref, vd_ref, km_sc,
                  *, nb, nst, qscale_a, qscale_d):
    s_idx = pl.program_id(0) % nst
    tm = x_ref.shape[0]

    @pl.when(s_idx == 0)
    def _():
        km_sc[...] = jnp.zeros(km_sc.shape, F32)

    h = _rms(x_ref[...], g_ref[...]).astype(BF16)
    rest = jnp.dot(h, w_ref[0, :, COL_SMALL:], preferred_element_type=F32)
    for c in range(0, COL_SMALL, 512):
        om_ref[:, c:c + 512] = jnp.dot(h, w_ref[0, :, c:c + 512], preferred_element_type=F32)
    sm = rest[:, 0:128]
    om_ref[:, COL_SMALL:COL_SMALL + 128] = sm
    cq = rest[:, 128:384]
    ckv = rest[:, 384:512]
    kp = _spread_heads(rest[:, 512:768], HD, HP, HEADS * HP)
    qp = _spread_heads(rest[:, 768:1024], HD, HP, HEADS * HP)
    vc = _spread_heads(rest[:, 1024:1280], HD, VR, V_COLS)

    nblk = tm // MOBA_BLOCK
    lane_k = lax.broadcasted_iota(jnp.int32, (MOBA_BLOCK, HEADS * HP), 1)
    for j in range(nblk):
        kj = kp[j * MOBA_BLOCK:(j + 1) * MOBA_BLOCK]
        blk_id = s_idx * nblk + j
        km_sc[pl.ds(blk_id, 1), :] = jnp.mean(kj, axis=0, keepdims=True)
        ka_ref[j * MOBA_BLOCK:(j + 1) * MOBA_BLOCK, :] = jnp.where(
            (lane_k & (HP - 1)) == HD + blk_id, 1.0, kj).astype(BF16)

    qT = qp.T
    km = km_sc[...]
    pos = s_idx * tm + lax.broadcasted_iota(jnp.int32, (nb, tm), 1)
    cur = lax.shift_right_logical(pos, int(math.log2(MOBA_BLOCK)))
    blk = lax.broadcasted_iota(jnp.int32, (nb, tm), 0)
    past = blk < cur
    for hd in range(HEADS):
        qh = qT[hd * HP:(hd + 1) * HP, :]
        gate = jnp.dot(km[:, hd * HP:(hd + 1) * HP], qh, precision=HIGHEST,
                       preferred_element_type=F32)
        gate = jnp.where(past, gate, -jnp.inf)
        bias = jnp.where(blk == cur, 0.0, NEG)
        for _ in range(MOBA_TOPK):
            m = jnp.max(gate, axis=0, keepdims=True)
            cand = jnp.logical_and(gate == m, gate > -jnp.inf)
            idx = jnp.min(jnp.where(cand, blk, nb), axis=0, keepdims=True)
            pick = blk == idx
            bias = jnp.where(pick, 0.0, bias)
            gate = jnp.where(pick, -jnp.inf, gate)
        rows = [qh[:HD] * qscale_a, bias]
        if HP - HD - nb:
            rows.append(jnp.zeros((HP - HD - nb, tm), F32))
        qa_ref[0, hd * HP:(hd + 1) * HP, :] = jnp.concatenate(rows, axis=0).astype(BF16)
    vt = vc.T[:HEADS * VR]
    va_ref[0] = (vt + _ones_rows(vt.shape)).astype(BF16)

    qn = _rms(cq, gq_ref[...]).astype(BF16)
    q2 = jnp.dot(qn, wq_ref[...], preferred_element_type=F32).T
    nq = HEADS * HP
    ct4 = jnp.concatenate([ct_ref[0]] * HEADS, axis=0)
    st4 = jnp.concatenate([st_ref[0]] * HEADS, axis=0)
    qd_ref[0] = ((q2[:nq] * ct4 + q2[nq:] * st4) * qscale_d).astype(BF16)
    kvn = _rms(ckv, gkv_ref[...]).astype(BF16)
    kn = jnp.dot(kvn, wk_ref[...], preferred_element_type=F32)
    pe = (sm * c_ref[...] + pltpu.roll(sm, HP - MLA_ROPE, 1) * s_ref[...]) * msk_ref[...]
    kd_ref[...] = (kn + jnp.concatenate([pe] * HEADS, axis=1)).astype(BF16)
    vtd = jnp.dot(kvn, wv_ref[...], preferred_element_type=F32).T[:HEADS * VR]
    vd_ref[0] = (vtd + _ones_rows(vtd.shape)).astype(BF16)


def _in_proj(x2, g, w_all, layer, c_tok, s_tok, c_tr, s_tr, gq, wq2, gkv, wk, wv, pe_mask, bsz, seq, tm=512):
    t = x2.shape[0]
    nst = seq // tm
    nb = seq // MOBA_BLOCK
    assert nb <= HP - HD
    tok128 = pl.BlockSpec((tm, HP), lambda i: (i, 0))
    tr128 = pl.BlockSpec((1, HP, tm), lambda i: (i // nst, 0, i % nst))
    tr_q = pl.BlockSpec((1, HEADS * HP, tm), lambda i: (i // nst, 0, i % nst))
    tr_v = pl.BlockSpec((1, HEADS * VR, tm), lambda i: (i // nst, 0, i % nst))
    tok_k = pl.BlockSpec((tm, HEADS * HP), lambda i: (i, 0))

    def full(shape):
        return pl.BlockSpec(shape, lambda i: tuple(0 for _ in shape))

    q_shape = jax.ShapeDtypeStruct((bsz, HEADS * HP, seq), BF16)
    k_shape = jax.ShapeDtypeStruct((t, HEADS * HP), BF16)
    v_shape = jax.ShapeDtypeStruct((bsz, HEADS * VR, seq), BF16)
    return pl.pallas_call(
        functools.partial(_in_proj_body, nb=nb, nst=nst, qscale_a=(HD ** -0.5) * LOG2E,
                          qscale_d=((MLA_NOPE + MLA_ROPE) ** -0.5) * LOG2E),
        grid=(t // tm,),
        in_specs=[
            pl.BlockSpec((tm, D_MODEL), lambda i: (i, 0)),
            full((1, D_MODEL)), pl.BlockSpec((1, D_MODEL, N_ALL), lambda i: (layer, 0, 0)),
            tok128, tok128, tr128, tr128,
            full((1, 256)), full((256, 2 * HEADS * HP)), full((1, 128)),
            full((128, HEADS * HP)), full((128, V_COLS)), full((1, HP)),
        ],
        out_specs=[pl.BlockSpec((tm, N_MAIN), lambda i: (i, 0)), tok_k, tr_q, tr_v, tr_q, tok_k, tr_v],
        out_shape=[jax.ShapeDtypeStruct((t, N_MAIN), F32), k_shape, q_shape, v_shape,
                   q_shape, k_shape, v_shape],
        scratch_shapes=[pltpu.VMEM((nb, HEADS * HP), F32)],
        compiler_params=_cparams(("arbitrary",)),
        name="in_proj",
    )(x2, g, w_all, c_tok, s_tok, c_tr, s_tr, gq, wq2, gkv, wk, wv, pe_mask)


def _rope_body(pos_ref, inv_ref, sgn_ref, c_ref, s_ref, ct_ref, st_ref):
    ts = pos_ref.shape[2]
    ang = inv_ref[...] * pos_ref[0].astype(F32)
    c32 = jnp.cos(ang)
    s32 = jnp.sin(ang) * sgn_ref[...]
    ct = jnp.concatenate([jnp.ones((MLA_NOPE, ts), F32), c32,
                          jnp.ones((HP - MLA_NOPE - MLA_ROPE, ts), F32)], axis=0)
    st = jnp.concatenate([jnp.zeros((MLA_NOPE, ts), F32), s32,
                          jnp.zeros((HP - MLA_NOPE - MLA_ROPE, ts), F32)], axis=0)
    ct_ref[0] = ct
    st_ref[0] = st
    c_ref[...] = ct.T
    s_ref[...] = st.T


def _rope_tables(pos3, inv_col, sgn_col, bsz, seq, ts=512):
    t = bsz * seq
    nst = seq // ts
    tok = pl.BlockSpec((ts, HP), lambda i: (i, 0))
    tr = pl.BlockSpec((1, HP, ts), lambda i: (i // nst, 0, i % nst))
    col = pl.BlockSpec((MLA_ROPE, 1), lambda i: (0, 0))
    return pl.pallas_call(
        _rope_body,
        grid=(t // ts,),
        in_specs=[pl.BlockSpec((1, 1, ts), lambda i: (i // nst, 0, i % nst)), col, col],
        out_specs=[tok, tok, tr, tr],
        out_shape=[
            jax.ShapeDtypeStruct((t, HP), F32),
            jax.ShapeDtypeStruct((t, HP), F32),
            jax.ShapeDtypeStruct((bsz, HP, seq), F32),
            jax.ShapeDtypeStruct((bsz, HP, seq), F32),
        ],
        compiler_params=_cparams(("arbitrary",)),
        name="rope_tables",
    )(pos3, inv_col, sgn_col)


def _attn_body(ii_ref, jj_ref, qT_ref, k_ref, vT_ref, o_ref, m_sc, acc_sc, *, tq, tk, ahead):
    p = pl.program_id(1)
    i = ii_ref[p]
    j = jj_ref[p]

    @pl.when(j == 0)
    def _():
        m_sc[...] = jnp.full(m_sc.shape, NEG, F32)
        acc_sc[...] = jnp.zeros(acc_sc.shape, F32)

    def scores(u):
        h, qc, kb, _ = u
        return jnp.dot(k_ref[kb * ATT_UK:(kb + 1) * ATT_UK, h * HP:(h + 1) * HP],
                       qT_ref[0, h * HP:(h + 1) * HP, qc * ATT_UQ:(qc + 1) * ATT_UQ],
                       preferred_element_type=F32)

    def update(u, s):
        h, qc, kb, masked = u
        if masked:
            kpos = kb * ATT_UK + lax.broadcasted_iota(jnp.int32, s.shape, 0)
            qpos = qc * ATT_UQ + lax.broadcasted_iota(jnp.int32, s.shape, 1)
            s = jnp.where(kpos <= qpos, s, NEG)
        alphas, pms = [], []
        for c in range(ATT_UQ // 128):
            ql = slice(qc * ATT_UQ + c * 128, qc * ATT_UQ + (c + 1) * 128)
            sc = s[:, c * 128:(c + 1) * 128]
            m_old = m_sc[h:h + 1, ql]
            m_new = jnp.maximum(m_old, jnp.max(sc, axis=0, keepdims=True))
            alphas.append(jnp.exp2(m_old - m_new))
            pms.append(jnp.exp2(sc - m_new).astype(BF16))
            m_sc[h:h + 1, ql] = m_new
        ql = slice(qc * ATT_UQ, (qc + 1) * ATT_UQ)
        rows = slice(h * VR, (h + 1) * VR)
        acc_sc[rows, ql] = jnp.concatenate(alphas, axis=1) * acc_sc[rows, ql] + jnp.dot(
            vT_ref[0, rows, kb * ATT_UK:(kb + 1) * ATT_UK], jnp.concatenate(pms, axis=1),
            preferred_element_type=F32)

    def run(units):
        pending = [scores(u) for u in units[:ahead]]
        for n, u in enumerate(units):
            s = pending.pop(0)
            if n + ahead < len(units):
                pending.append(scores(units[n + ahead]))
            update(u, s)

    def unit_list(diag):
        units = []
        for kb in range(tk // ATT_UK):
            for h in range(HEADS):
                for qc in range(tq // ATT_UQ):
                    k_lo, k_hi = kb * ATT_UK, (kb + 1) * ATT_UK - 1
                    q_lo, q_hi = qc * ATT_UQ, (qc + 1) * ATT_UQ - 1
                    if diag and k_lo > q_hi:
                        continue
                    units.append((h, qc, kb, diag and k_hi > q_lo))
        return units

    @pl.when(j < i)
    def _():
        run(unit_list(False))

    @pl.when(j == i)
    def _():
        run(unit_list(True))
        parts = []
        for h in range(HEADS):
            den = acc_sc[h * VR + HD:h * VR + HD + 1, :]
            parts.append(acc_sc[h * VR:h * VR + HD, :] * (1.0 / den))
        o_ref[...] = jnp.concatenate(parts, axis=0).T


def _attention(qT, k, vT, bsz, seq, name, tq=1024, ahead=5):
    t = k.shape[0]
    nq = seq // tq
    ii = np.concatenate([np.full(i + 1, i, np.int32) for i in range(nq)])
    jj = np.concatenate([np.arange(i + 1, dtype=np.int32) for i in range(nq)])
    grid_spec = pltpu.PrefetchScalarGridSpec(
        num_scalar_prefetch=2,
        grid=(bsz, len(ii)),
        in_specs=[
            pl.BlockSpec((1, HEADS * HP, tq), lambda b, p, ii, jj: (b, 0, ii[p])),
            pl.BlockSpec((tq, HEADS * HP), lambda b, p, ii, jj: (b * nq + jj[p], 0)),
            pl.BlockSpec((1, HEADS * VR, tq), lambda b, p, ii, jj: (b, 0, jj[p])),
        ],
        out_specs=pl.BlockSpec((tq, GROUP_W), lambda b, p, ii, jj: (b * nq + ii[p], 0)),
        scratch_shapes=[
            pltpu.VMEM((8, tq), F32),
            pltpu.VMEM((HEADS * VR, tq), F32),
        ],
    )
    return pl.pallas_call(
        functools.partial(_attn_body, tq=tq, tk=tq, ahead=ahead),
        grid_spec=grid_spec,
        out_shape=jax.ShapeDtypeStruct((t, GROUP_W), F32),
        compiler_params=_cparams(("arbitrary", "arbitrary")),
        name=name,
    )(jnp.asarray(ii), jnp.asarray(jj), qT, k, vT)


def _conv_silu(xpad, src, cw_ref, cb_ref, first, ls):
    width = xpad.shape[1]

    @pl.when(first)
    def _():
        xpad[0:8, :] = jnp.zeros((8, width), F32)

    @pl.when(jnp.logical_not(first))
    def _():
        xpad[0:8, :] = xpad[ls:ls + 8, :]

    xpad[8:ls + 8, :] = src
    acc = cb_ref[...] + cw_ref[0:1, :] * xpad[pl.ds(8 - CONV_K + 1, ls), :]
    for kk in range(1, CONV_K):
        acc = acc + cw_ref[kk:kk + 1, :] * xpad[pl.ds(8 - CONV_K + 1 + kk, ls), :]
    return acc * _sigmoid(acc)


def _ssd_body(zx_ref, sm_ref, cw_ref, cb_ref, dtb_ref, arow_ref, dexp_ref, ng_ref, tri_ref, hexp_ref,
              o_ref, xpad, state, *, ls):
    first = pl.program_id(1) == 0

    @pl.when(first)
    def _():
        state[...] = jnp.zeros(state.shape, F32)

    xc = _conv_silu(xpad, zx_ref[:, GROUP_W:GROUP_W + SSM_XBC], cw_ref, cb_ref, first, ls)
    dt_all = _softplus(sm_ref[...] + dtb_ref[...])
    da_all = dt_all * arow_ref[...]
    lane = lax.broadcasted_iota(jnp.int32, (1, 128), 1)
    half = lane < HD
    head_lanes = lane < HEADS
    hexp = hexp_ref[...].astype(BF16)
    tril = _tril(CHUNK)

    def expand(x):
        x = jnp.where(head_lanes, x, 0.0)
        hi = x.astype(BF16)
        lo = (x - hi.astype(F32)).astype(BF16)
        return (jnp.dot(hi, hexp, preferred_element_type=F32)
                + jnp.dot(lo, hexp, preferred_element_type=F32))

    for c in range(ls // CHUNK):
        r0 = c * CHUNK
        acum = jnp.dot(tri_ref[...], da_all[r0:r0 + CHUNK], precision=HIGHEST,
                       preferred_element_type=F32)
        acum_t = acum.T
        alast = acum[CHUNK - 1:CHUNK, :]
        ex = expand(jnp.concatenate([dt_all[r0:r0 + CHUNK], jnp.exp(acum), jnp.exp(alast - acum),
                                     jnp.exp(alast), jnp.zeros((7, 128), F32)], axis=0))
        dt_x = ex[0:CHUNK]
        ecol_x = ex[CHUNK:2 * CHUNK]
        dout_x = ex[2 * CHUNK:3 * CHUNK]
        cdec_x = ex[3 * CHUNK:3 * CHUNK + 1]
        for g in range(2):
            h0, h1 = 2 * g, 2 * g + 1
            gl = slice(g * 128, (g + 1) * 128)
            xg = xc[r0:r0 + CHUNK, gl]
            bg = xc[r0:r0 + CHUNK, GROUP_W + g * 128:GROUP_W + (g + 1) * 128]
            cg = xc[r0:r0 + CHUNK, 2 * GROUP_W + g * 128:2 * GROUP_W + (g + 1) * 128]
            cgb = cg.astype(BF16)
            gram = lax.dot_general(cgb, bg.astype(BF16), NT_DIMS, preferred_element_type=F32)
            xdt = xg * dt_x[:, gl]
            ydiag = jnp.zeros((CHUNK, 128), F32)
            for hh, msk in ((h0, half), (h1, jnp.logical_not(half))):
                dec = jnp.exp(jnp.where(tril, acum[:, hh:hh + 1] - acum_t[hh:hh + 1, :], -jnp.inf))
                ydiag = ydiag + jnp.dot((gram * dec).astype(BF16),
                                        jnp.where(msk, xdt, 0.0).astype(BF16),
                                        preferred_element_type=F32)
            st = state[g]
            yoff = jnp.dot(cgb, st.astype(BF16), preferred_element_type=F32) * ecol_x[:, gl]
            state[g] = (cdec_x[:, gl] * st
                        + jnp.dot(bg.T.astype(BF16), (xdt * dout_x[:, gl]).astype(BF16),
                                  preferred_element_type=F32))
            y = ydiag + yoff + xg * dexp_ref[:, g * 128:(g + 1) * 128]
            zg = zx_ref[r0:r0 + CHUNK, g * 128:(g + 1) * 128]
            y = y * (zg * _sigmoid(zg))
            o_ref[r0:r0 + CHUNK, g * 128:(g + 1) * 128] = _rms(y, ng_ref[:, g * 128:(g + 1) * 128])


def _ssd(main, cw, cb, dtb, arow, dexp, ng, tri, hexp, bsz, seq, ls=512):
    t = main.shape[0]
    ns = seq // ls

    def full(shape):
        return pl.BlockSpec(shape, lambda b, s: tuple(0 for _ in shape))

    return pl.pallas_call(
        functools.partial(_ssd_body, ls=ls),
        grid=(bsz, ns),
        in_specs=[
            pl.BlockSpec((ls, 1024), lambda b, s: (b * ns + s, COL_ZX // 1024)),
            pl.BlockSpec((ls, 128), lambda b, s: (b * ns + s, COL_SMALL // 128)),
            full((CONV_K, SSM_XBC)), full((1, SSM_XBC)), full((1, 128)), full((1, 128)),
            full((1, GROUP_W)), full((1, GROUP_W)), full((CHUNK, CHUNK)), full((128, GROUP_W)),
        ],
        out_specs=pl.BlockSpec((ls, GROUP_W), lambda b, s: (b * ns + s, 0)),
        out_shape=jax.ShapeDtypeStruct((t, GROUP_W), F32),
        scratch_shapes=[pltpu.VMEM((ls + 8, SSM_XBC), F32), pltpu.VMEM((2, SSM_STATE, 128), F32)],
        compiler_params=_cparams(("arbitrary", "arbitrary")),
        name="ssd",
    )(main, main, cw, cb, dtb, arow, dexp, ng, tri, hexp)


def _mlstm_body(c_ref, sm_ref, cw_ref, cb_ref, igb_ref, fgb_ref, ng_ref, tri_ref, seg_ref, segm_ref,
                bmask_ref, o_ref, xpad, cst, nst, mst, *, ls):
    first = pl.program_id(1) == 0

    @pl.when(first)
    def _():
        cst[...] = jnp.zeros(cst.shape, F32)
        nst[...] = jnp.zeros(nst.shape, F32)
        mst[...] = jnp.zeros(mst.shape, F32)

    qk = _conv_silu(xpad, c_ref[:, 0:2 * GROUP_W], cw_ref, cb_ref, first, ls)
    sm = sm_ref[...]
    ig_all = sm + igb_ref[...]
    fx = sm + fgb_ref[...]
    lf_all = jnp.minimum(fx, 0.0) - jnp.log1p(jnp.exp(-jnp.abs(fx)))
    lane = lax.broadcasted_iota(jnp.int32, (1, GROUP_W), 1)
    hms = [jnp.logical_and(lane >= h * HD, lane < (h + 1) * HD) for h in range(HEADS)]
    tril = _tril(CHUNK)
    lane_s = lax.broadcasted_iota(jnp.int32, (1, 128), 1)
    gate_lanes = jnp.logical_and(lane_s >= GATE_LANE, lane_s < GATE_LANE + HEADS)
    segb = seg_ref[...].astype(BF16)
    hexp = seg_ref[...].T.astype(BF16)
    segm_b = segm_ref[...].astype(BF16)
    zero_b = jnp.zeros((), BF16)
    n_chunks = ls // CHUNK

    def expand(x):
        x = jnp.where(gate_lanes, x, 0.0)
        hi = x.astype(BF16)
        lo = (x - hi.astype(F32)).astype(BF16)
        return (jnp.dot(hi, hexp, preferred_element_type=F32)
                + jnp.dot(lo, hexp, preferred_element_type=F32))

    local = []
    for c in range(n_chunks):
        r0 = c * CHUNK
        q = qk[r0:r0 + CHUNK, 0:GROUP_W]
        k = qk[r0:r0 + CHUNK, GROUP_W:2 * GROUP_W] * (HD ** -0.5)
        ig = ig_all[r0:r0 + CHUNK]
        bcs = jnp.dot(tri_ref[...], lf_all[r0:r0 + CHUNK], precision=HIGHEST,
                      preferred_element_type=F32)
        bcs_t = bcs.T
        ig_t = ig.T
        ig_al = pltpu.roll(ig, GATE_LANE - 4, 1)
        k_t = k.T.astype(BF16)
        qb = q.astype(BF16)
        vb = c_ref[r0:r0 + CHUNK, 2 * GROUP_W:3 * GROUP_W].astype(BF16)
        num_loc = jnp.zeros((CHUNK, GROUP_W), F32)
        m_loc_all = jnp.zeros((CHUNK, 128), F32)
        den_loc_all = jnp.zeros((CHUNK, 128), F32)
        for h in range(HEADS):
            bcol = bcs[:, GATE_LANE + h:GATE_LANE + h + 1]
            dmat = jnp.where(tril, bcol - bcs_t[GATE_LANE + h:GATE_LANE + h + 1, :] + ig_t[4 + h:5 + h, :],
                             -jnp.inf)
            m_loc = jnp.max(dmat, axis=1, keepdims=True)
            s_qk = jnp.dot(jnp.where(hms[h], qb, zero_b), k_t, preferred_element_type=F32)
            sc = s_qk * jnp.exp(dmat - m_loc)
            num_loc = num_loc + jnp.dot(sc.astype(BF16), jnp.where(hms[h], vb, zero_b),
                                        preferred_element_type=F32)
            on_lane = lane_s == GATE_LANE + h
            m_loc_all = jnp.where(on_lane, m_loc, m_loc_all)
            den_loc_all = jnp.where(on_lane, jnp.sum(sc, axis=1, keepdims=True), den_loc_all)
        blast = bcs[CHUNK - 1:CHUNK, :]
        gcol = blast - bcs + ig_al
        gmax = jnp.max(gcol, axis=0, keepdims=True)
        kw = k * expand(jnp.exp(gcol - gmax))
        kwv = bmask_ref[...] * jnp.dot(kw.T.astype(BF16), vb, preferred_element_type=F32)
        local.append((q, num_loc, bcs, m_loc_all, den_loc_all, blast, gmax, kwv,
                      jnp.sum(kw, axis=0, keepdims=True)))

    houts = []
    for c in range(n_chunks):
        q, num_loc, bcs, m_loc_all, den_loc_all, blast, gmax, kwv, ksum = local[c]
        c_old = cst[...]
        n_old = nst[0:1, :]
        m_prev = mst[0:1, :]
        q_c = jnp.dot(q.astype(BF16), c_old.astype(BF16), preferred_element_type=F32)
        q_n = jnp.dot((q * n_old).astype(BF16), segb, preferred_element_type=F32)
        inter = bcs + m_prev
        m_t = jnp.maximum(inter, m_loc_all)
        f = jnp.exp(m_loc_all - m_t)
        w_inter = jnp.exp(inter - m_t)
        den = f * den_loc_all + w_inter * q_n
        inv_den = 1.0 / jnp.maximum(jnp.abs(den), jnp.exp(-m_t))
        m_new = jnp.maximum(blast + m_prev, gmax)
        scal = jnp.concatenate([jnp.exp(gmax - m_new), jnp.exp(blast + m_prev - m_new),
                                jnp.zeros((6, 128), F32)], axis=0)
        ex = expand(jnp.concatenate([f, w_inter, inv_den, scal], axis=0))
        f_x = ex[0:CHUNK]
        w_inter_x = ex[CHUNK:2 * CHUNK]
        inv_den_x = ex[2 * CHUNK:3 * CHUNK]
        sfac_x = ex[3 * CHUNK:3 * CHUNK + 1]
        keep_x = ex[3 * CHUNK + 1:3 * CHUNK + 2]
        mst[0:1, :] = m_new
        cst[...] = keep_x * c_old + sfac_x * kwv
        nst[0:1, :] = keep_x * n_old + sfac_x * ksum
        houts.append((f_x * num_loc + w_inter_x * q_c) * inv_den_x)

    for c in range(n_chunks):
        r0 = c * CHUNK
        hh = houts[c] * _sigmoid(c_ref[r0:r0 + CHUNK, 3 * GROUP_W:4 * GROUP_W])
        sq = hh * hh
        sq_hi = sq.astype(BF16)
        sq_lo = (sq - sq_hi.astype(F32)).astype(BF16)
        ms = (jnp.dot(sq_hi, segm_b, preferred_element_type=F32)
              + jnp.dot(sq_lo, segm_b, preferred_element_type=F32))
        o_ref[r0:r0 + CHUNK, :] = hh * lax.rsqrt(ms + EPS) * ng_ref[...]


def _mlstm(main, cw, cb, igb, fgb, ng, tri, seg, segm, bmask, bsz, seq, ls=1024):
    t = main.shape[0]
    ns = seq // ls

    def full(shape):
        return pl.BlockSpec(shape, lambda b, s: tuple(0 for _ in shape))

    return pl.pallas_call(
        functools.partial(_mlstm_body, ls=ls),
        grid=(bsz, ns),
        in_specs=[
            pl.BlockSpec((ls, 1024), lambda b, s: (b * ns + s, COL_ML // 1024)),
            pl.BlockSpec((ls, 128), lambda b, s: (b * ns + s, COL_SMALL // 128)),
            full((CONV_K, 2 * GROUP_W)), full((1, 2 * GROUP_W)), full((1, 128)), full((1, 128)),
            full((1, GROUP_W)), full((CHUNK, CHUNK)), full((GROUP_W, 128)), full((GROUP_W, GROUP_W)),
            full((GROUP_W, GROUP_W)),
        ],
        out_specs=pl.BlockSpec((ls, GROUP_W), lambda b, s: (b * ns + s, 0)),
        out_shape=jax.ShapeDtypeStruct((t, GROUP_W), F32),
        scratch_shapes=[
            pltpu.VMEM((ls + 8, 2 * GROUP_W), F32),
            pltpu.VMEM((GROUP_W, GROUP_W), F32),
            pltpu.VMEM((8, GROUP_W), F32),
            pltpu.VMEM((8, 128), F32),
        ],
        compiler_params=_cparams(("arbitrary", "arbitrary")),
        name="mlstm",
    )(main, main, cw, cb, igb, fgb, ng, tri, seg, segm, bmask)


def _outproj_body(x_ref, ya_ref, yb_ref, yc_ref, yd_ref, wo_ref, g2_ref, wr_ref, br_ref,
                  xn_ref, h2_ref, cb_ref):
    sub = x_ref.shape[0] // 2
    accs = []
    for s in range(2):
        rows = slice(s * sub, (s + 1) * sub)
        acc = x_ref[rows, :]
        for gi, y_ref in enumerate((ya_ref, yb_ref, yc_ref, yd_ref)):
            acc = acc + jnp.dot(y_ref[rows, :].astype(BF16), wo_ref[gi * GROUP_W:(gi + 1) * GROUP_W, :],
                                preferred_element_type=F32)
        accs.append(acc)
    for s in range(2):
        _route(accs[s], slice(s * sub, (s + 1) * sub), g2_ref, wr_ref, br_ref, xn_ref, h2_ref, cb_ref)


def _route(acc, rows, g2_ref, wr_ref, br_ref, xn_ref, h2_ref, cb_ref):
    xn_ref[rows, :] = acc
    h2 = _rms(acc, g2_ref[...])
    h_hi = h2.astype(BF16)
    h2_ref[rows, :] = h_hi
    h_lo = (h2 - h_hi.astype(F32)).astype(BF16)
    hh = jnp.dot(h_hi, wr_ref[...], preferred_element_type=F32)
    lh = jnp.dot(h_lo, wr_ref[...], preferred_element_type=F32)
    logits = hh[:, :128] + hh[:, 128:] + lh[:, :128] + br_ref[...]
    lane = lax.broadcasted_iota(jnp.int32, logits.shape, 1)
    big = jnp.int32(1 << 20)
    n_grp = N_EXPERTS // EXPERTS_PER_GROUP
    gl = jnp.where(jnp.logical_and(lane >= N_EXPERTS, lane < N_EXPERTS + n_grp), logits, -jnp.inf)
    gmax = jnp.max(gl, axis=1, keepdims=True)
    g_w = 1.0 / jnp.sum(jnp.exp(gl - gmax), axis=1, keepdims=True)
    gidx = jnp.min(jnp.where(gl == gmax, lane, big), axis=1, keepdims=True) - N_EXPERTS
    in_grp = jnp.logical_and(lane < N_EXPERTS,
                             lax.shift_right_logical(lane, int(math.log2(EXPERTS_PER_GROUP))) == gidx)
    el = jnp.where(in_grp, logits, -jnp.inf)
    emax = jnp.max(el, axis=1, keepdims=True)
    esum = jnp.sum(jnp.exp(el - emax), axis=1, keepdims=True)
    i1 = jnp.min(jnp.where(el == emax, lane, big), axis=1, keepdims=True)
    el2 = jnp.where(lane == i1, -jnp.inf, el)
    emax2 = jnp.max(el2, axis=1, keepdims=True)
    i2 = jnp.min(jnp.where(el2 == emax2, lane, big), axis=1, keepdims=True)
    p1 = 1.0 / esum
    p2 = jnp.exp(emax2 - emax) / esum
    w1 = p1 / (p1 + p2)
    w2 = p2 / (p1 + p2)
    comb = g_w * (jnp.where(lane == i1, w1, 0.0) + jnp.where(lane == i2, w2, 0.0))
    cb_ref[rows, :] = jnp.where(lane == GID_LANE, gidx.astype(F32), comb)


def _outproj(x2, ya, yb, yc, yd, wo, g2, wr, br, tm=512):
    t = x2.shape[0]
    row = pl.BlockSpec((tm, D_MODEL), lambda i: (i, 0))
    grp = pl.BlockSpec((tm, GROUP_W), lambda i: (i, 0))

    def full(shape):
        return pl.BlockSpec(shape, lambda i: tuple(0 for _ in shape))

    return pl.pallas_call(
        _outproj_body,
        grid=(t // tm,),
        in_specs=[row, grp, grp, grp, grp, full((D_MODEL, D_MODEL)), full((1, D_MODEL)),
                  full((D_MODEL, 256)), full((1, 128))],
        out_specs=[row, row, pl.BlockSpec((tm, 128), lambda i: (i, 0))],
        out_shape=[
            jax.ShapeDtypeStruct((t, D_MODEL), F32),
            jax.ShapeDtypeStruct((t, D_MODEL), BF16),
            jax.ShapeDtypeStruct((t, 128), F32),
        ],
        compiler_params=_cparams(("arbitrary",)),
        name="outproj_router",
    )(x2, ya, yb, yc, yd, wo, g2, wr, br)


MOE_EB = EXPERTS_PER_GROUP
MOE_RB = 128


def _moe_body(h_ref, cb_ref, xn_ref, tri_ref, wg_ref, wu_ref, wd_ref, gf_ref, o_ref,
              perm_t, hs, cs, ys, seg_smem, *, final_norm, rb):
    g = pl.program_id(1)
    tm = h_ref.shape[0]
    n_grp = N_EXPERTS // EXPERTS_PER_GROUP
    lane = lax.broadcasted_iota(jnp.int32, (tm, 128), 1)

    @pl.when(g == 0)
    def _():
        cb = cb_ref[...]
        gid = jnp.sum(jnp.where(lane == GID_LANE, cb, 0.0), axis=1, keepdims=True).astype(jnp.int32)
        onehot = jnp.where(lane == gid, 1.0, 0.0)
        oh16 = onehot.astype(BF16)
        tot = jnp.zeros((1, 128), F32)
        parts = []
        for r0 in range(0, tm, CHUNK):
            part = jnp.dot(tri_ref[...], oh16[r0:r0 + CHUNK], preferred_element_type=F32) + tot
            parts.append(part)
            tot = part[CHUNK - 1:CHUNK, :]
        cum = jnp.concatenate(parts, axis=0)
        lane1 = lane[0:1, :]
        base = jnp.zeros((1, 128), F32)
        run = jnp.zeros((1, 1), F32)
        for k in range(n_grp):
            base = base + jnp.where(lane1 == k, run, 0.0)
            seg_smem[k] = run[0, 0].astype(jnp.int32)
            seg_smem[n_grp + k] = tot[0, k].astype(jnp.int32)
            run = run + tot[:, k:k + 1]
        pos = jnp.sum(onehot * (base + cum - 1.0), axis=1, keepdims=True).astype(jnp.int32)
        pos_row = jnp.broadcast_to(pos.astype(F32), (tm, 128)).T[0:1, :].astype(jnp.int32)
        perm_t[...] = jnp.where(lax.broadcasted_iota(jnp.int32, (tm, tm), 1) == pos, 1.0, 0.0).astype(BF16)
        p = jnp.where(lax.broadcasted_iota(jnp.int32, (tm, tm), 0) == pos_row, 1.0, 0.0).astype(BF16)
        hs[...] = jnp.dot(p, h_ref[...], preferred_element_type=F32).astype(BF16)
        cb_hi = cb.astype(BF16)
        cb_lo = (cb - cb_hi.astype(F32)).astype(BF16)
        cs2 = jnp.dot(p, jnp.concatenate([cb_hi, cb_lo], axis=1), preferred_element_type=F32)
        cs[...] = cs2[:, :128] + cs2[:, 128:]
        ys[...] = jnp.zeros(ys.shape, F32)

    seg_lo = seg_smem[g]
    seg_hi = seg_lo + seg_smem[n_grp + g]
    lane_rb = lax.broadcasted_iota(jnp.int32, (rb, 128), 1)

    def run_block(lo):
        x = hs[lo:lo + rb, :]
        cw = cs[lo:lo + rb, :]

        def gate_up(j):
            return (jnp.dot(x, wg_ref[j], preferred_element_type=F32),
                    jnp.dot(x, wu_ref[j], preferred_element_type=F32))

        def down(j, gu):
            col = jnp.sum(jnp.where(lane_rb == g * MOE_EB + j, cw, 0.0), axis=1,
                          keepdims=True)
            gt, up = gu
            hid = (gt * _sigmoid(gt)) * up * col
            return jnp.dot(hid.astype(BF16), wd_ref[j], preferred_element_type=F32)

        total = jnp.zeros((rb, D_MODEL), F32)
        gu = gate_up(0)
        for j in range(MOE_EB):
            gu_next = gate_up(j + 1) if j + 1 < MOE_EB else None
            total = total + down(j, gu)
            gu = gu_next
        ys[lo:lo + rb, :] += total

    for b in range(tm // rb):
        lo = b * rb
        pl.when(jnp.logical_and(seg_lo < lo + rb, seg_hi > lo))(functools.partial(run_block, lo))

    @pl.when(g == n_grp - 1)
    def _():
        y = xn_ref[...] + jnp.dot(perm_t[...], ys[...].astype(BF16), preferred_element_type=F32)
        if final_norm:
            y = _rms(y, gf_ref[...])
        o_ref[...] = y


def _moe(h2, comb, xn, tri, wg, wu, wd, gf, final_norm, tm=1024, rb=MOE_RB):
    t = h2.shape[0]
    return pl.pallas_call(
        functools.partial(_moe_body, final_norm=final_norm, rb=rb),
        grid=(t // tm, N_EXPERTS // MOE_EB),
        in_specs=[
            pl.BlockSpec((tm, D_MODEL), lambda i, e: (i, 0)),
            pl.BlockSpec((tm, 128), lambda i, e: (i, 0)),
            pl.BlockSpec((tm, D_MODEL), lambda i, e: (i, 0)),
            pl.BlockSpec((CHUNK, CHUNK), lambda i, e: (0, 0)),
            pl.BlockSpec((MOE_EB, D_MODEL, D_EXPERT), lambda i, e: (e, 0, 0)),
            pl.BlockSpec((MOE_EB, D_MODEL, D_EXPERT), lambda i, e: (e, 0, 0)),
            pl.BlockSpec((MOE_EB, D_EXPERT, D_MODEL), lambda i, e: (e, 0, 0)),
            pl.BlockSpec((1, D_MODEL), lambda i, e: (0, 0)),
        ],
        out_specs=pl.BlockSpec((tm, D_MODEL), lambda i, e: (i, 0)),
        out_shape=jax.ShapeDtypeStruct((t, D_MODEL), F32),
        scratch_shapes=[
            pltpu.VMEM((tm, tm), BF16),
            pltpu.VMEM((tm, D_MODEL), BF16),
            pltpu.VMEM((tm, 128), F32),
            pltpu.VMEM((tm, D_MODEL), F32),
            pltpu.SMEM((2 * (N_EXPERTS // MOE_EB),), jnp.int32),
        ],
        compiler_params=_cparams(("arbitrary", "arbitrary")),
        name="moe",
    )(h2, comb, xn, tri, wg, wu, wd, gf)


def _spread_heads(w, width, slab, total):
    rows = w.shape[0]
    pieces = []
    for h in range(HEADS):
        pieces.append(w[:, h * width:(h + 1) * width])
        pieces.append(jnp.zeros((rows, slab - width), w.dtype))
    if total > HEADS * slab:
        pieces.append(jnp.zeros((rows, total - HEADS * slab), w.dtype))
    return jnp.concatenate(pieces, axis=1)


def _pad_heads(w, width):
    return _spread_heads(w, width, HP, HEADS * HP)


def _v_cols(w):
    return _spread_heads(w, HD, VR, V_COLS)


def _lane_row(vals, offset, width=128):
    return jnp.zeros((1, width), F32).at[0, offset:offset + vals.shape[0]].set(vals.astype(F32))


def _layer_weights(l, p):
    half = MLA_ROPE // 2
    w = {"g_mix": p["norm_mix_g"][l][None, :]}
    w["ssm_cw"] = p["ssm_conv_w"][l]
    w["ssm_cb"] = p["ssm_conv_b"][l][None, :]
    w["ssm_dtb"] = _lane_row(p["ssm_dt_bias"][l], 0)
    w["ssm_arow"] = _lane_row(-jnp.exp(p["ssm_a_log"][l].astype(F32)), 0)
    w["ssm_dexp"] = jnp.repeat(p["ssm_d"][l].astype(F32), HD)[None, :]
    w["ssm_ng"] = p["ssm_norm_g"][l][None, :]
    w["ml_cw"] = p["ml_conv_w"][l]
    w["ml_cb"] = p["ml_conv_b"][l][None, :]
    w["ml_igb"] = _lane_row(p["ml_ig_bias"][l], 4)
    w["ml_fgb"] = _lane_row(p["ml_fg_bias"][l], 8)
    w["ml_ng"] = p["ml_norm_g"][l][None, :]
    wuq = p["mla_w_uq"][l]
    qd = MLA_NOPE + MLA_ROPE
    swapped = []
    for h in range(HEADS):
        c0 = h * qd + MLA_NOPE
        swapped += [jnp.zeros((256, MLA_NOPE), F32), wuq[:, c0 + half:c0 + 2 * half], wuq[:, c0:c0 + half]]
    wq = _spread_heads(wuq, qd, HP, HEADS * HP)
    wqs = _spread_heads(jnp.concatenate(swapped, axis=1), qd, HP, HEADS * HP)
    w["mla_wq2"] = jnp.concatenate([wq, wqs], axis=1).astype(BF16)
    wukv = p["mla_w_ukv"][l]
    wk_cols = jnp.concatenate([wukv[:, h * 2 * HD:h * 2 * HD + MLA_NOPE] for h in range(HEADS)], axis=1)
    wv_cols = jnp.concatenate([wukv[:, h * 2 * HD + MLA_NOPE:(h + 1) * 2 * HD] for h in range(HEADS)], axis=1)
    w["mla_wk"] = _pad_heads(wk_cols, MLA_NOPE).astype(BF16)
    w["mla_wv"] = _v_cols(wv_cols).astype(BF16)
    w["mla_gq"] = p["mla_q_norm_g"][l][None, :]
    w["mla_gkv"] = p["mla_kv_norm_g"][l][None, :]
    w["w_out"] = p["w_out"][l].astype(BF16)
    w["g_ffn"] = p["norm_ffn_g"][l][None, :]
    wr = jnp.zeros((D_MODEL, 128), F32)
    wr = wr.at[:, :N_EXPERTS].set(p["moe_w_router"][l]).at[:, N_EXPERTS:N_EXPERTS + 4].set(p["moe_w_group"][l])
    wr_hi = wr.astype(BF16)
    w["w_router"] = jnp.concatenate([wr_hi, (wr - wr_hi.astype(F32)).astype(BF16)], axis=1)
    w["b_router"] = _lane_row(p["moe_b_router"][l], 0) + _lane_row(p["moe_b_group"][l], N_EXPERTS)
    w["w_gate"] = p["moe_w_gate"][l].astype(BF16)
    w["w_up"] = p["moe_w_up"][l].astype(BF16)
    w["w_down"] = p["moe_w_down"][l].astype(BF16)
    return w


def kernel(x, positions, norm_mix_g, w_in, ssm_conv_w, ssm_conv_b, ssm_dt_bias, ssm_a_log, ssm_d, ssm_norm_g, ml_conv_w, ml_conv_b, ml_ig_bias, ml_fg_bias, ml_norm_g, mla_q_norm_g, mla_w_uq, mla_kv_norm_g, mla_w_ukv, w_out, norm_ffn_g, moe_w_group, moe_b_group, moe_w_router, moe_b_router, moe_w_gate, moe_w_up, moe_w_down, final_norm_g):
    p = dict(norm_mix_g=norm_mix_g, w_all=_relayout_w_in(w_in), ssm_conv_w=ssm_conv_w, ssm_conv_b=ssm_conv_b,
             ssm_dt_bias=ssm_dt_bias, ssm_a_log=ssm_a_log, ssm_d=ssm_d, ssm_norm_g=ssm_norm_g,
             ml_conv_w=ml_conv_w, ml_conv_b=ml_conv_b, ml_ig_bias=ml_ig_bias, ml_fg_bias=ml_fg_bias,
             ml_norm_g=ml_norm_g, mla_q_norm_g=mla_q_norm_g, mla_w_uq=mla_w_uq,
             mla_kv_norm_g=mla_kv_norm_g, mla_w_ukv=mla_w_ukv, w_out=w_out, norm_ffn_g=norm_ffn_g,
             moe_w_group=moe_w_group, moe_b_group=moe_b_group, moe_w_router=moe_w_router,
             moe_b_router=moe_b_router, moe_w_gate=moe_w_gate, moe_w_up=moe_w_up, moe_w_down=moe_w_down)
    bsz, seq, _ = x.shape
    depth = w_in.shape[0]
    x2 = x.reshape(bsz * seq, D_MODEL)

    half = MLA_ROPE // 2
    inv = ROPE_THETA ** (-jnp.arange(0, MLA_ROPE, 2, dtype=F32) / MLA_ROPE)
    inv_col = jnp.concatenate([inv, inv])[:, None]
    sgn_col = jnp.concatenate([-jnp.ones((half,), F32), jnp.ones((half,), F32)])[:, None]
    pe_mask = _lane_row(jnp.ones((MLA_ROPE,), F32), MLA_NOPE)
    c_tok, s_tok, c_tr, s_tr = _rope_tables(positions.reshape(bsz, 1, seq), inv_col, sgn_col, bsz, seq)
    tri = jnp.tril(jnp.ones((CHUNK, CHUNK), F32))
    tri_moe = tri.astype(BF16)
    head_of = jnp.arange(GROUP_W) // HD
    seg = (head_of[:, None] + GATE_LANE == jnp.arange(128)[None, :]).astype(F32)
    ssd_hexp = (jnp.arange(128)[:, None] == head_of[None, :]).astype(F32)
    bmask = (head_of[:, None] == head_of[None, :]).astype(F32)
    segm = bmask / HD

    for l in range(depth):
        w = _layer_weights(l, p)
        main, k_a, qT_a, vT_a, qT_d, k_d, vT_d = _in_proj(
            x2, w["g_mix"], p["w_all"], l, c_tok, s_tok, c_tr, s_tr, w["mla_gq"], w["mla_wq2"], w["mla_gkv"],
            w["mla_wk"], w["mla_wv"], pe_mask, bsz, seq)
        y_a = _attention(qT_a, k_a, vT_a, bsz, seq, "attn_moba")
        y_b = _ssd(main, w["ssm_cw"], w["ssm_cb"], w["ssm_dtb"], w["ssm_arow"], w["ssm_dexp"], w["ssm_ng"],
                   tri, ssd_hexp, bsz, seq)
        y_c = _mlstm(main, w["ml_cw"], w["ml_cb"], w["ml_igb"], w["ml_fgb"], w["ml_ng"], tri, seg, segm,
                     bmask, bsz, seq)
        y_d = _attention(qT_d, k_d, vT_d, bsz, seq, "attn_mla")
        xn, h2, comb = _outproj(x2, y_a, y_b, y_c, y_d, w["w_out"], w["g_ffn"], w["w_router"], w["b_router"])
        x2 = _moe(h2, comb, xn, tri_moe, w["w_gate"], w["w_up"], w["w_down"], final_norm_g[None, :],
                  final_norm=(l == depth - 1), tm=MOE_TM)
    return x2.reshape(bsz, seq, D_MODEL)
```

```python
import functools
import math

import numpy as np
import jax
import jax.numpy as jnp
from jax import lax
from jax.experimental import pallas as pl
from jax.experimental.pallas import tpu as pltpu

F32 = jnp.float32
BF16 = jnp.bfloat16
HIGHEST = lax.Precision.HIGHEST

EPS = 1e-6
NEG = -1e30
LOG2E = 1.4426950408889634

D_MODEL = 1024
GROUP_W = 256
HEADS = 4
HD = 64
HP = 128
MOBA_BLOCK = 256
MOBA_TOPK = 3
SSM_STATE = 128
SSM_XBC = 768
CONV_K = 4
CHUNK = 128
MLA_NOPE = 64
MLA_ROPE = 32
ROPE_THETA = 10000.0
N_EXPERTS = 16
EXPERTS_PER_GROUP = 4
D_EXPERT = 256
GATE_LANE = 8
GID_LANE = 20
MOE_TM = 1024

COL_ZX = 0
COL_ML = 1024
COL_SMALL = 2048
N_MAIN = 2176
VR = 80
V_COLS = 384
N_ALL = N_MAIN + 256 + 128 + 3 * GROUP_W
ATT_UQ = 256
ATT_UK = 256

V7X_VMEM_LIMIT = 56 * 1024 * 1024

NT_DIMS = (((1,), (1,)), ((), ()))


def _cparams(sem):
    return pltpu.CompilerParams(dimension_semantics=sem, vmem_limit_bytes=V7X_VMEM_LIMIT)


def _rms(x, g):
    ms = jnp.mean(x * x, axis=-1, keepdims=True)
    return x * lax.rsqrt(ms + EPS) * g


def _sigmoid(x):
    return 0.5 * jnp.tanh(0.5 * x) + 0.5


def _softplus(x):
    return jnp.maximum(x, 0.0) + jnp.log1p(jnp.exp(-jnp.abs(x)))


def _tril(n):
    r = lax.broadcasted_iota(jnp.int32, (n, n), 0)
    c = lax.broadcasted_iota(jnp.int32, (n, n), 1)
    return c <= r


W_IN_SIZES = (3 * GROUP_W, GROUP_W, SSM_XBC, HEADS, 2 * GROUP_W, GROUP_W, HEADS, HEADS, GROUP_W, 256, 128,
              MLA_ROPE)


def _relayout_body(w_ref, o_ref):
    w = w_ref[0]
    rows = w.shape[0]
    offs = np.cumsum((0,) + W_IN_SIZES)
    a_qkv, s_z, s_xbc, s_dt, m_qk, m_v, m_i, m_f, m_o, d_cq, d_ckv, d_kr = [
        w[:, int(offs[i]):int(offs[i + 1])] for i in range(len(W_IN_SIZES))]
    half = MLA_ROPE // 2

    def zeros(n):
        return jnp.zeros((rows, n), F32)

    pieces = [s_z, s_xbc, m_qk, m_v, m_o,
              s_dt, m_i, m_f, zeros(64 - 3 * HEADS), d_kr, d_kr[:, half:], d_kr[:, :half],
              d_cq, d_ckv,
              a_qkv[:, GROUP_W:2 * GROUP_W], a_qkv[:, :GROUP_W], a_qkv[:, 2 * GROUP_W:]]
    o_ref[0] = jnp.concatenate(pieces, axis=1).astype(BF16)


def _relayout_w_in(w, rows=256):
    nl, r, c = w.shape
    return pl.pallas_call(
        _relayout_body,
        grid=(nl, r // rows),
        in_specs=[pl.BlockSpec((1, rows, c), lambda l, i: (l, i, 0))],
        out_specs=pl.BlockSpec((1, rows, N_ALL), lambda l, i: (l, i, 0)),
        out_shape=jax.ShapeDtypeStruct((nl, r, N_ALL), BF16),
        compiler_params=_cparams(("arbitrary", "arbitrary")),
        name="relayout_w_in",
    )(w)


def _ones_rows(shape):
    row = lax.broadcasted_iota(jnp.int32, shape, 0)
    hit = row == HD
    for h in range(1, HEADS):
        hit = jnp.logical_or(hit, row == h * VR + HD)
    return jnp.where(hit, 1.0, 0.0)


def _in_proj_body(x_ref, g_ref, w_ref, c_ref, s_ref, ct_ref, st_ref, gq_ref, wq_ref, gkv_ref, wk_ref,
                  wv_ref, msk_ref, om_ref, ka_ref, qa_ref, va_ref, qd_ref, kd_ref, vd_ref, km_sc,
                  *, nb, nst, qscale_a, qscale_d):
    s_idx = pl.program_id(0) % nst
    tm = x_ref.shape[0]

    @pl.when(s_idx == 0)
    def _():
        km_sc[...] = jnp.zeros(km_sc.shape, F32)

    h = _rms(x_ref[...], g_ref[...]).astype(BF16)
    rest = jnp.dot(h, w_ref[0, :, COL_SMALL:], preferred_element_type=F32)
    for c in range(0, COL_SMALL, 512):
        om_ref[:, c:c + 512] = jnp.dot(h, w_ref[0, :, c:c + 512], preferred_element_type=F32)
    sm = rest[:, 0:128]
    om_ref[:, COL_SMALL:COL_SMALL + 128] = sm
    cq = rest[:, 128:384]
    ckv = rest[:, 384:512]
    kp = _spread_heads(rest[:, 512:768], HD, HP, HEADS * HP)
    qp = _spread_heads(rest[:, 768:1024], HD, HP, HEADS * HP)
    vc = _spread_heads(rest[:, 1024:1280], HD, VR, V_COLS)

    nblk = tm // MOBA_BLOCK
    lane_k = lax.broadcasted_iota(jnp.int32, (MOBA_BLOCK, HEADS * HP), 1)
    for j in range(nblk):
        kj = kp[j * MOBA_BLOCK:(j + 1) * MOBA_BLOCK]
        blk_id = s_idx * nblk + j
        km_sc[pl.ds(blk_id, 1), :] = jnp.mean(kj, axis=0, keepdims=True)
        ka_ref[j * MOBA_BLOCK:(j + 1) * MOBA_BLOCK, :] = jnp.where(
            (lane_k & (HP - 1)) == HD + blk_id, 1.0, kj).astype(BF16)

    qT = qp.T
    km = km_sc[...]
    pos = s_idx * tm + lax.broadcasted_iota(jnp.int32, (nb, tm), 1)
    cur = lax.shift_right_logical(pos, int(math.log2(MOBA_BLOCK)))
    blk = lax.broadcasted_iota(jnp.int32, (nb, tm), 0)
    past = blk < cur
    for hd in range(HEADS):
        qh = qT[hd * HP:(hd + 1) * HP, :]
        gate = jnp.dot(km[:, hd * HP:(hd + 1) * HP], qh, precision=HIGHEST,
                       preferred_element_type=F32)
        gate = jnp.where(past, gate, -jnp.inf)
        bias = jnp.where(blk == cur, 0.0, NEG)
        for _ in range(MOBA_TOPK):
            m = jnp.max(gate, axis=0, keepdims=True)
            cand = jnp.logical_and(gate == m, gate > -jnp.inf)
            idx = jnp.min(jnp.where(cand, blk, nb), axis=0, keepdims=True)
            pick = blk == idx
            bias = jnp.where(pick, 0.0, bias)
            gate = jnp.where(pick, -jnp.inf, gate)
        rows = [qh[:HD] * qscale_a, bias]
        if HP - HD - nb:
            rows.append(jnp.zeros((HP - HD - nb, tm), F32))
        qa_ref[0, hd * HP:(hd + 1) * HP, :] = jnp.concatenate(rows, axis=0).astype(BF16)
    vt = vc.T[:HEADS * VR]
    va_ref[0] = (vt + _ones_rows(vt.shape)).astype(BF16)

    qn = _rms(cq, gq_ref[...]).astype(BF16)
    q2 = jnp.dot(qn, wq_ref[...], preferred_element_type=F32).T
    nq = HEADS * HP
    ct4 = jnp.concatenate([ct_ref[0]] * HEADS, axis=0)
    st4 = jnp.concatenate([st_ref[0]] * HEADS, axis=0)
    qd_ref[0] = ((q2[:nq] * ct4 + q2[nq:] * st4) * qscale_d).astype(BF16)
    kvn = _rms(ckv, gkv_ref[...]).astype(BF16)
    kn = jnp.dot(kvn, wk_ref[...], preferred_element_type=F32)
    pe = (sm * c_ref[...] + pltpu.roll(sm, HP - MLA_ROPE, 1) * s_ref[...]) * msk_ref[...]
    kd_ref[...] = (kn + jnp.concatenate([pe] * HEADS, axis=1)).astype(BF16)
    vtd = jnp.dot(kvn, wv_ref[...], preferred_element_type=F32).T[:HEADS * VR]
    vd_ref[0] = (vtd + _ones_rows(vtd.shape)).astype(BF16)


def _in_proj(x2, g, w_all, layer, c_tok, s_tok, c_tr, s_tr, gq, wq2, gkv, wk, wv, pe_mask, bsz, seq, tm=512):
    t = x2.shape[0]
    nst = seq // tm
    nb = seq // MOBA_BLOCK
    assert nb <= HP - HD
    tok128 = pl.BlockSpec((tm, HP), lambda i: (i, 0))
    tr128 = pl.BlockSpec((1, HP, tm), lambda i: (i // nst, 0, i % nst))
    tr_q = pl.BlockSpec((1, HEADS * HP, tm), lambda i: (i // nst, 0, i % nst))
    tr_v = pl.BlockSpec((1, HEADS * VR, tm), lambda i: (i // nst, 0, i % nst))
    tok_k = pl.BlockSpec((tm, HEADS * HP), lambda i: (i, 0))

    def full(shape):
        return pl.BlockSpec(shape, lambda i: tuple(0 for _ in shape))

    q_shape = jax.ShapeDtypeStruct((bsz, HEADS * HP, seq), BF16)
    k_shape = jax.ShapeDtypeStruct((t, HEADS * HP), BF16)
    v_shape = jax.ShapeDtypeStruct((bsz, HEADS * VR, seq), BF16)
    return pl.pallas_call(
        functools.partial(_in_proj_body, nb=nb, nst=nst, qscale_a=(HD ** -0.5) * LOG2E,
                          qscale_d=((MLA_NOPE + MLA_ROPE) ** -0.5) * LOG2E),
        grid=(t // tm,),
        in_specs=[
            pl.BlockSpec((tm, D_MODEL), lambda i: (i, 0)),
            full((1, D_MODEL)), pl.BlockSpec((1, D_MODEL, N_ALL), lambda i: (layer, 0, 0)),
            tok128, tok128, tr128, tr128,
            full((1, 256)), full((256, 2 * HEADS * HP)), full((1, 128)),
            full((128, HEADS * HP)), full((128, V_COLS)), full((1, HP)),
        ],
        out_specs=[pl.BlockSpec((tm, N_MAIN), lambda i: (i, 0)), tok_k, tr_q, tr_v, tr_q, tok_k, tr_v],
        out_shape=[jax.ShapeDtypeStruct((t, N_MAIN), F32), k_shape, q_shape, v_shape,
                   q_shape, k_shape, v_shape],
        scratch_shapes=[pltpu.VMEM((nb, HEADS * HP), F32)],
        compiler_params=_cparams(("arbitrary",)),
        name="in_proj",
    )(x2, g, w_all, c_tok, s_tok, c_tr, s_tr, gq, wq2, gkv, wk, wv, pe_mask)


def _rope_body(pos_ref, inv_ref, sgn_ref, c_ref, s_ref, ct_ref, st_ref):
    ts = pos_ref.shape[2]
    ang = inv_ref[...] * pos_ref[0].astype(F32)
    c32 = jnp.cos(ang)
    s32 = jnp.sin(ang) * sgn_ref[...]
    ct = jnp.concatenate([jnp.ones((MLA_NOPE, ts), F32), c32,
                          jnp.ones((HP - MLA_NOPE - MLA_ROPE, ts), F32)], axis=0)
    st = jnp.concatenate([jnp.zeros((MLA_NOPE, ts), F32), s32,
                          jnp.zeros((HP - MLA_NOPE - MLA_ROPE, ts), F32)], axis=0)
    ct_ref[0] = ct
    st_ref[0] = st
    c_ref[...] = ct.T
    s_ref[...] = st.T


def _rope_tables(pos3, inv_col, sgn_col, bsz, seq, ts=512):
    t = bsz * seq
    nst = seq // ts
    tok = pl.BlockSpec((ts, HP), lambda i: (i, 0))
    tr = pl.BlockSpec((1, HP, ts), lambda i: (i // nst, 0, i % nst))
    col = pl.BlockSpec((MLA_ROPE, 1), lambda i: (0, 0))
    return pl.pallas_call(
        _rope_body,
        grid=(t // ts,),
        in_specs=[pl.BlockSpec((1, 1, ts), lambda i: (i // nst, 0, i % nst)), col, col],
        out_specs=[tok, tok, tr, tr],
        out_shape=[
            jax.ShapeDtypeStruct((t, HP), F32),
            jax.ShapeDtypeStruct((t, HP), F32),
            jax.ShapeDtypeStruct((bsz, HP, seq), F32),
            jax.ShapeDtypeStruct((bsz, HP, seq), F32),
        ],
        compiler_params=_cparams(("arbitrary",)),
        name="rope_tables",
    )(pos3, inv_col, sgn_col)


def _attn_body(ii_ref, jj_ref, qT_ref, k_ref, vT_ref, o_ref, m_sc, acc_sc, *, tq, tk, ahead):
    p = pl.program_id(1)
    i = ii_ref[p]
    j = jj_ref[p]

    @pl.when(j == 0)
    def _():
        m_sc[...] = jnp.full(m_sc.shape, NEG, F32)
        acc_sc[...] = jnp.zeros(acc_sc.shape, F32)

    def scores(u):
        h, qc, kb, _ = u
        return jnp.dot(k_ref[kb * ATT_UK:(kb + 1) * ATT_UK, h * HP:(h + 1) * HP],
                       qT_ref[0, h * HP:(h + 1) * HP, qc * ATT_UQ:(qc + 1) * ATT_UQ],
                       preferred_element_type=F32)

    def update(u, s):
        h, qc, kb, masked = u
        if masked:
            kpos = kb * ATT_UK + lax.broadcasted_iota(jnp.int32, s.shape, 0)
            qpos = qc * ATT_UQ + lax.broadcasted_iota(jnp.int32, s.shape, 1)
            s = jnp.where(kpos <= qpos, s, NEG)
        alphas, pms = [], []
        for c in range(ATT_UQ // 128):
            ql = slice(qc * ATT_UQ + c * 128, qc * ATT_UQ + (c + 1) * 128)
            sc = s[:, c * 128:(c + 1) * 128]
            m_old = m_sc[h:h + 1, ql]
            m_new = jnp.maximum(m_old, jnp.max(sc, axis=0, keepdims=True))
            alphas.append(jnp.exp2(m_old - m_new))
            pms.append(jnp.exp2(sc - m_new).astype(BF16))
            m_sc[h:h + 1, ql] = m_new
        ql = slice(qc * ATT_UQ, (qc + 1) * ATT_UQ)
        rows = slice(h * VR, (h + 1) * VR)
        acc_sc[rows, ql] = jnp.concatenate(alphas, axis=1) * acc_sc[rows, ql] + jnp.dot(
            vT_ref[0, rows, kb * ATT_UK:(kb + 1) * ATT_UK], jnp.concatenate(pms, axis=1),
            preferred_element_type=F32)

    def run(units):
        pending = [scores(u) for u in units[:ahead]]
        for n, u in enumerate(units):
            s = pending.pop(0)
            if n + ahead < len(units):
                pending.append(scores(units[n + ahead]))
            update(u, s)

    def unit_list(diag):
        units = []
        for kb in range(tk // ATT_UK):
            for h in range(HEADS):
                for qc in range(tq // ATT_UQ):
                    k_lo, k_hi = kb * ATT_UK, (kb + 1) * ATT_UK - 1
                    q_lo, q_hi = qc * ATT_UQ, (qc + 1) * ATT_UQ - 1
                    if diag and k_lo > q_hi:
                        continue
                    units.append((h, qc, kb, diag and k_hi > q_lo))
        return units

    @pl.when(j < i)
    def _():
        run(unit_list(False))

    @pl.when(j == i)
    def _():
        run(unit_list(True))
        parts = []
        for h in range(HEADS):
            den = acc_sc[h * VR + HD:h * VR + HD + 1, :]
            parts.append(acc_sc[h * VR:h * VR + HD, :] * (1.0 / den))
        o_ref[...] = jnp.concatenate(parts, axis=0).T


def _attention(qT, k, vT, bsz, seq, name, tq=1024, ahead=5):
    t = k.shape[0]
    nq = seq // tq
    ii = np.concatenate([np.full(i + 1, i, np.int32) for i in range(nq)])
    jj = np.concatenate([np.arange(i + 1, dtype=np.int32) for i in range(nq)])
    grid_spec = pltpu.PrefetchScalarGridSpec(
        num_scalar_prefetch=2,
        grid=(bsz, len(ii)),
        in_specs=[
            pl.BlockSpec((1, HEADS * HP, tq), lambda b, p, ii, jj: (b, 0, ii[p])),
            pl.BlockSpec((tq, HEADS * HP), lambda b, p, ii, jj: (b * nq + jj[p], 0)),
            pl.BlockSpec((1, HEADS * VR, tq), lambda b, p, ii, jj: (b, 0, jj[p])),
        ],
        out_specs=pl.BlockSpec((tq, GROUP_W), lambda b, p, ii, jj: (b * nq + ii[p], 0)),
        scratch_shapes=[
            pltpu.VMEM((8, tq), F32),
            pltpu.VMEM((HEADS * VR, tq), F32),
        ],
    )
    return pl.pallas_call(
        functools.partial(_attn_body, tq=tq, tk=tq, ahead=ahead),
        grid_spec=grid_spec,
        out_shape=jax.ShapeDtypeStruct((t, GROUP_W), F32),
        compiler_params=_cparams(("arbitrary", "arbitrary")),
        name=name,
    )(jnp.asarray(ii), jnp.asarray(jj), qT, k, vT)


def _conv_silu(xpad, src, cw_ref, cb_ref, first, ls):
    width = xpad.shape[1]

    @pl.when(first)
    def _():
        xpad[0:8, :] = jnp.zeros((8, width), F32)

    @pl.when(jnp.logical_not(first))
    def _():
        xpad[0:8, :] = xpad[ls:ls + 8, :]

    xpad[8:ls + 8, :] = src
    acc = cb_ref[...] + cw_ref[0:1, :] * xpad[pl.ds(8 - CONV_K + 1, ls), :]
    for kk in range(1, CONV_K):
        acc = acc + cw_ref[kk:kk + 1, :] * xpad[pl.ds(8 - CONV_K + 1 + kk, ls), :]
    return acc * _sigmoid(acc)


def _ssd_body(zx_ref, sm_ref, cw_ref, cb_ref, dtb_ref, arow_ref, dexp_ref, ng_ref, tri_ref, hexp_ref,
              o_ref, xpad, state, *, ls):
    first = pl.program_id(1) == 0

    @pl.when(first)
    def _():
        state[...] = jnp.zeros(state.shape, F32)

    xc = _conv_silu(xpad, zx_ref[:, GROUP_W:GROUP_W + SSM_XBC], cw_ref, cb_ref, first, ls)
    dt_all = _softplus(sm_ref[...] + dtb_ref[...])
    da_all = dt_all * arow_ref[...]
    lane = lax.broadcasted_iota(jnp.int32, (1, 128), 1)
    half = lane < HD
    head_lanes = lane < HEADS
    hexp = hexp_ref[...].astype(BF16)
    tril = _tril(CHUNK)

    def expand(x):
        x = jnp.where(head_lanes, x, 0.0)
        hi = x.astype(BF16)
        lo = (x - hi.astype(F32)).astype(BF16)
        return (jnp.dot(hi, hexp, preferred_element_type=F32)
                + jnp.dot(lo, hexp, preferred_element_type=F32))

    for c in range(ls // CHUNK):
        r0 = c * CHUNK
        acum = jnp.dot(tri_ref[...], da_all[r0:r0 + CHUNK], precision=HIGHEST,
                       preferred_element_type=F32)
        acum_t = acum.T
        alast = acum[CHUNK - 1:CHUNK, :]
        ex = expand(jnp.concatenate([dt_all[r0:r0 + CHUNK], jnp.exp(acum), jnp.exp(alast - acum),
                                     jnp.exp(alast), jnp.zeros((7, 128), F32)], axis=0))
        dt_x = ex[0:CHUNK]
        ecol_x = ex[CHUNK:2 * CHUNK]
        dout_x = ex[2 * CHUNK:3 * CHUNK]
        cdec_x = ex[3 * CHUNK:3 * CHUNK + 1]
        for g in range(2):
            h0, h1 = 2 * g, 2 * g + 1
            gl = slice(g * 128, (g + 1) * 128)
            xg = xc[r0:r0 + CHUNK, gl]
            bg = xc[r0:r0 + CHUNK, GROUP_W + g * 128:GROUP_W + (g + 1) * 128]
            cg = xc[r0:r0 + CHUNK, 2 * GROUP_W + g * 128:2 * GROUP_W + (g + 1) * 128]
            cgb = cg.astype(BF16)
            gram = lax.dot_general(cgb, bg.astype(BF16), NT_DIMS, preferred_element_type=F32)
            xdt = xg * dt_x[:, gl]
            ydiag = jnp.zeros((CHUNK, 128), F32)
            for hh, msk in ((h0, half), (h1, jnp.logical_not(half))):
                dec = jnp.exp(jnp.where(tril, acum[:, hh:hh + 1] - acum_t[hh:hh + 1, :], -jnp.inf))
                ydiag = ydiag + jnp.dot((gram * dec).astype(BF16),
                                        jnp.where(msk, xdt, 0.0).astype(BF16),
                                        preferred_element_type=F32)
            st = state[g]
            yoff = jnp.dot(cgb, st.astype(BF16), preferred_element_type=F32) * ecol_x[:, gl]
            state[g] = (cdec_x[:, gl] * st
                        + jnp.dot(bg.T.astype(BF16), (xdt * dout_x[:, gl]).astype(BF16),
                                  preferred_element_type=F32))
            y = ydiag + yoff + xg * dexp_ref[:, g * 128:(g + 1) * 128]
            zg = zx_ref[r0:r0 + CHUNK, g * 128:(g + 1) * 128]
            y = y * (zg * _sigmoid(zg))
            o_ref[r0:r0 + CHUNK, g * 128:(g + 1) * 128] = _rms(y, ng_ref[:, g * 128:(g + 1) * 128])


def _ssd(main, cw, cb, dtb, arow, dexp, ng, tri, hexp, bsz, seq, ls=512):
    t = main.shape[0]
    ns = seq // ls

    def full(shape):
        return pl.BlockSpec(shape, lambda b, s: tuple(0 for _ in shape))

    return pl.pallas_call(
        functools.partial(_ssd_body, ls=ls),
        grid=(bsz, ns),
        in_specs=[
            pl.BlockSpec((ls, 1024), lambda b, s: (b * ns + s, COL_ZX // 1024)),
            pl.BlockSpec((ls, 128), lambda b, s: (b * ns + s, COL_SMALL // 128)),
            full((CONV_K, SSM_XBC)), full((1, SSM_XBC)), full((1, 128)), full((1, 128)),
            full((1, GROUP_W)), full((1, GROUP_W)), full((CHUNK, CHUNK)), full((128, GROUP_W)),
        ],
        out_specs=pl.BlockSpec((ls, GROUP_W), lambda b, s: (b * ns + s, 0)),
        out_shape=jax.ShapeDtypeStruct((t, GROUP_W), F32),
        scratch_shapes=[pltpu.VMEM((ls + 8, SSM_XBC), F32), pltpu.VMEM((2, SSM_STATE, 128), F32)],
        compiler_params=_cparams(("arbitrary", "arbitrary")),
        name="ssd",
    )(main, main, cw, cb, dtb, arow, dexp, ng, tri, hexp)


def _mlstm_body(c_ref, sm_ref, cw_ref, cb_ref, igb_ref, fgb_ref, ng_ref, tri_ref, seg_ref, segm_ref,
                bmask_ref, o_ref, xpad, cst, nst, mst, *, ls):
    first = pl.program_id(1) == 0

    @pl.when(first)
    def _():
        cst[...] = jnp.zeros(cst.shape, F32)
        nst[...] = jnp.zeros(nst.shape, F32)
        mst[...] = jnp.zeros(mst.shape, F32)

    qk = _conv_silu(xpad, c_ref[:, 0:2 * GROUP_W], cw_ref, cb_ref, first, ls)
    sm = sm_ref[...]
    ig_all = sm + igb_ref[...]
    fx = sm + fgb_ref[...]
    lf_all = jnp.minimum(fx, 0.0) - jnp.log1p(jnp.exp(-jnp.abs(fx)))
    lane = lax.broadcasted_iota(jnp.int32, (1, GROUP_W), 1)
    hms = [jnp.logical_and(lane >= h * HD, lane < (h + 1) * HD) for h in range(HEADS)]
    tril = _tril(CHUNK)
    lane_s = lax.broadcasted_iota(jnp.int32, (1, 128), 1)
    gate_lanes = jnp.logical_and(lane_s >= GATE_LANE, lane_s < GATE_LANE + HEADS)
    segb = seg_ref[...].astype(BF16)
    hexp = seg_ref[...].T.astype(BF16)
    segm_b = segm_ref[...].astype(BF16)
    zero_b = jnp.zeros((), BF16)
    n_chunks = ls // CHUNK

    def expand(x):
        x = jnp.where(gate_lanes, x, 0.0)
        hi = x.astype(BF16)
        lo = (x - hi.astype(F32)).astype(BF16)
        return (jnp.dot(hi, hexp, preferred_element_type=F32)
                + jnp.dot(lo, hexp, preferred_element_type=F32))

    local = []
    for c in range(n_chunks):
        r0 = c * CHUNK
        q = qk[r0:r0 + CHUNK, 0:GROUP_W]
        k = qk[r0:r0 + CHUNK, GROUP_W:2 * GROUP_W] * (HD ** -0.5)
        ig = ig_all[r0:r0 + CHUNK]
        bcs = jnp.dot(tri_ref[...], lf_all[r0:r0 + CHUNK], precision=HIGHEST,
                      preferred_element_type=F32)
        bcs_t = bcs.T
        ig_t = ig.T
        ig_al = pltpu.roll(ig, GATE_LANE - 4, 1)
        k_t = k.T.astype(BF16)
        qb = q.astype(BF16)
        vb = c_ref[r0:r0 + CHUNK, 2 * GROUP_W:3 * GROUP_W].astype(BF16)
        num_loc = jnp.zeros((CHUNK, GROUP_W), F32)
        m_loc_all = jnp.zeros((CHUNK, 128), F32)
        den_loc_all = jnp.zeros((CHUNK, 128), F32)
        for h in range(HEADS):
            bcol = bcs[:, GATE_LANE + h:GATE_LANE + h + 1]
            dmat = jnp.where(tril, bcol - bcs_t[GATE_LANE + h:GATE_LANE + h + 1, :] + ig_t[4 + h:5 + h, :],
                             -jnp.inf)
            m_loc = jnp.max(dmat, axis=1, keepdims=True)
            s_qk = jnp.dot(jnp.where(hms[h], qb, zero_b), k_t, preferred_element_type=F32)
            sc = s_qk * jnp.exp(dmat - m_loc)
            num_loc = num_loc + jnp.dot(sc.astype(BF16), jnp.where(hms[h], vb, zero_b),
                                        preferred_element_type=F32)
            on_lane = lane_s == GATE_LANE + h
            m_loc_all = jnp.where(on_lane, m_loc, m_loc_all)
            den_loc_all = jnp.where(on_lane, jnp.sum(sc, axis=1, keepdims=True), den_loc_all)
        blast = bcs[CHUNK - 1:CHUNK, :]
        gcol = blast - bcs + ig_al
        gmax = jnp.max(gcol, axis=0, keepdims=True)
        kw = k * expand(jnp.exp(gcol - gmax))
        kwv = bmask_ref[...] * jnp.dot(kw.T.astype(BF16), vb, preferred_element_type=F32)
        local.append((q, num_loc, bcs, m_loc_all, den_loc_all, blast, gmax, kwv,
                      jnp.sum(kw, axis=0, keepdims=True)))

    houts = []
    for c in range(n_chunks):
        q, num_loc, bcs, m_loc_all, den_loc_all, blast, gmax, kwv, ksum = local[c]
        c_old = cst[...]
        n_old = nst[0:1, :]
        m_prev = mst[0:1, :]
        q_c = jnp.dot(q.astype(BF16), c_old.astype(BF16), preferred_element_type=F32)
        q_n = jnp.dot((q * n_old).astype(BF16), segb, preferred_element_type=F32)
        inter = bcs + m_prev
        m_t = jnp.maximum(inter, m_loc_all)
        f = jnp.exp(m_loc_all - m_t)
        w_inter = jnp.exp(inter - m_t)
        den = f * den_loc_all + w_inter * q_n
        inv_den = 1.0 / jnp.maximum(jnp.abs(den), jnp.exp(-m_t))
        m_new = jnp.maximum(blast + m_prev, gmax)
        scal = jnp.concatenate([jnp.exp(gmax - m_new), jnp.exp(blast + m_prev - m_new),
                                jnp.zeros((6, 128), F32)], axis=0)
        ex = expand(jnp.concatenate([f, w_inter, inv_den, scal], axis=0))
        f_x = ex[0:CHUNK]
        w_inter_x = ex[CHUNK:2 * CHUNK]
        inv_den_x = ex[2 * CHUNK:3 * CHUNK]
        sfac_x = ex[3 * CHUNK:3 * CHUNK + 1]
        keep_x = ex[3 * CHUNK + 1:3 * CHUNK + 2]
        mst[0:1, :] = m_new
        cst[...] = keep_x * c_old + sfac_x * kwv
        nst[0:1, :] = keep_x * n_old + sfac_x * ksum
        houts.append((f_x * num_loc + w_inter_x * q_c) * inv_den_x)

    for c in range(n_chunks):
        r0 = c * CHUNK
        hh = houts[c] * _sigmoid(c_ref[r0:r0 + CHUNK, 3 * GROUP_W:4 * GROUP_W])
        sq = hh * hh
        sq_hi = sq.astype(BF16)
        sq_lo = (sq - sq_hi.astype(F32)).astype(BF16)
        ms = (jnp.dot(sq_hi, segm_b, preferred_element_type=F32)
              + jnp.dot(sq_lo, segm_b, preferred_element_type=F32))
        o_ref[r0:r0 + CHUNK, :] = hh * lax.rsqrt(ms + EPS) * ng_ref[...]


def _mlstm(main, cw, cb, igb, fgb, ng, tri, seg, segm, bmask, bsz, seq, ls=512):
    t = main.shape[0]
    ns = seq // ls

    def full(shape):
        return pl.BlockSpec(shape, lambda b, s: tuple(0 for _ in shape))

    return pl.pallas_call(
        functools.partial(_mlstm_body, ls=ls),
        grid=(bsz, ns),
        in_specs=[
            pl.BlockSpec((ls, 1024), lambda b, s: (b * ns + s, COL_ML // 1024)),
            pl.BlockSpec((ls, 128), lambda b, s: (b * ns + s, COL_SMALL // 128)),
            full((CONV_K, 2 * GROUP_W)), full((1, 2 * GROUP_W)), full((1, 128)), full((1, 128)),
            full((1, GROUP_W)), full((CHUNK, CHUNK)), full((GROUP_W, 128)), full((GROUP_W, GROUP_W)),
            full((GROUP_W, GROUP_W)),
        ],
        out_specs=pl.BlockSpec((ls, GROUP_W), lambda b, s: (b * ns + s, 0)),
        out_shape=jax.ShapeDtypeStruct((t, GROUP_W), F32),
        scratch_shapes=[
            pltpu.VMEM((ls + 8, 2 * GROUP_W), F32),
            pltpu.VMEM((GROUP_W, GROUP_W), F32),
            pltpu.VMEM((8, GROUP_W), F32),
            pltpu.VMEM((8, 128), F32),
        ],
        compiler_params=_cparams(("arbitrary", "arbitrary")),
        name="mlstm",
    )(main, main, cw, cb, igb, fgb, ng, tri, seg, segm, bmask)


def _outproj_body(x_ref, ya_ref, yb_ref, yc_ref, yd_ref, wo_ref, g2_ref, wr_ref, br_ref,
                  xn_ref, h2_ref, cb_ref):
    sub = x_ref.shape[0] // 2
    accs = []
    for s in range(2):
        rows = slice(s * sub, (s + 1) * sub)
        acc = x_ref[rows, :]
        for gi, y_ref in enumerate((ya_ref, yb_ref, yc_ref, yd_ref)):
            acc = acc + jnp.dot(y_ref[rows, :].astype(BF16), wo_ref[gi * GROUP_W:(gi + 1) * GROUP_W, :],
                                preferred_element_type=F32)
        accs.append(acc)
    for s in range(2):
        _route(accs[s], slice(s * sub, (s + 1) * sub), g2_ref, wr_ref, br_ref, xn_ref, h2_ref, cb_ref)


def _route(acc, rows, g2_ref, wr_ref, br_ref, xn_ref, h2_ref, cb_ref):
    xn_ref[rows, :] = acc
    h2 = _rms(acc, g2_ref[...])
    h_hi = h2.astype(BF16)
    h2_ref[rows, :] = h_hi
    h_lo = (h2 - h_hi.astype(F32)).astype(BF16)
    hh = jnp.dot(h_hi, wr_ref[...], preferred_element_type=F32)
    lh = jnp.dot(h_lo, wr_ref[...], preferred_element_type=F32)
    logits = hh[:, :128] + hh[:, 128:] + lh[:, :128] + br_ref[...]
    lane = lax.broadcasted_iota(jnp.int32, logits.shape, 1)
    big = jnp.int32(1 << 20)
    n_grp = N_EXPERTS // EXPERTS_PER_GROUP
    gl = jnp.where(jnp.logical_and(lane >= N_EXPERTS, lane < N_EXPERTS + n_grp), logits, -jnp.inf)
    gmax = jnp.max(gl, axis=1, keepdims=True)
    g_w = 1.0 / jnp.sum(jnp.exp(gl - gmax), axis=1, keepdims=True)
    gidx = jnp.min(jnp.where(gl == gmax, lane, big), axis=1, keepdims=True) - N_EXPERTS
    in_grp = jnp.logical_and(lane < N_EXPERTS,
                             lax.shift_right_logical(lane, int(math.log2(EXPERTS_PER_GROUP))) == gidx)
    el = jnp.where(in_grp, logits, -jnp.inf)
    emax = jnp.max(el, axis=1, keepdims=True)
    esum = jnp.sum(jnp.exp(el - emax), axis=1, keepdims=True)
    i1 = jnp.min(jnp.where(el == emax, lane, big), axis=1, keepdims=True)
    el2 = jnp.where(lane == i1, -jnp.inf, el)
    emax2 = jnp.max(el2, axis=1, keepdims=True)
    i2 = jnp.min(jnp.where(el2 == emax2, lane, big), axis=1, keepdims=True)
    p1 = 1.0 / esum
    p2 = jnp.exp(emax2 - emax) / esum
    w1 = p1 / (p1 + p2)
    w2 = p2 / (p1 + p2)
    comb = g_w * (jnp.where(lane == i1, w1, 0.0) + jnp.where(lane == i2, w2, 0.0))
    cb_ref[rows, :] = jnp.where(lane == GID_LANE, gidx.astype(F32), comb)


def _outproj(x2, ya, yb, yc, yd, wo, g2, wr, br, tm=512):
    t = x2.shape[0]
    row = pl.BlockSpec((tm, D_MODEL), lambda i: (i, 0))
    grp = pl.BlockSpec((tm, GROUP_W), lambda i: (i, 0))

    def full(shape):
        return pl.BlockSpec(shape, lambda i: tuple(0 for _ in shape))

    return pl.pallas_call(
        _outproj_body,
        grid=(t // tm,),
        in_specs=[row, grp, grp, grp, grp, full((D_MODEL, D_MODEL)), full((1, D_MODEL)),
                  full((D_MODEL, 256)), full((1, 128))],
        out_specs=[row, row, pl.BlockSpec((tm, 128), lambda i: (i, 0))],
        out_shape=[
            jax.ShapeDtypeStruct((t, D_MODEL), F32),
            jax.ShapeDtypeStruct((t, D_MODEL), BF16),
            jax.ShapeDtypeStruct((t, 128), F32),
        ],
        compiler_params=_cparams(("arbitrary",)),
        name="outproj_router",
    )(x2, ya, yb, yc, yd, wo, g2, wr, br)


MOE_EB = EXPERTS_PER_GROUP
MOE_RB = 128


def _moe_body(h_ref, cb_ref, xn_ref, tri_ref, wg_ref, wu_ref, wd_ref, gf_ref, o_ref,
              perm_t, hs, cs, ys, seg_smem, *, final_norm, rb):
    g = pl.program_id(1)
    tm = h_ref.shape[0]
    n_grp = N_EXPERTS // EXPERTS_PER_GROUP
    lane = lax.broadcasted_iota(jnp.int32, (tm, 128), 1)

    @pl.when(g == 0)
    def _():
        cb = cb_ref[...]
        gid = jnp.sum(jnp.where(lane == GID_LANE, cb, 0.0), axis=1, keepdims=True).astype(jnp.int32)
        onehot = jnp.where(lane == gid, 1.0, 0.0)
        oh16 = onehot.astype(BF16)
        tot = jnp.zeros((1, 128), F32)
        parts = []
        for r0 in range(0, tm, CHUNK):
            part = jnp.dot(tri_ref[...], oh16[r0:r0 + CHUNK], preferred_element_type=F32) + tot
            parts.append(part)
            tot = part[CHUNK - 1:CHUNK, :]
        cum = jnp.concatenate(parts, axis=0)
        lane1 = lane[0:1, :]
        base = jnp.zeros((1, 128), F32)
        run = jnp.zeros((1, 1), F32)
        for k in range(n_grp):
            base = base + jnp.where(lane1 == k, run, 0.0)
            seg_smem[k] = run[0, 0].astype(jnp.int32)
            seg_smem[n_grp + k] = tot[0, k].astype(jnp.int32)
            run = run + tot[:, k:k + 1]
        pos = jnp.sum(onehot * (base + cum - 1.0), axis=1, keepdims=True).astype(jnp.int32)
        pos_row = jnp.broadcast_to(pos.astype(F32), (tm, 128)).T[0:1, :].astype(jnp.int32)
        perm_t[...] = jnp.where(lax.broadcasted_iota(jnp.int32, (tm, tm), 1) == pos, 1.0, 0.0).astype(BF16)
        p = jnp.where(lax.broadcasted_iota(jnp.int32, (tm, tm), 0) == pos_row, 1.0, 0.0).astype(BF16)
        hs[...] = jnp.dot(p, h_ref[...], preferred_element_type=F32).astype(BF16)
        cb_hi = cb.astype(BF16)
        cb_lo = (cb - cb_hi.astype(F32)).astype(BF16)
        cs2 = jnp.dot(p, jnp.concatenate([cb_hi, cb_lo], axis=1), preferred_element_type=F32)
        cs[...] = cs2[:, :128] + cs2[:, 128:]
        ys[...] = jnp.zeros(ys.shape, F32)

    seg_lo = seg_smem[g]
    seg_hi = seg_lo + seg_smem[n_grp + g]
    lane_rb = lax.broadcasted_iota(jnp.int32, (rb, 128), 1)

    def run_block(lo):
        x = hs[lo:lo + rb, :]
        cw = cs[lo:lo + rb, :]

        def gate_up(j):
            return (jnp.dot(x, wg_ref[j], preferred_element_type=F32),
                    jnp.dot(x, wu_ref[j], preferred_element_type=F32))

        def down(j, gu):
            col = jnp.sum(jnp.where(lane_rb == g * MOE_EB + j, cw, 0.0), axis=1,
                          keepdims=True)
            gt, up = gu
            hid = (gt * _sigmoid(gt)) * up * col
            return jnp.dot(hid.astype(BF16), wd_ref[j], preferred_element_type=F32)

        total = jnp.zeros((rb, D_MODEL), F32)
        gu = gate_up(0)
        for j in range(MOE_EB):
            gu_next = gate_up(j + 1) if j + 1 < MOE_EB else None
            total = total + down(j, gu)
            gu = gu_next
        ys[lo:lo + rb, :] += total

    for b in range(tm // rb):
        lo = b * rb
        pl.when(jnp.logical_and(seg_lo < lo + rb, seg_hi > lo))(functools.partial(run_block, lo))

    @pl.when(g == n_grp - 1)
    def _():
        y = xn_ref[...] + jnp.dot(perm_t[...], ys[...].astype(BF16), preferred_element_type=F32)
        if final_norm:
            y = _rms(y, gf_ref[...])
        o_ref[...] = y


def _moe(h2, comb, xn, tri, wg, wu, wd, gf, final_norm, tm=1024, rb=MOE_RB):
    t = h2.shape[0]
    return pl.pallas_call(
        functools.partial(_moe_body, final_norm=final_norm, rb=rb),
        grid=(t // tm, N_EXPERTS // MOE_EB),
        in_specs=[
            pl.BlockSpec((tm, D_MODEL), lambda i, e: (i, 0)),
            pl.BlockSpec((tm, 128), lambda i, e: (i, 0)),
            pl.BlockSpec((tm, D_MODEL), lambda i, e: (i, 0)),
            pl.BlockSpec((CHUNK, CHUNK), lambda i, e: (0, 0)),
            pl.BlockSpec((MOE_EB, D_MODEL, D_EXPERT), lambda i, e: (e, 0, 0)),
            pl.BlockSpec((MOE_EB, D_MODEL, D_EXPERT), lambda i, e: (e, 0, 0)),
            pl.BlockSpec((MOE_EB, D_EXPERT, D_MODEL), lambda i, e: (e, 0, 0)),
            pl.BlockSpec((1, D_MODEL), lambda i, e: (0, 0)),
        ],
        out_specs=pl.BlockSpec((tm, D_MODEL), lambda i, e: (i, 0)),
        out_shape=jax.ShapeDtypeStruct((t, D_MODEL), F32),
        scratch_shapes=[
            pltpu.VMEM((tm, tm), BF16),
            pltpu.VMEM((tm, D_MODEL), BF16),
            pltpu.VMEM((tm, 128), F32),
            pltpu.VMEM((tm, D_MODEL), F32),
            pltpu.SMEM((2 * (N_EXPERTS // MOE_EB),), jnp.int32),
        ],
        compiler_params=_cparams(("arbitrary", "arbitrary")),
        name="moe",
    )(h2, comb, xn, tri, wg, wu, wd, gf)


def _spread_heads(w, width, slab, total):
    rows = w.shape[0]
    pieces = []
    for h in range(HEADS):
        pieces.append(w[:, h * width:(h + 1) * width])
        pieces.append(jnp.zeros((rows, slab - width), w.dtype))
    if total > HEADS * slab:
        pieces.append(jnp.zeros((rows, total - HEADS * slab), w.dtype))
    return jnp.concatenate(pieces, axis=1)


def _pad_heads(w, width):
    return _spread_heads(w, width, HP, HEADS * HP)


def _v_cols(w):
    return _spread_heads(w, HD, VR, V_COLS)


def _lane_row(vals, offset, width=128):
    return jnp.zeros((1, width), F32).at[0, offset:offset + vals.shape[0]].set(vals.astype(F32))


def _layer_weights(l, p):
    half = MLA_ROPE // 2
    w = {"g_mix": p["norm_mix_g"][l][None, :]}
    w["ssm_cw"] = p["ssm_conv_w"][l]
    w["ssm_cb"] = p["ssm_conv_b"][l][None, :]
    w["ssm_dtb"] = _lane_row(p["ssm_dt_bias"][l], 0)
    w["ssm_arow"] = _lane_row(-jnp.exp(p["ssm_a_log"][l].astype(F32)), 0)
    w["ssm_dexp"] = jnp.repeat(p["ssm_d"][l].astype(F32), HD)[None, :]
    w["ssm_ng"] = p["ssm_norm_g"][l][None, :]
    w["ml_cw"] = p["ml_conv_w"][l]
    w["ml_cb"] = p["ml_conv_b"][l][None, :]
    w["ml_igb"] = _lane_row(p["ml_ig_bias"][l], 4)
    w["ml_fgb"] = _lane_row(p["ml_fg_bias"][l], 8)
    w["ml_ng"] = p["ml_norm_g"][l][None, :]
    wuq = p["mla_w_uq"][l]
    qd = MLA_NOPE + MLA_ROPE
    swapped = []
    for h in range(HEADS):
        c0 = h * qd + MLA_NOPE
        swapped += [jnp.zeros((256, MLA_NOPE), F32), wuq[:, c0 + half:c0 + 2 * half], wuq[:, c0:c0 + half]]
    wq = _spread_heads(wuq, qd, HP, HEADS * HP)
    wqs = _spread_heads(jnp.concatenate(swapped, axis=1), qd, HP, HEADS * HP)
    w["mla_wq2"] = jnp.concatenate([wq, wqs], axis=1).astype(BF16)
    wukv = p["mla_w_ukv"][l]
    wk_cols = jnp.concatenate([wukv[:, h * 2 * HD:h * 2 * HD + MLA_NOPE] for h in range(HEADS)], axis=1)
    wv_cols = jnp.concatenate([wukv[:, h * 2 * HD + MLA_NOPE:(h + 1) * 2 * HD] for h in range(HEADS)], axis=1)
    w["mla_wk"] = _pad_heads(wk_cols, MLA_NOPE).astype(BF16)
    w["mla_wv"] = _v_cols(wv_cols).astype(BF16)
    w["mla_gq"] = p["mla_q_norm_g"][l][None, :]
    w["mla_gkv"] = p["mla_kv_norm_g"][l][None, :]
    w["w_out"] = p["w_out"][l].astype(BF16)
    w["g_ffn"] = p["norm_ffn_g"][l][None, :]
    wr = jnp.zeros((D_MODEL, 128), F32)
    wr = wr.at[:, :N_EXPERTS].set(p["moe_w_router"][l]).at[:, N_EXPERTS:N_EXPERTS + 4].set(p["moe_w_group"][l])
    wr_hi = wr.astype(BF16)
    w["w_router"] = jnp.concatenate([wr_hi, (wr - wr_hi.astype(F32)).astype(BF16)], axis=1)
    w["b_router"] = _lane_row(p["moe_b_router"][l], 0) + _lane_row(p["moe_b_group"][l], N_EXPERTS)
    w["w_gate"] = p["moe_w_gate"][l].astype(BF16)
    w["w_up"] = p["moe_w_up"][l].astype(BF16)
    w["w_down"] = p["moe_w_down"][l].astype(BF16)
    return w


def kernel(x, positions, norm_mix_g, w_in, ssm_conv_w, ssm_conv_b, ssm_dt_bias, ssm_a_log, ssm_d, ssm_norm_g, ml_conv_w, ml_conv_b, ml_ig_bias, ml_fg_bias, ml_norm_g, mla_q_norm_g, mla_w_uq, mla_kv_norm_g, mla_w_ukv, w_out, norm_ffn_g, moe_w_group, moe_b_group, moe_w_router, moe_b_router, moe_w_gate, moe_w_up, moe_w_down, final_norm_g):
    p = dict(norm_mix_g=norm_mix_g, w_all=_relayout_w_in(w_in), ssm_conv_w=ssm_conv_w, ssm_conv_b=ssm_conv_b,
             ssm_dt_bias=ssm_dt_bias, ssm_a_log=ssm_a_log, ssm_d=ssm_d, ssm_norm_g=ssm_norm_g,
             ml_conv_w=ml_conv_w, ml_conv_b=ml_conv_b, ml_ig_bias=ml_ig_bias, ml_fg_bias=ml_fg_bias,
             ml_norm_g=ml_norm_g, mla_q_norm_g=mla_q_norm_g, mla_w_uq=mla_w_uq,
             mla_kv_norm_g=mla_kv_norm_g, mla_w_ukv=mla_w_ukv, w_out=w_out, norm_ffn_g=norm_ffn_g,
             moe_w_group=moe_w_group, moe_b_group=moe_b_group, moe_w_router=moe_w_router,
             moe_b_router=moe_b_router, moe_w_gate=moe_w_gate, moe_w_up=moe_w_up, moe_w_down=moe_w_down)
    bsz, seq, _ = x.shape
    depth = w_in.shape[0]
    x2 = x.reshape(bsz * seq, D_MODEL)

    half = MLA_ROPE // 2
    inv = ROPE_THETA ** (-jnp.arange(0, MLA_ROPE, 2, dtype=F32) / MLA_ROPE)
    inv_col = jnp.concatenate([inv, inv])[:, None]
    sgn_col = jnp.concatenate([-jnp.ones((half,), F32), jnp.ones((half,), F32)])[:, None]
    pe_mask = _lane_row(jnp.ones((MLA_ROPE,), F32), MLA_NOPE)
    c_tok, s_tok, c_tr, s_tr = _rope_tables(positions.reshape(bsz, 1, seq), inv_col, sgn_col, bsz, seq)
    tri = jnp.tril(jnp.ones((CHUNK, CHUNK), F32))
    tri_moe = tri.astype(BF16)
    head_of = jnp.arange(GROUP_W) // HD
    seg = (head_of[:, None] + GATE_LANE == jnp.arange(128)[None, :]).astype(F32)
    ssd_hexp = (jnp.arange(128)[:, None] == head_of[None, :]).astype(F32)
    bmask = (head_of[:, None] == head_of[None, :]).astype(F32)
    segm = bmask / HD

    for l in range(depth):
        w = _layer_weights(l, p)
        main, k_a, qT_a, vT_a, qT_d, k_d, vT_d = _in_proj(
            x2, w["g_mix"], p["w_all"], l, c_tok, s_tok, c_tr, s_tr, w["mla_gq"], w["mla_wq2"], w["mla_gkv"],
            w["mla_wk"], w["mla_wv"], pe_mask, bsz, seq)
        y_a = _attention(qT_a, k_a, vT_a, bsz, seq, "attn_moba")
        y_b = _ssd(main, w["ssm_cw"], w["ssm_cb"], w["ssm_dtb"], w["ssm_arow"], w["ssm_dexp"], w["ssm_ng"],
                   tri, ssd_hexp, bsz, seq)
        y_c = _mlstm(main, w["ml_cw"], w["ml_cb"], w["ml_igb"], w["ml_fgb"], w["ml_ng"], tri, seg, segm,
                     bmask, bsz, seq)
        y_d = _attention(qT_d, k_d, vT_d, bsz, seq, "attn_mla")
        xn, h2, comb = _outproj(x2, y_a, y_b, y_c, y_d, w["w_out"], w["g_ffn"], w["w_router"], w["b_router"])
        x2 = _moe(h2, comb, xn, tri_moe, w["w_gate"], w["w_up"], w["w_down"], final_norm_g[None, :],
                  final_norm=(l == depth - 1), tm=MOE_TM)
    return x2.reshape(bsz, seq, D_MODEL)
```

```python
import functools
import math

import numpy as np
import jax
import jax.numpy as jnp
from jax import lax
from jax.experimental import pallas as pl
from jax.experimental.pallas import tpu as pltpu

F32 = jnp.float32
BF16 = jnp.bfloat16
HIGHEST = lax.Precision.HIGHEST

EPS = 1e-6
NEG = -1e30
LOG2E = 1.4426950408889634

D_MODEL = 1024
GROUP_W = 256
HEADS = 4
HD = 64
HP = 128
MOBA_BLOCK = 256
MOBA_TOPK = 3
SSM_STATE = 128
SSM_XBC = 768
CONV_K = 4
CHUNK = 128
MLA_NOPE = 64
MLA_ROPE = 32
ROPE_THETA = 10000.0
N_EXPERTS = 16
EXPERTS_PER_GROUP = 4
D_EXPERT = 256
GATE_LANE = 8
GID_LANE = 20
MOE_TM = 1024

COL_ZX = 0
COL_ML = 1024
COL_SMALL = 2048
N_MAIN = 2176
VR = 80
V_COLS = 384
N_ALL = N_MAIN + 256 + 128 + 3 * GROUP_W
ATT_UQ = 256
ATT_UK = 256

V7X_VMEM_LIMIT = 56 * 1024 * 1024

NT_DIMS = (((1,), (1,)), ((), ()))


def _cparams(sem):
    return pltpu.CompilerParams(dimension_semantics=sem, vmem_limit_bytes=V7X_VMEM_LIMIT)


def _rms(x, g):
    ms = jnp.mean(x * x, axis=-1, keepdims=True)
    return x * lax.rsqrt(ms + EPS) * g


def _sigmoid(x):
    return 0.5 * jnp.tanh(0.5 * x) + 0.5


def _softplus(x):
    return jnp.maximum(x, 0.0) + jnp.log1p(jnp.exp(-jnp.abs(x)))


def _tril(n):
    r = lax.broadcasted_iota(jnp.int32, (n, n), 0)
    c = lax.broadcasted_iota(jnp.int32, (n, n), 1)
    return c <= r


W_IN_SIZES = (3 * GROUP_W, GROUP_W, SSM_XBC, HEADS, 2 * GROUP_W, GROUP_W, HEADS, HEADS, GROUP_W, 256, 128,
              MLA_ROPE)


def _relayout_body(w_ref, o_ref):
    w = w_ref[0]
    rows = w.shape[0]
    offs = np.cumsum((0,) + W_IN_SIZES)
    a_qkv, s_z, s_xbc, s_dt, m_qk, m_v, m_i, m_f, m_o, d_cq, d_ckv, d_kr = [
        w[:, int(offs[i]):int(offs[i + 1])] for i in range(len(W_IN_SIZES))]
    half = MLA_ROPE // 2

    def zeros(n):
        return jnp.zeros((rows, n), F32)

    pieces = [s_z, s_xbc, m_qk, m_v, m_o,
              s_dt, m_i, m_f, zeros(64 - 3 * HEADS), d_kr, d_kr[:, half:], d_kr[:, :half],
              d_cq, d_ckv,
              a_qkv[:, GROUP_W:2 * GROUP_W], a_qkv[:, :GROUP_W], a_qkv[:, 2 * GROUP_W:]]
    o_ref[0] = jnp.concatenate(pieces, axis=1).astype(BF16)


def _relayout_w_in(w, rows=256):
    nl, r, c = w.shape
    return pl.pallas_call(
        _relayout_body,
        grid=(nl, r // rows),
        in_specs=[pl.BlockSpec((1, rows, c), lambda l, i: (l, i, 0))],
        out_specs=pl.BlockSpec((1, rows, N_ALL), lambda l, i: (l, i, 0)),
        out_shape=jax.ShapeDtypeStruct((nl, r, N_ALL), BF16),
        compiler_params=_cparams(("arbitrary", "arbitrary")),
        name="relayout_w_in",
    )(w)


def _ones_rows(shape):
    row = lax.broadcasted_iota(jnp.int32, shape, 0)
    hit = row == HD
    for h in range(1, HEADS):
        hit = jnp.logical_or(hit, row == h * VR + HD)
    return jnp.where(hit, 1.0, 0.0)


def _in_proj_body(x_ref, g_ref, w_ref, c_ref, s_ref, ct_ref, st_ref, gq_ref, wq_ref, gkv_ref, wk_ref,
                  wv_ref, msk_ref, om_ref, ka_ref, qa_ref, va_ref, qd_ref, kd_ref, vd_ref, km_sc,
                  *, nb, nst, qscale_a, qscale_d):
    s_idx = pl.program_id(0) % nst
    tm = x_ref.shape[0]

    @pl.when(s_idx == 0)
    def _():
        km_sc[...] = jnp.zeros(km_sc.shape, F32)

    h = _rms(x_ref[...], g_ref[...]).astype(BF16)
    rest = jnp.dot(h, w_ref[0, :, COL_SMALL:], preferred_element_type=F32)
    for c in range(0, COL_SMALL, 512):
        om_ref[:, c:c + 512] = jnp.dot(h, w_ref[0, :, c:c + 512], preferred_element_type=F32)
    sm = rest[:, 0:128]
    om_ref[:, COL_SMALL:COL_SMALL + 128] = sm
    cq = rest[:, 128:384]
    ckv = rest[:, 384:512]
    kp = _spread_heads(rest[:, 512:768], HD, HP, HEADS * HP)
    qp = _spread_heads(rest[:, 768:1024], HD, HP, HEADS * HP)
    vc = _spread_heads(rest[:, 1024:1280], HD, VR, V_COLS)

    nblk = tm // MOBA_BLOCK
    lane_k = lax.broadcasted_iota(jnp.int32, (MOBA_BLOCK, HEADS * HP), 1)
    for j in range(nblk):
        kj = kp[j * MOBA_BLOCK:(j + 1) * MOBA_BLOCK]
        blk_id = s_idx * nblk + j
        km_sc[pl.ds(blk_id, 1), :] = jnp.mean(kj, axis=0, keepdims=True)
        ka_ref[j * MOBA_BLOCK:(j + 1) * MOBA_BLOCK, :] = jnp.where(
            (lane_k & (HP - 1)) == HD + blk_id, 1.0, kj).astype(BF16)

    qT = qp.T
    km = km_sc[...]
    pos = s_idx * tm + lax.broadcasted_iota(jnp.int32, (nb, tm), 1)
    cur = lax.shift_right_logical(pos, int(math.log2(MOBA_BLOCK)))
    blk = lax.broadcasted_iota(jnp.int32, (nb, tm), 0)
    past = blk < cur
    for hd in range(HEADS):
        qh = qT[hd * HP:(hd + 1) * HP, :]
        gate = jnp.dot(km[:, hd * HP:(hd + 1) * HP], qh, precision=HIGHEST,
                       preferred_element_type=F32)
        gate = jnp.where(past, gate, -jnp.inf)
        bias = jnp.where(blk == cur, 0.0, NEG)
        for _ in range(MOBA_TOPK):
            m = jnp.max(gate, axis=0, keepdims=True)
            cand = jnp.logical_and(gate == m, gate > -jnp.inf)
            idx = jnp.min(jnp.where(cand, blk, nb), axis=0, keepdims=True)
            pick = blk == idx
            bias = jnp.where(pick, 0.0, bias)
            gate = jnp.where(pick, -jnp.inf, gate)
        rows = [qh[:HD] * qscale_a, bias]
        if HP - HD - nb:
            rows.append(jnp.zeros((HP - HD - nb, tm), F32))
        qa_ref[0, hd * HP:(hd + 1) * HP, :] = jnp.concatenate(rows, axis=0).astype(BF16)
    vt = vc.T[:HEADS * VR]
    va_ref[0] = (vt + _ones_rows(vt.shape)).astype(BF16)

    qn = _rms(cq, gq_ref[...]).astype(BF16)
    q2 = jnp.dot(qn, wq_ref[...], preferred_element_type=F32).T
    nq = HEADS * HP
    ct4 = jnp.concatenate([ct_ref[0]] * HEADS, axis=0)
    st4 = jnp.concatenate([st_ref[0]] * HEADS, axis=0)
    qd_ref[0] = ((q2[:nq] * ct4 + q2[nq:] * st4) * qscale_d).astype(BF16)
    kvn = _rms(ckv, gkv_ref[...]).astype(BF16)
    kn = jnp.dot(kvn, wk_ref[...], preferred_element_type=F32)
    pe = (sm * c_ref[...] + pltpu.roll(sm, HP - MLA_ROPE, 1) * s_ref[...]) * msk_ref[...]
    kd_ref[...] = (kn + jnp.concatenate([pe] * HEADS, axis=1)).astype(BF16)
    vtd = jnp.dot(kvn, wv_ref[...], preferred_element_type=F32).T[:HEADS * VR]
    vd_ref[0] = (vtd + _ones_rows(vtd.shape)).astype(BF16)


def _in_proj(x2, g, w_all, layer, c_tok, s_tok, c_tr, s_tr, gq, wq2, gkv, wk, wv, pe_mask, bsz, seq, tm=512):
    t = x2.shape[0]
    nst = seq // tm
    nb = seq // MOBA_BLOCK
    assert nb <= HP - HD
    tok128 = pl.BlockSpec((tm, HP), lambda i: (i, 0))
    tr128 = pl.BlockSpec((1, HP, tm), lambda i: (i // nst, 0, i % nst))
    tr_q = pl.BlockSpec((1, HEADS * HP, tm), lambda i: (i // nst, 0, i % nst))
    tr_v = pl.BlockSpec((1, HEADS * VR, tm), lambda i: (i // nst, 0, i % nst))
    tok_k = pl.BlockSpec((tm, HEADS * HP), lambda i: (i, 0))

    def full(shape):
        return pl.BlockSpec(shape, lambda i: tuple(0 for _ in shape))

    q_shape = jax.ShapeDtypeStruct((bsz, HEADS * HP, seq), BF16)
    k_shape = jax.ShapeDtypeStruct((t, HEADS * HP), BF16)
    v_shape = jax.ShapeDtypeStruct((bsz, HEADS * VR, seq), BF16)
    return pl.pallas_call(
        functools.partial(_in_proj_body, nb=nb, nst=nst, qscale_a=(HD ** -0.5) * LOG2E,
                          qscale_d=((MLA_NOPE + MLA_ROPE) ** -0.5) * LOG2E),
        grid=(t // tm,),
        in_specs=[
            pl.BlockSpec((tm, D_MODEL), lambda i: (i, 0)),
            full((1, D_MODEL)), pl.BlockSpec((1, D_MODEL, N_ALL), lambda i: (layer, 0, 0)),
            tok128, tok128, tr128, tr128,
            full((1, 256)), full((256, 2 * HEADS * HP)), full((1, 128)),
            full((128, HEADS * HP)), full((128, V_COLS)), full((1, HP)),
        ],
        out_specs=[pl.BlockSpec((tm, N_MAIN), lambda i: (i, 0)), tok_k, tr_q, tr_v, tr_q, tok_k, tr_v],
        out_shape=[jax.ShapeDtypeStruct((t, N_MAIN), F32), k_shape, q_shape, v_shape,
                   q_shape, k_shape, v_shape],
        scratch_shapes=[pltpu.VMEM((nb, HEADS * HP), F32)],
        compiler_params=_cparams(("arbitrary",)),
        name="in_proj",
    )(x2, g, w_all, c_tok, s_tok, c_tr, s_tr, gq, wq2, gkv, wk, wv, pe_mask)


def _rope_body(pos_ref, inv_ref, sgn_ref, c_ref, s_ref, ct_ref, st_ref):
    ts = pos_ref.shape[2]
    ang = inv_ref[...] * pos_ref[0].astype(F32)
    c32 = jnp.cos(ang)
    s32 = jnp.sin(ang) * sgn_ref[...]
    ct = jnp.concatenate([jnp.ones((MLA_NOPE, ts), F32), c32,
                          jnp.ones((HP - MLA_NOPE - MLA_ROPE, ts), F32)], axis=0)
    st = jnp.concatenate([jnp.zeros((MLA_NOPE, ts), F32), s32,
                          jnp.zeros((HP - MLA_NOPE - MLA_ROPE, ts), F32)], axis=0)
    ct_ref[0] = ct
    st_ref[0] = st
    c_ref[...] = ct.T
    s_ref[...] = st.T


def _rope_tables(pos3, inv_col, sgn_col, bsz, seq, ts=512):
    t = bsz * seq
    nst = seq // ts
    tok = pl.BlockSpec((ts, HP), lambda i: (i, 0))
    tr = pl.BlockSpec((1, HP, ts), lambda i: (i // nst, 0, i % nst))
    col = pl.BlockSpec((MLA_ROPE, 1), lambda i: (0, 0))
    return pl.pallas_call(
        _rope_body,
        grid=(t // ts,),
        in_specs=[pl.BlockSpec((1, 1, ts), lambda i: (i // nst, 0, i % nst)), col, col],
        out_specs=[tok, tok, tr, tr],
        out_shape=[
            jax.ShapeDtypeStruct((t, HP), F32),
            jax.ShapeDtypeStruct((t, HP), F32),
            jax.ShapeDtypeStruct((bsz, HP, seq), F32),
            jax.ShapeDtypeStruct((bsz, HP, seq), F32),
        ],
        compiler_params=_cparams(("arbitrary",)),
        name="rope_tables",
    )(pos3, inv_col, sgn_col)


def _attn_body(ii_ref, jj_ref, qT_ref, k_ref, vT_ref, o_ref, m_sc, acc_sc, *, tq, tk, ahead):
    p = pl.program_id(1)
    i = ii_ref[p]
    j = jj_ref[p]

    @pl.when(j == 0)
    def _():
        m_sc[...] = jnp.full(m_sc.shape, NEG, F32)
        acc_sc[...] = jnp.zeros(acc_sc.shape, F32)

    def scores(u):
        h, qc, kb, _ = u
        return jnp.dot(k_ref[kb * ATT_UK:(kb + 1) * ATT_UK, h * HP:(h + 1) * HP],
                       qT_ref[0, h * HP:(h + 1) * HP, qc * ATT_UQ:(qc + 1) * ATT_UQ],
                       preferred_element_type=F32)

    def update(u, s):
        h, qc, kb, masked = u
        if masked:
            kpos = kb * ATT_UK + lax.broadcasted_iota(jnp.int32, s.shape, 0)
            qpos = qc * ATT_UQ + lax.broadcasted_iota(jnp.int32, s.shape, 1)
            s = jnp.where(kpos <= qpos, s, NEG)
        alphas, pms = [], []
        for c in range(ATT_UQ // 128):
            ql = slice(qc * ATT_UQ + c * 128, qc * ATT_UQ + (c + 1) * 128)
            sc = s[:, c * 128:(c + 1) * 128]
            m_old = m_sc[h:h + 1, ql]
            m_new = jnp.maximum(m_old, jnp.max(sc, axis=0, keepdims=True))
            alphas.append(jnp.exp2(m_old - m_new))
            pms.append(jnp.exp2(sc - m_new).astype(BF16))
            m_sc[h:h + 1, ql] = m_new
        ql = slice(qc * ATT_UQ, (qc + 1) * ATT_UQ)
        rows = slice(h * VR, (h + 1) * VR)
        acc_sc[rows, ql] = jnp.concatenate(alphas, axis=1) * acc_sc[rows, ql] + jnp.dot(
            vT_ref[0, rows, kb * ATT_UK:(kb + 1) * ATT_UK], jnp.concatenate(pms, axis=1),
            preferred_element_type=F32)

    def run(units):
        pending = [scores(u) for u in units[:ahead]]
        for n, u in enumerate(units):
            s = pending.pop(0)
            if n + ahead < len(units):
                pending.append(scores(units[n + ahead]))
            update(u, s)

    def unit_list(diag):
        units = []
        for kb in range(tk // ATT_UK):
            for h in range(HEADS):
                for qc in range(tq // ATT_UQ):
                    k_lo, k_hi = kb * ATT_UK, (kb + 1) * ATT_UK - 1
                    q_lo, q_hi = qc * ATT_UQ, (qc + 1) * ATT_UQ - 1
                    if diag and k_lo > q_hi:
                        continue
                    units.append((h, qc, kb, diag and k_hi > q_lo))
        return units

    @pl.when(j < i)
    def _():
        run(unit_list(False))

    @pl.when(j == i)
    def _():
        run(unit_list(True))
        parts = []
        for h in range(HEADS):
            den = acc_sc[h * VR + HD:h * VR + HD + 1, :]
            parts.append(acc_sc[h * VR:h * VR + HD, :] * (1.0 / den))
        o_ref[...] = jnp.concatenate(parts, axis=0).T


def _attention(qT, k, vT, bsz, seq, name, tq=1024, ahead=5):
    t = k.shape[0]
    nq = seq // tq
    ii = np.concatenate([np.full(i + 1, i, np.int32) for i in range(nq)])
    jj = np.concatenate([np.arange(i + 1, dtype=np.int32) for i in range(nq)])
    grid_spec = pltpu.PrefetchScalarGridSpec(
        num_scalar_prefetch=2,
        grid=(bsz, len(ii)),
        in_specs=[
            pl.BlockSpec((1, HEADS * HP, tq), lambda b, p, ii, jj: (b, 0, ii[p])),
            pl.BlockSpec((tq, HEADS * HP), lambda b, p, ii, jj: (b * nq + jj[p], 0)),
            pl.BlockSpec((1, HEADS * VR, tq), lambda b, p, ii, jj: (b, 0, jj[p])),
        ],
        out_specs=pl.BlockSpec((tq, GROUP_W), lambda b, p, ii, jj: (b * nq + ii[p], 0)),
        scratch_shapes=[
            pltpu.VMEM((8, tq), F32),
            pltpu.VMEM((HEADS * VR, tq), F32),
        ],
    )
    return pl.pallas_call(
        functools.partial(_attn_body, tq=tq, tk=tq, ahead=ahead),
        grid_spec=grid_spec,
        out_shape=jax.ShapeDtypeStruct((t, GROUP_W), F32),
        compiler_params=_cparams(("arbitrary", "arbitrary")),
        name=name,
    )(jnp.asarray(ii), jnp.asarray(jj), qT, k, vT)


def _conv_silu(xpad, src, cw_ref, cb_ref, first, ls):
    width = xpad.shape[1]

    @pl.when(first)
    def _():
        xpad[0:8, :] = jnp.zeros((8, width), F32)

    @pl.when(jnp.logical_not(first))
    def _():
        xpad[0:8, :] = xpad[ls:ls + 8, :]

    xpad[8:ls + 8, :] = src
    acc = cb_ref[...] + cw_ref[0:1, :] * xpad[pl.ds(8 - CONV_K + 1, ls), :]
    for kk in range(1, CONV_K):
        acc = acc + cw_ref[kk:kk + 1, :] * xpad[pl.ds(8 - CONV_K + 1 + kk, ls), :]
    return acc * _sigmoid(acc)


def _ssd_body(zx_ref, sm_ref, cw_ref, cb_ref, dtb_ref, arow_ref, dexp_ref, ng_ref, tri_ref, hexp_ref,
              o_ref, xpad, state, *, ls):
    first = pl.program_id(1) == 0

    @pl.when(first)
    def _():
        state[...] = jnp.zeros(state.shape, F32)

    xc = _conv_silu(xpad, zx_ref[:, GROUP_W:GROUP_W + SSM_XBC], cw_ref, cb_ref, first, ls)
    dt_all = _softplus(sm_ref[...] + dtb_ref[...])
    da_all = dt_all * arow_ref[...]
    lane = lax.broadcasted_iota(jnp.int32, (1, 128), 1)
    half = lane < HD
    head_lanes = lane < HEADS
    hexp = hexp_ref[...].astype(BF16)
    tril = _tril(CHUNK)

    def expand(x):
        x = jnp.where(head_lanes, x, 0.0)
        hi = x.astype(BF16)
        lo = (x - hi.astype(F32)).astype(BF16)
        return (jnp.dot(hi, hexp, preferred_element_type=F32)
                + jnp.dot(lo, hexp, preferred_element_type=F32))

    for c in range(ls // CHUNK):
        r0 = c * CHUNK
        acum = jnp.dot(tri_ref[...], da_all[r0:r0 + CHUNK], precision=HIGHEST,
                       preferred_element_type=F32)
        acum_t = acum.T
        alast = acum[CHUNK - 1:CHUNK, :]
        ex = expand(jnp.concatenate([dt_all[r0:r0 + CHUNK], jnp.exp(acum), jnp.exp(alast - acum),
                                     jnp.exp(alast), jnp.zeros((7, 128), F32)], axis=0))
        dt_x = ex[0:CHUNK]
        ecol_x = ex[CHUNK:2 * CHUNK]
        dout_x = ex[2 * CHUNK:3 * CHUNK]
        cdec_x = ex[3 * CHUNK:3 * CHUNK + 1]
        for g in range(2):
            h0, h1 = 2 * g, 2 * g + 1
            gl = slice(g * 128, (g + 1) * 128)
            xg = xc[r0:r0 + CHUNK, gl]
            bg = xc[r0:r0 + CHUNK, GROUP_W + g * 128:GROUP_W + (g + 1) * 128]
            cg = xc[r0:r0 + CHUNK, 2 * GROUP_W + g * 128:2 * GROUP_W + (g + 1) * 128]
            cgb = cg.astype(BF16)
            gram = lax.dot_general(cgb, bg.astype(BF16), NT_DIMS, preferred_element_type=F32)
            xdt = xg * dt_x[:, gl]
            ydiag = jnp.zeros((CHUNK, 128), F32)
            for hh, msk in ((h0, half), (h1, jnp.logical_not(half))):
                dec = jnp.exp(jnp.where(tril, acum[:, hh:hh + 1] - acum_t[hh:hh + 1, :], -jnp.inf))
                ydiag = ydiag + jnp.dot((gram * dec).astype(BF16),
                                        jnp.where(msk, xdt, 0.0).astype(BF16),
                                        preferred_element_type=F32)
            st = state[g]
            yoff = jnp.dot(cgb, st.astype(BF16), preferred_element_type=F32) * ecol_x[:, gl]
            state[g] = (cdec_x[:, gl] * st
                        + jnp.dot(bg.T.astype(BF16), (xdt * dout_x[:, gl]).astype(BF16),
                                  preferred_element_type=F32))
            y = ydiag + yoff + xg * dexp_ref[:, g * 128:(g + 1) * 128]
            zg = zx_ref[r0:r0 + CHUNK, g * 128:(g + 1) * 128]
            y = y * (zg * _sigmoid(zg))
            o_ref[r0:r0 + CHUNK, g * 128:(g + 1) * 128] = _rms(y, ng_ref[:, g * 128:(g + 1) * 128])


def _ssd(main, cw, cb, dtb, arow, dexp, ng, tri, hexp, bsz, seq, ls=512):
    t = main.shape[0]
    ns = seq // ls

    def full(shape):
        return pl.BlockSpec(shape, lambda b, s: tuple(0 for _ in shape))

    return pl.pallas_call(
        functools.partial(_ssd_body, ls=ls),
        grid=(bsz, ns),
        in_specs=[
            pl.BlockSpec((ls, 1024), lambda b, s: (b * ns + s, COL_ZX // 1024)),
            pl.BlockSpec((ls, 128), lambda b, s: (b * ns + s, COL_SMALL // 128)),
            full((CONV_K, SSM_XBC)), full((1, SSM_XBC)), full((1, 128)), full((1, 128)),
            full((1, GROUP_W)), full((1, GROUP_W)), full((CHUNK, CHUNK)), full((128, GROUP_W)),
        ],
        out_specs=pl.BlockSpec((ls, GROUP_W), lambda b, s: (b * ns + s, 0)),
        out_shape=jax.ShapeDtypeStruct((t, GROUP_W), F32),
        scratch_shapes=[pltpu.VMEM((ls + 8, SSM_XBC), F32), pltpu.VMEM((2, SSM_STATE, 128), F32)],
        compiler_params=_cparams(("arbitrary", "arbitrary")),
        name="ssd",
    )(main, main, cw, cb, dtb, arow, dexp, ng, tri, hexp)


def _mlstm_body(c_ref, sm_ref, cw_ref, cb_ref, igb_ref, fgb_ref, ng_ref, tri_ref, seg_ref, segm_ref,
                bmask_ref, o_ref, xpad, cst, nst, mst, *, ls):
    first = pl.program_id(1) == 0

    @pl.when(first)
    def _():
        cst[...] = jnp.zeros(cst.shape, F32)
        nst[...] = jnp.zeros(nst.shape, F32)
        mst[...] = jnp.zeros(mst.shape, F32)

    qk = _conv_silu(xpad, c_ref[:, 0:2 * GROUP_W], cw_ref, cb_ref, first, ls)
    sm = sm_ref[...]
    ig_all = sm + igb_ref[...]
    fx = sm + fgb_ref[...]
    lf_all = jnp.minimum(fx, 0.0) - jnp.log1p(jnp.exp(-jnp.abs(fx)))
    lane = lax.broadcasted_iota(jnp.int32, (1, GROUP_W), 1)
    hms = [jnp.logical_and(lane >= h * HD, lane < (h + 1) * HD) for h in range(HEADS)]
    tril = _tril(CHUNK)
    lane_s = lax.broadcasted_iota(jnp.int32, (1, 128), 1)
    gate_lanes = jnp.logical_and(lane_s >= GATE_LANE, lane_s < GATE_LANE + HEADS)
    segb = seg_ref[...].astype(BF16)
    hexp = seg_ref[...].T.astype(BF16)
    segm_b = segm_ref[...].astype(BF16)
    zero_b = jnp.zeros((), BF16)
    n_chunks = ls // CHUNK

    def expand(x):
        x = jnp.where(gate_lanes, x, 0.0)
        hi = x.astype(BF16)
        lo = (x - hi.astype(F32)).astype(BF16)
        return (jnp.dot(hi, hexp, preferred_element_type=F32)
                + jnp.dot(lo, hexp, preferred_element_type=F32))

    local = []
    for c in range(n_chunks):
        r0 = c * CHUNK
        q = qk[r0:r0 + CHUNK, 0:GROUP_W]
        k = qk[r0:r0 + CHUNK, GROUP_W:2 * GROUP_W] * (HD ** -0.5)
        ig = ig_all[r0:r0 + CHUNK]
        bcs = jnp.dot(tri_ref[...], lf_all[r0:r0 + CHUNK], precision=HIGHEST,
                      preferred_element_type=F32)
        bcs_t = bcs.T
        ig_t = ig.T
        ig_al = pltpu.roll(ig, GATE_LANE - 4, 1)
        k_t = k.T.astype(BF16)
        qb = q.astype(BF16)
        vb = c_ref[r0:r0 + CHUNK, 2 * GROUP_W:3 * GROUP_W].astype(BF16)
        num_loc = jnp.zeros((CHUNK, GROUP_W), F32)
        m_loc_all = jnp.zeros((CHUNK, 128), F32)
        den_loc_all = jnp.zeros((CHUNK, 128), F32)
        for h in range(HEADS):
            bcol = bcs[:, GATE_LANE + h:GATE_LANE + h + 1]
            dmat = jnp.where(tril, bcol - bcs_t[GATE_LANE + h:GATE_LANE + h + 1, :] + ig_t[4 + h:5 + h, :],
                             -jnp.inf)
            m_loc = jnp.max(dmat, axis=1, keepdims=True)
            s_qk = jnp.dot(jnp.where(hms[h], qb, zero_b), k_t, preferred_element_type=F32)
            sc = s_qk * jnp.exp(dmat - m_loc)
            num_loc = num_loc + jnp.dot(sc.astype(BF16), jnp.where(hms[h], vb, zero_b),
                                        preferred_element_type=F32)
            on_lane = lane_s == GATE_LANE + h
            m_loc_all = jnp.where(on_lane, m_loc, m_loc_all)
            den_loc_all = jnp.where(on_lane, jnp.sum(sc, axis=1, keepdims=True), den_loc_all)
        blast = bcs[CHUNK - 1:CHUNK, :]
        gcol = blast - bcs + ig_al
        gmax = jnp.max(gcol, axis=0, keepdims=True)
        kw = k * expand(jnp.exp(gcol - gmax))
        kwv = bmask_ref[...] * jnp.dot(kw.T.astype(BF16), vb, preferred_element_type=F32)
        local.append((q, num_loc, bcs, m_loc_all, den_loc_all, blast, gmax, kwv,
                      jnp.sum(kw, axis=0, keepdims=True)))

    houts = []
    for c in range(n_chunks):
        q, num_loc, bcs, m_loc_all, den_loc_all, blast, gmax, kwv, ksum = local[c]
        c_old = cst[...]
        n_old = nst[0:1, :]
        m_prev = mst[0:1, :]
        q_c = jnp.dot(q.astype(BF16), c_old.astype(BF16), preferred_element_type=F32)
        q_n = jnp.dot((q * n_old).astype(BF16), segb, preferred_element_type=F32)
        inter = bcs + m_prev
        m_t = jnp.maximum(inter, m_loc_all)
        f = jnp.exp(m_loc_all - m_t)
        w_inter = jnp.exp(inter - m_t)
        den = f * den_loc_all + w_inter * q_n
        inv_den = 1.0 / jnp.maximum(jnp.abs(den), jnp.exp(-m_t))
        m_new = jnp.maximum(blast + m_prev, gmax)
        scal = jnp.concatenate([jnp.exp(gmax - m_new), jnp.exp(blast + m_prev - m_new),
                                jnp.zeros((6, 128), F32)], axis=0)
        ex = expand(jnp.concatenate([f, w_inter, inv_den, scal], axis=0))
        f_x = ex[0:CHUNK]
        w_inter_x = ex[CHUNK:2 * CHUNK]
        inv_den_x = ex[2 * CHUNK:3 * CHUNK]
        sfac_x = ex[3 * CHUNK:3 * CHUNK + 1]
        keep_x = ex[3 * CHUNK + 1:3 * CHUNK + 2]
        mst[0:1, :] = m_new
        cst[...] = keep_x * c_old + sfac_x * kwv
        nst[0:1, :] = keep_x * n_old + sfac_x * ksum
        houts.append((f_x * num_loc + w_inter_x * q_c) * inv_den_x)

    for c in range(n_chunks):
        r0 = c * CHUNK
        hh = houts[c] * _sigmoid(c_ref[r0:r0 + CHUNK, 3 * GROUP_W:4 * GROUP_W])
        sq = hh * hh
        sq_hi = sq.astype(BF16)
        sq_lo = (sq - sq_hi.astype(F32)).astype(BF16)
        ms = (jnp.dot(sq_hi, segm_b, preferred_element_type=F32)
              + jnp.dot(sq_lo, segm_b, preferred_element_type=F32))
        o_ref[r0:r0 + CHUNK, :] = hh * lax.rsqrt(ms + EPS) * ng_ref[...]


def _mlstm(main, cw, cb, igb, fgb, ng, tri, seg, segm, bmask, bsz, seq, ls=512):
    t = main.shape[0]
    ns = seq // ls

    def full(shape):
        return pl.BlockSpec(shape, lambda b, s: tuple(0 for _ in shape))

    return pl.pallas_call(
        functools.partial(_mlstm_body, ls=ls),
        grid=(bsz, ns),
        in_specs=[
            pl.BlockSpec((ls, 1024), lambda b, s: (b * ns + s, COL_ML // 1024)),
            pl.BlockSpec((ls, 128), lambda b, s: (b * ns + s, COL_SMALL // 128)),
            full((CONV_K, 2 * GROUP_W)), full((1, 2 * GROUP_W)), full((1, 128)), full((1, 128)),
            full((1, GROUP_W)), full((CHUNK, CHUNK)), full((GROUP_W, 128)), full((GROUP_W, GROUP_W)),
            full((GROUP_W, GROUP_W)),
        ],
        out_specs=pl.BlockSpec((ls, GROUP_W), lambda b, s: (b * ns + s, 0)),
        out_shape=jax.ShapeDtypeStruct((t, GROUP_W), F32),
        scratch_shapes=[
            pltpu.VMEM((ls + 8, 2 * GROUP_W), F32),
            pltpu.VMEM((GROUP_W, GROUP_W), F32),
            pltpu.VMEM((8, GROUP_W), F32),
            pltpu.VMEM((8, 128), F32),
        ],
        compiler_params=_cparams(("arbitrary", "arbitrary")),
        name="mlstm",
    )(main, main, cw, cb, igb, fgb, ng, tri, seg, segm, bmask)


def _outproj_body(x_ref, ya_ref, yb_ref, yc_ref, yd_ref, wo_ref, g2_ref, wr_ref, br_ref,
                  xn_ref, h2_ref, cb_ref):
    sub = x_ref.shape[0] // 2
    accs = []
    for s in range(2):
        rows = slice(s * sub, (s + 1) * sub)
        acc = x_ref[rows, :]
        for gi, y_ref in enumerate((ya_ref, yb_ref, yc_ref, yd_ref)):
            acc = acc + jnp.dot(y_ref[rows, :].astype(BF16), wo_ref[gi * GROUP_W:(gi + 1) * GROUP_W, :],
                                preferred_element_type=F32)
        accs.append(acc)
    for s in range(2):
        _route(accs[s], slice(s * sub, (s + 1) * sub), g2_ref, wr_ref, br_ref, xn_ref, h2_ref, cb_ref)


def _route(acc, rows, g2_ref, wr_ref, br_ref, xn_ref, h2_ref, cb_ref):
    xn_ref[rows, :] = acc
    h2 = _rms(acc, g2_ref[...])
    h_hi = h2.astype(BF16)
    h2_ref[rows, :] = h_hi
    h_lo = (h2 - h_hi.astype(F32)).astype(BF16)
    hh = jnp.dot(h_hi, wr_ref[...], preferred_element_type=F32)
    lh = jnp.dot(h_lo, wr_ref[...], preferred_element_type=F32)
    logits = hh[:, :128] + hh[:, 128:] + lh[:, :128] + br_ref[...]
    lane = lax.broadcasted_iota(jnp.int32, logits.shape, 1)
    big = jnp.int32(1 << 20)
    n_grp = N_EXPERTS // EXPERTS_PER_GROUP
    gl = jnp.where(jnp.logical_and(lane >= N_EXPERTS, lane < N_EXPERTS + n_grp), logits, -jnp.inf)
    gmax = jnp.max(gl, axis=1, keepdims=True)
    g_w = 1.0 / jnp.sum(jnp.exp(gl - gmax), axis=1, keepdims=True)
    gidx = jnp.min(jnp.where(gl == gmax, lane, big), axis=1, keepdims=True) - N_EXPERTS
    in_grp = jnp.logical_and(lane < N_EXPERTS,
                             lax.shift_right_logical(lane, int(math.log2(EXPERTS_PER_GROUP))) == gidx)
    el = jnp.where(in_grp, logits, -jnp.inf)
    emax = jnp.max(el, axis=1, keepdims=True)
    esum = jnp.sum(jnp.exp(el - emax), axis=1, keepdims=True)
    i1 = jnp.min(jnp.where(el == emax, lane, big), axis=1, keepdims=True)
    el2 = jnp.where(lane == i1, -jnp.inf, el)
    emax2 = jnp.max(el2, axis=1, keepdims=True)
    i2 = jnp.min(jnp.where(el2 == emax2, lane, big), axis=1, keepdims=True)
    p1 = 1.0 / esum
    p2 = jnp.exp(emax2 - emax) / esum
    w1 = p1 / (p1 + p2)
    w2 = p2 / (p1 + p2)
    comb = g_w * (jnp.where(lane == i1, w1, 0.0) + jnp.where(lane == i2, w2, 0.0))
    cb_ref[rows, :] = jnp.where(lane == GID_LANE, gidx.astype(F32), comb)


def _outproj(x2, ya, yb, yc, yd, wo, g2, wr, br, tm=512):
    t = x2.shape[0]
    row = pl.BlockSpec((tm, D_MODEL), lambda i: (i, 0))
    grp = pl.BlockSpec((tm, GROUP_W), lambda i: (i, 0))

    def full(shape):
        return pl.BlockSpec(shape, lambda i: tuple(0 for _ in shape))

    return pl.pallas_call(
        _outproj_body,
        grid=(t // tm,),
        in_specs=[row, grp, grp, grp, grp, full((D_MODEL, D_MODEL)), full((1, D_MODEL)),
                  full((D_MODEL, 256)), full((1, 128))],
        out_specs=[row, row, pl.BlockSpec((tm, 128), lambda i: (i, 0))],
        out_shape=[
            jax.ShapeDtypeStruct((t, D_MODEL), F32),
            jax.ShapeDtypeStruct((t, D_MODEL), BF16),
            jax.ShapeDtypeStruct((t, 128), F32),
        ],
        compiler_params=_cparams(("arbitrary",)),
        name="outproj_router",
    )(x2, ya, yb, yc, yd, wo, g2, wr, br)


MOE_EB = EXPERTS_PER_GROUP
MOE_RB = 128


def _moe_body(h_ref, cb_ref, xn_ref, tri_ref, wg_ref, wu_ref, wd_ref, gf_ref, o_ref,
              perm_t, hs, cs, ys, seg_smem, *, final_norm, rb):
    g = pl.program_id(1)
    tm = h_ref.shape[0]
    n_grp = N_EXPERTS // EXPERTS_PER_GROUP
    lane = lax.broadcasted_iota(jnp.int32, (tm, 128), 1)

    @pl.when(g == 0)
    def _():
        cb = cb_ref[...]
        gid = jnp.sum(jnp.where(lane == GID_LANE, cb, 0.0), axis=1, keepdims=True).astype(jnp.int32)
        onehot = jnp.where(lane == gid, 1.0, 0.0)
        oh16 = onehot.astype(BF16)
        tot = jnp.zeros((1, 128), F32)
        parts = []
        for r0 in range(0, tm, CHUNK):
            part = jnp.dot(tri_ref[...], oh16[r0:r0 + CHUNK], preferred_element_type=F32) + tot
            parts.append(part)
            tot = part[CHUNK - 1:CHUNK, :]
        cum = jnp.concatenate(parts, axis=0)
        lane1 = lane[0:1, :]
        base = jnp.zeros((1, 128), F32)
        run = jnp.zeros((1, 1), F32)
        for k in range(n_grp):
            base = base + jnp.where(lane1 == k, run, 0.0)
            seg_smem[k] = run[0, 0].astype(jnp.int32)
            seg_smem[n_grp + k] = tot[0, k].astype(jnp.int32)
            run = run + tot[:, k:k + 1]
        pos = jnp.sum(onehot * (base + cum - 1.0), axis=1, keepdims=True).astype(jnp.int32)
        pos_row = jnp.broadcast_to(pos.astype(F32), (tm, 128)).T[0:1, :].astype(jnp.int32)
        perm_t[...] = jnp.where(lax.broadcasted_iota(jnp.int32, (tm, tm), 1) == pos, 1.0, 0.0).astype(BF16)
        p = jnp.where(lax.broadcasted_iota(jnp.int32, (tm, tm), 0) == pos_row, 1.0, 0.0).astype(BF16)
        hs[...] = jnp.dot(p, h_ref[...], preferred_element_type=F32).astype(BF16)
        cb_hi = cb.astype(BF16)
        cb_lo = (cb - cb_hi.astype(F32)).astype(BF16)
        cs2 = jnp.dot(p, jnp.concatenate([cb_hi, cb_lo], axis=1), preferred_element_type=F32)
        cs[...] = cs2[:, :128] + cs2[:, 128:]
        ys[...] = jnp.zeros(ys.shape, F32)

    seg_lo = seg_smem[g]
    seg_hi = seg_lo + seg_smem[n_grp + g]
    lane_rb = lax.broadcasted_iota(jnp.int32, (rb, 128), 1)

    def run_block(lo):
        x = hs[lo:lo + rb, :]
        cw = cs[lo:lo + rb, :]

        def gate_up(j):
            return (jnp.dot(x, wg_ref[j], preferred_element_type=F32),
                    jnp.dot(x, wu_ref[j], preferred_element_type=F32))

        def down(j, gu):
            col = jnp.sum(jnp.where(lane_rb == g * MOE_EB + j, cw, 0.0), axis=1,
                          keepdims=True)
            gt, up = gu
            hid = (gt * _sigmoid(gt)) * up * col
            return jnp.dot(hid.astype(BF16), wd_ref[j], preferred_element_type=F32)

        total = jnp.zeros((rb, D_MODEL), F32)
        gu = gate_up(0)
        for j in range(MOE_EB):
            gu_next = gate_up(j + 1) if j + 1 < MOE_EB else None
            total = total + down(j, gu)
            gu = gu_next
        ys[lo:lo + rb, :] += total

    for b in range(tm // rb):
        lo = b * rb
        pl.when(jnp.logical_and(seg_lo < lo + rb, seg_hi > lo))(functools.partial(run_block, lo))

    @pl.when(g == n_grp - 1)
    def _():
        y = xn_ref[...] + jnp.dot(perm_t[...], ys[...].astype(BF16), preferred_element_type=F32)
        if final_norm:
            y = _rms(y, gf_ref[...])
        o_ref[...] = y


def _moe(h2, comb, xn, tri, wg, wu, wd, gf, final_norm, tm=1024, rb=MOE_RB):
    t = h2.shape[0]
    return pl.pallas_call(
        functools.partial(_moe_body, final_norm=final_norm, rb=rb),
        grid=(t // tm, N_EXPERTS // MOE_EB),
        in_specs=[
            pl.BlockSpec((tm, D_MODEL), lambda i, e: (i, 0)),
            pl.BlockSpec((tm, 128), lambda i, e: (i, 0)),
            pl.BlockSpec((tm, D_MODEL), lambda i, e: (i, 0)),
            pl.BlockSpec((CHUNK, CHUNK), lambda i, e: (0, 0)),
            pl.BlockSpec((MOE_EB, D_MODEL, D_EXPERT), lambda i, e: (e, 0, 0)),
            pl.BlockSpec((MOE_EB, D_MODEL, D_EXPERT), lambda i, e: (e, 0, 0)),
            pl.BlockSpec((MOE_EB, D_EXPERT, D_MODEL), lambda i, e: (e, 0, 0)),
            pl.BlockSpec((1, D_MODEL), lambda i, e: (0, 0)),
        ],
        out_specs=pl.BlockSpec((tm, D_MODEL), lambda i, e: (i, 0)),
        out_shape=jax.ShapeDtypeStruct((t, D_MODEL), F32),
        scratch_shapes=[
            pltpu.VMEM((tm, tm), BF16),
            pltpu.VMEM((tm, D_MODEL), BF16),
            pltpu.VMEM((tm, 128), F32),
            pltpu.VMEM((tm, D_MODEL), F32),
            pltpu.SMEM((2 * (N_EXPERTS // MOE_EB),), jnp.int32),
        ],
        compiler_params=_cparams(("arbitrary", "arbitrary")),
        name="moe",
    )(h2, comb, xn, tri, wg, wu, wd, gf)


def _spread_heads(w, width, slab, total):
    rows = w.shape[0]
    pieces = []
    for h in range(HEADS):
        pieces.append(w[:, h * width:(h + 1) * width])
        pieces.append(jnp.zeros((rows, slab - width), w.dtype))
    if total > HEADS * slab:
        pieces.append(jnp.zeros((rows, total - HEADS * slab), w.dtype))
    return jnp.concatenate(pieces, axis=1)


def _pad_heads(w, width):
    return _spread_heads(w, width, HP, HEADS * HP)


def _v_cols(w):
    return _spread_heads(w, HD, VR, V_COLS)


def _lane_row(vals, offset, width=128):
    return jnp.zeros((1, width), F32).at[0, offset:offset + vals.shape[0]].set(vals.astype(F32))


def _layer_weights(l, p):
    half = MLA_ROPE // 2
    w = {"g_mix": p["norm_mix_g"][l][None, :]}
    w["ssm_cw"] = p["ssm_conv_w"][l]
    w["ssm_cb"] = p["ssm_conv_b"][l][None, :]
    w["ssm_dtb"] = _lane_row(p["ssm_dt_bias"][l], 0)
    w["ssm_arow"] = _lane_row(-jnp.exp(p["ssm_a_log"][l].astype(F32)), 0)
    w["ssm_dexp"] = jnp.repeat(p["ssm_d"][l].astype(F32), HD)[None, :]
    w["ssm_ng"] = p["ssm_norm_g"][l][None, :]
    w["ml_cw"] = p["ml_conv_w"][l]
    w["ml_cb"] = p["ml_conv_b"][l][None, :]
    w["ml_igb"] = _lane_row(p["ml_ig_bias"][l], 4)
    w["ml_fgb"] = _lane_row(p["ml_fg_bias"][l], 8)
    w["ml_ng"] = p["ml_norm_g"][l][None, :]
    wuq = p["mla_w_uq"][l]
    qd = MLA_NOPE + MLA_ROPE
    swapped = []
    for h in range(HEADS):
        c0 = h * qd + MLA_NOPE
        swapped += [jnp.zeros((256, MLA_NOPE), F32), wuq[:, c0 + half:c0 + 2 * half], wuq[:, c0:c0 + half]]
    wq = _spread_heads(wuq, qd, HP, HEADS * HP)
    wqs = _spread_heads(jnp.concatenate(swapped, axis=1), qd, HP, HEADS * HP)
    w["mla_wq2"] = jnp.concatenate([wq, wqs], axis=1).astype(BF16)
    wukv = p["mla_w_ukv"][l]
    wk_cols = jnp.concatenate([wukv[:, h * 2 * HD:h * 2 * HD + MLA_NOPE] for h in range(HEADS)], axis=1)
    wv_cols = jnp.concatenate([wukv[:, h * 2 * HD + MLA_NOPE:(h + 1) * 2 * HD] for h in range(HEADS)], axis=1)
    w["mla_wk"] = _pad_heads(wk_cols, MLA_NOPE).astype(BF16)
    w["mla_wv"] = _v_cols(wv_cols).astype(BF16)
    w["mla_gq"] = p["mla_q_norm_g"][l][None, :]
    w["mla_gkv"] = p["mla_kv_norm_g"][l][None, :]
    w["w_out"] = p["w_out"][l].astype(BF16)
    w["g_ffn"] = p["norm_ffn_g"][l][None, :]
    wr = jnp.zeros((D_MODEL, 128), F32)
    wr = wr.at[:, :N_EXPERTS].set(p["moe_w_router"][l]).at[:, N_EXPERTS:N_EXPERTS + 4].set(p["moe_w_group"][l])
    wr_hi = wr.astype(BF16)
    w["w_router"] = jnp.concatenate([wr_hi, (wr - wr_hi.astype(F32)).astype(BF16)], axis=1)
    w["b_router"] = _lane_row(p["moe_b_router"][l], 0) + _lane_row(p["moe_b_group"][l], N_EXPERTS)
    w["w_gate"] = p["moe_w_gate"][l].astype(BF16)
    w["w_up"] = p["moe_w_up"][l].astype(BF16)
    w["w_down"] = p["moe_w_down"][l].astype(BF16)
    return w


def kernel(x, positions, norm_mix_g, w_in, ssm_conv_w, ssm_conv_b, ssm_dt_bias, ssm_a_log, ssm_d, ssm_norm_g, ml_conv_w, ml_conv_b, ml_ig_bias, ml_fg_bias, ml_norm_g, mla_q_norm_g, mla_w_uq, mla_kv_norm_g, mla_w_ukv, w_out, norm_ffn_g, moe_w_group, moe_b_group, moe_w_router, moe_b_router, moe_w_gate, moe_w_up, moe_w_down, final_norm_g):
    p = dict(norm_mix_g=norm_mix_g, w_all=_relayout_w_in(w_in), ssm_conv_w=ssm_conv_w, ssm_conv_b=ssm_conv_b,
             ssm_dt_bias=ssm_dt_bias, ssm_a_log=ssm_a_log, ssm_d=ssm_d, ssm_norm_g=ssm_norm_g,
             ml_conv_w=ml_conv_w, ml_conv_b=ml_conv_b, ml_ig_bias=ml_ig_bias, ml_fg_bias=ml_fg_bias,
             ml_norm_g=ml_norm_g, mla_q_norm_g=mla_q_norm_g, mla_w_uq=mla_w_uq,
             mla_kv_norm_g=mla_kv_norm_g, mla_w_ukv=mla_w_ukv, w_out=w_out, norm_ffn_g=norm_ffn_g,
             moe_w_group=moe_w_group, moe_b_group=moe_b_group, moe_w_router=moe_w_router,
             moe_b_router=moe_b_router, moe_w_gate=moe_w_gate, moe_w_up=moe_w_up, moe_w_down=moe_w_down)
    bsz, seq, _ = x.shape
    depth = w_in.shape[0]
    x2 = x.reshape(bsz * seq, D_MODEL)

    half = MLA_ROPE // 2
    inv = ROPE_THETA ** (-jnp.arange(0, MLA_ROPE, 2, dtype=F32) / MLA_ROPE)
    inv_col = jnp.concatenate([inv, inv])[:, None]
    sgn_col = jnp.concatenate([-jnp.ones((half,), F32), jnp.ones((half,), F32)])[:, None]
    pe_mask = _lane_row(jnp.ones((MLA_ROPE,), F32), MLA_NOPE)
    c_tok, s_tok, c_tr, s_tr = _rope_tables(positions.reshape(bsz, 1, seq), inv_col, sgn_col, bsz, seq)
    tri = jnp.tril(jnp.ones((CHUNK, CHUNK), F32))
    tri_moe = tri.astype(BF16)
    head_of = jnp.arange(GROUP_W) // HD
    seg = (head_of[:, None] + GATE_LANE == jnp.arange(128)[None, :]).astype(F32)
    ssd_hexp = (jnp.arange(128)[:, None] == head_of[None, :]).astype(F32)
    bmask = (head_of[:, None] == head_of[None, :]).astype(F32)
    segm = bmask / HD

    for l in range(depth):
        w = _layer_weights(l, p)
        main, k_a, qT_a, vT_a, qT_d, k_d, vT_d = _in_proj(
            x2, w["g_mix"], p["w_all"], l, c_tok, s_tok, c_tr, s_tr, w["mla_gq"], w["mla_wq2"], w["mla_gkv"],
            w["mla_wk"], w["mla_wv"], pe_mask, bsz, seq)
        y_a = _attention(qT_a, k_a, vT_a, bsz, seq, "attn_moba", ahead=(6, 7)[l % 2])
        y_b = _ssd(main, w["ssm_cw"], w["ssm_cb"], w["ssm_dtb"], w["ssm_arow"], w["ssm_dexp"], w["ssm_ng"],
                   tri, ssd_hexp, bsz, seq)
        y_c = _mlstm(main, w["ml_cw"], w["ml_cb"], w["ml_igb"], w["ml_fgb"], w["ml_ng"], tri, seg, segm,
                     bmask, bsz, seq)
        y_d = _attention(qT_d, k_d, vT_d, bsz, seq, "attn_mla")
        xn, h2, comb = _outproj(x2, y_a, y_b, y_c, y_d, w["w_out"], w["g_ffn"], w["w_router"], w["b_router"])
        x2 = _moe(h2, comb, xn, tri_moe, w["w_gate"], w["w_up"], w["w_down"], final_norm_g[None, :],
                  final_norm=(l == depth - 1), tm=MOE_TM)
    return x2.reshape(bsz, seq, D_MODEL)
```

```python
import functools
import math

import numpy as np
import jax
import jax.numpy as jnp
from jax import lax
from jax.experimental import pallas as pl
from jax.experimental.pallas import tpu as pltpu

F32 = jnp.float32
BF16 = jnp.bfloat16
HIGHEST = lax.Precision.HIGHEST

EPS = 1e-6
NEG = -1e30
LOG2E = 1.4426950408889634

D_MODEL = 1024
GROUP_W = 256
HEADS = 4
HD = 64
HP = 128
MOBA_BLOCK = 256
MOBA_TOPK = 3
SSM_STATE = 128
SSM_XBC = 768
CONV_K = 4
CHUNK = 128
MLA_NOPE = 64
MLA_ROPE = 32
ROPE_THETA = 10000.0
N_EXPERTS = 16
EXPERTS_PER_GROUP = 4
D_EXPERT = 256
GATE_LANE = 8
GID_LANE = 20
MOE_TM = 1024

COL_ZX = 0
COL_ML = 1024
COL_SMALL = 2048
N_MAIN = 2176
VR = 80
V_COLS = 384
N_ALL = N_MAIN + 256 + 128 + 3 * GROUP_W
ATT_UQ = 256
ATT_UK = 256

V7X_VMEM_LIMIT = 56 * 1024 * 1024

NT_DIMS = (((1,), (1,)), ((), ()))


def _cparams(sem):
    return pltpu.CompilerParams(dimension_semantics=sem, vmem_limit_bytes=V7X_VMEM_LIMIT)


def _rms(x, g):
    ms = jnp.mean(x * x, axis=-1, keepdims=True)
    return x * lax.rsqrt(ms + EPS) * g


def _sigmoid(x):
    return 0.5 * jnp.tanh(0.5 * x) + 0.5


def _softplus(x):
    return jnp.maximum(x, 0.0) + jnp.log1p(jnp.exp(-jnp.abs(x)))


def _tril(n):
    r = lax.broadcasted_iota(jnp.int32, (n, n), 0)
    c = lax.broadcasted_iota(jnp.int32, (n, n), 1)
    return c <= r


W_IN_SIZES = (3 * GROUP_W, GROUP_W, SSM_XBC, HEADS, 2 * GROUP_W, GROUP_W, HEADS, HEADS, GROUP_W, 256, 128,
              MLA_ROPE)


def _relayout_body(w_ref, o_ref):
    w = w_ref[0]
    rows = w.shape[0]
    offs = np.cumsum((0,) + W_IN_SIZES)
    a_qkv, s_z, s_xbc, s_dt, m_qk, m_v, m_i, m_f, m_o, d_cq, d_ckv, d_kr = [
        w[:, int(offs[i]):int(offs[i + 1])] for i in range(len(W_IN_SIZES))]
    half = MLA_ROPE // 2

    def zeros(n):
        return jnp.zeros((rows, n), F32)

    pieces = [s_z, s_xbc, m_qk, m_v, m_o,
              s_dt, m_i, m_f, zeros(64 - 3 * HEADS), d_kr, d_kr[:, half:], d_kr[:, :half],
              d_cq, d_ckv,
              a_qkv[:, GROUP_W:2 * GROUP_W], a_qkv[:, :GROUP_W], a_qkv[:, 2 * GROUP_W:]]
    o_ref[0] = jnp.concatenate(pieces, axis=1).astype(BF16)


def _relayout_w_in(w, rows=256):
    nl, r, c = w.shape
    return pl.pallas_call(
        _relayout_body,
        grid=(nl, r // rows),
        in_specs=[pl.BlockSpec((1, rows, c), lambda l, i: (l, i, 0))],
        out_specs=pl.BlockSpec((1, rows, N_ALL), lambda l, i: (l, i, 0)),
        out_shape=jax.ShapeDtypeStruct((nl, r, N_ALL), BF16),
        compiler_params=_cparams(("arbitrary", "arbitrary")),
        name="relayout_w_in",
    )(w)


def _ones_rows(shape):
    row = lax.broadcasted_iota(jnp.int32, shape, 0)
    hit = row == HD
    for h in range(1, HEADS):
        hit = jnp.logical_or(hit, row == h * VR + HD)
    return jnp.where(hit, 1.0, 0.0)


def _in_proj_body(x_ref, g_ref, w_ref, c_ref, s_ref, ct_ref, st_ref, gq_ref, wq_ref, gkv_ref, wk_ref,
                  wv_ref, msk_ref, om_ref, ka_ref, qa_ref, va_ref, qd_ref, kd_ref, vd_ref, km_sc,
                  *, nb, nst, qscale_a, qscale_d):
    s_idx = pl.program_id(0) % nst
    tm = x_ref.shape[0]

    @pl.when(s_idx == 0)
    def _():
        km_sc[...] = jnp.zeros(km_sc.shape, F32)

    h = _rms(x_ref[...], g_ref[...]).astype(BF16)
    rest = jnp.dot(h, w_ref[0, :, COL_SMALL:], preferred_element_type=F32)
    for c in range(0, COL_SMALL, 512):
        om_ref[:, c:c + 512] = jnp.dot(h, w_ref[0, :, c:c + 512], preferred_element_type=F32)
    sm = rest[:, 0:128]
    om_ref[:, COL_SMALL:COL_SMALL + 128] = sm
    cq = rest[:, 128:384]
    ckv = rest[:, 384:512]
    kp = _spread_heads(rest[:, 512:768], HD, HP, HEADS * HP)
    qp = _spread_heads(rest[:, 768:1024], HD, HP, HEADS * HP)
    vc = _spread_heads(rest[:, 1024:1280], HD, VR, V_COLS)

    nblk = tm // MOBA_BLOCK
    lane_k = lax.broadcasted_iota(jnp.int32, (MOBA_BLOCK, HEADS * HP), 1)
    for j in range(nblk):
        kj = kp[j * MOBA_BLOCK:(j + 1) * MOBA_BLOCK]
        blk_id = s_idx * nblk + j
        km_sc[pl.ds(blk_id, 1), :] = jnp.mean(kj, axis=0, keepdims=True)
        ka_ref[j * MOBA_BLOCK:(j + 1) * MOBA_BLOCK, :] = jnp.where(
            (lane_k & (HP - 1)) == HD + blk_id, 1.0, kj).astype(BF16)

    qT = qp.T
    km = km_sc[...]
    pos = s_idx * tm + lax.broadcasted_iota(jnp.int32, (nb, tm), 1)
    cur = lax.shift_right_logical(pos, int(math.log2(MOBA_BLOCK)))
    blk = lax.broadcasted_iota(jnp.int32, (nb, tm), 0)
    past = blk < cur
    for hd in range(HEADS):
        qh = qT[hd * HP:(hd + 1) * HP, :]
        gate = jnp.dot(km[:, hd * HP:(hd + 1) * HP], qh, precision=HIGHEST,
                       preferred_element_type=F32)
        gate = jnp.where(past, gate, -jnp.inf)
        bias = jnp.where(blk == cur, 0.0, NEG)
        for _ in range(MOBA_TOPK):
            m = jnp.max(gate, axis=0, keepdims=True)
            cand = jnp.logical_and(gate == m, gate > -jnp.inf)
            idx = jnp.min(jnp.where(cand, blk, nb), axis=0, keepdims=True)
            pick = blk == idx
            bias = jnp.where(pick, 0.0, bias)
            gate = jnp.where(pick, -jnp.inf, gate)
        rows = [qh[:HD] * qscale_a, bias]
        if HP - HD - nb:
            rows.append(jnp.zeros((HP - HD - nb, tm), F32))
        qa_ref[0, hd * HP:(hd + 1) * HP, :] = jnp.concatenate(rows, axis=0).astype(BF16)
    vt = vc.T[:HEADS * VR]
    va_ref[0] = (vt + _ones_rows(vt.shape)).astype(BF16)

    qn = _rms(cq, gq_ref[...]).astype(BF16)
    q2 = jnp.dot(qn, wq_ref[...], preferred_element_type=F32).T
    nq = HEADS * HP
    ct4 = jnp.concatenate([ct_ref[0]] * HEADS, axis=0)
    st4 = jnp.concatenate([st_ref[0]] * HEADS, axis=0)
    qd_ref[0] = ((q2[:nq] * ct4 + q2[nq:] * st4) * qscale_d).astype(BF16)
    kvn = _rms(ckv, gkv_ref[...]).astype(BF16)
    kn = jnp.dot(kvn, wk_ref[...], preferred_element_type=F32)
    pe = (sm * c_ref[...] + pltpu.roll(sm, HP - MLA_ROPE, 1) * s_ref[...]) * msk_ref[...]
    kd_ref[...] = (kn + jnp.concatenate([pe] * HEADS, axis=1)).astype(BF16)
    vtd = jnp.dot(kvn, wv_ref[...], preferred_element_type=F32).T[:HEADS * VR]
    vd_ref[0] = (vtd + _ones_rows(vtd.shape)).astype(BF16)


def _in_proj(x2, g, w_all, layer, c_tok, s_tok, c_tr, s_tr, gq, wq2, gkv, wk, wv, pe_mask, bsz, seq, tm=512):
    t = x2.shape[0]
    nst = seq // tm
    nb = seq // MOBA_BLOCK
    assert nb <= HP - HD
    tok128 = pl.BlockSpec((tm, HP), lambda i: (i, 0))
    tr128 = pl.BlockSpec((1, HP, tm), lambda i: (i // nst, 0, i % nst))
    tr_q = pl.BlockSpec((1, HEADS * HP, tm), lambda i: (i // nst, 0, i % nst))
    tr_v = pl.BlockSpec((1, HEADS * VR, tm), lambda i: (i // nst, 0, i % nst))
    tok_k = pl.BlockSpec((tm, HEADS * HP), lambda i: (i, 0))

    def full(shape):
        return pl.BlockSpec(shape, lambda i: tuple(0 for _ in shape))

    q_shape = jax.ShapeDtypeStruct((bsz, HEADS * HP, seq), BF16)
    k_shape = jax.ShapeDtypeStruct((t, HEADS * HP), BF16)
    v_shape = jax.ShapeDtypeStruct((bsz, HEADS * VR, seq), BF16)
    return pl.pallas_call(
        functools.partial(_in_proj_body, nb=nb, nst=nst, qscale_a=(HD ** -0.5) * LOG2E,
                          qscale_d=((MLA_NOPE + MLA_ROPE) ** -0.5) * LOG2E),
        grid=(t // tm,),
        in_specs=[
            pl.BlockSpec((tm, D_MODEL), lambda i: (i, 0)),
            full((1, D_MODEL)), pl.BlockSpec((1, D_MODEL, N_ALL), lambda i: (layer, 0, 0)),
            tok128, tok128, tr128, tr128,
            full((1, 256)), full((256, 2 * HEADS * HP)), full((1, 128)),
            full((128, HEADS * HP)), full((128, V_COLS)), full((1, HP)),
        ],
        out_specs=[pl.BlockSpec((tm, N_MAIN), lambda i: (i, 0)), tok_k, tr_q, tr_v, tr_q, tok_k, tr_v],
        out_shape=[jax.ShapeDtypeStruct((t, N_MAIN), F32), k_shape, q_shape, v_shape,
                   q_shape, k_shape, v_shape],
        scratch_shapes=[pltpu.VMEM((nb, HEADS * HP), F32)],
        compiler_params=_cparams(("arbitrary",)),
        name="in_proj",
    )(x2, g, w_all, c_tok, s_tok, c_tr, s_tr, gq, wq2, gkv, wk, wv, pe_mask)


def _rope_body(pos_ref, inv_ref, sgn_ref, c_ref, s_ref, ct_ref, st_ref):
    ts = pos_ref.shape[2]
    ang = inv_ref[...] * pos_ref[0].astype(F32)
    c32 = jnp.cos(ang)
    s32 = jnp.sin(ang) * sgn_ref[...]
    ct = jnp.concatenate([jnp.ones((MLA_NOPE, ts), F32), c32,
                          jnp.ones((HP - MLA_NOPE - MLA_ROPE, ts), F32)], axis=0)
    st = jnp.concatenate([jnp.zeros((MLA_NOPE, ts), F32), s32,
                          jnp.zeros((HP - MLA_NOPE - MLA_ROPE, ts), F32)], axis=0)
    ct_ref[0] = ct
    st_ref[0] = st
    c_ref[...] = ct.T
    s_ref[...] = st.T


def _rope_tables(pos3, inv_col, sgn_col, bsz, seq, ts=512):
    t = bsz * seq
    nst = seq // ts
    tok = pl.BlockSpec((ts, HP), lambda i: (i, 0))
    tr = pl.BlockSpec((1, HP, ts), lambda i: (i // nst, 0, i % nst))
    col = pl.BlockSpec((MLA_ROPE, 1), lambda i: (0, 0))
    return pl.pallas_call(
        _rope_body,
        grid=(t // ts,),
        in_specs=[pl.BlockSpec((1, 1, ts), lambda i: (i // nst, 0, i % nst)), col, col],
        out_specs=[tok, tok, tr, tr],
        out_shape=[
            jax.ShapeDtypeStruct((t, HP), F32),
            jax.ShapeDtypeStruct((t, HP), F32),
            jax.ShapeDtypeStruct((bsz, HP, seq), F32),
            jax.ShapeDtypeStruct((bsz, HP, seq), F32),
        ],
        compiler_params=_cparams(("arbitrary",)),
        name="rope_tables",
    )(pos3, inv_col, sgn_col)


def _attn_body(ii_ref, jj_ref, qT_ref, k_ref, vT_ref, o_ref, m_sc, acc_sc, *, tq, tk, ahead):
    p = pl.program_id(1)
    i = ii_ref[p]
    j = jj_ref[p]

    @pl.when(j == 0)
    def _():
        m_sc[...] = jnp.full(m_sc.shape, NEG, F32)
        acc_sc[...] = jnp.zeros(acc_sc.shape, F32)

    def scores(u):
        h, qc, kb, _ = u
        return jnp.dot(k_ref[kb * ATT_UK:(kb + 1) * ATT_UK, h * HP:(h + 1) * HP],
                       qT_ref[0, h * HP:(h + 1) * HP, qc * ATT_UQ:(qc + 1) * ATT_UQ],
                       preferred_element_type=F32)

    def update(u, s):
        h, qc, kb, masked = u
        if masked:
            kpos = kb * ATT_UK + lax.broadcasted_iota(jnp.int32, s.shape, 0)
            qpos = qc * ATT_UQ + lax.broadcasted_iota(jnp.int32, s.shape, 1)
            s = jnp.where(kpos <= qpos, s, NEG)
        alphas, pms = [], []
        for c in range(ATT_UQ // 128):
            ql = slice(qc * ATT_UQ + c * 128, qc * ATT_UQ + (c + 1) * 128)
            sc = s[:, c * 128:(c + 1) * 128]
            m_old = m_sc[h:h + 1, ql]
            m_new = jnp.maximum(m_old, jnp.max(sc, axis=0, keepdims=True))
            alphas.append(jnp.exp2(m_old - m_new))
            pms.append(jnp.exp2(sc - m_new).astype(BF16))
            m_sc[h:h + 1, ql] = m_new
        ql = slice(qc * ATT_UQ, (qc + 1) * ATT_UQ)
        rows = slice(h * VR, (h + 1) * VR)
        acc_sc[rows, ql] = jnp.concatenate(alphas, axis=1) * acc_sc[rows, ql] + jnp.dot(
            vT_ref[0, rows, kb * ATT_UK:(kb + 1) * ATT_UK], jnp.concatenate(pms, axis=1),
            preferred_element_type=F32)

    def run(units):
        pending = [scores(u) for u in units[:ahead]]
        for n, u in enumerate(units):
            s = pending.pop(0)
            if n + ahead < len(units):
                pending.append(scores(units[n + ahead]))
            update(u, s)

    def unit_list(diag):
        units = []
        for kb in range(tk // ATT_UK):
            for h in range(HEADS):
                for qc in range(tq // ATT_UQ):
                    k_lo, k_hi = kb * ATT_UK, (kb + 1) * ATT_UK - 1
                    q_lo, q_hi = qc * ATT_UQ, (qc + 1) * ATT_UQ - 1
                    if diag and k_lo > q_hi:
                        continue
                    units.append((h, qc, kb, diag and k_hi > q_lo))
        return units

    @pl.when(j < i)
    def _():
        run(unit_list(False))

    @pl.when(j == i)
    def _():
        run(unit_list(True))
        parts = []
        for h in range(HEADS):
            den = acc_sc[h * VR + HD:h * VR + HD + 1, :]
            parts.append(acc_sc[h * VR:h * VR + HD, :] * (1.0 / den))
        o_ref[...] = jnp.concatenate(parts, axis=0).T


def _attention(qT, k, vT, bsz, seq, name, tq=2048, ahead=5):
    t = k.shape[0]
    nq = seq // tq
    ii = np.concatenate([np.full(i + 1, i, np.int32) for i in range(nq)])
    jj = np.concatenate([np.arange(i + 1, dtype=np.int32) for i in range(nq)])
    grid_spec = pltpu.PrefetchScalarGridSpec(
        num_scalar_prefetch=2,
        grid=(bsz, len(ii)),
        in_specs=[
            pl.BlockSpec((1, HEADS * HP, tq), lambda b, p, ii, jj: (b, 0, ii[p])),
            pl.BlockSpec((tq, HEADS * HP), lambda b, p, ii, jj: (b * nq + jj[p], 0)),
            pl.BlockSpec((1, HEADS * VR, tq), lambda b, p, ii, jj: (b, 0, jj[p])),
        ],
        out_specs=pl.BlockSpec((tq, GROUP_W), lambda b, p, ii, jj: (b * nq + ii[p], 0)),
        scratch_shapes=[
            pltpu.VMEM((8, tq), F32),
            pltpu.VMEM((HEADS * VR, tq), F32),
        ],
    )
    return pl.pallas_call(
        functools.partial(_attn_body, tq=tq, tk=tq, ahead=ahead),
        grid_spec=grid_spec,
        out_shape=jax.ShapeDtypeStruct((t, GROUP_W), F32),
        compiler_params=_cparams(("arbitrary", "arbitrary")),
        name=name,
    )(jnp.asarray(ii), jnp.asarray(jj), qT, k, vT)


def _conv_silu(xpad, src, cw_ref, cb_ref, first, ls):
    width = xpad.shape[1]

    @pl.when(first)
    def _():
        xpad[0:8, :] = jnp.zeros((8, width), F32)

    @pl.when(jnp.logical_not(first))
    def _():
        xpad[0:8, :] = xpad[ls:ls + 8, :]

    xpad[8:ls + 8, :] = src
    acc = cb_ref[...] + cw_ref[0:1, :] * xpad[pl.ds(8 - CONV_K + 1, ls), :]
    for kk in range(1, CONV_K):
        acc = acc + cw_ref[kk:kk + 1, :] * xpad[pl.ds(8 - CONV_K + 1 + kk, ls), :]
    return acc * _sigmoid(acc)


def _ssd_body(zx_ref, sm_ref, cw_ref, cb_ref, dtb_ref, arow_ref, dexp_ref, ng_ref, tri_ref, hexp_ref,
              o_ref, xpad, state, *, ls):
    first = pl.program_id(1) == 0

    @pl.when(first)
    def _():
        state[...] = jnp.zeros(state.shape, F32)

    xc = _conv_silu(xpad, zx_ref[:, GROUP_W:GROUP_W + SSM_XBC], cw_ref, cb_ref, first, ls)
    dt_all = _softplus(sm_ref[...] + dtb_ref[...])
    da_all = dt_all * arow_ref[...]
    lane = lax.broadcasted_iota(jnp.int32, (1, 128), 1)
    half = lane < HD
    head_lanes = lane < HEADS
    hexp = hexp_ref[...].astype(BF16)
    tril = _tril(CHUNK)

    def expand(x):
        x = jnp.where(head_lanes, x, 0.0)
        hi = x.astype(BF16)
        lo = (x - hi.astype(F32)).astype(BF16)
        return (jnp.dot(hi, hexp, preferred_element_type=F32)
                + jnp.dot(lo, hexp, preferred_element_type=F32))

    for c in range(ls // CHUNK):
        r0 = c * CHUNK
        acum = jnp.dot(tri_ref[...], da_all[r0:r0 + CHUNK], precision=HIGHEST,
                       preferred_element_type=F32)
        acum_t = acum.T
        alast = acum[CHUNK - 1:CHUNK, :]
        ex = expand(jnp.concatenate([dt_all[r0:r0 + CHUNK], jnp.exp(acum), jnp.exp(alast - acum),
                                     jnp.exp(alast), jnp.zeros((7, 128), F32)], axis=0))
        dt_x = ex[0:CHUNK]
        ecol_x = ex[CHUNK:2 * CHUNK]
        dout_x = ex[2 * CHUNK:3 * CHUNK]
        cdec_x = ex[3 * CHUNK:3 * CHUNK + 1]
        for g in range(2):
            h0, h1 = 2 * g, 2 * g + 1
            gl = slice(g * 128, (g + 1) * 128)
            xg = xc[r0:r0 + CHUNK, gl]
            bg = xc[r0:r0 + CHUNK, GROUP_W + g * 128:GROUP_W + (g + 1) * 128]
            cg = xc[r0:r0 + CHUNK, 2 * GROUP_W + g * 128:2 * GROUP_W + (g + 1) * 128]
            cgb = cg.astype(BF16)
            gram = lax.dot_general(cgb, bg.astype(BF16), NT_DIMS, preferred_element_type=F32)
            xdt = xg * dt_x[:, gl]
            ydiag = jnp.zeros((CHUNK, 128), F32)
            for hh, msk in ((h0, half), (h1, jnp.logical_not(half))):
                dec = jnp.exp(jnp.where(tril, acum[:, hh:hh + 1] - acum_t[hh:hh + 1, :], -jnp.inf))
                ydiag = ydiag + jnp.dot((gram * dec).astype(BF16),
                                        jnp.where(msk, xdt, 0.0).astype(BF16),
                                        preferred_element_type=F32)
            st = state[g]
            yoff = jnp.dot(cgb, st.astype(BF16), preferred_element_type=F32) * ecol_x[:, gl]
            state[g] = (cdec_x[:, gl] * st
                        + jnp.dot(bg.T.astype(BF16), (xdt * dout_x[:, gl]).astype(BF16),
                                  preferred_element_type=F32))
            y = ydiag + yoff + xg * dexp_ref[:, g * 128:(g + 1) * 128]
            zg = zx_ref[r0:r0 + CHUNK, g * 128:(g + 1) * 128]
            y = y * (zg * _sigmoid(zg))
            o_ref[r0:r0 + CHUNK, g * 128:(g + 1) * 128] = _rms(y, ng_ref[:, g * 128:(g + 1) * 128])


def _ssd(main, cw, cb, dtb, arow, dexp, ng, tri, hexp, bsz, seq, ls=512):
    t = main.shape[0]
    ns = seq // ls

    def full(shape):
        return pl.BlockSpec(shape, lambda b, s: tuple(0 for _ in shape))

    return pl.pallas_call(
        functools.partial(_ssd_body, ls=ls),
        grid=(bsz, ns),
        in_specs=[
            pl.BlockSpec((ls, 1024), lambda b, s: (b * ns + s, COL_ZX // 1024)),
            pl.BlockSpec((ls, 128), lambda b, s: (b * ns + s, COL_SMALL // 128)),
            full((CONV_K, SSM_XBC)), full((1, SSM_XBC)), full((1, 128)), full((1, 128)),
            full((1, GROUP_W)), full((1, GROUP_W)), full((CHUNK, CHUNK)), full((128, GROUP_W)),
        ],
        out_specs=pl.BlockSpec((ls, GROUP_W), lambda b, s: (b * ns + s, 0)),
        out_shape=jax.ShapeDtypeStruct((t, GROUP_W), F32),
        scratch_shapes=[pltpu.VMEM((ls + 8, SSM_XBC), F32), pltpu.VMEM((2, SSM_STATE, 128), F32)],
        compiler_params=_cparams(("arbitrary", "arbitrary")),
        name="ssd",
    )(main, main, cw, cb, dtb, arow, dexp, ng, tri, hexp)


def _mlstm_body(c_ref, sm_ref, cw_ref, cb_ref, igb_ref, fgb_ref, ng_ref, tri_ref, seg_ref, segm_ref,
                bmask_ref, o_ref, xpad, cst, nst, mst, *, ls):
    first = pl.program_id(1) == 0

    @pl.when(first)
    def _():
        cst[...] = jnp.zeros(cst.shape, F32)
        nst[...] = jnp.zeros(nst.shape, F32)
        mst[...] = jnp.zeros(mst.shape, F32)

    qk = _conv_silu(xpad, c_ref[:, 0:2 * GROUP_W], cw_ref, cb_ref, first, ls)
    sm = sm_ref[...]
    ig_all = sm + igb_ref[...]
    fx = sm + fgb_ref[...]
    lf_all = jnp.minimum(fx, 0.0) - jnp.log1p(jnp.exp(-jnp.abs(fx)))
    lane = lax.broadcasted_iota(jnp.int32, (1, GROUP_W), 1)
    hms = [jnp.logical_and(lane >= h * HD, lane < (h + 1) * HD) for h in range(HEADS)]
    tril = _tril(CHUNK)
    lane_s = lax.broadcasted_iota(jnp.int32, (1, 128), 1)
    gate_lanes = jnp.logical_and(lane_s >= GATE_LANE, lane_s < GATE_LANE + HEADS)
    segb = seg_ref[...].astype(BF16)
    hexp = seg_ref[...].T.astype(BF16)
    segm_b = segm_ref[...].astype(BF16)
    zero_b = jnp.zeros((), BF16)
    n_chunks = ls // CHUNK

    def expand(x):
        x = jnp.where(gate_lanes, x, 0.0)
        hi = x.astype(BF16)
        lo = (x - hi.astype(F32)).astype(BF16)
        return (jnp.dot(hi, hexp, preferred_element_type=F32)
                + jnp.dot(lo, hexp, preferred_element_type=F32))

    local = []
    for c in range(n_chunks):
        r0 = c * CHUNK
        q = qk[r0:r0 + CHUNK, 0:GROUP_W]
        k = qk[r0:r0 + CHUNK, GROUP_W:2 * GROUP_W] * (HD ** -0.5)
        ig = ig_all[r0:r0 + CHUNK]
        bcs = jnp.dot(tri_ref[...], lf_all[r0:r0 + CHUNK], precision=HIGHEST,
                      preferred_element_type=F32)
        bcs_t = bcs.T
        ig_t = ig.T
        ig_al = pltpu.roll(ig, GATE_LANE - 4, 1)
        k_t = k.T.astype(BF16)
        qb = q.astype(BF16)
        vb = c_ref[r0:r0 + CHUNK, 2 * GROUP_W:3 * GROUP_W].astype(BF16)
        num_loc = jnp.zeros((CHUNK, GROUP_W), F32)
        m_loc_all = jnp.zeros((CHUNK, 128), F32)
        den_loc_all = jnp.zeros((CHUNK, 128), F32)
        for h in range(HEADS):
            bcol = bcs[:, GATE_LANE + h:GATE_LANE + h + 1]
            dmat = jnp.where(tril, bcol - bcs_t[GATE_LANE + h:GATE_LANE + h + 1, :] + ig_t[4 + h:5 + h, :],
                             -jnp.inf)
            m_loc = jnp.max(dmat, axis=1, keepdims=True)
            s_qk = jnp.dot(jnp.where(hms[h], qb, zero_b), k_t, preferred_element_type=F32)
            sc = s_qk * jnp.exp(dmat - m_loc)
            num_loc = num_loc + jnp.dot(sc.astype(BF16), jnp.where(hms[h], vb, zero_b),
                                        preferred_element_type=F32)
            on_lane = lane_s == GATE_LANE + h
            m_loc_all = jnp.where(on_lane, m_loc, m_loc_all)
            den_loc_all = jnp.where(on_lane, jnp.sum(sc, axis=1, keepdims=True), den_loc_all)
        blast = bcs[CHUNK - 1:CHUNK, :]
        gcol = blast - bcs + ig_al
        gmax = jnp.max(gcol, axis=0, keepdims=True)
        kw = k * expand(jnp.exp(gcol - gmax))
        kwv = bmask_ref[...] * jnp.dot(kw.T.astype(BF16), vb, preferred_element_type=F32)
        local.append((q, num_loc, bcs, m_loc_all, den_loc_all, blast, gmax, kwv,
                      jnp.sum(kw, axis=0, keepdims=True)))

    houts = []
    for c in range(n_chunks):
        q, num_loc, bcs, m_loc_all, den_loc_all, blast, gmax, kwv, ksum = local[c]
        c_old = cst[...]
        n_old = nst[0:1, :]
        m_prev = mst[0:1, :]
        q_c = jnp.dot(q.astype(BF16), c_old.astype(BF16), preferred_element_type=F32)
        q_n = jnp.dot((q * n_old).astype(BF16), segb, preferred_element_type=F32)
        inter = bcs + m_prev
        m_t = jnp.maximum(inter, m_loc_all)
        f = jnp.exp(m_loc_all - m_t)
        w_inter = jnp.exp(inter - m_t)
        den = f * den_loc_all + w_inter * q_n
        inv_den = 1.0 / jnp.maximum(jnp.abs(den), jnp.exp(-m_t))
        m_new = jnp.maximum(blast + m_prev, gmax)
        scal = jnp.concatenate([jnp.exp(gmax - m_new), jnp.exp(blast + m_prev - m_new),
                                jnp.zeros((6, 128), F32)], axis=0)
        ex = expand(jnp.concatenate([f, w_inter, inv_den, scal], axis=0))
        f_x = ex[0:CHUNK]
        w_inter_x = ex[CHUNK:2 * CHUNK]
        inv_den_x = ex[2 * CHUNK:3 * CHUNK]
        sfac_x = ex[3 * CHUNK:3 * CHUNK + 1]
        keep_x = ex[3 * CHUNK + 1:3 * CHUNK + 2]
        mst[0:1, :] = m_new
        cst[...] = keep_x * c_old + sfac_x * kwv
        nst[0:1, :] = keep_x * n_old + sfac_x * ksum
        houts.append((f_x * num_loc + w_inter_x * q_c) * inv_den_x)

    for c in range(n_chunks):
        r0 = c * CHUNK
        hh = houts[c] * _sigmoid(c_ref[r0:r0 + CHUNK, 3 * GROUP_W:4 * GROUP_W])
        sq = hh * hh
        sq_hi = sq.astype(BF16)
        sq_lo = (sq - sq_hi.astype(F32)).astype(BF16)
        ms = (jnp.dot(sq_hi, segm_b, preferred_element_type=F32)
              + jnp.dot(sq_lo, segm_b, preferred_element_type=F32))
        o_ref[r0:r0 + CHUNK, :] = hh * lax.rsqrt(ms + EPS) * ng_ref[...]


def _mlstm(main, cw, cb, igb, fgb, ng, tri, seg, segm, bmask, bsz, seq, ls=512):
    t = main.shape[0]
    ns = seq // ls

    def full(shape):
        return pl.BlockSpec(shape, lambda b, s: tuple(0 for _ in shape))

    return pl.pallas_call(
        functools.partial(_mlstm_body, ls=ls),
        grid=(bsz, ns),
        in_specs=[
            pl.BlockSpec((ls, 1024), lambda b, s: (b * ns + s, COL_ML // 1024)),
            pl.BlockSpec((ls, 128), lambda b, s: (b * ns + s, COL_SMALL // 128)),
            full((CONV_K, 2 * GROUP_W)), full((1, 2 * GROUP_W)), full((1, 128)), full((1, 128)),
            full((1, GROUP_W)), full((CHUNK, CHUNK)), full((GROUP_W, 128)), full((GROUP_W, GROUP_W)),
            full((GROUP_W, GROUP_W)),
        ],
        out_specs=pl.BlockSpec((ls, GROUP_W), lambda b, s: (b * ns + s, 0)),
        out_shape=jax.ShapeDtypeStruct((t, GROUP_W), F32),
        scratch_shapes=[
            pltpu.VMEM((ls + 8, 2 * GROUP_W), F32),
            pltpu.VMEM((GROUP_W, GROUP_W), F32),
            pltpu.VMEM((8, GROUP_W), F32),
            pltpu.VMEM((8, 128), F32),
        ],
        compiler_params=_cparams(("arbitrary", "arbitrary")),
        name="mlstm",
    )(main, main, cw, cb, igb, fgb, ng, tri, seg, segm, bmask)


def _outproj_body(x_ref, ya_ref, yb_ref, yc_ref, yd_ref, wo_ref, g2_ref, wr_ref, br_ref,
                  xn_ref, h2_ref, cb_ref):
    sub = x_ref.shape[0] // 2
    accs = []
    for s in range(2):
        rows = slice(s * sub, (s + 1) * sub)
        acc = x_ref[rows, :]
        for gi, y_ref in enumerate((ya_ref, yb_ref, yc_ref, yd_ref)):
            acc = acc + jnp.dot(y_ref[rows, :].astype(BF16), wo_ref[gi * GROUP_W:(gi + 1) * GROUP_W, :],
                                preferred_element_type=F32)
        accs.append(acc)
    for s in range(2):
        _route(accs[s], slice(s * sub, (s + 1) * sub), g2_ref, wr_ref, br_ref, xn_ref, h2_ref, cb_ref)


def _route(acc, rows, g2_ref, wr_ref, br_ref, xn_ref, h2_ref, cb_ref):
    xn_ref[rows, :] = acc
    h2 = _rms(acc, g2_ref[...])
    h_hi = h2.astype(BF16)
    h2_ref[rows, :] = h_hi
    h_lo = (h2 - h_hi.astype(F32)).astype(BF16)
    hh = jnp.dot(h_hi, wr_ref[...], preferred_element_type=F32)
    lh = jnp.dot(h_lo, wr_ref[...], preferred_element_type=F32)
    logits = hh[:, :128] + hh[:, 128:] + lh[:, :128] + br_ref[...]
    lane = lax.broadcasted_iota(jnp.int32, logits.shape, 1)
    big = jnp.int32(1 << 20)
    n_grp = N_EXPERTS // EXPERTS_PER_GROUP
    gl = jnp.where(jnp.logical_and(lane >= N_EXPERTS, lane < N_EXPERTS + n_grp), logits, -jnp.inf)
    gmax = jnp.max(gl, axis=1, keepdims=True)
    g_w = 1.0 / jnp.sum(jnp.exp(gl - gmax), axis=1, keepdims=True)
    gidx = jnp.min(jnp.where(gl == gmax, lane, big), axis=1, keepdims=True) - N_EXPERTS
    in_grp = jnp.logical_and(lane < N_EXPERTS,
                             lax.shift_right_logical(lane, int(math.log2(EXPERTS_PER_GROUP))) == gidx)
    el = jnp.where(in_grp, logits, -jnp.inf)
    emax = jnp.max(el, axis=1, keepdims=True)
    esum = jnp.sum(jnp.exp(el - emax), axis=1, keepdims=True)
    i1 = jnp.min(jnp.where(el == emax, lane, big), axis=1, keepdims=True)
    el2 = jnp.where(lane == i1, -jnp.inf, el)
    emax2 = jnp.max(el2, axis=1, keepdims=True)
    i2 = jnp.min(jnp.where(el2 == emax2, lane, big), axis=1, keepdims=True)
    p1 = 1.0 / esum
    p2 = jnp.exp(emax2 - emax) / esum
    w1 = p1 / (p1 + p2)
    w2 = p2 / (p1 + p2)
    comb = g_w * (jnp.where(lane == i1, w1, 0.0) + jnp.where(lane == i2, w2, 0.0))
    cb_ref[rows, :] = jnp.where(lane == GID_LANE, gidx.astype(F32), comb)


def _outproj(x2, ya, yb, yc, yd, wo, g2, wr, br, tm=512):
    t = x2.shape[0]
    row = pl.BlockSpec((tm, D_MODEL), lambda i: (i, 0))
    grp = pl.BlockSpec((tm, GROUP_W), lambda i: (i, 0))

    def full(shape):
        return pl.BlockSpec(shape, lambda i: tuple(0 for _ in shape))

    return pl.pallas_call(
        _outproj_body,
        grid=(t // tm,),
        in_specs=[row, grp, grp, grp, grp, full((D_MODEL, D_MODEL)), full((1, D_MODEL)),
                  full((D_MODEL, 256)), full((1, 128))],
        out_specs=[row, row, pl.BlockSpec((tm, 128), lambda i: (i, 0))],
        out_shape=[
            jax.ShapeDtypeStruct((t, D_MODEL), F32),
            jax.ShapeDtypeStruct((t, D_MODEL), BF16),
            jax.ShapeDtypeStruct((t, 128), F32),
        ],
        compiler_params=_cparams(("arbitrary",)),
        name="outproj_router",
    )(x2, ya, yb, yc, yd, wo, g2, wr, br)


MOE_EB = EXPERTS_PER_GROUP
MOE_RB = 128


def _moe_body(h_ref, cb_ref, xn_ref, tri_ref, wg_ref, wu_ref, wd_ref, gf_ref, o_ref,
              perm_t, hs, cs, ys, seg_smem, *, final_norm, rb):
    g = pl.program_id(1)
    tm = h_ref.shape[0]
    n_grp = N_EXPERTS // EXPERTS_PER_GROUP
    lane = lax.broadcasted_iota(jnp.int32, (tm, 128), 1)

    @pl.when(g == 0)
    def _():
        cb = cb_ref[...]
        gid = jnp.sum(jnp.where(lane == GID_LANE, cb, 0.0), axis=1, keepdims=True).astype(jnp.int32)
        onehot = jnp.where(lane == gid, 1.0, 0.0)
        oh16 = onehot.astype(BF16)
        tot = jnp.zeros((1, 128), F32)
        parts = []
        for r0 in range(0, tm, CHUNK):
            part = jnp.dot(tri_ref[...], oh16[r0:r0 + CHUNK], preferred_element_type=F32) + tot
            parts.append(part)
            tot = part[CHUNK - 1:CHUNK, :]
        cum = jnp.concatenate(parts, axis=0)
        lane1 = lane[0:1, :]
        base = jnp.zeros((1, 128), F32)
        run = jnp.zeros((1, 1), F32)
        for k in range(n_grp):
            base = base + jnp.where(lane1 == k, run, 0.0)
            seg_smem[k] = run[0, 0].astype(jnp.int32)
            seg_smem[n_grp + k] = tot[0, k].astype(jnp.int32)
            run = run + tot[:, k:k + 1]
        pos = jnp.sum(onehot * (base + cum - 1.0), axis=1, keepdims=True).astype(jnp.int32)
        pos_row = jnp.broadcast_to(pos.astype(F32), (tm, 128)).T[0:1, :].astype(jnp.int32)
        perm_t[...] = jnp.where(lax.broadcasted_iota(jnp.int32, (tm, tm), 1) == pos, 1.0, 0.0).astype(BF16)
        p = jnp.where(lax.broadcasted_iota(jnp.int32, (tm, tm), 0) == pos_row, 1.0, 0.0).astype(BF16)
        hs[...] = jnp.dot(p, h_ref[...], preferred_element_type=F32).astype(BF16)
        cb_hi = cb.astype(BF16)
        cb_lo = (cb - cb_hi.astype(F32)).astype(BF16)
        cs2 = jnp.dot(p, jnp.concatenate([cb_hi, cb_lo], axis=1), preferred_element_type=F32)
        cs[...] = cs2[:, :128] + cs2[:, 128:]
        ys[...] = jnp.zeros(ys.shape, F32)

    seg_lo = seg_smem[g]
    seg_hi = seg_lo + seg_smem[n_grp + g]
    lane_rb = lax.broadcasted_iota(jnp.int32, (rb, 128), 1)

    def run_block(lo):
        x = hs[lo:lo + rb, :]
        cw = cs[lo:lo + rb, :]

        def gate_up(j):
            return (jnp.dot(x, wg_ref[j], preferred_element_type=F32),
                    jnp.dot(x, wu_ref[j], preferred_element_type=F32))

        def down(j, gu):
            col = jnp.sum(jnp.where(lane_rb == g * MOE_EB + j, cw, 0.0), axis=1,
                          keepdims=True)
            gt, up = gu
            hid = (gt * _sigmoid(gt)) * up * col
            return jnp.dot(hid.astype(BF16), wd_ref[j], preferred_element_type=F32)

        total = jnp.zeros((rb, D_MODEL), F32)
        gu = gate_up(0)
        for j in range(MOE_EB):
            gu_next = gate_up(j + 1) if j + 1 < MOE_EB else None
            total = total + down(j, gu)
            gu = gu_next
        ys[lo:lo + rb, :] += total

    for b in range(tm // rb):
        lo = b * rb
        pl.when(jnp.logical_and(seg_lo < lo + rb, seg_hi > lo))(functools.partial(run_block, lo))

    @pl.when(g == n_grp - 1)
    def _():
        y = xn_ref[...] + jnp.dot(perm_t[...], ys[...].astype(BF16), preferred_element_type=F32)
        if final_norm:
            y = _rms(y, gf_ref[...])
        o_ref[...] = y


def _moe(h2, comb, xn, tri, wg, wu, wd, gf, final_norm, tm=1024, rb=MOE_RB):
    t = h2.shape[0]
    return pl.pallas_call(
        functools.partial(_moe_body, final_norm=final_norm, rb=rb),
        grid=(t // tm, N_EXPERTS // MOE_EB),
        in_specs=[
            pl.BlockSpec((tm, D_MODEL), lambda i, e: (i, 0)),
            pl.BlockSpec((tm, 128), lambda i, e: (i, 0)),
            pl.BlockSpec((tm, D_MODEL), lambda i, e: (i, 0)),
            pl.BlockSpec((CHUNK, CHUNK), lambda i, e: (0, 0)),
            pl.BlockSpec((MOE_EB, D_MODEL, D_EXPERT), lambda i, e: (e, 0, 0)),
            pl.BlockSpec((MOE_EB, D_MODEL, D_EXPERT), lambda i, e: (e, 0, 0)),
            pl.BlockSpec((MOE_EB, D_EXPERT, D_MODEL), lambda i, e: (e, 0, 0)),
            pl.BlockSpec((1, D_MODEL), lambda i, e: (0, 0)),
        ],
        out_specs=pl.BlockSpec((tm, D_MODEL), lambda i, e: (i, 0)),
        out_shape=jax.ShapeDtypeStruct((t, D_MODEL), F32),
        scratch_shapes=[
            pltpu.VMEM((tm, tm), BF16),
            pltpu.VMEM((tm, D_MODEL), BF16),
            pltpu.VMEM((tm, 128), F32),
            pltpu.VMEM((tm, D_MODEL), F32),
            pltpu.SMEM((2 * (N_EXPERTS // MOE_EB),), jnp.int32),
        ],
        compiler_params=_cparams(("arbitrary", "arbitrary")),
        name="moe",
    )(h2, comb, xn, tri, wg, wu, wd, gf)


def _spread_heads(w, width, slab, total):
    rows = w.shape[0]
    pieces = []
    for h in range(HEADS):
        pieces.append(w[:, h * width:(h + 1) * width])
        pieces.append(jnp.zeros((rows, slab - width), w.dtype))
    if total > HEADS * slab:
        pieces.append(jnp.zeros((rows, total - HEADS * slab), w.dtype))
    return jnp.concatenate(pieces, axis=1)


def _pad_heads(w, width):
    return _spread_heads(w, width, HP, HEADS * HP)


def _v_cols(w):
    return _spread_heads(w, HD, VR, V_COLS)


def _lane_row(vals, offset, width=128):
    return jnp.zeros((1, width), F32).at[0, offset:offset + vals.shape[0]].set(vals.astype(F32))


def _layer_weights(l, p):
    half = MLA_ROPE // 2
    w = {"g_mix": p["norm_mix_g"][l][None, :]}
    w["ssm_cw"] = p["ssm_conv_w"][l]
    w["ssm_cb"] = p["ssm_conv_b"][l][None, :]
    w["ssm_dtb"] = _lane_row(p["ssm_dt_bias"][l], 0)
    w["ssm_arow"] = _lane_row(-jnp.exp(p["ssm_a_log"][l].astype(F32)), 0)
    w["ssm_dexp"] = jnp.repeat(p["ssm_d"][l].astype(F32), HD)[None, :]
    w["ssm_ng"] = p["ssm_norm_g"][l][None, :]
    w["ml_cw"] = p["ml_conv_w"][l]
    w["ml_cb"] = p["ml_conv_b"][l][None, :]
    w["ml_igb"] = _lane_row(p["ml_ig_bias"][l], 4)
    w["ml_fgb"] = _lane_row(p["ml_fg_bias"][l], 8)
    w["ml_ng"] = p["ml_norm_g"][l][None, :]
    wuq = p["mla_w_uq"][l]
    qd = MLA_NOPE + MLA_ROPE
    swapped = []
    for h in range(HEADS):
        c0 = h * qd + MLA_NOPE
        swapped += [jnp.zeros((256, MLA_NOPE), F32), wuq[:, c0 + half:c0 + 2 * half], wuq[:, c0:c0 + half]]
    wq = _spread_heads(wuq, qd, HP, HEADS * HP)
    wqs = _spread_heads(jnp.concatenate(swapped, axis=1), qd, HP, HEADS * HP)
    w["mla_wq2"] = jnp.concatenate([wq, wqs], axis=1).astype(BF16)
    wukv = p["mla_w_ukv"][l]
    wk_cols = jnp.concatenate([wukv[:, h * 2 * HD:h * 2 * HD + MLA_NOPE] for h in range(HEADS)], axis=1)
    wv_cols = jnp.concatenate([wukv[:, h * 2 * HD + MLA_NOPE:(h + 1) * 2 * HD] for h in range(HEADS)], axis=1)
    w["mla_wk"] = _pad_heads(wk_cols, MLA_NOPE).astype(BF16)
    w["mla_wv"] = _v_cols(wv_cols).astype(BF16)
    w["mla_gq"] = p["mla_q_norm_g"][l][None, :]
    w["mla_gkv"] = p["mla_kv_norm_g"][l][None, :]
    w["w_out"] = p["w_out"][l].astype(BF16)
    w["g_ffn"] = p["norm_ffn_g"][l][None, :]
    wr = jnp.zeros((D_MODEL, 128), F32)
    wr = wr.at[:, :N_EXPERTS].set(p["moe_w_router"][l]).at[:, N_EXPERTS:N_EXPERTS + 4].set(p["moe_w_group"][l])
    wr_hi = wr.astype(BF16)
    w["w_router"] = jnp.concatenate([wr_hi, (wr - wr_hi.astype(F32)).astype(BF16)], axis=1)
    w["b_router"] = _lane_row(p["moe_b_router"][l], 0) + _lane_row(p["moe_b_group"][l], N_EXPERTS)
    w["w_gate"] = p["moe_w_gate"][l].astype(BF16)
    w["w_up"] = p["moe_w_up"][l].astype(BF16)
    w["w_down"] = p["moe_w_down"][l].astype(BF16)
    return w


def kernel(x, positions, norm_mix_g, w_in, ssm_conv_w, ssm_conv_b, ssm_dt_bias, ssm_a_log, ssm_d, ssm_norm_g, ml_conv_w, ml_conv_b, ml_ig_bias, ml_fg_bias, ml_norm_g, mla_q_norm_g, mla_w_uq, mla_kv_norm_g, mla_w_ukv, w_out, norm_ffn_g, moe_w_group, moe_b_group, moe_w_router, moe_b_router, moe_w_gate, moe_w_up, moe_w_down, final_norm_g):
    p = dict(norm_mix_g=norm_mix_g, w_all=_relayout_w_in(w_in), ssm_conv_w=ssm_conv_w, ssm_conv_b=ssm_conv_b,
             ssm_dt_bias=ssm_dt_bias, ssm_a_log=ssm_a_log, ssm_d=ssm_d, ssm_norm_g=ssm_norm_g,
             ml_conv_w=ml_conv_w, ml_conv_b=ml_conv_b, ml_ig_bias=ml_ig_bias, ml_fg_bias=ml_fg_bias,
             ml_norm_g=ml_norm_g, mla_q_norm_g=mla_q_norm_g, mla_w_uq=mla_w_uq,
             mla_kv_norm_g=mla_kv_norm_g, mla_w_ukv=mla_w_ukv, w_out=w_out, norm_ffn_g=norm_ffn_g,
             moe_w_group=moe_w_group, moe_b_group=moe_b_group, moe_w_router=moe_w_router,
             moe_b_router=moe_b_router, moe_w_gate=moe_w_gate, moe_w_up=moe_w_up, moe_w_down=moe_w_down)
    bsz, seq, _ = x.shape
    depth = w_in.shape[0]
    x2 = x.reshape(bsz * seq, D_MODEL)

    half = MLA_ROPE // 2
    inv = ROPE_THETA ** (-jnp.arange(0, MLA_ROPE, 2, dtype=F32) / MLA_ROPE)
    inv_col = jnp.concatenate([inv, inv])[:, None]
    sgn_col = jnp.concatenate([-jnp.ones((half,), F32), jnp.ones((half,), F32)])[:, None]
    pe_mask = _lane_row(jnp.ones((MLA_ROPE,), F32), MLA_NOPE)
    c_tok, s_tok, c_tr, s_tr = _rope_tables(positions.reshape(bsz, 1, seq), inv_col, sgn_col, bsz, seq)
    tri = jnp.tril(jnp.ones((CHUNK, CHUNK), F32))
    tri_moe = tri.astype(BF16)
    head_of = jnp.arange(GROUP_W) // HD
    seg = (head_of[:, None] + GATE_LANE == jnp.arange(128)[None, :]).astype(F32)
    ssd_hexp = (jnp.arange(128)[:, None] == head_of[None, :]).astype(F32)
    bmask = (head_of[:, None] == head_of[None, :]).astype(F32)
    segm = bmask / HD

    for l in range(depth):
        w = _layer_weights(l, p)
        main, k_a, qT_a, vT_a, qT_d, k_d, vT_d = _in_proj(
            x2, w["g_mix"], p["w_all"], l, c_tok, s_tok, c_tr, s_tr, w["mla_gq"], w["mla_wq2"], w["mla_gkv"],
            w["mla_wk"], w["mla_wv"], pe_mask, bsz, seq)
        y_a = _attention(qT_a, k_a, vT_a, bsz, seq, "attn_moba")
        y_b = _ssd(main, w["ssm_cw"], w["ssm_cb"], w["ssm_dtb"], w["ssm_arow"], w["ssm_dexp"], w["ssm_ng"],
                   tri, ssd_hexp, bsz, seq)
        y_c = _mlstm(main, w["ml_cw"], w["ml_cb"], w["ml_igb"], w["ml_fgb"], w["ml_ng"], tri, seg, segm,
                     bmask, bsz, seq)
        y_d = _attention(qT_d, k_d, vT_d, bsz, seq, "attn_mla")
        xn, h2, comb = _outproj(x2, y_a, y_b, y_c, y_d, w["w_out"], w["g_ffn"], w["w_router"], w["b_router"])
        x2 = _moe(h2, comb, xn, tri_moe, w["w_gate"], w["w_up"], w["w_down"], final_norm_g[None, :],
                  final_norm=(l == depth - 1), tm=MOE_TM)
    return x2.reshape(bsz, seq, D_MODEL)
```
